```python
import jax
import jax.numpy as jnp
from jax import lax
import numpy as np

D_MODEL = 1024
BATCH = 32
SEQ = 256
DEPTH = 4
DEC_BATCH = 4
DEC_SEQ = 2048
PAST_LEN = 512

GRID_W = 64
ROPE_BASE = 10000.0
NORM_EPS = 1e-6
Q_BLOCK = 128
N_MIXERS = 2
N_A = (DEPTH + 1) // 2
N_B = DEPTH // 2

MLA_HEADS = 16
MLA_NOPE = 64
MLA_ROPE = 32
MLA_QK = MLA_NOPE + MLA_ROPE
MLA_V = 64
MLA_Q_RANK = 384
MLA_KV_RANK = 256

RET_HEADS = 4
RET_DK = D_MODEL // RET_HEADS
RET_DV = 2 * D_MODEL // RET_HEADS
RET_CHUNK = 128

FFN_DIM = 2816
N_EXPERTS = 8
TOP_K = 2
EXPERT_DIM = 3584
MOE_BLOCK = 128

kernel_name = 'hybrid_mla_retention_diffusion_step'


def rms_norm(x, w):
    xf = x.astype(jnp.float32)
    y = xf * lax.rsqrt(jnp.mean(xf * xf, axis=-1, keepdims=True) + NORM_EPS)
    return (y * w.astype(jnp.float32)).astype(x.dtype)


def adaln(cond, w, b):
    m = jax.nn.silu(cond) @ w + b
    m = m.reshape(-1, 1, 6 * D_MODEL)
    return jnp.split(m, 6, axis=-1)


def modulate(x, w_norm, shift, scale):
    return rms_norm(x, w_norm) * (1 + scale) + shift


def grid_rope_tables(n_tok, rot_dim):
    rows = n_tok // GRID_W
    row = jnp.repeat(jnp.arange(rows), GRID_W)
    col = jnp.tile(jnp.arange(GRID_W), rows)
    nf = rot_dim // 4
    inv = ROPE_BASE ** (-jnp.arange(nf, dtype=jnp.float32) / nf)
    ang = jnp.stack([row, col], axis=-1).astype(jnp.float32)[:, :, None] * inv
    return jnp.cos(ang), jnp.sin(ang)


def apply_grid_rope(x, cos, sin):
    nf = x.shape[-1] // 4
    xs = x.astype(jnp.float32).reshape(x.shape[:-1] + (2, 2, nf))
    x1, x2 = xs[..., 0, :], xs[..., 1, :]
    cs, sn = cos[:, None], sin[:, None]
    out = jnp.stack([x1 * cs - x2 * sn, x2 * cs + x1 * sn], axis=-2)
    return out.reshape(x.shape).astype(x.dtype)


def rope_tail(x, cos, sin):
    return jnp.concatenate([x[..., :MLA_NOPE], apply_grid_rope(x[..., MLA_NOPE:], cos, sin)], axis=-1)


def block_attention(q, k, v):
    b, tq, h, dk = q.shape
    nb = tq // Q_BLOCK
    qb = q.reshape(b, nb, Q_BLOCK, h, dk).transpose(1, 0, 2, 3, 4)
    scale = dk ** -0.5

    def one(qi):
        s = jnp.einsum('bqhd,bkhd->bhqk', qi, k, preferred_element_type=jnp.float32) * scale
        p = jax.nn.softmax(s, axis=-1).astype(v.dtype)
        return jnp.einsum('bhqk,bkhd->bqhd', p, v)

    o = lax.map(one, qb)
    return o.transpose(1, 0, 2, 3, 4).reshape(b, tq, h, v.shape[-1])


def mla_query(h, wq_a, q_a_norm, wq_b, q_norm):
    b, t, _ = h.shape
    q = rms_norm(h @ wq_a, q_a_norm) @ wq_b
    return rms_norm(q.reshape(b, t, MLA_HEADS, MLA_QK), q_norm)


def mla_compress_kv(h, wkv_a, kv_norm):
    kv = h @ wkv_a
    return rms_norm(kv[..., :MLA_KV_RANK], kv_norm), kv[..., MLA_KV_RANK:]


def mla_expand_kv(ckv, kpe, w_uk, w_uv, k_norm):
    b, t, _ = ckv.shape
    k_nope = (ckv @ w_uk).reshape(b, t, MLA_HEADS, MLA_NOPE)
    k_pe = jnp.broadcast_to(kpe[:, :, None, :], (b, t, MLA_HEADS, MLA_ROPE))
    k = rms_norm(jnp.concatenate([k_nope, k_pe], axis=-1), k_norm)
    v = (ckv @ w_uv).reshape(b, t, MLA_HEADS, MLA_V)
    return k, v


def chunk_retention(q, k, v, log_gamma, s0):
    b, t, h, _ = q.shape
    n = t // RET_CHUNK

    def chunks(a):
        return a.reshape(b, n, RET_CHUNK, h, a.shape[-1]).transpose(1, 0, 3, 2, 4)

    pos = jnp.arange(RET_CHUNK, dtype=jnp.float32)
    rel = pos[:, None] - pos[None, :]
    dmask = jnp.exp(jnp.where(rel >= 0, rel * log_gamma[:, None, None], -jnp.inf))
    q_decay = jnp.exp((pos + 1) * log_gamma[:, None])[..., None]
    k_decay = jnp.exp((RET_CHUNK - 1 - pos) * log_gamma[:, None])[..., None]
    c_decay = jnp.exp(RET_CHUNK * log_gamma)[:, None, None]

    def step(s, inp):
        qc, kc, vc = inp
        a = jnp.einsum('bhid,bhjd->bhij', qc, kc) * dmask
        o = jnp.einsum('bhij,bhjv->bhiv', a, vc) + jnp.einsum('bhid,bhdv->bhiv', qc * q_decay, s)
        s = s * c_decay + jnp.einsum('bhjd,bhjv->bhdv', kc * k_decay, vc)
        return s, o

    s, o = lax.scan(step, s0, (chunks(q), chunks(k), chunks(v)))
    return o.transpose(1, 0, 3, 2, 4).reshape(b, t, h, v.shape[-1]), s


def retention(h, s0_fwd, s0_bwd, rope, wq, wk, wv, wg, wo, decay, gn):
    b, t, _ = h.shape
    q = (h @ wq).reshape(b, t, RET_HEADS, RET_DK)
    k = (h @ wk).reshape(b, t, RET_HEADS, RET_DK) * (RET_DK ** -0.5)
    v = (h @ wv).reshape(b, t, RET_HEADS, RET_DV)
    if rope is not None:
        q = apply_grid_rope(q, rope[0], rope[1])
        k = apply_grid_rope(k, rope[0], rope[1])
    q, k, v = q.astype(jnp.float32), k.astype(jnp.float32), v.astype(jnp.float32)
    log_gamma = -jnp.exp(decay.astype(jnp.float32))
    o_f, s_f = chunk_retention(q, k, v, log_gamma[0], s0_fwd)
    o_b, s_b = chunk_retention(q[:, ::-1], k[:, ::-1], v[:, ::-1], log_gamma[1], s0_bwd)
    y = rms_norm(o_f + o_b[:, ::-1], gn.reshape(RET_HEADS, RET_DV))
    y = y.reshape(b, t, RET_HEADS * RET_DV).astype(h.dtype)
    return (jax.nn.silu(h @ wg) * y) @ wo, s_f, s_b


def swiglu(h, w_gate, w_up, w_down):
    return (jax.nn.silu(h @ w_gate) * (h @ w_up)) @ w_down


def moe_swiglu(x, w_router, w_gate, w_up, w_down):
    b, t, d = x.shape
    n = b * t
    xf = x.reshape(n, d)
    logits = (xf @ w_router).astype(jnp.float32)
    top_v, top_i = lax.top_k(logits, TOP_K)
    top_w = jax.nn.softmax(top_v, axis=-1)
    a = n * TOP_K
    e = top_i.reshape(a)
    tok = jnp.repeat(jnp.arange(n, dtype=jnp.int32), TOP_K)
    wts = top_w.reshape(a)
    order = jnp.argsort(e)
    e_s, tok_s, w_s = e[order], tok[order], wts[order]
    counts = jnp.bincount(e, length=N_EXPERTS)
    padded = (counts + MOE_BLOCK - 1) // MOE_BLOCK * MOE_BLOCK
    pad_end = jnp.cumsum(padded)
    pad_start = pad_end - padded
    start = jnp.cumsum(counts) - counts
    pos = pad_start[e_s] + jnp.arange(a, dtype=jnp.int32) - start[e_s]
    n_blocks = -(-a // MOE_BLOCK) + N_EXPERTS
    p = n_blocks * MOE_BLOCK
    buf_tok = jnp.full((p,), n, jnp.int32).at[pos].set(tok_s)
    buf_w = jnp.zeros((p,), jnp.float32).at[pos].set(w_s)
    blk_e = jnp.searchsorted(pad_end, jnp.arange(n_blocks, dtype=jnp.int32) * MOE_BLOCK, side='right')
    blk_e = jnp.minimum(blk_e, N_EXPERTS - 1)
    x_pad = jnp.concatenate([xf, jnp.zeros((1, d), xf.dtype)], axis=0)
    xb = x_pad[buf_tok].reshape(n_blocks, MOE_BLOCK, d)

    def expert_block(args):
        xi, ei = args
        return swiglu(xi, w_gate[ei], w_up[ei], w_down[ei])

    yb = lax.map(expert_block, (xb, blk_e)).reshape(p, d)
    out = jnp.zeros((n + 1, d), jnp.float32).at[buf_tok].add(yb.astype(jnp.float32) * buf_w[:, None])[:n]
    return out.astype(x.dtype).reshape(b, t, d)


def setup_inputs(seed: int = 0) -> dict:
    key = jax.random.key(seed)
    ks = iter(jax.random.split(key, 48))
    f32 = jnp.float32
    d = D_MODEL

    def nrm(shape, scale):
        return jax.random.normal(next(ks), shape, f32) * scale

    def lin(*shape):
        return nrm(shape, shape[-2] ** -0.5)

    def gain(*shape):
        return 1.0 + nrm(shape, 0.02)

    retnet_decay = jnp.log(-jnp.log1p(-(2.0 ** (-5.0 - jnp.arange(RET_HEADS, dtype=f32)))))
    inputs = {}
    inputs['x_prompt'] = nrm((BATCH, SEQ, d), 1.0)
    inputs['x_sample'] = nrm((DEC_BATCH, DEC_SEQ, d), 1.0)
    inputs['cache_ckv'] = nrm((DEC_BATCH, N_A, PAST_LEN, MLA_KV_RANK), 1.0)
    inputs['cache_kpe'] = nrm((DEC_BATCH, N_A, PAST_LEN, MLA_ROPE), 1.0)
    inputs['state_ret'] = nrm((DEC_BATCH, N_B, 2, RET_HEADS, RET_DK, RET_DV), 0.5)
    inputs['c'] = nrm((DEC_BATCH, d), 1.0)
    inputs['c_ctx'] = nrm((d,), 1.0)
    inputs['mod_w'] = nrm((DEPTH, d, 6 * d), 0.5 * d ** -0.5)
    inputs['mod_b'] = nrm((DEPTH, 6 * d), 0.02)
    inputs['norm1_w'] = gain(DEPTH, d)
    inputs['norm2_w'] = gain(DEPTH, d)
    inputs['mla_wq_a'] = lin(N_A, d, MLA_Q_RANK)
    inputs['mla_q_a_norm'] = gain(N_A, MLA_Q_RANK)
    inputs['mla_wq_b'] = lin(N_A, MLA_Q_RANK, MLA_HEADS * MLA_QK)
    inputs['mla_wkv_a'] = lin(N_A, d, MLA_KV_RANK + MLA_ROPE)
    inputs['mla_kv_norm'] = gain(N_A, MLA_KV_RANK)
    inputs['mla_w_uk'] = lin(N_A, MLA_KV_RANK, MLA_HEADS * MLA_NOPE)
    inputs['mla_w_uv'] = lin(N_A, MLA_KV_RANK, MLA_HEADS * MLA_V)
    inputs['mla_q_norm'] = gain(N_A, MLA_QK)
    inputs['mla_k_norm'] = gain(N_A, MLA_QK)
    inputs['mla_wo'] = lin(N_A, MLA_HEADS * MLA_V, d)
    inputs['ret_wq'] = lin(N_B, d, RET_HEADS * RET_DK)
    inputs['ret_wk'] = lin(N_B, d, RET_HEADS * RET_DK)
    inputs['ret_wv'] = lin(N_B, d, RET_HEADS * RET_DV)
    inputs['ret_wg'] = lin(N_B, d, RET_HEADS * RET_DV)
    inputs['ret_wo'] = lin(N_B, RET_HEADS * RET_DV, d)
    inputs['ret_decay'] = retnet_decay + nrm((N_B, 2, RET_HEADS), 0.1)
    inputs['ret_gn'] = gain(N_B, RET_HEADS * RET_DV)
    inputs['ffn_w_gate'] = lin(N_A, d, FFN_DIM)
    inputs['ffn_w_up'] = lin(N_A, d, FFN_DIM)
    inputs['ffn_w_down'] = lin(N_A, FFN_DIM, d)
    inputs['moe_router'] = lin(N_B, d, N_EXPERTS)
    inputs['moe_w_gate'] = lin(N_B, N_EXPERTS, d, EXPERT_DIM)
    inputs['moe_w_up'] = lin(N_B, N_EXPERTS, d, EXPERT_DIM)
    inputs['moe_w_down'] = lin(N_B, N_EXPERTS, EXPERT_DIM, d)
    return inputs


def reference(x_prompt, x_sample, cache_ckv, cache_kpe, state_ret, c, c_ctx, mod_w, mod_b, norm1_w, norm2_w,
              mla_wq_a, mla_q_a_norm, mla_wq_b, mla_wkv_a, mla_kv_norm, mla_w_uk, mla_w_uv, mla_q_norm, mla_k_norm,
              mla_wo, ret_wq, ret_wk, ret_wv, ret_wg, ret_wo, ret_decay, ret_gn, ffn_w_gate, ffn_w_up, ffn_w_down,
              moe_router, moe_w_gate, moe_w_up, moe_w_down):
    xc, xl = x_prompt, x_sample
    bc, tc, _ = xc.shape
    bl, tl, _ = xl.shape
    rope_a = grid_rope_tables(tl, MLA_ROPE)
    rope_b = grid_rope_tables(tl, RET_DK)
    zero_state = jnp.zeros((bc, RET_HEADS, RET_DK, RET_DV), jnp.float32)
    new_ckv, new_kpe, new_ret = [], [], []
    for i in range(DEPTH):
        j = i // N_MIXERS
        sh1c, sc1c, g1c, sh2c, sc2c, g2c = adaln(c_ctx, mod_w[i], mod_b[i])
        sh1l, sc1l, g1l, sh2l, sc2l, g2l = adaln(c, mod_w[i], mod_b[i])
        hc = modulate(xc, norm1_w[i], sh1c, sc1c)
        hl = modulate(xl, norm1_w[i], sh1l, sc1l)
        if i % N_MIXERS == 0:
            qp = (mla_wq_a[j], mla_q_a_norm[j], mla_wq_b[j], mla_q_norm[j])
            kvp = (mla_wkv_a[j], mla_kv_norm[j])
            ep = (mla_w_uk[j], mla_w_uv[j], mla_k_norm[j])
            ckv_c, kpe_c = mla_compress_kv(hc, *kvp)
            k_c, v_c = mla_expand_kv(ckv_c, kpe_c, *ep)
            oc = block_attention(mla_query(hc, *qp), k_c, v_c)
            new_ckv.append(ckv_c)
            new_kpe.append(kpe_c)
            ckv_l, kpe_l = mla_compress_kv(hl, *kvp)
            k_l, v_l = mla_expand_kv(ckv_l, kpe_l, *ep)
            k_x, v_x = mla_expand_kv(cache_ckv[:, j], cache_kpe[:, j], *ep)
            q_l = rope_tail(mla_query(hl, *qp), rope_a[0], rope_a[1])
            k_all = jnp.concatenate([rope_tail(k_l, rope_a[0], rope_a[1]), k_x], axis=1)
            v_all = jnp.concatenate([v_l, v_x], axis=1)
            ol = block_attention(q_l, k_all, v_all)
            oc = oc.reshape(bc, tc, MLA_HEADS * MLA_V) @ mla_wo[j]
            ol = ol.reshape(bl, tl, MLA_HEADS * MLA_V) @ mla_wo[j]
        else:
            rp = (ret_wq[j], ret_wk[j], ret_wv[j], ret_wg[j], ret_wo[j], ret_decay[j], ret_gn[j])
            oc, s_f, s_b = retention(hc, zero_state, zero_state, None, *rp)
            new_ret.append(jnp.stack([s_f, s_b], axis=1))
            ol, _, _ = retention(hl, state_ret[:, j, 0].astype(jnp.float32), state_ret[:, j, 1].astype(jnp.float32),
                                 rope_b, *rp)
        xc = xc + g1c * oc
        xl = xl + g1l * ol
        hc = modulate(xc, norm2_w[i], sh2c, sc2c)
        hl = modulate(xl, norm2_w[i], sh2l, sc2l)
        if i % 2 == 0:
            fp = (ffn_w_gate[j], ffn_w_up[j], ffn_w_down[j])
            fc = swiglu(hc, *fp)
            fl = swiglu(hl, *fp)
        else:
            mp = (moe_router[j], moe_w_gate[j], moe_w_up[j], moe_w_down[j])
            fc = moe_swiglu(hc, *mp)
            fl = moe_swiglu(hl, *mp)
        xc = xc + g2c * fc
        xl = xl + g2l * fl
    return (xc, xl, jnp.stack(new_ckv, axis=1), jnp.stack(new_kpe, axis=1), jnp.stack(new_ret, axis=1))
```

```python
import functools

import jax
import jax.numpy as jnp
from jax import lax
from jax.experimental import pallas as pl
from jax.experimental.pallas import tpu as pltpu

F32 = jnp.float32
BF16 = jnp.bfloat16

D_MODEL = 1024
BATCH = 32
SEQ = 256
DEPTH = 4
DEC_BATCH = 4
DEC_SEQ = 2048
PAST_LEN = 512
GRID_W = 64
ROPE_BASE = 10000.0
NORM_EPS = 1e-6

MLA_HEADS = 16
MLA_NOPE = 64
MLA_ROPE = 32
MLA_QK = MLA_NOPE + MLA_ROPE
MLA_V = 64
MLA_Q_RANK = 384
MLA_KV_RANK = 256
HEAD_SLOT = 128

RET_HEADS = 4
RET_DK = 256
RET_DV = 512
RET_CHUNK = 128

FFN_DIM = 2816
N_EXPERTS = 8
TOP_K = 2
EXPERT_DIM = 3584

N_CTX = BATCH * SEQ
N_LAT = DEC_BATCH * DEC_SEQ
N_TOK = N_CTX + N_LAT
N_COND = 8

ROW_TILE = 512
CTX_TILES = N_CTX // ROW_TILE
TILES_PER_LAT_BATCH = DEC_SEQ // ROW_TILE

FFN_TILE = 1408
MOE_ROWS = 512
MOE_TILE = 896
N_ASSIGN = N_TOK * TOP_K
MOE_BLOCKS = N_ASSIGN // MOE_ROWS + N_EXPERTS
MOE_PAD_ROWS = MOE_BLOCKS * MOE_ROWS
COMBINE_TILE = 256

VMEM_LIMIT = 56 * 1024 * 1024


def _params(*sem):
    return pltpu.CompilerParams(dimension_semantics=sem, vmem_limit_bytes=VMEM_LIMIT)


def _cond_of_tile(i):
    return jnp.where(i < CTX_TILES, 0, 1 + (i - CTX_TILES) // TILES_PER_LAT_BATCH)


def _dot(a, b):
    return jnp.dot(a, b, preferred_element_type=F32)


def _silu(x):
    return x * (1.0 / (1.0 + jnp.exp(-x)))


def _rms(x, w):
    return x * lax.rsqrt(jnp.mean(x * x, axis=-1, keepdims=True) + NORM_EPS) * w


def _modulate(x, nw, shift, scale):
    return _rms(x, nw) * (1.0 + scale) + shift


def _adaln_kernel(c_ref, w_ref, b_ref, o_ref):
    a = _silu(c_ref[...]).astype(BF16)
    o_ref[...] = _dot(a, w_ref[...].astype(BF16)) + b_ref[...]


def _adaln(cond, mod_w, mod_b):
    tn = 1024
    return pl.pallas_call(
        _adaln_kernel,
        grid=(DEPTH, 6 * D_MODEL // tn),
        in_specs=[
            pl.BlockSpec((N_COND, D_MODEL), lambda l, j: (0, 0)),
            pl.BlockSpec((None, D_MODEL, tn), lambda l, j: (l, 0, j)),
            pl.BlockSpec((None, 1, tn), lambda l, j: (l, 0, j)),
        ],
        out_specs=pl.BlockSpec((None, N_COND, tn), lambda l, j: (l, 0, j)),
        out_shape=jax.ShapeDtypeStruct((DEPTH, N_COND, 6 * D_MODEL), F32),
        compiler_params=_params("parallel", "parallel"),
        name="adaln",
    )(cond, mod_w, mod_b.reshape(DEPTH, 1, 6 * D_MODEL))


def _head_norm_rope(xh, ga, gb, valid):
    ss = jnp.sum(jnp.where(valid, xh * xh, 0.0), axis=-1, keepdims=True) * (1.0 / MLA_QK)
    r = lax.rsqrt(ss + NORM_EPS)
    return r * (xh * ga + pltpu.roll(xh, HEAD_SLOT - MLA_ROPE, 1) * gb)


def _mla_proj_kernel(x_ref, nw_ref, sh_ref, sc_ref, wqa_ref, qan_ref, wqb_ref, wkv_ref, kvn_ref,
                     ga_ref, gb_ref, ca_ref, sb_ref, q_ref, ckv_ref, kpe_ref):
    h = _modulate(x_ref[...], nw_ref[...], sh_ref[...], sc_ref[...]).astype(BF16)
    qa = _rms(_dot(h, wqa_ref[...]), qan_ref[...]).astype(BF16)
    q = _dot(qa, wqb_ref[...])
    kv = _dot(h, wkv_ref[...])
    ckv_ref[...] = _rms(kv[:, :MLA_KV_RANK], kvn_ref[...])
    kpe_ref[...] = kv[:, MLA_KV_RANK:]
    ga = ga_ref[...] * ca_ref[...]
    gb = gb_ref[...] * sb_ref[...]
    valid = lax.broadcasted_iota(jnp.int32, ga.shape, 1) < MLA_QK
    for hd in range(MLA_HEADS):
        sl = slice(hd * HEAD_SLOT, (hd + 1) * HEAD_SLOT)
        q_ref[:, sl] = _head_norm_rope(q[:, sl], ga, gb, valid).astype(BF16)


def _mla_proj(x, nw, shift, scale, wqa, qan, wqb, wkv, kvn, ga, gb, ca, sb):
    tm = ROW_TILE
    row = lambda i: (i, 0)
    fixed = lambda i: (0, 0)
    cond = lambda i: (_cond_of_tile(i), 0, 0)
    qw = MLA_HEADS * HEAD_SLOT
    kvw = MLA_KV_RANK + HEAD_SLOT
    return pl.pallas_call(
        _mla_proj_kernel,
        grid=(N_TOK // tm,),
        in_specs=[
            pl.BlockSpec((tm, D_MODEL), row),
            pl.BlockSpec((1, D_MODEL), fixed),
            pl.BlockSpec((None, 1, D_MODEL), cond),
            pl.BlockSpec((None, 1, D_MODEL), cond),
            pl.BlockSpec((D_MODEL, MLA_Q_RANK), fixed),
            pl.BlockSpec((1, MLA_Q_RANK), fixed),
            pl.BlockSpec((MLA_Q_RANK, qw), fixed),
            pl.BlockSpec((D_MODEL, kvw), fixed),
            pl.BlockSpec((1, MLA_KV_RANK), fixed),
            pl.BlockSpec((1, HEAD_SLOT), fixed),
            pl.BlockSpec((1, HEAD_SLOT), fixed),
            pl.BlockSpec((tm, HEAD_SLOT), row),
            pl.BlockSpec((tm, HEAD_SLOT), row),
        ],
        out_specs=[
            pl.BlockSpec((tm, qw), row),
            pl.BlockSpec((tm, MLA_KV_RANK), row),
            pl.BlockSpec((tm, HEAD_SLOT), row),
        ],
        out_shape=[
            jax.ShapeDtypeStruct((N_TOK, qw), BF16),
            jax.ShapeDtypeStruct((N_TOK, MLA_KV_RANK), F32),
            jax.ShapeDtypeStruct((N_TOK, HEAD_SLOT), F32),
        ],
        compiler_params=_params("parallel"),
        name="mla_proj",
    )(x, nw, shift, scale, wqa, qan, wqb, wkv, kvn, ga, gb, ca, sb)


def _kv_expand_kernel(ckv_ref, kpe_ref, wuk_ref, wuv_ref, ga_ref, gb_ref, ca_ref, sb_ref, k_ref, v_ref):
    c = ckv_ref[...].astype(BF16)
    kn = _dot(c, wuk_ref[...])
    v_ref[...] = _dot(c, wuv_ref[...]).astype(BF16)
    kpe = kpe_ref[...]
    ga = ga_ref[...] * ca_ref[...]
    gb = gb_ref[...] * sb_ref[...]
    valid = lax.broadcasted_iota(jnp.int32, ga.shape, 1) < MLA_QK
    for hd in range(MLA_HEADS):
        sl = slice(hd * HEAD_SLOT, (hd + 1) * HEAD_SLOT)
        k_ref[:, sl] = _head_norm_rope(kn[:, sl] + kpe, ga, gb, valid).astype(BF16)


def _kv_expand(ckv, kpe, wuk, wuv, ga, gb, ca, sb):
    n = ckv.shape[0]
    tm = ROW_TILE
    row = lambda i: (i, 0)
    fixed = lambda i: (0, 0)
    kw = MLA_HEADS * HEAD_SLOT
    vw = MLA_HEADS * MLA_V
    return pl.pallas_call(
        _kv_expand_kernel,
        grid=(n // tm,),
        in_specs=[
            pl.BlockSpec((tm, MLA_KV_RANK), row),
            pl.BlockSpec((tm, HEAD_SLOT), row),
            pl.BlockSpec((MLA_KV_RANK, kw), fixed),
            pl.BlockSpec((MLA_KV_RANK, vw), fixed),
            pl.BlockSpec((1, HEAD_SLOT), fixed),
            pl.BlockSpec((1, HEAD_SLOT), fixed),
            pl.BlockSpec((tm, HEAD_SLOT), row),
            pl.BlockSpec((tm, HEAD_SLOT), row),
        ],
        out_specs=[pl.BlockSpec((tm, kw), row), pl.BlockSpec((tm, vw), row)],
        out_shape=[jax.ShapeDtypeStruct((n, kw), BF16), jax.ShapeDtypeStruct((n, vw), BF16)],
        compiler_params=_params("parallel"),
        name="kv_expand",
    )(ckv, kpe, wuk, wuv, ga, gb, ca, sb)


def _attn_kernel(q_ref, k_ref, v_ref, o_ref, *, heads):
    for hd in range(heads):
        qh = q_ref[:, hd * HEAD_SLOT:(hd + 1) * HEAD_SLOT]
        kh = k_ref[:, hd * HEAD_SLOT:(hd + 1) * HEAD_SLOT]
        vh = v_ref[:, hd * MLA_V:(hd + 1) * MLA_V]
        s = lax.dot_general(qh, kh, (((1,), (1,)), ((), ())), preferred_element_type=F32)
        p = jnp.exp(s - jnp.max(s, axis=-1, keepdims=True))
        inv = 1.0 / jnp.sum(p, axis=-1, keepdims=True)
        o_ref[:, hd * MLA_V:(hd + 1) * MLA_V] = (_dot(p.astype(BF16), vh) * inv).astype(BF16)


def _attention(q, k, v, *, batch, q_off, tq_total, tq, heads):
    tk = k.shape[1]
    groups = MLA_HEADS // heads
    return pl.pallas_call(
        functools.partial(_attn_kernel, heads=heads),
        grid=(batch, groups, tq_total // tq),
        in_specs=[
            pl.BlockSpec((None, tq, heads * HEAD_SLOT), lambda b, g, i: (b + q_off, i, g)),
            pl.BlockSpec((None, tk, heads * HEAD_SLOT), lambda b, g, i: (b, 0, g)),
            pl.BlockSpec((None, tk, heads * MLA_V), lambda b, g, i: (b, 0, g)),
        ],
        out_specs=pl.BlockSpec((None, tq, heads * MLA_V), lambda b, g, i: (b, i, g)),
        out_shape=jax.ShapeDtypeStruct((batch, tq_total, MLA_HEADS * MLA_V), BF16),
        compiler_params=_params("parallel", "parallel", "parallel"),
        name="attention",
    )(q, k, v)


def _proj_residual_kernel(a_ref, w_ref, x_ref, g_ref, nw_ref, sh_ref, sc_ref, *rest, router):
    x1 = x_ref[...] + g_ref[...] * _dot(a_ref[...], w_ref[...])
    h2 = _modulate(x1, nw_ref[...], sh_ref[...], sc_ref[...])
    if router:
        rhi_ref, rlo_ref, x1_ref, h2_ref, lg_ref = rest
        hi = h2.astype(BF16)
        lo = (h2 - hi.astype(F32)).astype(BF16)
        lg_ref[...] = _dot(hi, rhi_ref[...]) + (_dot(lo, rhi_ref[...]) + _dot(hi, rlo_ref[...]))
    else:
        x1_ref, h2_ref = rest
    x1_ref[...] = x1
    h2_ref[...] = h2.astype(h2_ref.dtype)


def _proj_residual(a, w, x, gate, nw, shift, scale, router_w=None):
    tm = ROW_TILE
    kdim = a.shape[1]
    row = lambda i: (i, 0)
    fixed = lambda i: (0, 0)
    cond = lambda i: (_cond_of_tile(i), 0, 0)
    router = router_w is not None
    in_specs = [
        pl.BlockSpec((tm, kdim), row),
        pl.BlockSpec((kdim, D_MODEL), fixed),
        pl.BlockSpec((tm, D_MODEL), row),
        pl.BlockSpec((None, 1, D_MODEL), cond),
        pl.BlockSpec((1, D_MODEL), fixed),
        pl.BlockSpec((None, 1, D_MODEL), cond),
        pl.BlockSpec((None, 1, D_MODEL), cond),
    ]
    out_specs = [pl.BlockSpec((tm, D_MODEL), row), pl.BlockSpec((tm, D_MODEL), row)]
    out_shape = [jax.ShapeDtypeStruct((N_TOK, D_MODEL), F32),
                 jax.ShapeDtypeStruct((N_TOK, D_MODEL), F32 if router else BF16)]
    args = [a, w, x, gate, nw, shift, scale]
    if router:
        in_specs += [pl.BlockSpec((D_MODEL, 128), fixed), pl.BlockSpec((D_MODEL, 128), fixed)]
        out_specs.append(pl.BlockSpec((tm, 128), row))
        out_shape.append(jax.ShapeDtypeStruct((N_TOK, 128), F32))
        args += list(router_w)
    return pl.pallas_call(
        functools.partial(_proj_residual_kernel, router=router),
        grid=(N_TOK // tm,),
        in_specs=in_specs,
        out_specs=out_specs,
        out_shape=out_shape,
        compiler_params=_params("parallel"),
        name="proj_residual",
    )(*args)


def _swiglu_partial(x, wg, wu, wd):
    g = _dot(x, wg)
    u = _dot(x, wu)
    return _dot((_silu(g) * u).astype(BF16), wd)


def _ffn_kernel(h_ref, x1_ref, g_ref, wg_ref, wu_ref, wd_ref, o_ref, acc_ref):
    f = pl.program_id(1)
    part = _swiglu_partial(h_ref[...], wg_ref[...], wu_ref[...], wd_ref[...])

    @pl.when(f == 0)
    def _():
        acc_ref[...] = part

    @pl.when(f > 0)
    def _():
        acc_ref[...] += part

    @pl.when(f == pl.num_programs(1) - 1)
    def _():
        o_ref[...] = x1_ref[...] + g_ref[...] * acc_ref[...]


def _ffn(h2, x1, gate, wg, wu, wd):
    tm, tf = ROW_TILE, FFN_TILE
    return pl.pallas_call(
        _ffn_kernel,
        grid=(N_TOK // tm, FFN_DIM // tf),
        in_specs=[
            pl.BlockSpec((tm, D_MODEL), lambda i, f: (i, 0)),
            pl.BlockSpec((tm, D_MODEL), lambda i, f: (i, 0)),
            pl.BlockSpec((None, 1, D_MODEL), lambda i, f: (_cond_of_tile(i), 0, 0)),
            pl.BlockSpec((D_MODEL, tf), lambda i, f: (0, f)),
            pl.BlockSpec((D_MODEL, tf), lambda i, f: (0, f)),
            pl.BlockSpec((tf, D_MODEL), lambda i, f: (f, 0)),
        ],
        out_specs=pl.BlockSpec((tm, D_MODEL), lambda i, f: (i, 0)),
        out_shape=jax.ShapeDtypeStruct((N_TOK, D_MODEL), F32),
        scratch_shapes=[pltpu.VMEM((tm, D_MODEL), F32)],
        compiler_params=_params("parallel", "arbitrary"),
        name="ffn",
    )(h2, x1, gate, wg, wu, wd)


def _moe_kernel(blk_e_ref, nact_ref, tok_ref, x_hbm, wg_ref, wu_ref, wd_ref, o_ref, xrows, xb, acc_ref, sem):
    b = pl.program_id(0)
    f = pl.program_id(1)
    active = b < nact_ref[0]

    def row_copy(r, tok):
        return pltpu.make_async_copy(x_hbm.at[pl.ds(tok, 1), :], xrows.at[pl.ds(r, 1), :], sem)

    @pl.when(active & (f == 0))
    def _():
        base = b * MOE_ROWS

        def issue(r, c):
            row_copy(r, tok_ref[base + r]).start()
            return c

        lax.fori_loop(0, MOE_ROWS, issue, 0)

        pltpu.make_async_copy(x_hbm.at[pl.ds(0, MOE_ROWS), :], xrows, sem).wait()
        xb[...] = xrows[...].astype(BF16)

    @pl.when(active)
    def _():
        part = _swiglu_partial(xb[...], wg_ref[...], wu_ref[...], wd_ref[...])

        @pl.when(f == 0)
        def _():
            acc_ref[...] = part

        @pl.when(f > 0)
        def _():
            acc_ref[...] += part

    @pl.when(f == pl.num_programs(1) - 1)
    def _():
        @pl.when(active)
        def _():
            o_ref[...] = acc_ref[...]

        @pl.when(jnp.logical_not(active))
        def _():
            o_ref[...] = jnp.zeros_like(o_ref)


def _moe(blk_e, nact, buf_tok, h2, wg, wu, wd):
    tm, tf = MOE_ROWS, MOE_TILE
    grid_spec = pltpu.PrefetchScalarGridSpec(
        num_scalar_prefetch=3,
        grid=(MOE_BLOCKS, EXPERT_DIM // tf),
        in_specs=[
            pl.BlockSpec(memory_space=pl.ANY),
            pl.BlockSpec((None, D_MODEL, tf), lambda b, f, e, n, t: (e[b], 0, f)),
            pl.BlockSpec((None, D_MODEL, tf), lambda b, f, e, n, t: (e[b], 0, f)),
            pl.BlockSpec((None, tf, D_MODEL), lambda b, f, e, n, t: (e[b], f, 0)),
        ],
        out_specs=pl.BlockSpec((tm, D_MODEL), lambda b, f, e, n, t: (b, 0)),
        scratch_shapes=[
            pltpu.VMEM((tm, D_MODEL), F32),
            pltpu.VMEM((tm, D_MODEL), BF16),
            pltpu.VMEM((tm, D_MODEL), F32),
            pltpu.SemaphoreType.DMA(()),
        ],
    )
    return pl.pallas_call(
        _moe_kernel,
        grid_spec=grid_spec,
        out_shape=jax.ShapeDtypeStruct((MOE_PAD_ROWS, D_MODEL), F32),
        compiler_params=_params("arbitrary", "arbitrary"),
        name="moe",
    )(blk_e, nact, buf_tok, h2, wg, wu, wd)


def _combine_kernel(pos_ref, y_hbm, x1_ref, g_ref, w_ref, o_ref, rows, sem):
    base = pl.program_id(0) * COMBINE_TILE

    def row_copy(k, r, p):
        return pltpu.make_async_copy(y_hbm.at[pl.ds(p, 1), :], rows.at[k, pl.ds(r, 1), :], sem.at[k])

    def issue(r, c):
        for k in range(TOP_K):
            row_copy(k, r, pos_ref[(base + r) * TOP_K + k]).start()
        return c

    lax.fori_loop(0, COMBINE_TILE, issue, 0)

    for k in range(TOP_K):
        pltpu.make_async_copy(y_hbm.at[pl.ds(0, COMBINE_TILE), :], rows.at[k], sem.at[k]).wait()
    w = w_ref[...]
    y = w[:, 0:1] * rows[0] + w[:, 1:2] * rows[1]
    o_ref[...] = x1_ref[...] + g_ref[...] * y


def _combine(pos, yb, x1, gate, top_w):
    tm = COMBINE_TILE
    tiles_per_cond = ROW_TILE // tm
    grid_spec = pltpu.PrefetchScalarGridSpec(
        num_scalar_prefetch=1,
        grid=(N_TOK // tm,),
        in_specs=[
            pl.BlockSpec(memory_space=pl.ANY),
            pl.BlockSpec((tm, D_MODEL), lambda i, p: (i, 0)),
            pl.BlockSpec((None, 1, D_MODEL), lambda i, p: (_cond_of_tile(i // tiles_per_cond), 0, 0)),
            pl.BlockSpec((tm, TOP_K), lambda i, p: (i, 0)),
        ],
        out_specs=pl.BlockSpec((tm, D_MODEL), lambda i, p: (i, 0)),
        scratch_shapes=[pltpu.VMEM((TOP_K, tm, D_MODEL), F32), pltpu.SemaphoreType.DMA((TOP_K,))],
    )
    return pl.pallas_call(
        _combine_kernel,
        grid_spec=grid_spec,
        out_shape=jax.ShapeDtypeStruct((N_TOK, D_MODEL), F32),
        compiler_params=_params("arbitrary"),
        name="moe_combine",
    )(pos, yb, x1, gate, top_w)


def _route(logits):
    top_v, top_i = lax.top_k(logits, TOP_K)
    top_w = jax.nn.softmax(top_v, axis=-1)
    e = top_i.reshape(N_ASSIGN)
    onehot = (e[:, None] == jnp.arange(N_EXPERTS, dtype=e.dtype)[None, :]).astype(jnp.int32)
    csum = jnp.cumsum(onehot, axis=0)
    rank = jnp.sum(csum * onehot, axis=1) - 1
    counts = csum[-1]
    padded = (counts + MOE_ROWS - 1) // MOE_ROWS * MOE_ROWS
    pad_end = jnp.cumsum(padded)
    pad_start = pad_end - padded
    pos = (pad_start[e] + rank).astype(jnp.int32)
    tok = jnp.repeat(jnp.arange(N_TOK, dtype=jnp.int32), TOP_K)
    buf_tok = jnp.zeros((MOE_PAD_ROWS,), jnp.int32).at[pos].set(tok)
    blk_start = jnp.arange(MOE_BLOCKS, dtype=jnp.int32) * MOE_ROWS
    blk_e = jnp.minimum(jnp.searchsorted(pad_end, blk_start, side="right"), N_EXPERTS - 1).astype(jnp.int32)
    nact = (pad_end[-1:] // MOE_ROWS).astype(jnp.int32)
    return top_w, pos, buf_tok, blk_e, nact


def _ret_proj_kernel(x_ref, nw_ref, sh_ref, sc_ref, w_ref, c_ref, s_ref, o_ref, h_sc):
    i = pl.program_id(0)
    j = pl.program_id(1)

    @pl.when(j == 0)
    def _():
        h_sc[...] = _modulate(x_ref[...], nw_ref[...], sh_ref[...], sc_ref[...]).astype(BF16)

    acc = _dot(h_sc[...], w_ref[...])
    n_qk = 2 * RET_HEADS * RET_DK // o_ref.shape[1]
    n_v = RET_HEADS * RET_DV // o_ref.shape[1]

    @pl.when(j >= n_qk + n_v)
    def _():
        o_ref[...] = _silu(acc).astype(BF16)

    @pl.when((j >= n_qk) & (j < n_qk + n_v))
    def _():
        o_ref[...] = acc.astype(BF16)

    @pl.when(j < n_qk)
    def _():
        x = acc * jnp.where(j >= n_qk // 2, RET_DK ** -0.5, 1.0)

        @pl.when(i < CTX_TILES)
        def _():
            o_ref[...] = x.astype(BF16)

        @pl.when(i >= CTX_TILES)
        def _():
            half = RET_DK // 4
            for g in range(x.shape[1] // 128):
                sl = slice(g * 128, (g + 1) * 128)
                tl = slice((g % 2) * 128, (g % 2 + 1) * 128)
                xs = x[:, sl]
                o_ref[:, sl] = (xs * c_ref[:, tl] + pltpu.roll(xs, half, 1) * s_ref[:, tl]).astype(BF16)


def _ret_proj(x, nw, shift, scale, w, rope_c, rope_s):
    tm, tn = ROW_TILE, 1024
    width = w.shape[1]
    cond = lambda i, j: (_cond_of_tile(i), 0, 0)
    tab = lambda i, j: (jnp.where(i < CTX_TILES, 0, (i - CTX_TILES) % TILES_PER_LAT_BATCH), 0)
    return pl.pallas_call(
        _ret_proj_kernel,
        grid=(N_TOK // tm, width // tn),
        in_specs=[
            pl.BlockSpec((tm, D_MODEL), lambda i, j: (i, 0)),
            pl.BlockSpec((1, D_MODEL), lambda i, j: (0, 0)),
            pl.BlockSpec((None, 1, D_MODEL), cond),
            pl.BlockSpec((None, 1, D_MODEL), cond),
            pl.BlockSpec((D_MODEL, tn), lambda i, j: (0, j)),
            pl.BlockSpec((tm, RET_DK), tab),
            pl.BlockSpec((tm, RET_DK), tab),
        ],
        out_specs=pl.BlockSpec((tm, tn), lambda i, j: (i, j)),
        out_shape=jax.ShapeDtypeStruct((N_TOK, width), BF16),
        scratch_shapes=[pltpu.VMEM((tm, D_MODEL), BF16)],
        compiler_params=_params("parallel", "arbitrary"),
        name="ret_proj",
    )(x, nw, shift, scale, w, rope_c, rope_s)


def _retention_kernel(lg_ref, q_ref, k_ref, v_ref, sg_ref, gn_ref, *rest, zero_init, n_chunks):
    if zero_init:
        y_ref, sf_ref, sb_ref, o_acc, state = rest
    else:
        s0f_ref, s0b_ref, y_ref, o_acc, state = rest
    hd = pl.program_id(1)
    c = RET_CHUNK
    ri = lax.broadcasted_iota(jnp.int32, (c, c), 0).astype(F32)
    ci = lax.broadcasted_iota(jnp.int32, (c, c), 1).astype(F32)
    pos = lax.broadcasted_iota(jnp.int32, (c, 1), 0).astype(F32)

    def run(direction):
        lg = lg_ref[direction, hd]
        rel = (ri - ci) if direction == 0 else (ci - ri)
        dmask = jnp.where(rel >= 0, jnp.exp(jnp.maximum(rel, 0.0) * lg), 0.0)
        if direction == 0:
            q_decay = jnp.exp((pos + 1.0) * lg)
            k_decay = jnp.exp((c - 1.0 - pos) * lg)
        else:
            q_decay = jnp.exp((c - pos) * lg)
            k_decay = jnp.exp(pos * lg)
        c_decay = jnp.exp(jnp.full((1, 1), c, F32) * lg)

        def step(t, carry):
            ch = t if direction == 0 else n_chunks - 1 - t
            rows = pl.ds(pl.multiple_of(ch * c, c), c)
            qc = q_ref[rows, :]
            kc = k_ref[rows, :]
            vc = v_ref[rows, :]
            a = lax.dot_general(qc, kc, (((1,), (1,)), ((), ())), preferred_element_type=F32) * dmask
            s = state[...]
            o = _dot(a.astype(BF16), vc) + _dot((qc.astype(F32) * q_decay).astype(BF16), s.astype(BF16))
            if direction == 0:
                o_acc[rows, :] = o
            else:
                o_acc[rows, :] += o
            kd = (kc.astype(F32) * k_decay).astype(BF16)
            state[...] = s * c_decay + lax.dot_general(kd, vc, (((0,), (0,)), ((), ())),
                                                       preferred_element_type=F32)
            return carry

        lax.fori_loop(0, n_chunks, step, 0)

    for direction in range(2):
        if zero_init:
            state[...] = jnp.zeros_like(state)
        else:
            state[...] = (s0f_ref if direction == 0 else s0b_ref)[...]
        run(direction)
        if zero_init:
            (sf_ref if direction == 0 else sb_ref)[...] = state[...]

    o = o_acc[...]
    y = _rms(o, gn_ref[...])
    y_ref[...] = (sg_ref[...].astype(F32) * y).astype(BF16)


def _retention(log_gamma, qkvg, gn, state0, *, batch, b_off, t):
    zero_init = state0 is None
    qblk = RET_HEADS * RET_DK // RET_DK
    vblk = 2 * RET_HEADS * RET_DK // RET_DV
    gblk = vblk + RET_HEADS
    in_specs = [
        pl.BlockSpec(memory_space=pltpu.SMEM),
        pl.BlockSpec((None, t, RET_DK), lambda b, h: (b + b_off, 0, h)),
        pl.BlockSpec((None, t, RET_DK), lambda b, h: (b + b_off, 0, qblk + h)),
        pl.BlockSpec((None, t, RET_DV), lambda b, h: (b + b_off, 0, vblk + h)),
        pl.BlockSpec((None, t, RET_DV), lambda b, h: (b + b_off, 0, gblk + h)),
        pl.BlockSpec((None, 1, RET_DV), lambda b, h: (h, 0, 0)),
    ]
    args = [log_gamma, qkvg, qkvg, qkvg, qkvg, gn]
    y_spec = pl.BlockSpec((None, t, RET_DV), lambda b, h: (b, 0, h))
    y_shape = jax.ShapeDtypeStruct((batch, t, RET_HEADS * RET_DV), BF16)
    st_spec = pl.BlockSpec((None, None, None, RET_DK, RET_DV), lambda b, h: (b, 0, h, 0, 0))
    if zero_init:
        st_shape = jax.ShapeDtypeStruct((batch, 1, RET_HEADS, RET_DK, RET_DV), F32)
        out_specs = [y_spec, st_spec, st_spec]
        out_shape = [y_shape, st_shape, st_shape]
    else:
        in_specs += [pl.BlockSpec((None, None, None, RET_DK, RET_DV), lambda b, h: (b, 0, h, 0, 0)),
                     pl.BlockSpec((None, None, None, RET_DK, RET_DV), lambda b, h: (b, 1, h, 0, 0))]
        args += [state0, state0]
        out_specs = y_spec
        out_shape = y_shape
    return pl.pallas_call(
        functools.partial(_retention_kernel, zero_init=zero_init, n_chunks=t // RET_CHUNK),
        grid=(batch, RET_HEADS),
        in_specs=in_specs,
        out_specs=out_specs,
        out_shape=out_shape,
        scratch_shapes=[pltpu.VMEM((t, RET_DV), F32), pltpu.VMEM((RET_DK, RET_DV), F32)],
        compiler_params=_params("parallel", "parallel"),
        name="retention",
    )(*args)


def _rope_tables(rot_dim):
    rows = DEC_SEQ // GRID_W
    row = jnp.repeat(jnp.arange(rows), GRID_W)
    col = jnp.tile(jnp.arange(GRID_W), rows)
    nf = rot_dim // 4
    inv = ROPE_BASE ** (-jnp.arange(nf, dtype=F32) / nf)
    ang = jnp.stack([row, col], axis=-1).astype(F32)[:, :, None] * inv
    cos, sin = jnp.cos(ang), jnp.sin(ang)
    c = jnp.stack([cos, cos], axis=2).reshape(DEC_SEQ, rot_dim)
    s = jnp.stack([-sin, sin], axis=2).reshape(DEC_SEQ, rot_dim)
    return c, s


def _swap_perm():
    nf = MLA_ROPE // 4
    return jnp.arange(MLA_ROPE) ^ nf


def _head_slot_tables(c32, s32):
    t = c32.shape[0]
    ca = jnp.concatenate([jnp.ones((t, MLA_NOPE), F32), c32, jnp.zeros((t, MLA_ROPE), F32)], axis=1)
    sb = jnp.concatenate([jnp.zeros((t, MLA_NOPE), F32), s32, jnp.zeros((t, MLA_ROPE), F32)], axis=1)
    return ca, sb


def _head_slot_gains(g, scale):
    perm = _swap_perm()
    zeros = jnp.zeros((MLA_ROPE,), F32)
    ga = jnp.concatenate([g, zeros]) * scale
    gb = jnp.concatenate([jnp.zeros((MLA_NOPE,), F32), g[MLA_NOPE:][perm], zeros]) * scale
    return ga.reshape(1, HEAD_SLOT), gb.reshape(1, HEAD_SLOT)


def _rows3(m):
    return m.reshape(N_COND, 1, D_MODEL)


def kernel(x_prompt, x_sample, cache_ckv, cache_kpe, state_ret, c, c_ctx, mod_w, mod_b, norm1_w, norm2_w, mla_wq_a, mla_q_a_norm, mla_wq_b, mla_wkv_a, mla_kv_norm, mla_w_uk, mla_w_uv, mla_q_norm, mla_k_norm, mla_wo, ret_wq, ret_wk, ret_wv, ret_wg, ret_wo, ret_decay, ret_gn, ffn_w_gate, ffn_w_up, ffn_w_down, moe_router, moe_w_gate, moe_w_up, moe_w_down):
    x = jnp.concatenate([x_prompt.reshape(N_CTX, D_MODEL), x_sample.reshape(N_LAT, D_MODEL)], axis=0)
    cond = jnp.concatenate([c_ctx[None, :], c, jnp.zeros((N_COND - 1 - DEC_BATCH, D_MODEL), F32)], axis=0)
    mods = _adaln(cond, mod_w, mod_b).reshape(DEPTH, N_COND, 6, D_MODEL)

    perm = _swap_perm()
    c32, s32 = _rope_tables(MLA_ROPE)
    ca_lat, sb_lat = _head_slot_tables(c32, s32)
    ca_id, sb_id = _head_slot_tables(jnp.ones((1, MLA_ROPE), F32), jnp.zeros((1, MLA_ROPE), F32))
    ca_q = jnp.concatenate([jnp.broadcast_to(ca_id, (N_CTX, HEAD_SLOT)), jnp.tile(ca_lat, (DEC_BATCH, 1))], axis=0)
    sb_q = jnp.concatenate([jnp.broadcast_to(sb_id, (N_CTX, HEAD_SLOT)), jnp.tile(sb_lat, (DEC_BATCH, 1))], axis=0)
    lat_kv = DEC_SEQ + PAST_LEN
    ca_kl = jnp.tile(jnp.concatenate([ca_lat, jnp.broadcast_to(ca_id, (PAST_LEN, HEAD_SLOT))], axis=0), (DEC_BATCH, 1))
    sb_kl = jnp.tile(jnp.concatenate([sb_lat, jnp.broadcast_to(sb_id, (PAST_LEN, HEAD_SLOT))], axis=0), (DEC_BATCH, 1))
    ca_kc = jnp.broadcast_to(ca_id, (N_CTX, HEAD_SLOT))
    sb_kc = jnp.broadcast_to(sb_id, (N_CTX, HEAD_SLOT))
    rc256, rs256 = _rope_tables(RET_DK)

    new_ckv, new_kpe, new_ret = [], [], []
    for i in range(DEPTH):
        j = i // 2
        m = mods[i]
        sh1, sc1, g1, sh2, sc2, g2 = (_rows3(m[:, t]) for t in range(6))
        nw1 = norm1_w[i].reshape(1, D_MODEL)
        nw2 = norm2_w[i].reshape(1, D_MODEL)
        if i % 2 == 0:
            wqb = mla_wq_b[j].reshape(MLA_Q_RANK, MLA_HEADS, MLA_QK)
            wqb = jnp.concatenate([wqb, wqb[:, :, MLA_NOPE:][:, :, perm]], axis=2)
            wqb = wqb.reshape(MLA_Q_RANK, MLA_HEADS * HEAD_SLOT).astype(BF16)
            wkv = mla_wkv_a[j]
            wkv = jnp.concatenate([wkv[:, :MLA_KV_RANK], jnp.zeros((D_MODEL, MLA_NOPE), F32),
                                   wkv[:, MLA_KV_RANK:], wkv[:, MLA_KV_RANK:][:, perm]], axis=1).astype(BF16)
            wuk = mla_w_uk[j].reshape(MLA_KV_RANK, MLA_HEADS, MLA_NOPE)
            wuk = jnp.concatenate([wuk, jnp.zeros_like(wuk)], axis=2)
            wuk = wuk.reshape(MLA_KV_RANK, MLA_HEADS * HEAD_SLOT).astype(BF16)
            gaq, gbq = _head_slot_gains(mla_q_norm[j], MLA_QK ** -0.5)
            gak, gbk = _head_slot_gains(mla_k_norm[j], 1.0)

            q, ckv, kpe = _mla_proj(x, nw1, sh1, sc1, mla_wq_a[j].astype(BF16),
                                    mla_q_a_norm[j].reshape(1, MLA_Q_RANK), wqb, wkv,
                                    mla_kv_norm[j].reshape(1, MLA_KV_RANK), gaq, gbq, ca_q, sb_q)
            new_ckv.append(ckv[:N_CTX].reshape(BATCH, SEQ, MLA_KV_RANK))
            new_kpe.append(kpe[:N_CTX, MLA_NOPE:MLA_QK].reshape(BATCH, SEQ, MLA_ROPE))

            kpe_x = cache_kpe[:, j]
            kpe_x = jnp.concatenate([jnp.zeros((DEC_BATCH, PAST_LEN, MLA_NOPE), F32), kpe_x, kpe_x[:, :, perm]], axis=2)
            ckv_l = jnp.concatenate([ckv[N_CTX:].reshape(DEC_BATCH, DEC_SEQ, MLA_KV_RANK), cache_ckv[:, j]], axis=1)
            kpe_l = jnp.concatenate([kpe[N_CTX:].reshape(DEC_BATCH, DEC_SEQ, HEAD_SLOT), kpe_x], axis=1)
            wuv = mla_w_uv[j].astype(BF16)
            k_c, v_c = _kv_expand(ckv[:N_CTX], kpe[:N_CTX], wuk, wuv, gak, gbk, ca_kc, sb_kc)
            k_l, v_l = _kv_expand(ckv_l.reshape(DEC_BATCH * lat_kv, MLA_KV_RANK),
                                  kpe_l.reshape(DEC_BATCH * lat_kv, HEAD_SLOT), wuk, wuv, gak, gbk, ca_kl, sb_kl)

            qw = MLA_HEADS * HEAD_SLOT
            o_c = _attention(q.reshape(N_TOK // SEQ, SEQ, qw), k_c.reshape(BATCH, SEQ, qw),
                             v_c.reshape(BATCH, SEQ, MLA_HEADS * MLA_V),
                             batch=BATCH, q_off=0, tq_total=SEQ, tq=SEQ, heads=MLA_HEADS)
            o_l = _attention(q.reshape(N_TOK // DEC_SEQ, DEC_SEQ, qw), k_l.reshape(DEC_BATCH, lat_kv, qw),
                             v_l.reshape(DEC_BATCH, lat_kv, MLA_HEADS * MLA_V),
                             batch=DEC_BATCH, q_off=N_CTX // DEC_SEQ, tq_total=DEC_SEQ, tq=256, heads=2)
            mix = jnp.concatenate([o_c.reshape(N_CTX, D_MODEL), o_l.reshape(N_LAT, D_MODEL)], axis=0)
            w_out = mla_wo[j].astype(BF16)
        else:
            w_in = jnp.concatenate([ret_wq[j], ret_wk[j], ret_wv[j], ret_wg[j]], axis=1).astype(BF16)
            qkvg = _ret_proj(x, nw1, sh1, sc1, w_in, rc256, rs256)
            log_gamma = -jnp.exp(ret_decay[j].astype(F32))
            gn = ret_gn[j].reshape(RET_HEADS, 1, RET_DV)
            width = qkvg.shape[1]
            y_c, s_f, s_b = _retention(log_gamma, qkvg.reshape(N_TOK // SEQ, SEQ, width), gn, None,
                                       batch=BATCH, b_off=0, t=SEQ)
            new_ret.append(jnp.concatenate([s_f, s_b], axis=1))
            y_l = _retention(log_gamma, qkvg.reshape(N_TOK // DEC_SEQ, DEC_SEQ, width), gn, state_ret[:, j],
                             batch=DEC_BATCH, b_off=N_CTX // DEC_SEQ, t=DEC_SEQ)
            mix = jnp.concatenate([y_c.reshape(N_CTX, RET_HEADS * RET_DV), y_l.reshape(N_LAT, RET_HEADS * RET_DV)], axis=0)
            w_out = ret_wo[j].astype(BF16)

        if i % 2 == 0:
            x1, h2 = _proj_residual(mix, w_out, x, g1, nw2, sh2, sc2)
            x = _ffn(h2, x1, g2, ffn_w_gate[j].astype(BF16), ffn_w_up[j].astype(BF16), ffn_w_down[j].astype(BF16))
        else:
            wr = jnp.pad(moe_router[j], ((0, 0), (0, 128 - N_EXPERTS)))
            wr_hi = wr.astype(BF16)
            wr_lo = (wr - wr_hi.astype(F32)).astype(BF16)
            x1, h2, logits = _proj_residual(mix, w_out, x, g1, nw2, sh2, sc2, router_w=(wr_hi, wr_lo))
            top_w, pos, buf_tok, blk_e, nact = _route(logits[:, :N_EXPERTS])
            yb = _moe(blk_e, nact, buf_tok, h2, moe_w_gate[j].astype(BF16), moe_w_up[j].astype(BF16),
                      moe_w_down[j].astype(BF16))
            x = _combine(pos, yb, x1, g2, top_w)

    y_prompt = x[:N_CTX].reshape(BATCH, SEQ, D_MODEL)
    y_sample = x[N_CTX:].reshape(DEC_BATCH, DEC_SEQ, D_MODEL)
    return (y_prompt, y_sample, jnp.stack(new_ckv, axis=1), jnp.stack(new_kpe, axis=1), jnp.stack(new_ret, axis=1))
```

```python
import functools

import jax
import jax.numpy as jnp
from jax import lax
from jax.experimental import pallas as pl
from jax.experimental.pallas import tpu as pltpu

F32 = jnp.float32
BF16 = jnp.bfloat16

D_MODEL = 1024
BATCH = 32
SEQ = 256
DEPTH = 4
DEC_BATCH = 4
DEC_SEQ = 2048
PAST_LEN = 512
GRID_W = 64
ROPE_BASE = 10000.0
NORM_EPS = 1e-6

MLA_HEADS = 16
MLA_NOPE = 64
MLA_ROPE = 32
MLA_QK = MLA_NOPE + MLA_ROPE
MLA_V = 64
MLA_Q_RANK = 384
MLA_KV_RANK = 256
HEAD_SLOT = 128

RET_HEADS = 4
RET_DK = 256
RET_DV = 512
RET_CHUNK = 128

FFN_DIM = 2816
N_EXPERTS = 8
TOP_K = 2
EXPERT_DIM = 3584

N_CTX = BATCH * SEQ
N_LAT = DEC_BATCH * DEC_SEQ
N_TOK = N_CTX + N_LAT
N_COND = 8

ROW_TILE = 512
CTX_TILES = N_CTX // ROW_TILE
TILES_PER_LAT_BATCH = DEC_SEQ // ROW_TILE

FFN_TILE = 1408
MOE_ROWS = 512
MOE_TILE = 896
MOE_STEPS = EXPERT_DIM // MOE_TILE
MOE_STEP_ROWS = MOE_ROWS // MOE_STEPS
N_ASSIGN = N_TOK * TOP_K
MOE_BLOCKS = N_ASSIGN // MOE_ROWS + N_EXPERTS

VMEM_LIMIT = 56 * 1024 * 1024


def _params(*sem):
    return pltpu.CompilerParams(dimension_semantics=sem, vmem_limit_bytes=VMEM_LIMIT)


def _cond_of_tile(i):
    return jnp.where(i < CTX_TILES, 0, 1 + (i - CTX_TILES) // TILES_PER_LAT_BATCH)


def _dot(a, b):
    return jnp.dot(a, b, preferred_element_type=F32)


def _silu(x):
    return x * (1.0 / (1.0 + jnp.exp(-x)))


def _rms(x, w):
    return x * lax.rsqrt(jnp.mean(x * x, axis=-1, keepdims=True) + NORM_EPS) * w


def _modulate(x, nw, shift, scale):
    return _rms(x, nw) * (1.0 + scale) + shift


def _adaln_kernel(c_ref, w_ref, b_ref, o_ref):
    a = _silu(c_ref[...]).astype(BF16)
    o_ref[...] = _dot(a, w_ref[...].astype(BF16)) + b_ref[...]


def _adaln(cond, mod_w, mod_b):
    tn = 1024
    return pl.pallas_call(
        _adaln_kernel,
        grid=(DEPTH, 6 * D_MODEL // tn),
        in_specs=[
            pl.BlockSpec((N_COND, D_MODEL), lambda l, j: (0, 0)),
            pl.BlockSpec((None, D_MODEL, tn), lambda l, j: (l, 0, j)),
            pl.BlockSpec((None, 1, tn), lambda l, j: (l, 0, j)),
        ],
        out_specs=pl.BlockSpec((None, N_COND, tn), lambda l, j: (l, 0, j)),
        out_shape=jax.ShapeDtypeStruct((DEPTH, N_COND, 6 * D_MODEL), F32),
        compiler_params=_params("parallel", "parallel"),
        name="adaln",
    )(cond, mod_w, mod_b.reshape(DEPTH, 1, 6 * D_MODEL))


def _head_norm_rope(xh, ga, gb, valid):
    ss = jnp.sum(jnp.where(valid, xh * xh, 0.0), axis=-1, keepdims=True) * (1.0 / MLA_QK)
    r = lax.rsqrt(ss + NORM_EPS)
    return r * (xh * ga + pltpu.roll(xh, HEAD_SLOT - MLA_ROPE, 1) * gb)


def _mla_proj_kernel(x_ref, nw_ref, sh_ref, sc_ref, wqa_ref, qan_ref, wqb_ref, wkv_ref, kvn_ref,
                     ga_ref, gb_ref, ca_ref, sb_ref, q_ref, ckv_ref, kpe_ref):
    h = _modulate(x_ref[...], nw_ref[...], sh_ref[...], sc_ref[...]).astype(BF16)
    qa = _rms(_dot(h, wqa_ref[...]), qan_ref[...]).astype(BF16)
    q = _dot(qa, wqb_ref[...])
    kv = _dot(h, wkv_ref[...])
    ckv_ref[...] = _rms(kv[:, :MLA_KV_RANK], kvn_ref[...])
    kpe_ref[...] = kv[:, MLA_KV_RANK:]
    ga = ga_ref[...] * ca_ref[...]
    gb = gb_ref[...] * sb_ref[...]
    valid = lax.broadcasted_iota(jnp.int32, ga.shape, 1) < MLA_QK
    for hd in range(MLA_HEADS):
        sl = slice(hd * HEAD_SLOT, (hd + 1) * HEAD_SLOT)
        q_ref[:, sl] = _head_norm_rope(q[:, sl], ga, gb, valid).astype(BF16)


def _mla_proj(x, nw, shift, scale, wqa, qan, wqb, wkv, kvn, ga, gb, ca, sb):
    tm = ROW_TILE
    row = lambda i: (i, 0)
    fixed = lambda i: (0, 0)
    cond = lambda i: (_cond_of_tile(i), 0, 0)
    qw = MLA_HEADS * HEAD_SLOT
    kvw = MLA_KV_RANK + HEAD_SLOT
    return pl.pallas_call(
        _mla_proj_kernel,
        grid=(N_TOK // tm,),
        in_specs=[
            pl.BlockSpec((tm, D_MODEL), row),
            pl.BlockSpec((1, D_MODEL), fixed),
            pl.BlockSpec((None, 1, D_MODEL), cond),
            pl.BlockSpec((None, 1, D_MODEL), cond),
            pl.BlockSpec((D_MODEL, MLA_Q_RANK), fixed),
            pl.BlockSpec((1, MLA_Q_RANK), fixed),
            pl.BlockSpec((MLA_Q_RANK, qw), fixed),
            pl.BlockSpec((D_MODEL, kvw), fixed),
            pl.BlockSpec((1, MLA_KV_RANK), fixed),
            pl.BlockSpec((1, HEAD_SLOT), fixed),
            pl.BlockSpec((1, HEAD_SLOT), fixed),
            pl.BlockSpec((tm, HEAD_SLOT), row),
            pl.BlockSpec((tm, HEAD_SLOT), row),
        ],
        out_specs=[
            pl.BlockSpec((tm, qw), row),
            pl.BlockSpec((tm, MLA_KV_RANK), row),
            pl.BlockSpec((tm, HEAD_SLOT), row),
        ],
        out_shape=[
            jax.ShapeDtypeStruct((N_TOK, qw), BF16),
            jax.ShapeDtypeStruct((N_TOK, MLA_KV_RANK), F32),
            jax.ShapeDtypeStruct((N_TOK, HEAD_SLOT), F32),
        ],
        compiler_params=_params("parallel"),
        name="mla_proj",
    )(x, nw, shift, scale, wqa, qan, wqb, wkv, kvn, ga, gb, ca, sb)


def _kv_expand_kernel(ckv_ref, kpe_ref, wuk_ref, wuv_ref, ga_ref, gb_ref, ca_ref, sb_ref, k_ref, v_ref):
    c = ckv_ref[...].astype(BF16)
    kn = _dot(c, wuk_ref[...])
    v_ref[...] = _dot(c, wuv_ref[...]).astype(BF16)
    kpe = kpe_ref[...]
    ga = ga_ref[...] * ca_ref[...]
    gb = gb_ref[...] * sb_ref[...]
    valid = lax.broadcasted_iota(jnp.int32, ga.shape, 1) < MLA_QK
    for hd in range(MLA_HEADS):
        sl = slice(hd * HEAD_SLOT, (hd + 1) * HEAD_SLOT)
        k_ref[:, sl] = _head_norm_rope(kn[:, sl] + kpe, ga, gb, valid).astype(BF16)


def _kv_expand(ckv, kpe, wuk, wuv, ga, gb, ca, sb):
    n = ckv.shape[0]
    tm = ROW_TILE
    row = lambda i: (i, 0)
    fixed = lambda i: (0, 0)
    kw = MLA_HEADS * HEAD_SLOT
    vw = MLA_HEADS * MLA_V
    return pl.pallas_call(
        _kv_expand_kernel,
        grid=(n // tm,),
        in_specs=[
            pl.BlockSpec((tm, MLA_KV_RANK), row),
            pl.BlockSpec((tm, HEAD_SLOT), row),
            pl.BlockSpec((MLA_KV_RANK, kw), fixed),
            pl.BlockSpec((MLA_KV_RANK, vw), fixed),
            pl.BlockSpec((1, HEAD_SLOT), fixed),
            pl.BlockSpec((1, HEAD_SLOT), fixed),
            pl.BlockSpec((tm, HEAD_SLOT), row),
            pl.BlockSpec((tm, HEAD_SLOT), row),
        ],
        out_specs=[pl.BlockSpec((tm, kw), row), pl.BlockSpec((tm, vw), row)],
        out_shape=[jax.ShapeDtypeStruct((n, kw), BF16), jax.ShapeDtypeStruct((n, vw), BF16)],
        compiler_params=_params("parallel"),
        name="kv_expand",
    )(ckv, kpe, wuk, wuv, ga, gb, ca, sb)


def _attn_kernel(q_ref, k_ref, v_ref, o_ref, *, heads):
    for hd in range(heads):
        qh = q_ref[:, hd * HEAD_SLOT:(hd + 1) * HEAD_SLOT]
        kh = k_ref[:, hd * HEAD_SLOT:(hd + 1) * HEAD_SLOT]
        vh = v_ref[:, hd * MLA_V:(hd + 1) * MLA_V]
        s = lax.dot_general(qh, kh, (((1,), (1,)), ((), ())), preferred_element_type=F32)
        p = jnp.exp(s - jnp.max(s, axis=-1, keepdims=True))
        inv = 1.0 / jnp.sum(p, axis=-1, keepdims=True)
        o_ref[:, hd * MLA_V:(hd + 1) * MLA_V] = (_dot(p.astype(BF16), vh) * inv).astype(BF16)


def _attention(q, k, v, *, batch, q_off, tq_total, tq, heads):
    tk = k.shape[1]
    groups = MLA_HEADS // heads
    return pl.pallas_call(
        functools.partial(_attn_kernel, heads=heads),
        grid=(batch, groups, tq_total // tq),
        in_specs=[
            pl.BlockSpec((None, tq, heads * HEAD_SLOT), lambda b, g, i: (b + q_off, i, g)),
            pl.BlockSpec((None, tk, heads * HEAD_SLOT), lambda b, g, i: (b, 0, g)),
            pl.BlockSpec((None, tk, heads * MLA_V), lambda b, g, i: (b, 0, g)),
        ],
        out_specs=pl.BlockSpec((None, tq, heads * MLA_V), lambda b, g, i: (b, i, g)),
        out_shape=jax.ShapeDtypeStruct((batch, tq_total, MLA_HEADS * MLA_V), BF16),
        compiler_params=_params("parallel", "parallel", "parallel"),
        name="attention",
    )(q, k, v)


def _proj_residual_kernel(ac_ref, al_ref, w_ref, x_ref, g_ref, nw_ref, sh_ref, sc_ref, *rest, router):
    if router:
        rhi_ref, rlo_ref, x1_ref, h2_ref, lg_ref, y_sc = rest
    else:
        x1_ref, h2_ref, y_sc = rest
    i = pl.program_id(0)

    @pl.when(i < CTX_TILES)
    def _():
        y_sc[...] = _dot(ac_ref[...], w_ref[...])

    @pl.when(i >= CTX_TILES)
    def _():
        y_sc[...] = _dot(al_ref[...], w_ref[...])

    x1 = x_ref[...] + g_ref[...] * y_sc[...]
    h2 = _modulate(x1, nw_ref[...], sh_ref[...], sc_ref[...])
    if router:
        hi = h2.astype(BF16)
        lo = (h2 - hi.astype(F32)).astype(BF16)
        lg_ref[...] = _dot(hi, rhi_ref[...]) + (_dot(lo, rhi_ref[...]) + _dot(hi, rlo_ref[...]))
    x1_ref[...] = x1
    h2_ref[...] = h2.astype(h2_ref.dtype)


def _proj_residual(a_ctx, a_lat, w, x, gate, nw, shift, scale, router_w=None):
    tm = ROW_TILE
    kdim = a_ctx.shape[1]
    row = lambda i: (i, 0)
    fixed = lambda i: (0, 0)
    cond = lambda i: (_cond_of_tile(i), 0, 0)
    router = router_w is not None
    in_specs = [
        pl.BlockSpec((tm, kdim), lambda i: (jnp.minimum(i, CTX_TILES - 1), 0)),
        pl.BlockSpec((tm, kdim), lambda i: (jnp.maximum(i - CTX_TILES, 0), 0)),
        pl.BlockSpec((kdim, D_MODEL), fixed),
        pl.BlockSpec((tm, D_MODEL), row),
        pl.BlockSpec((None, 1, D_MODEL), cond),
        pl.BlockSpec((1, D_MODEL), fixed),
        pl.BlockSpec((None, 1, D_MODEL), cond),
        pl.BlockSpec((None, 1, D_MODEL), cond),
    ]
    out_specs = [pl.BlockSpec((tm, D_MODEL), row), pl.BlockSpec((tm, D_MODEL), row)]
    out_shape = [jax.ShapeDtypeStruct((N_TOK, D_MODEL), F32),
                 jax.ShapeDtypeStruct((N_TOK, D_MODEL), F32 if router else BF16)]
    args = [a_ctx, a_lat, w, x, gate, nw, shift, scale]
    if router:
        in_specs += [pl.BlockSpec((D_MODEL, 128), fixed), pl.BlockSpec((D_MODEL, 128), fixed)]
        out_specs.append(pl.BlockSpec((tm, 128), row))
        out_shape.append(jax.ShapeDtypeStruct((N_TOK, 128), F32))
        args += list(router_w)
    return pl.pallas_call(
        functools.partial(_proj_residual_kernel, router=router),
        grid=(N_TOK // tm,),
        in_specs=in_specs,
        out_specs=out_specs,
        out_shape=out_shape,
        scratch_shapes=[pltpu.VMEM((tm, D_MODEL), F32)],
        compiler_params=_params("arbitrary"),
        name="proj_residual",
    )(*args)


def _swiglu_partial(x, wg, wu, wd):
    g = _dot(x, wg)
    u = _dot(x, wu)
    return _dot((_silu(g) * u).astype(BF16), wd)


def _ffn_kernel(h_ref, x1_ref, g_ref, wg_ref, wu_ref, wd_ref, o_ref, acc_ref):
    f = pl.program_id(1)
    part = _swiglu_partial(h_ref[...], wg_ref[...], wu_ref[...], wd_ref[...])

    @pl.when(f == 0)
    def _():
        acc_ref[...] = part

    @pl.when(f > 0)
    def _():
        acc_ref[...] += part

    @pl.when(f == pl.num_programs(1) - 1)
    def _():
        o_ref[...] = x1_ref[...] + g_ref[...] * acc_ref[...]


def _ffn(h2, x1, gate, wg, wu, wd):
    tm, tf = ROW_TILE, FFN_TILE
    return pl.pallas_call(
        _ffn_kernel,
        grid=(N_TOK // tm, FFN_DIM // tf),
        in_specs=[
            pl.BlockSpec((tm, D_MODEL), lambda i, f: (i, 0)),
            pl.BlockSpec((tm, D_MODEL), lambda i, f: (i, 0)),
            pl.BlockSpec((None, 1, D_MODEL), lambda i, f: (_cond_of_tile(i), 0, 0)),
            pl.BlockSpec((D_MODEL, tf), lambda i, f: (0, f)),
            pl.BlockSpec((D_MODEL, tf), lambda i, f: (0, f)),
            pl.BlockSpec((tf, D_MODEL), lambda i, f: (f, 0)),
        ],
        out_specs=pl.BlockSpec((tm, D_MODEL), lambda i, f: (i, 0)),
        out_shape=jax.ShapeDtypeStruct((N_TOK, D_MODEL), F32),
        scratch_shapes=[pltpu.VMEM((tm, D_MODEL), F32)],
        compiler_params=_params("parallel", "arbitrary"),
        name="ffn",
    )(h2, x1, gate, wg, wu, wd)


def _moe_kernel(blk_e_ref, nact_ref, src_ref, x_hbm, wg_ref, wu_ref, wd_ref, ya_hbm,
                xrows0, xrows1, xb, acc0, acc1, gsem, ssem):
    b = pl.program_id(0)
    f = pl.program_id(1)
    nact = nact_ref[0]
    xrows = (xrows0, xrows1)
    acc = (acc0, acc1)

    def gather_copy(par, row, a):
        tok = jnp.minimum(a >> 1, N_TOK - 1)
        return pltpu.make_async_copy(x_hbm.at[pl.ds(tok, 1), :], xrows[par].at[pl.ds(row, 1), :], gsem.at[par])

    def scatter_copy(par, row, a):
        return pltpu.make_async_copy(acc[par].at[pl.ds(row, 1), :], ya_hbm.at[pl.ds(a, 1), :], ssem.at[par])

    def wait_gather(par):
        pltpu.make_async_copy(x_hbm.at[pl.ds(0, MOE_ROWS), :], xrows[par], gsem.at[par]).wait()

    def wait_scatter(par):
        pltpu.make_async_copy(acc[par], ya_hbm.at[pl.ds(0, MOE_ROWS), :], ssem.at[par]).wait()

    @pl.when((b == 0) & (f == 0))
    def _():
        acc1[...] = jnp.zeros_like(acc1)

        def issue(r, c):
            gather_copy(0, r, src_ref[MOE_ROWS + r]).start()
            return c

        lax.fori_loop(0, MOE_ROWS, issue, 0, unroll=8)

    def block_body(par):
        @pl.when(f == 0)
        def _():
            wait_gather(par)

            @pl.when(b >= 1)
            def _():
                wait_scatter(par)

            xb[...] = xrows[par][...].astype(BF16)

        row0 = f * MOE_STEP_ROWS
        nxt = (b + 2) * MOE_ROWS + row0
        prv = b * MOE_ROWS + row0
        for r in range(MOE_STEP_ROWS):
            gather_copy(1 - par, row0 + r, src_ref[nxt + r]).start()
            scatter_copy(1 - par, row0 + r, src_ref[prv + r]).start()
        part = _swiglu_partial(xb[...], wg_ref[...], wu_ref[...], wd_ref[...])

        @pl.when(f == 0)
        def _():
            acc[par][...] = part

        @pl.when(f > 0)
        def _():
            acc[par][...] += part

    def drain(par):
        wait_scatter(par)

        def issue(r, c):
            scatter_copy(1 - par, r, src_ref[b * MOE_ROWS + r]).start()
            return c

        lax.fori_loop(0, MOE_ROWS, issue, 0, unroll=8)
        wait_gather(par)
        wait_scatter(1 - par)

    for par in range(2):
        @pl.when((b < nact) & (b % 2 == par))
        def _():
            block_body(par)

        @pl.when((b == nact) & (f == 0) & (b % 2 == par))
        def _():
            drain(par)


def _moe(blk_e, nact, src, h2, wg, wu, wd, layer):
    tf = MOE_TILE
    wmap = lambda b, f, e, n, s: (layer, e[b], 0, f)
    grid_spec = pltpu.PrefetchScalarGridSpec(
        num_scalar_prefetch=3,
        grid=(MOE_BLOCKS, MOE_STEPS),
        in_specs=[
            pl.BlockSpec(memory_space=pl.ANY),
            pl.BlockSpec((None, None, D_MODEL, tf), wmap),
            pl.BlockSpec((None, None, D_MODEL, tf), wmap),
            pl.BlockSpec((None, None, tf, D_MODEL), lambda b, f, e, n, s: (layer, e[b], f, 0)),
        ],
        out_specs=pl.BlockSpec(memory_space=pl.ANY),
        scratch_shapes=[
            pltpu.VMEM((MOE_ROWS, D_MODEL), F32),
            pltpu.VMEM((MOE_ROWS, D_MODEL), F32),
            pltpu.VMEM((MOE_ROWS, D_MODEL), BF16),
            pltpu.VMEM((MOE_ROWS, D_MODEL), F32),
            pltpu.VMEM((MOE_ROWS, D_MODEL), F32),
            pltpu.SemaphoreType.DMA((2,)),
            pltpu.SemaphoreType.DMA((2,)),
        ],
    )
    return pl.pallas_call(
        _moe_kernel,
        grid_spec=grid_spec,
        out_shape=jax.ShapeDtypeStruct((N_ASSIGN + MOE_ROWS, D_MODEL), F32),
        compiler_params=_params("arbitrary", "arbitrary"),
        name="moe",
    )(blk_e, nact, src, h2, wg, wu, wd)


def _combine_kernel(ya_ref, x1_ref, g_ref, w_ref, *o_refs):
    w = w_ref[...]
    y = w[:, 0:1] * ya_ref[:, :D_MODEL] + w[:, 1:2] * ya_ref[:, D_MODEL:]
    out = x1_ref[...] + g_ref[...] * y
    if len(o_refs) == 1:
        o_refs[0][...] = out
    else:
        i = pl.program_id(0)

        @pl.when(i < CTX_TILES)
        def _():
            o_refs[0][...] = out

        @pl.when(i >= CTX_TILES)
        def _():
            o_refs[1][...] = out


def _combine(ya, x1, gate, top_w, split):
    tm = ROW_TILE
    ya2 = ya.reshape(ya.shape[0] // TOP_K, TOP_K * D_MODEL)
    row = lambda i: (i, 0)
    if split:
        out_specs = [pl.BlockSpec((tm, D_MODEL), lambda i: (jnp.minimum(i, CTX_TILES - 1), 0)),
                     pl.BlockSpec((tm, D_MODEL), lambda i: (jnp.maximum(i - CTX_TILES, 0), 0))]
        out_shape = [jax.ShapeDtypeStruct((N_CTX, D_MODEL), F32), jax.ShapeDtypeStruct((N_LAT, D_MODEL), F32)]
    else:
        out_specs = pl.BlockSpec((tm, D_MODEL), row)
        out_shape = jax.ShapeDtypeStruct((N_TOK, D_MODEL), F32)
    return pl.pallas_call(
        _combine_kernel,
        grid=(N_TOK // tm,),
        in_specs=[
            pl.BlockSpec((tm, TOP_K * D_MODEL), row),
            pl.BlockSpec((tm, D_MODEL), row),
            pl.BlockSpec((None, 1, D_MODEL), lambda i: (_cond_of_tile(i), 0, 0)),
            pl.BlockSpec((tm, TOP_K), row),
        ],
        out_specs=out_specs,
        out_shape=out_shape,
        compiler_params=_params("arbitrary"),
        name="moe_combine",
    )(ya2, x1, gate, top_w)


def _route(logits):
    top_v, top_i = lax.top_k(logits, TOP_K)
    top_w = jax.nn.softmax(top_v, axis=-1)
    e = top_i.reshape(N_ASSIGN)
    onehot = (e[:, None] == jnp.arange(N_EXPERTS, dtype=e.dtype)[None, :]).astype(jnp.int32)
    csum = jnp.cumsum(onehot, axis=0)
    rank = jnp.sum(csum * onehot, axis=1) - 1
    counts = csum[-1]
    padded = (counts + MOE_ROWS - 1) // MOE_ROWS * MOE_ROWS
    pad_end = jnp.cumsum(padded)
    pad_start = pad_end - padded
    pos = (pad_start[e] + rank).astype(jnp.int32)
    n_src = (MOE_BLOCKS + 2) * MOE_ROWS
    dump = N_ASSIGN + jnp.arange(n_src, dtype=jnp.int32) % MOE_ROWS
    src = dump.at[pos + MOE_ROWS].set(jnp.arange(N_ASSIGN, dtype=jnp.int32))
    blk_start = jnp.arange(MOE_BLOCKS, dtype=jnp.int32) * MOE_ROWS
    blk_e = jnp.minimum(jnp.searchsorted(pad_end, blk_start, side="right"), N_EXPERTS - 1).astype(jnp.int32)
    nact = (pad_end[-1:] // MOE_ROWS).astype(jnp.int32)
    return top_w, src, blk_e, nact


def _ret_proj_kernel(x_ref, nw_ref, sh_ref, sc_ref, w_ref, c_ref, s_ref, o_ref, h_sc):
    i = pl.program_id(0)
    j = pl.program_id(1)

    @pl.when(j == 0)
    def _():
        h_sc[...] = _modulate(x_ref[...], nw_ref[...], sh_ref[...], sc_ref[...]).astype(BF16)

    acc = _dot(h_sc[...], w_ref[...])
    n_qk = 2 * RET_HEADS * RET_DK // o_ref.shape[1]
    n_v = RET_HEADS * RET_DV // o_ref.shape[1]

    @pl.when(j >= n_qk + n_v)
    def _():
        o_ref[...] = _silu(acc).astype(BF16)

    @pl.when((j >= n_qk) & (j < n_qk + n_v))
    def _():
        o_ref[...] = acc.astype(BF16)

    @pl.when(j < n_qk)
    def _():
        x = acc * jnp.where(j >= n_qk // 2, RET_DK ** -0.5, 1.0)

        @pl.when(i < CTX_TILES)
        def _():
            o_ref[...] = x.astype(BF16)

        @pl.when(i >= CTX_TILES)
        def _():
            half = RET_DK // 4
            for g in range(x.shape[1] // 128):
                sl = slice(g * 128, (g + 1) * 128)
                tl = slice((g % 2) * 128, (g % 2 + 1) * 128)
                xs = x[:, sl]
                o_ref[:, sl] = (xs * c_ref[:, tl] + pltpu.roll(xs, half, 1) * s_ref[:, tl]).astype(BF16)


def _ret_proj(x, nw, shift, scale, w, rope_c, rope_s):
    tm, tn = ROW_TILE, 1024
    width = w.shape[1]
    cond = lambda i, j: (_cond_of_tile(i), 0, 0)
    tab = lambda i, j: (jnp.where(i < CTX_TILES, 0, (i - CTX_TILES) % TILES_PER_LAT_BATCH), 0)
    return pl.pallas_call(
        _ret_proj_kernel,
        grid=(N_TOK // tm, width // tn),
        in_specs=[
            pl.BlockSpec((tm, D_MODEL), lambda i, j: (i, 0)),
            pl.BlockSpec((1, D_MODEL), lambda i, j: (0, 0)),
            pl.BlockSpec((None, 1, D_MODEL), cond),
            pl.BlockSpec((None, 1, D_MODEL), cond),
            pl.BlockSpec((D_MODEL, tn), lambda i, j: (0, j)),
            pl.BlockSpec((tm, RET_DK), tab),
            pl.BlockSpec((tm, RET_DK), tab),
        ],
        out_specs=pl.BlockSpec((tm, tn), lambda i, j: (i, j)),
        out_shape=jax.ShapeDtypeStruct((N_TOK, width), BF16),
        scratch_shapes=[pltpu.VMEM((tm, D_MODEL), BF16)],
        compiler_params=_params("parallel", "arbitrary"),
        name="ret_proj",
    )(x, nw, shift, scale, w, rope_c, rope_s)


def _retention_kernel(lg_ref, q_ref, k_ref, v_ref, sg_ref, gn_ref, *rest, zero_init, n_chunks, layer):
    if zero_init:
        y_ref, st_ref, o_acc, state = rest[-4:]
    else:
        s0f_ref, s0b_ref, y_ref, o_acc, state = rest
    hd = pl.program_id(1)
    c = RET_CHUNK
    ri = lax.broadcasted_iota(jnp.int32, (c, c), 0).astype(F32)
    ci = lax.broadcasted_iota(jnp.int32, (c, c), 1).astype(F32)
    pos = lax.broadcasted_iota(jnp.int32, (c, 1), 0).astype(F32)

    def run(direction):
        lg = lg_ref[direction, hd]
        rel = (ri - ci) if direction == 0 else (ci - ri)
        dmask = jnp.where(rel >= 0, jnp.exp(jnp.maximum(rel, 0.0) * lg), 0.0)
        if direction == 0:
            q_decay = jnp.exp((pos + 1.0) * lg)
            k_decay = jnp.exp((c - 1.0 - pos) * lg)
        else:
            q_decay = jnp.exp((c - pos) * lg)
            k_decay = jnp.exp(pos * lg)
        c_decay = jnp.exp(jnp.full((1, 1), c, F32) * lg)

        def step(t, carry):
            ch = t if direction == 0 else n_chunks - 1 - t
            rows = pl.ds(pl.multiple_of(ch * c, c), c)
            qc = q_ref[rows, :]
            kc = k_ref[rows, :]
            vc = v_ref[rows, :]
            a = lax.dot_general(qc, kc, (((1,), (1,)), ((), ())), preferred_element_type=F32) * dmask
            s = state[...]
            o = _dot(a.astype(BF16), vc) + _dot((qc.astype(F32) * q_decay).astype(BF16), s.astype(BF16))
            if direction == 0:
                o_acc[rows, :] = o
            else:
                o_acc[rows, :] += o
            kd = (kc.astype(F32) * k_decay).astype(BF16)
            state[...] = s * c_decay + lax.dot_general(kd, vc, (((0,), (0,)), ((), ())),
                                                       preferred_element_type=F32)
            return carry

        lax.fori_loop(0, n_chunks, step, 0)

    for direction in range(2):
        if zero_init:
            state[...] = jnp.zeros_like(state)
        else:
            state[...] = (s0f_ref if direction == 0 else s0b_ref)[...]
        run(direction)
        if zero_init and st_ref.ndim == 3:
            st_ref[direction] = state[...]
        elif zero_init:
            st_ref[layer, direction] = state[...]
            for other in range(st_ref.shape[0]):
                if other != layer:
                    st_ref[other, direction] = jnp.zeros_like(state)

    o = o_acc[...]
    y = _rms(o, gn_ref[...])
    y_ref[...] = (sg_ref[...].astype(F32) * y).astype(BF16)


def _retention(log_gamma, qkvg, gn, state0, *, batch, b_off, t, layer, states=None):
    zero_init = state0 is None
    qblk = RET_HEADS * RET_DK // RET_DK
    vblk = 2 * RET_HEADS * RET_DK // RET_DV
    gblk = vblk + RET_HEADS
    in_specs = [
        pl.BlockSpec(memory_space=pltpu.SMEM),
        pl.BlockSpec((None, t, RET_DK), lambda b, h: (b + b_off, 0, h)),
        pl.BlockSpec((None, t, RET_DK), lambda b, h: (b + b_off, 0, qblk + h)),
        pl.BlockSpec((None, t, RET_DV), lambda b, h: (b + b_off, 0, vblk + h)),
        pl.BlockSpec((None, t, RET_DV), lambda b, h: (b + b_off, 0, gblk + h)),
        pl.BlockSpec((None, 1, RET_DV), lambda b, h: (h, 0, 0)),
    ]
    args = [log_gamma, qkvg, qkvg, qkvg, qkvg, gn]
    y_spec = pl.BlockSpec((None, t, RET_DV), lambda b, h: (b, 0, h))
    y_shape = jax.ShapeDtypeStruct((batch, t, RET_HEADS * RET_DV), BF16)
    aliases = {}
    if zero_init:
        n_ret = DEPTH // 2
        st_shape = jax.ShapeDtypeStruct((batch, n_ret, 2, RET_HEADS, RET_DK, RET_DV), F32)
        if states is None:
            st_spec = pl.BlockSpec((None, n_ret, 2, None, RET_DK, RET_DV), lambda b, h: (b, 0, 0, h, 0, 0))
        else:
            st_spec = pl.BlockSpec((None, None, 2, None, RET_DK, RET_DV), lambda b, h: (b, layer, 0, h, 0, 0))
            aliases = {len(args): 1}
            in_specs.append(pl.BlockSpec(memory_space=pl.ANY))
            args.append(states)
        out_specs = [y_spec, st_spec]
        out_shape = [y_shape, st_shape]
    else:
        in_specs += [
            pl.BlockSpec((None, None, None, None, RET_DK, RET_DV), lambda b, h: (b, layer, 0, h, 0, 0)),
            pl.BlockSpec((None, None, None, None, RET_DK, RET_DV), lambda b, h: (b, layer, 1, h, 0, 0))]
        args += [state0, state0]
        out_specs = y_spec
        out_shape = y_shape
    return pl.pallas_call(
        functools.partial(_retention_kernel, zero_init=zero_init, n_chunks=t // RET_CHUNK, layer=layer),
        grid=(batch, RET_HEADS),
        in_specs=in_specs,
        out_specs=out_specs,
        out_shape=out_shape,
        input_output_aliases=aliases,
        scratch_shapes=[pltpu.VMEM((t, RET_DV), F32), pltpu.VMEM((RET_DK, RET_DV), F32)],
        compiler_params=_params("parallel", "parallel"),
        name="retention",
    )(*args)


def _rope_tables(rot_dim):
    rows = DEC_SEQ // GRID_W
    row = jnp.repeat(jnp.arange(rows), GRID_W)
    col = jnp.tile(jnp.arange(GRID_W), rows)
    nf = rot_dim // 4
    inv = ROPE_BASE ** (-jnp.arange(nf, dtype=F32) / nf)
    ang = jnp.stack([row, col], axis=-1).astype(F32)[:, :, None] * inv
    cos, sin = jnp.cos(ang), jnp.sin(ang)
    c = jnp.stack([cos, cos], axis=2).reshape(DEC_SEQ, rot_dim)
    s = jnp.stack([-sin, sin], axis=2).reshape(DEC_SEQ, rot_dim)
    return c, s


def _swap_perm():
    nf = MLA_ROPE // 4
    return jnp.arange(MLA_ROPE) ^ nf


def _head_slot_tables(c32, s32):
    t = c32.shape[0]
    ca = jnp.concatenate([jnp.ones((t, MLA_NOPE), F32), c32, jnp.zeros((t, MLA_ROPE), F32)], axis=1)
    sb = jnp.concatenate([jnp.zeros((t, MLA_NOPE), F32), s32, jnp.zeros((t, MLA_ROPE), F32)], axis=1)
    return ca, sb


def _head_slot_gains(g, scale):
    perm = _swap_perm()
    zeros = jnp.zeros((MLA_ROPE,), F32)
    ga = jnp.concatenate([g, zeros]) * scale
    gb = jnp.concatenate([jnp.zeros((MLA_NOPE,), F32), g[MLA_NOPE:][perm], zeros]) * scale
    return ga.reshape(1, HEAD_SLOT), gb.reshape(1, HEAD_SLOT)


def _rows3(m):
    return m.reshape(N_COND, 1, D_MODEL)


def kernel(x_prompt, x_sample, cache_ckv, cache_kpe, state_ret, c, c_ctx, mod_w, mod_b, norm1_w, norm2_w, mla_wq_a, mla_q_a_norm, mla_wq_b, mla_wkv_a, mla_kv_norm, mla_w_uk, mla_w_uv, mla_q_norm, mla_k_norm, mla_wo, ret_wq, ret_wk, ret_wv, ret_wg, ret_wo, ret_decay, ret_gn, ffn_w_gate, ffn_w_up, ffn_w_down, moe_router, moe_w_gate, moe_w_up, moe_w_down):
    x = jnp.concatenate([x_prompt.reshape(N_CTX, D_MODEL), x_sample.reshape(N_LAT, D_MODEL)], axis=0)
    cond = jnp.concatenate([c_ctx[None, :], c, jnp.zeros((N_COND - 1 - DEC_BATCH, D_MODEL), F32)], axis=0)
    mods = _adaln(cond, mod_w, mod_b).reshape(DEPTH, N_COND, 6, D_MODEL)

    perm = _swap_perm()
    c32, s32 = _rope_tables(MLA_ROPE)
    ca_lat, sb_lat = _head_slot_tables(c32, s32)
    ca_id, sb_id = _head_slot_tables(jnp.ones((1, MLA_ROPE), F32), jnp.zeros((1, MLA_ROPE), F32))
    ca_q = jnp.concatenate([jnp.broadcast_to(ca_id, (N_CTX, HEAD_SLOT)), jnp.tile(ca_lat, (DEC_BATCH, 1))], axis=0)
    sb_q = jnp.concatenate([jnp.broadcast_to(sb_id, (N_CTX, HEAD_SLOT)), jnp.tile(sb_lat, (DEC_BATCH, 1))], axis=0)
    lat_kv = DEC_SEQ + PAST_LEN
    ca_kl = jnp.tile(jnp.concatenate([ca_lat, jnp.broadcast_to(ca_id, (PAST_LEN, HEAD_SLOT))], axis=0), (DEC_BATCH, 1))
    sb_kl = jnp.tile(jnp.concatenate([sb_lat, jnp.broadcast_to(sb_id, (PAST_LEN, HEAD_SLOT))], axis=0), (DEC_BATCH, 1))
    ca_kc = jnp.broadcast_to(ca_id, (N_CTX, HEAD_SLOT))
    sb_kc = jnp.broadcast_to(sb_id, (N_CTX, HEAD_SLOT))
    rc256, rs256 = _rope_tables(RET_DK)

    moe_wg, moe_wu, moe_wd = moe_w_gate.astype(BF16), moe_w_up.astype(BF16), moe_w_down.astype(BF16)

    new_ckv, new_kpe, new_ret = [], [], None
    for i in range(DEPTH):
        j = i // 2
        m = mods[i]
        sh1, sc1, g1, sh2, sc2, g2 = (_rows3(m[:, t]) for t in range(6))
        nw1 = norm1_w[i].reshape(1, D_MODEL)
        nw2 = norm2_w[i].reshape(1, D_MODEL)
        if i % 2 == 0:
            wqb = mla_wq_b[j].reshape(MLA_Q_RANK, MLA_HEADS, MLA_QK)
            wqb = jnp.concatenate([wqb, wqb[:, :, MLA_NOPE:][:, :, perm]], axis=2)
            wqb = wqb.reshape(MLA_Q_RANK, MLA_HEADS * HEAD_SLOT).astype(BF16)
            wkv = mla_wkv_a[j]
            wkv = jnp.concatenate([wkv[:, :MLA_KV_RANK], jnp.zeros((D_MODEL, MLA_NOPE), F32),
                                   wkv[:, MLA_KV_RANK:], wkv[:, MLA_KV_RANK:][:, perm]], axis=1).astype(BF16)
            wuk = mla_w_uk[j].reshape(MLA_KV_RANK, MLA_HEADS, MLA_NOPE)
            wuk = jnp.concatenate([wuk, jnp.zeros_like(wuk)], axis=2)
            wuk = wuk.reshape(MLA_KV_RANK, MLA_HEADS * HEAD_SLOT).astype(BF16)
            gaq, gbq = _head_slot_gains(mla_q_norm[j], MLA_QK ** -0.5)
            gak, gbk = _head_slot_gains(mla_k_norm[j], 1.0)

            q, ckv, kpe = _mla_proj(x, nw1, sh1, sc1, mla_wq_a[j].astype(BF16),
                                    mla_q_a_norm[j].reshape(1, MLA_Q_RANK), wqb, wkv,
                                    mla_kv_norm[j].reshape(1, MLA_KV_RANK), gaq, gbq, ca_q, sb_q)
            new_ckv.append(ckv[:N_CTX].reshape(BATCH, SEQ, MLA_KV_RANK))
            new_kpe.append(kpe[:N_CTX, MLA_NOPE:MLA_QK].reshape(BATCH, SEQ, MLA_ROPE))

            kpe_x = cache_kpe[:, j]
            kpe_x = jnp.concatenate([jnp.zeros((DEC_BATCH, PAST_LEN, MLA_NOPE), F32), kpe_x, kpe_x[:, :, perm]], axis=2)
            ckv_l = jnp.concatenate([ckv[N_CTX:].reshape(DEC_BATCH, DEC_SEQ, MLA_KV_RANK), cache_ckv[:, j]], axis=1)
            kpe_l = jnp.concatenate([kpe[N_CTX:].reshape(DEC_BATCH, DEC_SEQ, HEAD_SLOT), kpe_x], axis=1)
            wuv = mla_w_uv[j].astype(BF16)
            k_c, v_c = _kv_expand(ckv[:N_CTX], kpe[:N_CTX], wuk, wuv, gak, gbk, ca_kc, sb_kc)
            k_l, v_l = _kv_expand(ckv_l.reshape(DEC_BATCH * lat_kv, MLA_KV_RANK),
                                  kpe_l.reshape(DEC_BATCH * lat_kv, HEAD_SLOT), wuk, wuv, gak, gbk, ca_kl, sb_kl)

            qw = MLA_HEADS * HEAD_SLOT
            o_c = _attention(q.reshape(N_TOK // SEQ, SEQ, qw), k_c.reshape(BATCH, SEQ, qw),
                             v_c.reshape(BATCH, SEQ, MLA_HEADS * MLA_V),
                             batch=BATCH, q_off=0, tq_total=SEQ, tq=SEQ, heads=MLA_HEADS)
            o_l = _attention(q.reshape(N_TOK // DEC_SEQ, DEC_SEQ, qw), k_l.reshape(DEC_BATCH, lat_kv, qw),
                             v_l.reshape(DEC_BATCH, lat_kv, MLA_HEADS * MLA_V),
                             batch=DEC_BATCH, q_off=N_CTX // DEC_SEQ, tq_total=DEC_SEQ, tq=256, heads=2)
            mix_c, mix_l = o_c.reshape(N_CTX, D_MODEL), o_l.reshape(N_LAT, D_MODEL)
            w_out = mla_wo[j].astype(BF16)
        else:
            w_in = jnp.concatenate([ret_wq[j], ret_wk[j], ret_wv[j], ret_wg[j]], axis=1).astype(BF16)
            qkvg = _ret_proj(x, nw1, sh1, sc1, w_in, rc256, rs256)
            log_gamma = -jnp.exp(ret_decay[j].astype(F32))
            gn = ret_gn[j].reshape(RET_HEADS, 1, RET_DV)
            width = qkvg.shape[1]
            y_c, new_ret = _retention(log_gamma, qkvg.reshape(N_TOK // SEQ, SEQ, width), gn, None,
                                      batch=BATCH, b_off=0, t=SEQ, layer=j, states=new_ret)
            y_l = _retention(log_gamma, qkvg.reshape(N_TOK // DEC_SEQ, DEC_SEQ, width), gn, state_ret,
                             batch=DEC_BATCH, b_off=N_CTX // DEC_SEQ, t=DEC_SEQ, layer=j)
            mix_c, mix_l = y_c.reshape(N_CTX, RET_HEADS * RET_DV), y_l.reshape(N_LAT, RET_HEADS * RET_DV)
            w_out = ret_wo[j].astype(BF16)

        if i % 2 == 0:
            x1, h2 = _proj_residual(mix_c, mix_l, w_out, x, g1, nw2, sh2, sc2)
            x = _ffn(h2, x1, g2, ffn_w_gate[j].astype(BF16), ffn_w_up[j].astype(BF16), ffn_w_down[j].astype(BF16))
        else:
            wr = jnp.pad(moe_router[j], ((0, 0), (0, 128 - N_EXPERTS)))
            wr_hi = wr.astype(BF16)
            wr_lo = (wr - wr_hi.astype(F32)).astype(BF16)
            x1, h2, logits = _proj_residual(mix_c, mix_l, w_out, x, g1, nw2, sh2, sc2, router_w=(wr_hi, wr_lo))
            top_w, src, blk_e, nact = _route(logits[:, :N_EXPERTS])
            ya = _moe(blk_e, nact, src, h2, moe_wg, moe_wu, moe_wd, j)
            x = _combine(ya, x1, g2, top_w, split=(i == DEPTH - 1))

    y_prompt = x[0].reshape(BATCH, SEQ, D_MODEL)
    y_sample = x[1].reshape(DEC_BATCH, DEC_SEQ, D_MODEL)
    return (y_prompt, y_sample, jnp.stack(new_ckv, axis=1), jnp.stack(new_kpe, axis=1), new_ret)
```

```python
import functools

import jax
import jax.numpy as jnp
from jax import lax
from jax.experimental import pallas as pl
from jax.experimental.pallas import tpu as pltpu

F32 = jnp.float32
BF16 = jnp.bfloat16

D_MODEL = 1024
BATCH = 32
SEQ = 256
DEPTH = 4
DEC_BATCH = 4
DEC_SEQ = 2048
PAST_LEN = 512
GRID_W = 64
ROPE_BASE = 10000.0
NORM_EPS = 1e-6

MLA_HEADS = 16
MLA_NOPE = 64
MLA_ROPE = 32
MLA_QK = MLA_NOPE + MLA_ROPE
MLA_V = 64
MLA_Q_RANK = 384
MLA_KV_RANK = 256
HEAD_SLOT = 128

RET_HEADS = 4
RET_DK = 256
RET_DV = 512
RET_CHUNK = 128

FFN_DIM = 2816
N_EXPERTS = 8
TOP_K = 2
EXPERT_DIM = 3584

N_CTX = BATCH * SEQ
N_LAT = DEC_BATCH * DEC_SEQ
N_TOK = N_CTX + N_LAT
N_COND = 8

ROW_TILE = 512
CTX_TILES = N_CTX // ROW_TILE
TILES_PER_LAT_BATCH = DEC_SEQ // ROW_TILE

MOE_ROWS = 512
MOE_TILE = 1792
MOE_STEPS = EXPERT_DIM // MOE_TILE
MOE_STEP_ROWS = MOE_ROWS // MOE_STEPS
N_ASSIGN = N_TOK * TOP_K
MOE_BLOCKS = N_ASSIGN // MOE_ROWS + N_EXPERTS

VMEM_LIMIT = 56 * 1024 * 1024


def _params(*sem):
    return pltpu.CompilerParams(dimension_semantics=sem, vmem_limit_bytes=VMEM_LIMIT)


def _cond_of_tile(i):
    return jnp.where(i < CTX_TILES, 0, 1 + (i - CTX_TILES) // TILES_PER_LAT_BATCH)


def _dot(a, b):
    return jnp.dot(a, b, preferred_element_type=F32)


def _silu(x):
    return x * (1.0 / (1.0 + jnp.exp(-x)))


def _rms(x, w):
    return x * lax.rsqrt(jnp.mean(x * x, axis=-1, keepdims=True) + NORM_EPS) * w


def _modulate(x, nw, shift, scale):
    return _rms(x, nw) * (1.0 + scale) + shift


def _adaln_kernel(c_ref, w_ref, b_ref, o_ref):
    a = _silu(c_ref[...]).astype(BF16)
    o_ref[...] = _dot(a, w_ref[...].astype(BF16)) + b_ref[...]


def _adaln(cond, mod_w, mod_b):
    tn = 1024
    return pl.pallas_call(
        _adaln_kernel,
        grid=(DEPTH, 6 * D_MODEL // tn),
        in_specs=[
            pl.BlockSpec((N_COND, D_MODEL), lambda l, j: (0, 0)),
            pl.BlockSpec((None, D_MODEL, tn), lambda l, j: (l, 0, j)),
            pl.BlockSpec((None, 1, tn), lambda l, j: (l, 0, j)),
        ],
        out_specs=pl.BlockSpec((None, N_COND, tn), lambda l, j: (l, 0, j)),
        out_shape=jax.ShapeDtypeStruct((DEPTH, N_COND, 6 * D_MODEL), F32),
        compiler_params=_params("parallel", "parallel"),
        name="adaln",
    )(cond, mod_w, mod_b.reshape(DEPTH, 1, 6 * D_MODEL))


def _head_norm_rope(xh, ga, gb, valid):
    ss = jnp.sum(jnp.where(valid, xh * xh, 0.0), axis=-1, keepdims=True) * (1.0 / MLA_QK)
    r = lax.rsqrt(ss + NORM_EPS)
    return r * (xh * ga + pltpu.roll(xh, HEAD_SLOT - MLA_ROPE, 1) * gb)


def _mla_proj_kernel(x_ref, nw_ref, sh_ref, sc_ref, wqa_ref, qan_ref, wqb_ref, wkv_ref, kvn_ref,
                     ga_ref, gb_ref, ca_ref, sb_ref, q_ref, ckv_ref, kpe_ref):
    h = _modulate(x_ref[...], nw_ref[...], sh_ref[...], sc_ref[...]).astype(BF16)
    qa = _rms(_dot(h, wqa_ref[...]), qan_ref[...]).astype(BF16)
    q = _dot(qa, wqb_ref[...])
    kv = _dot(h, wkv_ref[...])
    ckv_ref[...] = _rms(kv[:, :MLA_KV_RANK], kvn_ref[...])
    kpe_ref[...] = kv[:, MLA_KV_RANK:]
    ga = ga_ref[...] * ca_ref[...]
    gb = gb_ref[...] * sb_ref[...]
    valid = lax.broadcasted_iota(jnp.int32, ga.shape, 1) < MLA_QK
    for hd in range(MLA_HEADS):
        sl = slice(hd * HEAD_SLOT, (hd + 1) * HEAD_SLOT)
        q_ref[:, sl] = _head_norm_rope(q[:, sl], ga, gb, valid).astype(BF16)


def _mla_proj(x, nw, shift, scale, wqa, qan, wqb, wkv, kvn, ga, gb, ca, sb):
    tm = ROW_TILE
    row = lambda i: (i, 0)
    fixed = lambda i: (0, 0)
    cond = lambda i: (_cond_of_tile(i), 0, 0)
    qw = MLA_HEADS * HEAD_SLOT
    kvw = MLA_KV_RANK + HEAD_SLOT
    return pl.pallas_call(
        _mla_proj_kernel,
        grid=(N_TOK // tm,),
        in_specs=[
            pl.BlockSpec((tm, D_MODEL), row),
            pl.BlockSpec((1, D_MODEL), fixed),
            pl.BlockSpec((None, 1, D_MODEL), cond),
            pl.BlockSpec((None, 1, D_MODEL), cond),
            pl.BlockSpec((D_MODEL, MLA_Q_RANK), fixed),
            pl.BlockSpec((1, MLA_Q_RANK), fixed),
            pl.BlockSpec((MLA_Q_RANK, qw), fixed),
            pl.BlockSpec((D_MODEL, kvw), fixed),
            pl.BlockSpec((1, MLA_KV_RANK), fixed),
            pl.BlockSpec((1, HEAD_SLOT), fixed),
            pl.BlockSpec((1, HEAD_SLOT), fixed),
            pl.BlockSpec((tm, HEAD_SLOT), row),
            pl.BlockSpec((tm, HEAD_SLOT), row),
        ],
        out_specs=[
            pl.BlockSpec((tm, qw), row),
            pl.BlockSpec((tm, MLA_KV_RANK), row),
            pl.BlockSpec((tm, HEAD_SLOT), row),
        ],
        out_shape=[
            jax.ShapeDtypeStruct((N_TOK, qw), BF16),
            jax.ShapeDtypeStruct((N_TOK, MLA_KV_RANK), F32),
            jax.ShapeDtypeStruct((N_TOK, HEAD_SLOT), F32),
        ],
        compiler_params=_params("parallel"),
        name="mla_proj",
    )(x, nw, shift, scale, wqa, qan, wqb, wkv, kvn, ga, gb, ca, sb)


def _kv_expand_kernel(ckv_ref, kpe_ref, wuk_ref, wuv_ref, ga_ref, gb_ref, ca_ref, sb_ref, k_ref, v_ref):
    c = ckv_ref[...].astype(BF16)
    kn = _dot(c, wuk_ref[...])
    v_ref[...] = _dot(c, wuv_ref[...]).astype(BF16)
    kpe = kpe_ref[...]
    ga = ga_ref[...] * ca_ref[...]
    gb = gb_ref[...] * sb_ref[...]
    valid = lax.broadcasted_iota(jnp.int32, ga.shape, 1) < MLA_QK
    for hd in range(MLA_HEADS):
        sl = slice(hd * HEAD_SLOT, (hd + 1) * HEAD_SLOT)
        k_ref[:, sl] = _head_norm_rope(kn[:, sl] + kpe, ga, gb, valid).astype(BF16)


def _kv_expand(ckv, kpe, wuk, wuv, ga, gb, ca, sb):
    n = ckv.shape[0]
    tm = ROW_TILE
    row = lambda i: (i, 0)
    fixed = lambda i: (0, 0)
    kw = MLA_HEADS * HEAD_SLOT
    vw = MLA_HEADS * MLA_V
    return pl.pallas_call(
        _kv_expand_kernel,
        grid=(n // tm,),
        in_specs=[
            pl.BlockSpec((tm, MLA_KV_RANK), row),
            pl.BlockSpec((tm, HEAD_SLOT), row),
            pl.BlockSpec((MLA_KV_RANK, kw), fixed),
            pl.BlockSpec((MLA_KV_RANK, vw), fixed),
            pl.BlockSpec((1, HEAD_SLOT), fixed),
            pl.BlockSpec((1, HEAD_SLOT), fixed),
            pl.BlockSpec((tm, HEAD_SLOT), row),
            pl.BlockSpec((tm, HEAD_SLOT), row),
        ],
        out_specs=[pl.BlockSpec((tm, kw), row), pl.BlockSpec((tm, vw), row)],
        out_shape=[jax.ShapeDtypeStruct((n, kw), BF16), jax.ShapeDtypeStruct((n, vw), BF16)],
        compiler_params=_params("parallel"),
        name="kv_expand",
    )(ckv, kpe, wuk, wuv, ga, gb, ca, sb)


def _attn_kernel(q_ref, k_ref, v_ref, o_ref, *, heads):
    for hd in range(heads):
        qh = q_ref[:, hd * HEAD_SLOT:(hd + 1) * HEAD_SLOT]
        kh = k_ref[:, hd * HEAD_SLOT:(hd + 1) * HEAD_SLOT]
        vh = v_ref[:, hd * MLA_V:(hd + 1) * MLA_V]
        s = lax.dot_general(qh, kh, (((1,), (1,)), ((), ())), preferred_element_type=F32)
        p = jnp.exp(s - jnp.max(s, axis=-1, keepdims=True))
        inv = 1.0 / jnp.sum(p, axis=-1, keepdims=True)
        o_ref[:, hd * MLA_V:(hd + 1) * MLA_V] = (_dot(p.astype(BF16), vh) * inv).astype(BF16)


def _attention(q, k, v, *, batch, q_off, tq_total, tq, heads):
    tk = k.shape[1]
    groups = MLA_HEADS // heads
    return pl.pallas_call(
        functools.partial(_attn_kernel, heads=heads),
        grid=(batch, groups, tq_total // tq),
        in_specs=[
            pl.BlockSpec((None, tq, heads * HEAD_SLOT), lambda b, g, i: (b + q_off, i, g)),
            pl.BlockSpec((None, tk, heads * HEAD_SLOT), lambda b, g, i: (b, 0, g)),
            pl.BlockSpec((None, tk, heads * MLA_V), lambda b, g, i: (b, 0, g)),
        ],
        out_specs=pl.BlockSpec((None, tq, heads * MLA_V), lambda b, g, i: (b, i, g)),
        out_shape=jax.ShapeDtypeStruct((batch, tq_total, MLA_HEADS * MLA_V), BF16),
        compiler_params=_params("parallel", "parallel", "parallel"),
        name="attention",
    )(q, k, v)


def _proj_residual_kernel(ac_ref, al_ref, w_ref, x_ref, g_ref, nw_ref, sh_ref, sc_ref, *rest, router):
    if router:
        rhi_ref, rlo_ref, x1_ref, h2_ref, lg_ref, y_sc = rest
    else:
        x1_ref, h2_ref, y_sc = rest
    i = pl.program_id(0)

    @pl.when(i < CTX_TILES)
    def _():
        y_sc[...] = _dot(ac_ref[...], w_ref[...])

    @pl.when(i >= CTX_TILES)
    def _():
        y_sc[...] = _dot(al_ref[...], w_ref[...])

    x1 = x_ref[...] + g_ref[...] * y_sc[...]
    h2 = _modulate(x1, nw_ref[...], sh_ref[...], sc_ref[...])
    if router:
        hi = h2.astype(BF16)
        lo = (h2 - hi.astype(F32)).astype(BF16)
        lg_ref[...] = _dot(hi, rhi_ref[...]) + (_dot(lo, rhi_ref[...]) + _dot(hi, rlo_ref[...]))
    x1_ref[...] = x1
    h2_ref[...] = h2.astype(h2_ref.dtype)


def _proj_residual(a_ctx, a_lat, w, x, gate, nw, shift, scale, router_w=None):
    tm = ROW_TILE
    kdim = a_ctx.shape[1]
    row = lambda i: (i, 0)
    fixed = lambda i: (0, 0)
    cond = lambda i: (_cond_of_tile(i), 0, 0)
    router = router_w is not None
    in_specs = [
        pl.BlockSpec((tm, kdim), lambda i: (jnp.minimum(i, CTX_TILES - 1), 0)),
        pl.BlockSpec((tm, kdim), lambda i: (jnp.maximum(i - CTX_TILES, 0), 0)),
        pl.BlockSpec((kdim, D_MODEL), fixed),
        pl.BlockSpec((tm, D_MODEL), row),
        pl.BlockSpec((None, 1, D_MODEL), cond),
        pl.BlockSpec((1, D_MODEL), fixed),
        pl.BlockSpec((None, 1, D_MODEL), cond),
        pl.BlockSpec((None, 1, D_MODEL), cond),
    ]
    out_specs = [pl.BlockSpec((tm, D_MODEL), row), pl.BlockSpec((tm, D_MODEL), row)]
    out_shape = [jax.ShapeDtypeStruct((N_TOK, D_MODEL), F32),
                 jax.ShapeDtypeStruct((N_TOK, D_MODEL), F32 if router else BF16)]
    args = [a_ctx, a_lat, w, x, gate, nw, shift, scale]
    if router:
        in_specs += [pl.BlockSpec((D_MODEL, 128), fixed), pl.BlockSpec((D_MODEL, 128), fixed)]
        out_specs.append(pl.BlockSpec((tm, 128), row))
        out_shape.append(jax.ShapeDtypeStruct((N_TOK, 128), F32))
        args += list(router_w)
    return pl.pallas_call(
        functools.partial(_proj_residual_kernel, router=router),
        grid=(N_TOK // tm,),
        in_specs=in_specs,
        out_specs=out_specs,
        out_shape=out_shape,
        scratch_shapes=[pltpu.VMEM((tm, D_MODEL), F32)],
        compiler_params=_params("arbitrary"),
        name="proj_residual",
    )(*args)


def _swiglu_partial(x, wg, wu, wd):
    g = _dot(x, wg)
    u = _dot(x, wu)
    return _dot((_silu(g) * u).astype(BF16), wd)


def _ffn_kernel(h_ref, x1_ref, g_ref, wg_ref, wu_ref, wd_ref, o_ref):
    part = _swiglu_partial(h_ref[...], wg_ref[...], wu_ref[...], wd_ref[...])
    o_ref[...] = x1_ref[...] + g_ref[...] * part


def _resident(shape):
    return pl.BlockSpec(shape, lambda *_: (0,) * len(shape), pipeline_mode=pl.Buffered(1))


def _ffn(h2, x1, gate, wg, wu, wd):
    tm = ROW_TILE
    row = lambda i: (i, 0)
    return pl.pallas_call(
        _ffn_kernel,
        grid=(N_TOK // tm,),
        in_specs=[
            pl.BlockSpec((tm, D_MODEL), row),
            pl.BlockSpec((tm, D_MODEL), row),
            pl.BlockSpec((None, 1, D_MODEL), lambda i: (_cond_of_tile(i), 0, 0)),
            _resident((D_MODEL, FFN_DIM)),
            _resident((D_MODEL, FFN_DIM)),
            _resident((FFN_DIM, D_MODEL)),
        ],
        out_specs=pl.BlockSpec((tm, D_MODEL), row),
        out_shape=jax.ShapeDtypeStruct((N_TOK, D_MODEL), F32),
        compiler_params=_params("parallel"),
        name="ffn",
    )(h2, x1, gate, wg, wu, wd)


def _moe_kernel(blk_e_ref, nact_ref, src_ref, x_hbm, wg_ref, wu_ref, wd_ref, ya_hbm,
                xrows0, xrows1, xb, acc0, acc1, gsem, ssem):
    b = pl.program_id(0)
    f = pl.program_id(1)
    nact = nact_ref[0]
    xrows = (xrows0, xrows1)
    acc = (acc0, acc1)

    def gather_copy(par, row, a):
        tok = a & (N_TOK - 1)
        return pltpu.make_async_copy(x_hbm.at[pl.ds(tok, 1), :], xrows[par].at[pl.ds(row, 1), :], gsem.at[par])

    def scatter_copy(par, row, a):
        return pltpu.make_async_copy(acc[par].at[pl.ds(row, 1), :], ya_hbm.at[pl.ds(a, 1), :], ssem.at[par])

    def wait_gather(par):
        pltpu.make_async_copy(x_hbm.at[pl.ds(0, MOE_ROWS), :], xrows[par], gsem.at[par]).wait()

    def wait_scatter(par):
        pltpu.make_async_copy(acc[par], ya_hbm.at[pl.ds(0, MOE_ROWS), :], ssem.at[par]).wait()

    @pl.when((b == 0) & (f == 0))
    def _():
        acc1[...] = jnp.zeros_like(acc1)

        def issue(r, c):
            gather_copy(0, r, src_ref[MOE_ROWS + r]).start()
            return c

        lax.fori_loop(0, MOE_ROWS, issue, 0, unroll=8)

    def block_body(par):
        @pl.when(f == 0)
        def _():
            wait_gather(par)

            @pl.when(b >= 1)
            def _():
                wait_scatter(par)

            xb[...] = xrows[par][...].astype(BF16)

        row0 = f * MOE_STEP_ROWS
        nxt = (b + 2) * MOE_ROWS + row0
        prv = b * MOE_ROWS + row0
        for r in range(MOE_STEP_ROWS):
            gather_copy(1 - par, row0 + r, src_ref[nxt + r]).start()
            scatter_copy(1 - par, row0 + r, src_ref[prv + r]).start()
        part = _swiglu_partial(xb[...], wg_ref[...], wu_ref[...], wd_ref[...])

        @pl.when(f == 0)
        def _():
            acc[par][...] = part

        @pl.when(f > 0)
        def _():
            acc[par][...] += part

    def drain(par):
        wait_scatter(par)

        def issue(r, c):
            scatter_copy(1 - par, r, src_ref[b * MOE_ROWS + r]).start()
            return c

        lax.fori_loop(0, MOE_ROWS, issue, 0, unroll=8)
        wait_gather(par)
        wait_scatter(1 - par)

    for par in range(2):
        @pl.when((b < nact) & (b % 2 == par))
        def _():
            block_body(par)

        @pl.when((b == nact) & (f == 0) & (b % 2 == par))
        def _():
            drain(par)


def _moe(blk_e, nact, src, h2, wg, wu, wd, layer):
    tf = MOE_TILE
    wmap = lambda b, f, e, n, s: (layer, e[b], 0, f)
    grid_spec = pltpu.PrefetchScalarGridSpec(
        num_scalar_prefetch=3,
        grid=(MOE_BLOCKS, MOE_STEPS),
        in_specs=[
            pl.BlockSpec(memory_space=pl.ANY),
            pl.BlockSpec((None, None, D_MODEL, tf), wmap),
            pl.BlockSpec((None, None, D_MODEL, tf), wmap),
            pl.BlockSpec((None, None, tf, D_MODEL), lambda b, f, e, n, s: (layer, e[b], f, 0)),
        ],
        out_specs=pl.BlockSpec(memory_space=pl.ANY),
        scratch_shapes=[
            pltpu.VMEM((MOE_ROWS, D_MODEL), F32),
            pltpu.VMEM((MOE_ROWS, D_MODEL), F32),
            pltpu.VMEM((MOE_ROWS, D_MODEL), BF16),
            pltpu.VMEM((MOE_ROWS, D_MODEL), F32),
            pltpu.VMEM((MOE_ROWS, D_MODEL), F32),
            pltpu.SemaphoreType.DMA((2,)),
            pltpu.SemaphoreType.DMA((2,)),
        ],
    )
    return pl.pallas_call(
        _moe_kernel,
        grid_spec=grid_spec,
        out_shape=jax.ShapeDtypeStruct((N_ASSIGN + MOE_ROWS, D_MODEL), F32),
        compiler_params=_params("arbitrary", "arbitrary"),
        name="moe",
    )(blk_e, nact, src, h2, wg, wu, wd)


def _combine_kernel(y0_ref, y1_ref, x1_ref, g_ref, w_ref, *o_refs):
    w = w_ref[...]
    y = w[:, 0:1] * y0_ref[...] + w[:, 1:2] * y1_ref[...]
    out = x1_ref[...] + g_ref[...] * y
    if len(o_refs) == 1:
        o_refs[0][...] = out
    else:
        i = pl.program_id(0)

        @pl.when(i < CTX_TILES)
        def _():
            o_refs[0][...] = out

        @pl.when(i >= CTX_TILES)
        def _():
            o_refs[1][...] = out


def _combine(ya, x1, gate, top_w, split):
    tm = ROW_TILE
    row = lambda i: (i, 0)
    if split:
        out_specs = [pl.BlockSpec((tm, D_MODEL), lambda i: (jnp.minimum(i, CTX_TILES - 1), 0)),
                     pl.BlockSpec((tm, D_MODEL), lambda i: (jnp.maximum(i - CTX_TILES, 0), 0))]
        out_shape = [jax.ShapeDtypeStruct((N_CTX, D_MODEL), F32), jax.ShapeDtypeStruct((N_LAT, D_MODEL), F32)]
    else:
        out_specs = pl.BlockSpec((tm, D_MODEL), row)
        out_shape = jax.ShapeDtypeStruct((N_TOK, D_MODEL), F32)
    return pl.pallas_call(
        _combine_kernel,
        grid=(N_TOK // tm,),
        in_specs=[
            pl.BlockSpec((tm, D_MODEL), row),
            pl.BlockSpec((tm, D_MODEL), lambda i: (i + N_TOK // tm, 0)),
            pl.BlockSpec((tm, D_MODEL), row),
            pl.BlockSpec((None, 1, D_MODEL), lambda i: (_cond_of_tile(i), 0, 0)),
            pl.BlockSpec((tm, TOP_K), row),
        ],
        out_specs=out_specs,
        out_shape=out_shape,
        compiler_params=_params("arbitrary"),
        name="moe_combine",
    )(ya, ya, x1, gate, top_w)


def _route(logits):
    top_v, top_i = lax.top_k(logits, TOP_K)
    top_w = jax.nn.softmax(top_v, axis=-1)
    e = top_i.reshape(N_ASSIGN)
    onehot = (e[:, None] == jnp.arange(N_EXPERTS, dtype=e.dtype)[None, :]).astype(jnp.int32)
    csum = jnp.cumsum(onehot, axis=0)
    rank = jnp.sum(csum * onehot, axis=1) - 1
    counts = csum[-1]
    padded = (counts + MOE_ROWS - 1) // MOE_ROWS * MOE_ROWS
    pad_end = jnp.cumsum(padded)
    pad_start = pad_end - padded
    pos = (pad_start[e] + rank).astype(jnp.int32)
    n_src = (MOE_BLOCKS + 2) * MOE_ROWS
    dump = N_ASSIGN + jnp.arange(n_src, dtype=jnp.int32) % MOE_ROWS
    a = jnp.arange(N_ASSIGN, dtype=jnp.int32)
    src = dump.at[pos + MOE_ROWS].set((a % TOP_K) * N_TOK + a // TOP_K)
    blk_start = jnp.arange(MOE_BLOCKS, dtype=jnp.int32) * MOE_ROWS
    blk_e = jnp.minimum(jnp.searchsorted(pad_end, blk_start, side="right"), N_EXPERTS - 1).astype(jnp.int32)
    nact = (pad_end[-1:] // MOE_ROWS).astype(jnp.int32)
    return top_w, src, blk_e, nact


RET_QK_W = RET_HEADS * RET_DK
RET_V_W = RET_HEADS * RET_DV
RET_COL_TILE = 1024


def _ret_proj_kernel(x_ref, nw_ref, sh_ref, sc_ref, w_ref, c_ref, s_ref, o_ref):
    def body(rope):
        h = _modulate(x_ref[...], nw_ref[...], sh_ref[...], sc_ref[...]).astype(BF16)
        for c0 in range(0, o_ref.shape[1], RET_COL_TILE):
            acc = _dot(h, w_ref[:, c0:c0 + RET_COL_TILE])
            if c0 >= 2 * RET_QK_W + RET_V_W:
                o_ref[:, c0:c0 + RET_COL_TILE] = _silu(acc).astype(BF16)
            elif c0 >= 2 * RET_QK_W:
                o_ref[:, c0:c0 + RET_COL_TILE] = acc.astype(BF16)
            else:
                x = acc * (RET_DK ** -0.5) if c0 >= RET_QK_W else acc
                if not rope:
                    o_ref[:, c0:c0 + RET_COL_TILE] = x.astype(BF16)
                    continue
                for g in range(RET_COL_TILE // 128):
                    tl = slice((g % 2) * 128, (g % 2 + 1) * 128)
                    xs = x[:, g * 128:(g + 1) * 128]
                    y = xs * c_ref[:, tl] + pltpu.roll(xs, RET_DK // 4, 1) * s_ref[:, tl]
                    o_ref[:, c0 + g * 128:c0 + (g + 1) * 128] = y.astype(BF16)

    i = pl.program_id(0)

    @pl.when(i < CTX_TILES)
    def _():
        body(False)

    @pl.when(i >= CTX_TILES)
    def _():
        body(True)


def _ret_proj(x, nw, shift, scale, w, rope_c, rope_s):
    tm = ROW_TILE
    width = w.shape[1]
    row = lambda i: (i, 0)
    cond = lambda i: (_cond_of_tile(i), 0, 0)
    tab = lambda i: (jnp.where(i < CTX_TILES, 0, (i - CTX_TILES) % TILES_PER_LAT_BATCH), 0)
    return pl.pallas_call(
        _ret_proj_kernel,
        grid=(N_TOK // tm,),
        in_specs=[
            pl.BlockSpec((tm, D_MODEL), row),
            pl.BlockSpec((1, D_MODEL), lambda i: (0, 0)),
            pl.BlockSpec((None, 1, D_MODEL), cond),
            pl.BlockSpec((None, 1, D_MODEL), cond),
            _resident((D_MODEL, width)),
            pl.BlockSpec((tm, RET_DK), tab),
            pl.BlockSpec((tm, RET_DK), tab),
        ],
        out_specs=pl.BlockSpec((tm, width), row),
        out_shape=jax.ShapeDtypeStruct((N_TOK, width), BF16),
        compiler_params=_params("parallel"),
        name="ret_proj",
    )(x, nw, shift, scale, w, rope_c, rope_s)


def _retention_kernel(lg_ref, q_ref, k_ref, v_ref, sg_ref, gn_ref, *rest, zero_init, n_chunks, layer):
    if zero_init:
        y_ref, st_ref, o_acc, state = rest[-4:]
    else:
        s0f_ref, s0b_ref, y_ref, o_acc, state = rest
    hd = pl.program_id(1)
    c = RET_CHUNK
    ri = lax.broadcasted_iota(jnp.int32, (c, c), 0).astype(F32)
    ci = lax.broadcasted_iota(jnp.int32, (c, c), 1).astype(F32)
    pos = lax.broadcasted_iota(jnp.int32, (c, 1), 0).astype(F32)

    def run(direction):
        lg = lg_ref[direction, hd]
        rel = (ri - ci) if direction == 0 else (ci - ri)
        dmask = jnp.where(rel >= 0, jnp.exp(jnp.maximum(rel, 0.0) * lg), 0.0)
        if direction == 0:
            q_decay = jnp.exp((pos + 1.0) * lg)
            k_decay = jnp.exp((c - 1.0 - pos) * lg)
        else:
            q_decay = jnp.exp((c - pos) * lg)
            k_decay = jnp.exp(pos * lg)
        c_decay = jnp.exp(jnp.full((1, 1), c, F32) * lg)

        def step(t, carry):
            ch = t if direction == 0 else n_chunks - 1 - t
            rows = pl.ds(pl.multiple_of(ch * c, c), c)
            qc = q_ref[rows, :]
            kc = k_ref[rows, :]
            vc = v_ref[rows, :]
            a = lax.dot_general(qc, kc, (((1,), (1,)), ((), ())), preferred_element_type=F32) * dmask
            s = state[...]
            o = _dot(a.astype(BF16), vc) + _dot((qc.astype(F32) * q_decay).astype(BF16), s.astype(BF16))
            if direction == 0:
                o_acc[rows, :] = o
            else:
                o_acc[rows, :] += o
            kd = (kc.astype(F32) * k_decay).astype(BF16)
            state[...] = s * c_decay + lax.dot_general(kd, vc, (((0,), (0,)), ((), ())),
                                                       preferred_element_type=F32)
            return carry

        lax.fori_loop(0, n_chunks, step, 0)

    for direction in range(2):
        if zero_init:
            state[...] = jnp.zeros_like(state)
        else:
            state[...] = (s0f_ref if direction == 0 else s0b_ref)[...]
        run(direction)
        if zero_init and st_ref.ndim == 3:
            st_ref[direction] = state[...]
        elif zero_init:
            st_ref[layer, direction] = state[...]
            for other in range(st_ref.shape[0]):
                if other != layer:
                    st_ref[other, direction] = jnp.zeros_like(state)

    o = o_acc[...]
    y = _rms(o, gn_ref[...])
    y_ref[...] = (sg_ref[...].astype(F32) * y).astype(BF16)


def _retention(log_gamma, qkvg, gn, state0, *, batch, b_off, t, layer, states=None):
    zero_init = state0 is None
    qblk = RET_HEADS * RET_DK // RET_DK
    vblk = 2 * RET_HEADS * RET_DK // RET_DV
    gblk = vblk + RET_HEADS
    in_specs = [
        pl.BlockSpec(memory_space=pltpu.SMEM),
        pl.BlockSpec((None, t, RET_DK), lambda b, h: (b + b_off, 0, h)),
        pl.BlockSpec((None, t, RET_DK), lambda b, h: (b + b_off, 0, qblk + h)),
        pl.BlockSpec((None, t, RET_DV), lambda b, h: (b + b_off, 0, vblk + h)),
        pl.BlockSpec((None, t, RET_DV), lambda b, h: (b + b_off, 0, gblk + h)),
        pl.BlockSpec((None, 1, RET_DV), lambda b, h: (h, 0, 0)),
    ]
    args = [log_gamma, qkvg, qkvg, qkvg, qkvg, gn]
    y_spec = pl.BlockSpec((None, t, RET_DV), lambda b, h: (b, 0, h))
    y_shape = jax.ShapeDtypeStruct((batch, t, RET_HEADS * RET_DV), BF16)
    aliases = {}
    if zero_init:
        n_ret = DEPTH // 2
        st_shape = jax.ShapeDtypeStruct((batch, n_ret, 2, RET_HEADS, RET_DK, RET_DV), F32)
        if states is None:
            st_spec = pl.BlockSpec((None, n_ret, 2, None, RET_DK, RET_DV), lambda b, h: (b, 0, 0, h, 0, 0))
        else:
            st_spec = pl.BlockSpec((None, None, 2, None, RET_DK, RET_DV), lambda b, h: (b, layer, 0, h, 0, 0))
            aliases = {len(args): 1}
            in_specs.append(pl.BlockSpec(memory_space=pl.ANY))
            args.append(states)
        out_specs = [y_spec, st_spec]
        out_shape = [y_shape, st_shape]
    else:
        in_specs += [
            pl.BlockSpec((None, None, None, None, RET_DK, RET_DV), lambda b, h: (b, layer, 0, h, 0, 0)),
            pl.BlockSpec((None, None, None, None, RET_DK, RET_DV), lambda b, h: (b, layer, 1, h, 0, 0))]
        args += [state0, state0]
        out_specs = y_spec
        out_shape = y_shape
    return pl.pallas_call(
        functools.partial(_retention_kernel, zero_init=zero_init, n_chunks=t // RET_CHUNK, layer=layer),
        grid=(batch, RET_HEADS),
        in_specs=in_specs,
        out_specs=out_specs,
        out_shape=out_shape,
        input_output_aliases=aliases,
        scratch_shapes=[pltpu.VMEM((t, RET_DV), F32), pltpu.VMEM((RET_DK, RET_DV), F32)],
        compiler_params=_params("parallel", "parallel"),
        name="retention",
    )(*args)


def _rope_tables(rot_dim):
    rows = DEC_SEQ // GRID_W
    row = jnp.repeat(jnp.arange(rows), GRID_W)
    col = jnp.tile(jnp.arange(GRID_W), rows)
    nf = rot_dim // 4
    inv = ROPE_BASE ** (-jnp.arange(nf, dtype=F32) / nf)
    ang = jnp.stack([row, col], axis=-1).astype(F32)[:, :, None] * inv
    cos, sin = jnp.cos(ang), jnp.sin(ang)
    c = jnp.stack([cos, cos], axis=2).reshape(DEC_SEQ, rot_dim)
    s = jnp.stack([-sin, sin], axis=2).reshape(DEC_SEQ, rot_dim)
    return c, s


def _swap_perm():
    nf = MLA_ROPE // 4
    return jnp.arange(MLA_ROPE) ^ nf


def _head_slot_tables(c32, s32):
    t = c32.shape[0]
    ca = jnp.concatenate([jnp.ones((t, MLA_NOPE), F32), c32, jnp.zeros((t, MLA_ROPE), F32)], axis=1)
    sb = jnp.concatenate([jnp.zeros((t, MLA_NOPE), F32), s32, jnp.zeros((t, MLA_ROPE), F32)], axis=1)
    return ca, sb


def _head_slot_gains(g, scale):
    perm = _swap_perm()
    zeros = jnp.zeros((MLA_ROPE,), F32)
    ga = jnp.concatenate([g, zeros]) * scale
    gb = jnp.concatenate([jnp.zeros((MLA_NOPE,), F32), g[MLA_NOPE:][perm], zeros]) * scale
    return ga.reshape(1, HEAD_SLOT), gb.reshape(1, HEAD_SLOT)


def _rows3(m):
    return m.reshape(N_COND, 1, D_MODEL)


def kernel(x_prompt, x_sample, cache_ckv, cache_kpe, state_ret, c, c_ctx, mod_w, mod_b, norm1_w, norm2_w, mla_wq_a, mla_q_a_norm, mla_wq_b, mla_wkv_a, mla_kv_norm, mla_w_uk, mla_w_uv, mla_q_norm, mla_k_norm, mla_wo, ret_wq, ret_wk, ret_wv, ret_wg, ret_wo, ret_decay, ret_gn, ffn_w_gate, ffn_w_up, ffn_w_down, moe_router, moe_w_gate, moe_w_up, moe_w_down):
    x = jnp.concatenate([x_prompt.reshape(N_CTX, D_MODEL), x_sample.reshape(N_LAT, D_MODEL)], axis=0)
    cond = jnp.concatenate([c_ctx[None, :], c, jnp.zeros((N_COND - 1 - DEC_BATCH, D_MODEL), F32)], axis=0)
    mods = _adaln(cond, mod_w, mod_b).reshape(DEPTH, N_COND, 6, D_MODEL)

    perm = _swap_perm()
    c32, s32 = _rope_tables(MLA_ROPE)
    ca_lat, sb_lat = _head_slot_tables(c32, s32)
    ca_id, sb_id = _head_slot_tables(jnp.ones((1, MLA_ROPE), F32), jnp.zeros((1, MLA_ROPE), F32))
    ca_q = jnp.concatenate([jnp.broadcast_to(ca_id, (N_CTX, HEAD_SLOT)), jnp.tile(ca_lat, (DEC_BATCH, 1))], axis=0)
    sb_q = jnp.concatenate([jnp.broadcast_to(sb_id, (N_CTX, HEAD_SLOT)), jnp.tile(sb_lat, (DEC_BATCH, 1))], axis=0)
    lat_kv = DEC_SEQ + PAST_LEN
    ca_kl = jnp.tile(jnp.concatenate([ca_lat, jnp.broadcast_to(ca_id, (PAST_LEN, HEAD_SLOT))], axis=0), (DEC_BATCH, 1))
    sb_kl = jnp.tile(jnp.concatenate([sb_lat, jnp.broadcast_to(sb_id, (PAST_LEN, HEAD_SLOT))], axis=0), (DEC_BATCH, 1))
    ca_kc = jnp.broadcast_to(ca_id, (N_CTX, HEAD_SLOT))
    sb_kc = jnp.broadcast_to(sb_id, (N_CTX, HEAD_SLOT))
    rc256, rs256 = _rope_tables(RET_DK)

    moe_wg, moe_wu, moe_wd = moe_w_gate.astype(BF16), moe_w_up.astype(BF16), moe_w_down.astype(BF16)

    new_ckv, new_kpe, new_ret = [], [], None
    for i in range(DEPTH):
        j = i // 2
        m = mods[i]
        sh1, sc1, g1, sh2, sc2, g2 = (_rows3(m[:, t]) for t in range(6))
        nw1 = norm1_w[i].reshape(1, D_MODEL)
        nw2 = norm2_w[i].reshape(1, D_MODEL)
        if i % 2 == 0:
            wqb = mla_wq_b[j].reshape(MLA_Q_RANK, MLA_HEADS, MLA_QK)
            wqb = jnp.concatenate([wqb, wqb[:, :, MLA_NOPE:][:, :, perm]], axis=2)
            wqb = wqb.reshape(MLA_Q_RANK, MLA_HEADS * HEAD_SLOT).astype(BF16)
            wkv = mla_wkv_a[j]
            wkv = jnp.concatenate([wkv[:, :MLA_KV_RANK], jnp.zeros((D_MODEL, MLA_NOPE), F32),
                                   wkv[:, MLA_KV_RANK:], wkv[:, MLA_KV_RANK:][:, perm]], axis=1).astype(BF16)
            wuk = mla_w_uk[j].reshape(MLA_KV_RANK, MLA_HEADS, MLA_NOPE)
            wuk = jnp.concatenate([wuk, jnp.zeros_like(wuk)], axis=2)
            wuk = wuk.reshape(MLA_KV_RANK, MLA_HEADS * HEAD_SLOT).astype(BF16)
            gaq, gbq = _head_slot_gains(mla_q_norm[j], MLA_QK ** -0.5)
            gak, gbk = _head_slot_gains(mla_k_norm[j], 1.0)

            q, ckv, kpe = _mla_proj(x, nw1, sh1, sc1, mla_wq_a[j].astype(BF16),
                                    mla_q_a_norm[j].reshape(1, MLA_Q_RANK), wqb, wkv,
                                    mla_kv_norm[j].reshape(1, MLA_KV_RANK), gaq, gbq, ca_q, sb_q)
            new_ckv.append(ckv[:N_CTX].reshape(BATCH, SEQ, MLA_KV_RANK))
            new_kpe.append(kpe[:N_CTX, MLA_NOPE:MLA_QK].reshape(BATCH, SEQ, MLA_ROPE))

            kpe_x = cache_kpe[:, j]
            kpe_x = jnp.concatenate([jnp.zeros((DEC_BATCH, PAST_LEN, MLA_NOPE), F32), kpe_x, kpe_x[:, :, perm]], axis=2)
            ckv_l = jnp.concatenate([ckv[N_CTX:].reshape(DEC_BATCH, DEC_SEQ, MLA_KV_RANK), cache_ckv[:, j]], axis=1)
            kpe_l = jnp.concatenate([kpe[N_CTX:].reshape(DEC_BATCH, DEC_SEQ, HEAD_SLOT), kpe_x], axis=1)
            wuv = mla_w_uv[j].astype(BF16)
            k_c, v_c = _kv_expand(ckv[:N_CTX], kpe[:N_CTX], wuk, wuv, gak, gbk, ca_kc, sb_kc)
            k_l, v_l = _kv_expand(ckv_l.reshape(DEC_BATCH * lat_kv, MLA_KV_RANK),
                                  kpe_l.reshape(DEC_BATCH * lat_kv, HEAD_SLOT), wuk, wuv, gak, gbk, ca_kl, sb_kl)

            qw = MLA_HEADS * HEAD_SLOT
            o_c = _attention(q.reshape(N_TOK // SEQ, SEQ, qw), k_c.reshape(BATCH, SEQ, qw),
                             v_c.reshape(BATCH, SEQ, MLA_HEADS * MLA_V),
                             batch=BATCH, q_off=0, tq_total=SEQ, tq=SEQ, heads=MLA_HEADS)
            o_l = _attention(q.reshape(N_TOK // DEC_SEQ, DEC_SEQ, qw), k_l.reshape(DEC_BATCH, lat_kv, qw),
                             v_l.reshape(DEC_BATCH, lat_kv, MLA_HEADS * MLA_V),
                             batch=DEC_BATCH, q_off=N_CTX // DEC_SEQ, tq_total=DEC_SEQ, tq=256, heads=2)
            mix_c, mix_l = o_c.reshape(N_CTX, D_MODEL), o_l.reshape(N_LAT, D_MODEL)
            w_out = mla_wo[j].astype(BF16)
        else:
            w_in = jnp.concatenate([ret_wq[j], ret_wk[j], ret_wv[j], ret_wg[j]], axis=1).astype(BF16)
            qkvg = _ret_proj(x, nw1, sh1, sc1, w_in, rc256, rs256)
            log_gamma = -jnp.exp(ret_decay[j].astype(F32))
            gn = ret_gn[j].reshape(RET_HEADS, 1, RET_DV)
            width = qkvg.shape[1]
            y_c, new_ret = _retention(log_gamma, qkvg.reshape(N_TOK // SEQ, SEQ, width), gn, None,
                                      batch=BATCH, b_off=0, t=SEQ, layer=j, states=new_ret)
            y_l = _retention(log_gamma, qkvg.reshape(N_TOK // DEC_SEQ, DEC_SEQ, width), gn, state_ret,
                             batch=DEC_BATCH, b_off=N_CTX // DEC_SEQ, t=DEC_SEQ, layer=j)
            mix_c, mix_l = y_c.reshape(N_CTX, RET_HEADS * RET_DV), y_l.reshape(N_LAT, RET_HEADS * RET_DV)
            w_out = ret_wo[j].astype(BF16)

        if i % 2 == 0:
            x1, h2 = _proj_residual(mix_c, mix_l, w_out, x, g1, nw2, sh2, sc2)
            x = _ffn(h2, x1, g2, ffn_w_gate[j].astype(BF16), ffn_w_up[j].astype(BF16), ffn_w_down[j].astype(BF16))
        else:
            wr = jnp.pad(moe_router[j], ((0, 0), (0, 128 - N_EXPERTS)))
            wr_hi = wr.astype(BF16)
            wr_lo = (wr - wr_hi.astype(F32)).astype(BF16)
            x1, h2, logits = _proj_residual(mix_c, mix_l, w_out, x, g1, nw2, sh2, sc2, router_w=(wr_hi, wr_lo))
            top_w, src, blk_e, nact = _route(logits[:, :N_EXPERTS])
            ya = _moe(blk_e, nact, src, h2, moe_wg, moe_wu, moe_wd, j)
            x = _combine(ya, x1, g2, top_w, split=(i == DEPTH - 1))

    y_prompt = x[0].reshape(BATCH, SEQ, D_MODEL)
    y_sample = x[1].reshape(DEC_BATCH, DEC_SEQ, D_MODEL)
    return (y_prompt, y_sample, jnp.stack(new_ckv, axis=1), jnp.stack(new_kpe, axis=1), new_ret)
```

```python
import functools

import jax
import jax.numpy as jnp
from jax import lax
from jax.experimental import pallas as pl
from jax.experimental.pallas import tpu as pltpu

F32 = jnp.float32
BF16 = jnp.bfloat16

D_MODEL = 1024
BATCH = 32
SEQ = 256
DEPTH = 4
DEC_BATCH = 4
DEC_SEQ = 2048
PAST_LEN = 512
GRID_W = 64
ROPE_BASE = 10000.0
NORM_EPS = 1e-6

MLA_HEADS = 16
MLA_NOPE = 64
MLA_ROPE = 32
MLA_QK = MLA_NOPE + MLA_ROPE
MLA_V = 64
MLA_Q_RANK = 384
MLA_KV_RANK = 256
HEAD_SLOT = 128

RET_HEADS = 4
RET_DK = 256
RET_DV = 512
RET_BLOCK = 256

FFN_DIM = 2816
N_EXPERTS = 8
TOP_K = 2
EXPERT_DIM = 3584

N_CTX = BATCH * SEQ
N_LAT = DEC_BATCH * DEC_SEQ
N_TOK = N_CTX + N_LAT
N_COND = 8

ROW_TILE = 512
CTX_TILES = N_CTX // ROW_TILE
TILES_PER_LAT_BATCH = DEC_SEQ // ROW_TILE

MOE_ROWS = 512
MOE_TILE = 1792
MOE_STEPS = EXPERT_DIM // MOE_TILE
MOE_STEP_ROWS = MOE_ROWS // MOE_STEPS
N_ASSIGN = N_TOK * TOP_K
MOE_BLOCKS = N_ASSIGN // MOE_ROWS + N_EXPERTS

VMEM_LIMIT = 56 * 1024 * 1024
LOG2_E = 1.4426950408889634


def _params(*sem):
    return pltpu.CompilerParams(dimension_semantics=sem, vmem_limit_bytes=VMEM_LIMIT)


def _cond_of_tile(i):
    return jnp.where(i < CTX_TILES, 0, 1 + (i - CTX_TILES) // TILES_PER_LAT_BATCH)


def _dot(a, b):
    return jnp.dot(a, b, preferred_element_type=F32)


def _silu(x):
    return x * (1.0 / (1.0 + jnp.exp(-x)))


def _rms(x, w):
    return x * lax.rsqrt(jnp.mean(x * x, axis=-1, keepdims=True) + NORM_EPS) * w


def _modulate(x, nw, shift, scale):
    return _rms(x, nw) * (1.0 + scale) + shift


def _adaln_kernel(c_ref, w_ref, b_ref, o_ref):
    a = _silu(c_ref[...]).astype(BF16)
    o_ref[...] = _dot(a, w_ref[...].astype(BF16)) + b_ref[...]


def _adaln(cond, mod_w, mod_b):
    tn = 1024
    return pl.pallas_call(
        _adaln_kernel,
        grid=(DEPTH, 6 * D_MODEL // tn),
        in_specs=[
            pl.BlockSpec((N_COND, D_MODEL), lambda l, j: (0, 0)),
            pl.BlockSpec((None, D_MODEL, tn), lambda l, j: (l, 0, j)),
            pl.BlockSpec((None, 1, tn), lambda l, j: (l, 0, j)),
        ],
        out_specs=pl.BlockSpec((None, N_COND, tn), lambda l, j: (l, 0, j)),
        out_shape=jax.ShapeDtypeStruct((DEPTH, N_COND, 6 * D_MODEL), F32),
        compiler_params=_params("parallel", "parallel"),
        name="adaln",
    )(cond, mod_w, mod_b.reshape(DEPTH, 1, 6 * D_MODEL))


def _head_norm_rope(xh, ga, gb, valid):
    ss = jnp.sum(jnp.where(valid, xh * xh, 0.0), axis=-1, keepdims=True) * (1.0 / MLA_QK)
    r = lax.rsqrt(ss + NORM_EPS)
    return r * (xh * ga + pltpu.roll(xh, HEAD_SLOT - MLA_ROPE, 1) * gb)


def _mla_proj_kernel(x_ref, nw_ref, sh_ref, sc_ref, wqa_ref, qan_ref, wqb_ref, wkv_ref, kvn_ref,
                     ga_ref, gb_ref, ca_ref, sb_ref, q_ref, ckv_ref, kpe_ref):
    h = _modulate(x_ref[...], nw_ref[...], sh_ref[...], sc_ref[...]).astype(BF16)
    qa = _rms(_dot(h, wqa_ref[...]), qan_ref[...]).astype(BF16)
    q = _dot(qa, wqb_ref[...])
    kv = _dot(h, wkv_ref[...])
    ckv_ref[...] = _rms(kv[:, :MLA_KV_RANK], kvn_ref[...])
    kpe_ref[...] = kv[:, MLA_KV_RANK:]
    ga = ga_ref[...] * ca_ref[...]
    gb = gb_ref[...] * sb_ref[...]
    valid = lax.broadcasted_iota(jnp.int32, ga.shape, 1) < MLA_QK
    for hd in range(MLA_HEADS):
        sl = slice(hd * HEAD_SLOT, (hd + 1) * HEAD_SLOT)
        q_ref[:, sl] = _head_norm_rope(q[:, sl], ga, gb, valid).astype(BF16)


def _mla_proj(x, nw, shift, scale, wqa, qan, wqb, wkv, kvn, ga, gb, ca, sb):
    tm = ROW_TILE
    row = lambda i: (i, 0)
    fixed = lambda i: (0, 0)
    cond = lambda i: (_cond_of_tile(i), 0, 0)
    qw = MLA_HEADS * HEAD_SLOT
    kvw = MLA_KV_RANK + HEAD_SLOT
    return pl.pallas_call(
        _mla_proj_kernel,
        grid=(N_TOK // tm,),
        in_specs=[
            pl.BlockSpec((tm, D_MODEL), row),
            pl.BlockSpec((1, D_MODEL), fixed),
            pl.BlockSpec((None, 1, D_MODEL), cond),
            pl.BlockSpec((None, 1, D_MODEL), cond),
            pl.BlockSpec((D_MODEL, MLA_Q_RANK), fixed),
            pl.BlockSpec((1, MLA_Q_RANK), fixed),
            pl.BlockSpec((MLA_Q_RANK, qw), fixed),
            pl.BlockSpec((D_MODEL, kvw), fixed),
            pl.BlockSpec((1, MLA_KV_RANK), fixed),
            pl.BlockSpec((1, HEAD_SLOT), fixed),
            pl.BlockSpec((1, HEAD_SLOT), fixed),
            pl.BlockSpec((tm, HEAD_SLOT), row),
            pl.BlockSpec((tm, HEAD_SLOT), row),
        ],
        out_specs=[
            pl.BlockSpec((tm, qw), row),
            pl.BlockSpec((tm, MLA_KV_RANK), row),
            pl.BlockSpec((tm, HEAD_SLOT), row),
        ],
        out_shape=[
            jax.ShapeDtypeStruct((N_TOK, qw), BF16),
            jax.ShapeDtypeStruct((N_TOK, MLA_KV_RANK), F32),
            jax.ShapeDtypeStruct((N_TOK, HEAD_SLOT), F32),
        ],
        compiler_params=_params("parallel"),
        name="mla_proj",
    )(x, nw, shift, scale, wqa, qan, wqb, wkv, kvn, ga, gb, ca, sb)


def _kv_expand_kernel(ckv_ref, kpe_ref, wuk_ref, wuv_ref, ga_ref, gb_ref, ca_ref, sb_ref, k_ref, v_ref):
    c = ckv_ref[...].astype(BF16)
    kn = _dot(c, wuk_ref[...])
    v_ref[...] = _dot(c, wuv_ref[...]).astype(BF16)
    kpe = kpe_ref[...]
    ga = ga_ref[...] * ca_ref[...]
    gb = gb_ref[...] * sb_ref[...]
    valid = lax.broadcasted_iota(jnp.int32, ga.shape, 1) < MLA_QK
    for hd in range(MLA_HEADS):
        sl = slice(hd * HEAD_SLOT, (hd + 1) * HEAD_SLOT)
        k_ref[:, sl] = _head_norm_rope(kn[:, sl] + kpe, ga, gb, valid).astype(BF16)


def _kv_expand(ckv, kpe, wuk, wuv, ga, gb, ca, sb):
    n = ckv.shape[0]
    tm = ROW_TILE
    row = lambda i: (i, 0)
    fixed = lambda i: (0, 0)
    kw = MLA_HEADS * HEAD_SLOT
    vw = MLA_HEADS * MLA_V
    return pl.pallas_call(
        _kv_expand_kernel,
        grid=(n // tm,),
        in_specs=[
            pl.BlockSpec((tm, MLA_KV_RANK), row),
            pl.BlockSpec((tm, HEAD_SLOT), row),
            pl.BlockSpec((MLA_KV_RANK, kw), fixed),
            pl.BlockSpec((MLA_KV_RANK, vw), fixed),
            pl.BlockSpec((1, HEAD_SLOT), fixed),
            pl.BlockSpec((1, HEAD_SLOT), fixed),
            pl.BlockSpec((tm, HEAD_SLOT), row),
            pl.BlockSpec((tm, HEAD_SLOT), row),
        ],
        out_specs=[pl.BlockSpec((tm, kw), row), pl.BlockSpec((tm, vw), row)],
        out_shape=[jax.ShapeDtypeStruct((n, kw), BF16), jax.ShapeDtypeStruct((n, vw), BF16)],
        compiler_params=_params("parallel"),
        name="kv_expand",
    )(ckv, kpe, wuk, wuv, ga, gb, ca, sb)


def _attn_kernel(q_ref, k_ref, v_ref, o_ref, *, heads):
    tk = k_ref.shape[0]
    half = tk // 2 if tk % 512 == 0 else tk
    for hd in range(heads):
        qh = q_ref[:, hd * HEAD_SLOT:(hd + 1) * HEAD_SLOT]
        kh = k_ref[:, hd * HEAD_SLOT:(hd + 1) * HEAD_SLOT]
        s = lax.dot_general(qh, kh, (((1,), (1,)), ((), ())), preferred_element_type=F32)
        p = jnp.exp2(s - jnp.max(s, axis=-1, keepdims=True))
        inv = 1.0 / jnp.sum(p, axis=-1, keepdims=True)
        pb = p.astype(BF16)
        o = _dot(pb[:, :half], v_ref[:half, hd * MLA_V:(hd + 1) * MLA_V])
        if half < tk:
            o = o + _dot(pb[:, half:], v_ref[half:, hd * MLA_V:(hd + 1) * MLA_V])
        o_ref[:, hd * MLA_V:(hd + 1) * MLA_V] = (o * inv).astype(BF16)


def _attention(q, k, v, *, batch, q_off, tq_total, tq, heads):
    tk = k.shape[1]
    groups = MLA_HEADS // heads
    return pl.pallas_call(
        functools.partial(_attn_kernel, heads=heads),
        grid=(batch, groups, tq_total // tq),
        in_specs=[
            pl.BlockSpec((None, tq, heads * HEAD_SLOT), lambda b, g, i: (b + q_off, i, g)),
            pl.BlockSpec((None, tk, heads * HEAD_SLOT), lambda b, g, i: (b, 0, g)),
            pl.BlockSpec((None, tk, heads * MLA_V), lambda b, g, i: (b, 0, g)),
        ],
        out_specs=pl.BlockSpec((None, tq, heads * MLA_V), lambda b, g, i: (b, i, g)),
        out_shape=jax.ShapeDtypeStruct((batch, tq_total, MLA_HEADS * MLA_V), BF16),
        compiler_params=_params("parallel", "parallel", "parallel"),
        name="attention",
    )(q, k, v)


def _proj_residual_kernel(ac_ref, al_ref, w_ref, x_ref, g_ref, nw_ref, sh_ref, sc_ref, *rest, router):
    if router:
        rhi_ref, rlo_ref, x1_ref, h2_ref, lg_ref, y_sc = rest
    else:
        x1_ref, h2_ref, y_sc = rest
    i = pl.program_id(0)

    @pl.when(i < CTX_TILES)
    def _():
        y_sc[...] = _dot(ac_ref[...], w_ref[...])

    @pl.when(i >= CTX_TILES)
    def _():
        y_sc[...] = _dot(al_ref[...], w_ref[...])

    x1 = x_ref[...] + g_ref[...] * y_sc[...]
    h2 = _modulate(x1, nw_ref[...], sh_ref[...], sc_ref[...])
    if router:
        hi = h2.astype(BF16)
        lo = (h2 - hi.astype(F32)).astype(BF16)
        lg_ref[...] = _dot(hi, rhi_ref[...]) + (_dot(lo, rhi_ref[...]) + _dot(hi, rlo_ref[...]))
    x1_ref[...] = x1
    h2_ref[...] = h2.astype(h2_ref.dtype)


def _proj_residual(a_ctx, a_lat, w, x, gate, nw, shift, scale, router_w=None):
    tm = ROW_TILE
    kdim = w.shape[0]
    row = lambda i: (i, 0)
    fixed = lambda i: (0, 0)
    cond = lambda i: (_cond_of_tile(i), 0, 0)
    router = router_w is not None
    in_specs = [
        pl.BlockSpec((tm, kdim), lambda i: (jnp.minimum(i, CTX_TILES - 1), 0)),
        pl.BlockSpec((tm, kdim), lambda i: (jnp.maximum(i - CTX_TILES, 0), 0)),
        pl.BlockSpec((kdim, D_MODEL), fixed),
        pl.BlockSpec((tm, D_MODEL), row),
        pl.BlockSpec((None, 1, D_MODEL), cond),
        pl.BlockSpec((1, D_MODEL), fixed),
        pl.BlockSpec((None, 1, D_MODEL), cond),
        pl.BlockSpec((None, 1, D_MODEL), cond),
    ]
    out_specs = [pl.BlockSpec((tm, D_MODEL), row), pl.BlockSpec((tm, D_MODEL), row)]
    out_shape = [jax.ShapeDtypeStruct((N_TOK, D_MODEL), F32),
                 jax.ShapeDtypeStruct((N_TOK, D_MODEL), F32 if router else BF16)]
    args = [a_ctx, a_lat, w, x, gate, nw, shift, scale]
    if router:
        in_specs += [pl.BlockSpec((D_MODEL, 128), fixed), pl.BlockSpec((D_MODEL, 128), fixed)]
        out_specs.append(pl.BlockSpec((tm, 128), row))
        out_shape.append(jax.ShapeDtypeStruct((N_TOK, 128), F32))
        args += list(router_w)
    return pl.pallas_call(
        functools.partial(_proj_residual_kernel, router=router),
        grid=(N_TOK // tm,),
        in_specs=in_specs,
        out_specs=out_specs,
        out_shape=out_shape,
        scratch_shapes=[pltpu.VMEM((tm, D_MODEL), F32)],
        compiler_params=_params("arbitrary"),
        name="proj_residual",
    )(*args)


def _swiglu_partial(x, wg, wu, wd):
    g = _dot(x, wg)
    u = _dot(x, wu)
    return _dot((_silu(g) * u).astype(BF16), wd)


def _ffn_kernel(h_ref, x1_ref, g_ref, wg_ref, wu_ref, wd_ref, o_ref):
    part = _swiglu_partial(h_ref[...], wg_ref[...], wu_ref[...], wd_ref[...])
    o_ref[...] = x1_ref[...] + g_ref[...] * part


def _resident(shape):
    return pl.BlockSpec(shape, lambda *_: (0,) * len(shape), pipeline_mode=pl.Buffered(1))


def _ffn(h2, x1, gate, wg, wu, wd):
    tm = ROW_TILE
    row = lambda i: (i, 0)
    return pl.pallas_call(
        _ffn_kernel,
        grid=(N_TOK // tm,),
        in_specs=[
            pl.BlockSpec((tm, D_MODEL), row),
            pl.BlockSpec((tm, D_MODEL), row),
            pl.BlockSpec((None, 1, D_MODEL), lambda i: (_cond_of_tile(i), 0, 0)),
            _resident((D_MODEL, FFN_DIM)),
            _resident((D_MODEL, FFN_DIM)),
            _resident((FFN_DIM, D_MODEL)),
        ],
        out_specs=pl.BlockSpec((tm, D_MODEL), row),
        out_shape=jax.ShapeDtypeStruct((N_TOK, D_MODEL), F32),
        compiler_params=_params("parallel"),
        name="ffn",
    )(h2, x1, gate, wg, wu, wd)


def _moe_kernel(blk_e_ref, nact_ref, src_ref, x_hbm, wg_ref, wu_ref, wd_ref, ya_hbm,
                xrows0, xrows1, xb, acc0, acc1, gsem, ssem):
    b = pl.program_id(0)
    f = pl.program_id(1)
    nact = nact_ref[0]
    xrows = (xrows0, xrows1)
    acc = (acc0, acc1)

    def gather_copy(par, row, a):
        tok = a & (N_TOK - 1)
        return pltpu.make_async_copy(x_hbm.at[pl.ds(tok, 1), :], xrows[par].at[pl.ds(row, 1), :], gsem.at[par])

    def scatter_copy(par, row, a):
        return pltpu.make_async_copy(acc[par].at[pl.ds(row, 1), :], ya_hbm.at[pl.ds(a, 1), :], ssem.at[par])

    def wait_gather(par):
        pltpu.make_async_copy(x_hbm.at[pl.ds(0, MOE_ROWS), :], xrows[par], gsem.at[par]).wait()

    def wait_scatter(par):
        pltpu.make_async_copy(acc[par], ya_hbm.at[pl.ds(0, MOE_ROWS), :], ssem.at[par]).wait()

    @pl.when((b == 0) & (f == 0))
    def _():
        acc1[...] = jnp.zeros_like(acc1)

        def issue(r, c):
            gather_copy(0, r, src_ref[MOE_ROWS + r]).start()
            return c

        lax.fori_loop(0, MOE_ROWS, issue, 0, unroll=8)

    def block_body(par):
        @pl.when(f == 0)
        def _():
            wait_gather(par)

            @pl.when(b >= 1)
            def _():
                wait_scatter(par)

            xb[...] = xrows[par][...].astype(BF16)

        row0 = f * MOE_STEP_ROWS
        nxt = (b + 2) * MOE_ROWS + row0
        prv = b * MOE_ROWS + row0
        for r in range(MOE_STEP_ROWS):
            gather_copy(1 - par, row0 + r, src_ref[nxt + r]).start()
            scatter_copy(1 - par, row0 + r, src_ref[prv + r]).start()
        part = _swiglu_partial(xb[...], wg_ref[...], wu_ref[...], wd_ref[...])

        @pl.when(f == 0)
        def _():
            acc[par][...] = part

        @pl.when(f > 0)
        def _():
            acc[par][...] += part

    def drain(par):
        wait_scatter(par)

        def issue(r, c):
            scatter_copy(1 - par, r, src_ref[b * MOE_ROWS + r]).start()
            return c

        lax.fori_loop(0, MOE_ROWS, issue, 0, unroll=8)
        wait_gather(par)
        wait_scatter(1 - par)

    for par in range(2):
        @pl.when((b < nact) & (b % 2 == par))
        def _():
            block_body(par)

        @pl.when((b == nact) & (f == 0) & (b % 2 == par))
        def _():
            drain(par)


def _moe(blk_e, nact, src, h2, wg, wu, wd, layer):
    tf = MOE_TILE
    wmap = lambda b, f, e, n, s: (layer, e[b], 0, f)
    grid_spec = pltpu.PrefetchScalarGridSpec(
        num_scalar_prefetch=3,
        grid=(MOE_BLOCKS, MOE_STEPS),
        in_specs=[
            pl.BlockSpec(memory_space=pl.ANY),
            pl.BlockSpec((None, None, D_MODEL, tf), wmap),
            pl.BlockSpec((None, None, D_MODEL, tf), wmap),
            pl.BlockSpec((None, None, tf, D_MODEL), lambda b, f, e, n, s: (layer, e[b], f, 0)),
        ],
        out_specs=pl.BlockSpec(memory_space=pl.ANY),
        scratch_shapes=[
            pltpu.VMEM((MOE_ROWS, D_MODEL), F32),
            pltpu.VMEM((MOE_ROWS, D_MODEL), F32),
            pltpu.VMEM((MOE_ROWS, D_MODEL), BF16),
            pltpu.VMEM((MOE_ROWS, D_MODEL), F32),
            pltpu.VMEM((MOE_ROWS, D_MODEL), F32),
            pltpu.SemaphoreType.DMA((2,)),
            pltpu.SemaphoreType.DMA((2,)),
        ],
    )
    return pl.pallas_call(
        _moe_kernel,
        grid_spec=grid_spec,
        out_shape=jax.ShapeDtypeStruct((N_ASSIGN + MOE_ROWS, D_MODEL), F32),
        compiler_params=_params("arbitrary", "arbitrary"),
        name="moe",
    )(blk_e, nact, src, h2, wg, wu, wd)


def _combine_kernel(y0_ref, y1_ref, x1_ref, g_ref, w_ref, *o_refs):
    w = w_ref[...]
    y = w[:, 0:1] * y0_ref[...] + w[:, 1:2] * y1_ref[...]
    out = x1_ref[...] + g_ref[...] * y
    if len(o_refs) == 1:
        o_refs[0][...] = out
    else:
        i = pl.program_id(0)

        @pl.when(i < CTX_TILES)
        def _():
            o_refs[0][...] = out

        @pl.when(i >= CTX_TILES)
        def _():
            o_refs[1][...] = out


def _combine(ya, x1, gate, top_w, split):
    tm = ROW_TILE
    row = lambda i: (i, 0)
    if split:
        out_specs = [pl.BlockSpec((tm, D_MODEL), lambda i: (jnp.minimum(i, CTX_TILES - 1), 0)),
                     pl.BlockSpec((tm, D_MODEL), lambda i: (jnp.maximum(i - CTX_TILES, 0), 0))]
        out_shape = [jax.ShapeDtypeStruct((N_CTX, D_MODEL), F32), jax.ShapeDtypeStruct((N_LAT, D_MODEL), F32)]
    else:
        out_specs = pl.BlockSpec((tm, D_MODEL), row)
        out_shape = jax.ShapeDtypeStruct((N_TOK, D_MODEL), F32)
    return pl.pallas_call(
        _combine_kernel,
        grid=(N_TOK // tm,),
        in_specs=[
            pl.BlockSpec((tm, D_MODEL), row),
            pl.BlockSpec((tm, D_MODEL), lambda i: (i + N_TOK // tm, 0)),
            pl.BlockSpec((tm, D_MODEL), row),
            pl.BlockSpec((None, 1, D_MODEL), lambda i: (_cond_of_tile(i), 0, 0)),
            pl.BlockSpec((tm, TOP_K), row),
        ],
        out_specs=out_specs,
        out_shape=out_shape,
        compiler_params=_params("arbitrary"),
        name="moe_combine",
    )(ya, ya, x1, gate, top_w)


def _route(logits):
    top_v, top_i = lax.top_k(logits, TOP_K)
    top_w = jax.nn.softmax(top_v, axis=-1)
    e = top_i.reshape(N_ASSIGN)
    onehot = (e[:, None] == jnp.arange(N_EXPERTS, dtype=e.dtype)[None, :]).astype(jnp.int32)
    csum = jnp.cumsum(onehot, axis=0)
    rank = jnp.sum(csum * onehot, axis=1) - 1
    counts = csum[-1]
    padded = (counts + MOE_ROWS - 1) // MOE_ROWS * MOE_ROWS
    pad_end = jnp.cumsum(padded)
    pad_start = pad_end - padded
    pos = (pad_start[e] + rank).astype(jnp.int32)
    n_src = (MOE_BLOCKS + 2) * MOE_ROWS
    dump = N_ASSIGN + jnp.arange(n_src, dtype=jnp.int32) % MOE_ROWS
    a = jnp.arange(N_ASSIGN, dtype=jnp.int32)
    src = dump.at[pos + MOE_ROWS].set((a % TOP_K) * N_TOK + a // TOP_K)
    blk_start = jnp.arange(MOE_BLOCKS, dtype=jnp.int32) * MOE_ROWS
    blk_e = jnp.minimum(jnp.searchsorted(pad_end, blk_start, side="right"), N_EXPERTS - 1).astype(jnp.int32)
    nact = (pad_end[-1:] // MOE_ROWS).astype(jnp.int32)
    return top_w, src, blk_e, nact


RET_QK_W = RET_HEADS * RET_DK
RET_V_W = RET_HEADS * RET_DV
RET_COL_TILE = 1024


def _ret_proj_kernel(x_ref, nw_ref, sh_ref, sc_ref, w_ref, c_ref, s_ref, o_ref):
    def body(rope):
        h = _modulate(x_ref[...], nw_ref[...], sh_ref[...], sc_ref[...]).astype(BF16)
        for c0 in range(0, o_ref.shape[1], RET_COL_TILE):
            acc = _dot(h, w_ref[:, c0:c0 + RET_COL_TILE])
            if c0 >= 2 * RET_QK_W + RET_V_W:
                o_ref[:, c0:c0 + RET_COL_TILE] = _silu(acc).astype(BF16)
            elif c0 >= 2 * RET_QK_W:
                o_ref[:, c0:c0 + RET_COL_TILE] = acc.astype(BF16)
            else:
                x = acc * (RET_DK ** -0.5) if c0 >= RET_QK_W else acc
                if not rope:
                    o_ref[:, c0:c0 + RET_COL_TILE] = x.astype(BF16)
                    continue
                for g in range(RET_COL_TILE // 128):
                    tl = slice((g % 2) * 128, (g % 2 + 1) * 128)
                    xs = x[:, g * 128:(g + 1) * 128]
                    y = xs * c_ref[:, tl] + pltpu.roll(xs, RET_DK // 4, 1) * s_ref[:, tl]
                    o_ref[:, c0 + g * 128:c0 + (g + 1) * 128] = y.astype(BF16)

    i = pl.program_id(0)

    @pl.when(i < CTX_TILES)
    def _():
        body(False)

    @pl.when(i >= CTX_TILES)
    def _():
        body(True)


def _ret_proj(x, nw, shift, scale, w, rope_c, rope_s):
    tm = ROW_TILE
    width = w.shape[1]
    row = lambda i: (i, 0)
    cond = lambda i: (_cond_of_tile(i), 0, 0)
    tab = lambda i: (jnp.where(i < CTX_TILES, 0, (i - CTX_TILES) % TILES_PER_LAT_BATCH), 0)
    return pl.pallas_call(
        _ret_proj_kernel,
        grid=(N_TOK // tm,),
        in_specs=[
            pl.BlockSpec((tm, D_MODEL), row),
            pl.BlockSpec((1, D_MODEL), lambda i: (0, 0)),
            pl.BlockSpec((None, 1, D_MODEL), cond),
            pl.BlockSpec((None, 1, D_MODEL), cond),
            _resident((D_MODEL, width)),
            pl.BlockSpec((tm, RET_DK), tab),
            pl.BlockSpec((tm, RET_DK), tab),
        ],
        out_specs=pl.BlockSpec((tm, width), row),
        out_shape=jax.ShapeDtypeStruct((N_TOK, width), BF16),
        compiler_params=_params("parallel"),
        name="ret_proj",
    )(x, nw, shift, scale, w, rope_c, rope_s)


def _retention_kernel(lg_ref, q_ref, k_ref, v_ref, sg_ref, gn_ref, *rest, zero_init, n_chunks, layer):
    if zero_init:
        y_ref, st_ref, o_acc, state, dmask = rest[-5:]
    else:
        s0f_ref, s0b_ref, y_ref, o_acc, state, dmask = rest
    hd = pl.program_id(1)
    c = RET_BLOCK
    ri = lax.broadcasted_iota(jnp.int32, (c, c), 0).astype(F32)
    ci = lax.broadcasted_iota(jnp.int32, (c, c), 1).astype(F32)
    pos = lax.broadcasted_iota(jnp.int32, (c, 1), 0).astype(F32)
    q_decay, k_decay, c_decay = [], [], []
    for d in range(2):
        lg = lg_ref[d, hd]
        rel = (ri - ci) if d == 0 else (ci - ri)
        dmask[d] = jnp.where(rel >= 0, jnp.exp(jnp.maximum(rel, 0.0) * lg), 0.0)
        q_decay.append(jnp.exp(((pos + 1.0) if d == 0 else (c - pos)) * lg))
        k_decay.append(jnp.exp(((c - 1.0 - pos) if d == 0 else pos) * lg))
        c_decay.append(jnp.exp(jnp.full((1, 1), c, F32) * lg))
        if zero_init:
            state[d] = jnp.zeros(state.shape[1:], F32)
        else:
            state[d] = (s0f_ref if d == 0 else s0b_ref)[...]

    def step(t, carry):
        for d in range(2):
            ch = t if d == 0 else n_chunks - 1 - t
            rows = pl.ds(pl.multiple_of(ch * c, c), c)
            qc = q_ref[rows, :]
            kc = k_ref[rows, :]
            vc = v_ref[rows, :]
            a = lax.dot_general(qc, kc, (((1,), (1,)), ((), ())), preferred_element_type=F32) * dmask[d]
            s = state[d]
            o_acc[d, rows, :] = (_dot(a.astype(BF16), vc)
                                 + _dot((qc.astype(F32) * q_decay[d]).astype(BF16), s.astype(BF16)))
            kd = (kc.astype(F32) * k_decay[d]).astype(BF16)
            state[d] = s * c_decay[d] + lax.dot_general(kd, vc, (((0,), (0,)), ((), ())),
                                                         preferred_element_type=F32)
        return carry

    lax.fori_loop(0, n_chunks, step, 0)

    if zero_init:
        for d in range(2):
            if st_ref.ndim == 3:
                st_ref[d] = state[d]
            else:
                st_ref[layer, d] = state[d]
                for other in range(st_ref.shape[0]):
                    if other != layer:
                        st_ref[other, d] = jnp.zeros(state.shape[1:], F32)

    y = _rms(o_acc[0] + o_acc[1], gn_ref[...])
    y_ref[...] = (sg_ref[...].astype(F32) * y).astype(BF16)


def _retention(log_gamma, qkvg, gn, state0, *, batch, b_off, t, layer, states=None):
    zero_init = state0 is None
    qblk = RET_HEADS * RET_DK // RET_DK
    vblk = 2 * RET_HEADS * RET_DK // RET_DV
    gblk = vblk + RET_HEADS
    in_specs = [
        pl.BlockSpec(memory_space=pltpu.SMEM),
        pl.BlockSpec((None, t, RET_DK), lambda b, h: (b + b_off, 0, h)),
        pl.BlockSpec((None, t, RET_DK), lambda b, h: (b + b_off, 0, qblk + h)),
        pl.BlockSpec((None, t, RET_DV), lambda b, h: (b + b_off, 0, vblk + h)),
        pl.BlockSpec((None, t, RET_DV), lambda b, h: (b + b_off, 0, gblk + h)),
        pl.BlockSpec((None, 1, RET_DV), lambda b, h: (h, 0, 0)),
    ]
    args = [log_gamma, qkvg, qkvg, qkvg, qkvg, gn]
    y_spec = pl.BlockSpec((None, t, RET_DV), lambda b, h: (b, 0, h))
    y_shape = jax.ShapeDtypeStruct((batch, t, RET_HEADS * RET_DV), BF16)
    aliases = {}
    if zero_init:
        n_ret = DEPTH // 2
        st_shape = jax.ShapeDtypeStruct((batch, n_ret, 2, RET_HEADS, RET_DK, RET_DV), F32)
        if states is None:
            st_spec = pl.BlockSpec((None, n_ret, 2, None, RET_DK, RET_DV), lambda b, h: (b, 0, 0, h, 0, 0))
        else:
            st_spec = pl.BlockSpec((None, None, 2, None, RET_DK, RET_DV), lambda b, h: (b, layer, 0, h, 0, 0))
            aliases = {len(args): 1}
            in_specs.append(pl.BlockSpec(memory_space=pl.ANY))
            args.append(states)
        out_specs = [y_spec, st_spec]
        out_shape = [y_shape, st_shape]
    else:
        in_specs += [
            pl.BlockSpec((None, None, None, None, RET_DK, RET_DV), lambda b, h: (b, layer, 0, h, 0, 0)),
            pl.BlockSpec((None, None, None, None, RET_DK, RET_DV), lambda b, h: (b, layer, 1, h, 0, 0))]
        args += [state0, state0]
        out_specs = y_spec
        out_shape = y_shape
    return pl.pallas_call(
        functools.partial(_retention_kernel, zero_init=zero_init, n_chunks=t // RET_BLOCK, layer=layer),
        grid=(batch, RET_HEADS),
        in_specs=in_specs,
        out_specs=out_specs,
        out_shape=out_shape,
        input_output_aliases=aliases,
        scratch_shapes=[pltpu.VMEM((2, t, RET_DV), F32), pltpu.VMEM((2, RET_DK, RET_DV), F32),
                        pltpu.VMEM((2, RET_BLOCK, RET_BLOCK), F32)],
        compiler_params=_params("parallel", "parallel"),
        name="retention",
    )(*args)


def _rope_tables(rot_dim):
    rows = DEC_SEQ // GRID_W
    row = jnp.repeat(jnp.arange(rows), GRID_W)
    col = jnp.tile(jnp.arange(GRID_W), rows)
    nf = rot_dim // 4
    inv = ROPE_BASE ** (-jnp.arange(nf, dtype=F32) / nf)
    ang = jnp.stack([row, col], axis=-1).astype(F32)[:, :, None] * inv
    cos, sin = jnp.cos(ang), jnp.sin(ang)
    c = jnp.stack([cos, cos], axis=2).reshape(DEC_SEQ, rot_dim)
    s = jnp.stack([-sin, sin], axis=2).reshape(DEC_SEQ, rot_dim)
    return c, s


def _swap_perm():
    nf = MLA_ROPE // 4
    return jnp.arange(MLA_ROPE) ^ nf


def _head_slot_tables(c32, s32):
    t = c32.shape[0]
    ca = jnp.concatenate([jnp.ones((t, MLA_NOPE), F32), c32, jnp.zeros((t, MLA_ROPE), F32)], axis=1)
    sb = jnp.concatenate([jnp.zeros((t, MLA_NOPE), F32), s32, jnp.zeros((t, MLA_ROPE), F32)], axis=1)
    return ca, sb


def _head_slot_gains(g, scale):
    perm = _swap_perm()
    zeros = jnp.zeros((MLA_ROPE,), F32)
    ga = jnp.concatenate([g, zeros]) * scale
    gb = jnp.concatenate([jnp.zeros((MLA_NOPE,), F32), g[MLA_NOPE:][perm], zeros]) * scale
    return ga.reshape(1, HEAD_SLOT), gb.reshape(1, HEAD_SLOT)


def _rows3(m):
    return m.reshape(N_COND, 1, D_MODEL)


def kernel(x_prompt, x_sample, cache_ckv, cache_kpe, state_ret, c, c_ctx, mod_w, mod_b, norm1_w, norm2_w, mla_wq_a, mla_q_a_norm, mla_wq_b, mla_wkv_a, mla_kv_norm, mla_w_uk, mla_w_uv, mla_q_norm, mla_k_norm, mla_wo, ret_wq, ret_wk, ret_wv, ret_wg, ret_wo, ret_decay, ret_gn, ffn_w_gate, ffn_w_up, ffn_w_down, moe_router, moe_w_gate, moe_w_up, moe_w_down):
    x = jnp.concatenate([x_prompt.reshape(N_CTX, D_MODEL), x_sample.reshape(N_LAT, D_MODEL)], axis=0)
    cond = jnp.concatenate([c_ctx[None, :], c, jnp.zeros((N_COND - 1 - DEC_BATCH, D_MODEL), F32)], axis=0)
    mods = _adaln(cond, mod_w, mod_b).reshape(DEPTH, N_COND, 6, D_MODEL)

    perm = _swap_perm()
    c32, s32 = _rope_tables(MLA_ROPE)
    ca_lat, sb_lat = _head_slot_tables(c32, s32)
    ca_id, sb_id = _head_slot_tables(jnp.ones((1, MLA_ROPE), F32), jnp.zeros((1, MLA_ROPE), F32))
    ca_q = jnp.concatenate([jnp.broadcast_to(ca_id, (N_CTX, HEAD_SLOT)), jnp.tile(ca_lat, (DEC_BATCH, 1))], axis=0)
    sb_q = jnp.concatenate([jnp.broadcast_to(sb_id, (N_CTX, HEAD_SLOT)), jnp.tile(sb_lat, (DEC_BATCH, 1))], axis=0)
    lat_kv = DEC_SEQ + PAST_LEN
    ca_kl = jnp.tile(jnp.concatenate([ca_lat, jnp.broadcast_to(ca_id, (PAST_LEN, HEAD_SLOT))], axis=0), (DEC_BATCH, 1))
    sb_kl = jnp.tile(jnp.concatenate([sb_lat, jnp.broadcast_to(sb_id, (PAST_LEN, HEAD_SLOT))], axis=0), (DEC_BATCH, 1))
    ca_kc = jnp.broadcast_to(ca_id, (N_CTX, HEAD_SLOT))
    sb_kc = jnp.broadcast_to(sb_id, (N_CTX, HEAD_SLOT))
    rc256, rs256 = _rope_tables(RET_DK)

    moe_wg, moe_wu, moe_wd = moe_w_gate.astype(BF16), moe_w_up.astype(BF16), moe_w_down.astype(BF16)

    new_ckv, new_kpe, new_ret = [], [], None
    for i in range(DEPTH):
        j = i // 2
        m = mods[i]
        sh1, sc1, g1, sh2, sc2, g2 = (_rows3(m[:, t]) for t in range(6))
        nw1 = norm1_w[i].reshape(1, D_MODEL)
        nw2 = norm2_w[i].reshape(1, D_MODEL)
        if i % 2 == 0:
            wqb = mla_wq_b[j].reshape(MLA_Q_RANK, MLA_HEADS, MLA_QK)
            wqb = jnp.concatenate([wqb, wqb[:, :, MLA_NOPE:][:, :, perm]], axis=2)
            wqb = wqb.reshape(MLA_Q_RANK, MLA_HEADS * HEAD_SLOT).astype(BF16)
            wkv = mla_wkv_a[j]
            wkv = jnp.concatenate([wkv[:, :MLA_KV_RANK], jnp.zeros((D_MODEL, MLA_NOPE), F32),
                                   wkv[:, MLA_KV_RANK:], wkv[:, MLA_KV_RANK:][:, perm]], axis=1).astype(BF16)
            wuk = mla_w_uk[j].reshape(MLA_KV_RANK, MLA_HEADS, MLA_NOPE)
            wuk = jnp.concatenate([wuk, jnp.zeros_like(wuk)], axis=2)
            wuk = wuk.reshape(MLA_KV_RANK, MLA_HEADS * HEAD_SLOT).astype(BF16)
            gaq, gbq = _head_slot_gains(mla_q_norm[j], MLA_QK ** -0.5 * LOG2_E)
            gak, gbk = _head_slot_gains(mla_k_norm[j], 1.0)

            q, ckv, kpe = _mla_proj(x, nw1, sh1, sc1, mla_wq_a[j].astype(BF16),
                                    mla_q_a_norm[j].reshape(1, MLA_Q_RANK), wqb, wkv,
                                    mla_kv_norm[j].reshape(1, MLA_KV_RANK), gaq, gbq, ca_q, sb_q)
            new_ckv.append(ckv[:N_CTX].reshape(BATCH, SEQ, MLA_KV_RANK))
            new_kpe.append(kpe[:N_CTX, MLA_NOPE:MLA_QK].reshape(BATCH, SEQ, MLA_ROPE))

            kpe_x = cache_kpe[:, j]
            kpe_x = jnp.concatenate([jnp.zeros((DEC_BATCH, PAST_LEN, MLA_NOPE), F32), kpe_x, kpe_x[:, :, perm]], axis=2)
            ckv_l = jnp.concatenate([ckv[N_CTX:].reshape(DEC_BATCH, DEC_SEQ, MLA_KV_RANK), cache_ckv[:, j]], axis=1)
            kpe_l = jnp.concatenate([kpe[N_CTX:].reshape(DEC_BATCH, DEC_SEQ, HEAD_SLOT), kpe_x], axis=1)
            wuv = mla_w_uv[j].astype(BF16)
            k_c, v_c = _kv_expand(ckv[:N_CTX], kpe[:N_CTX], wuk, wuv, gak, gbk, ca_kc, sb_kc)
            k_l, v_l = _kv_expand(ckv_l.reshape(DEC_BATCH * lat_kv, MLA_KV_RANK),
                                  kpe_l.reshape(DEC_BATCH * lat_kv, HEAD_SLOT), wuk, wuv, gak, gbk, ca_kl, sb_kl)

            qw = MLA_HEADS * HEAD_SLOT
            o_c = _attention(q.reshape(N_TOK // SEQ, SEQ, qw), k_c.reshape(BATCH, SEQ, qw),
                             v_c.reshape(BATCH, SEQ, MLA_HEADS * MLA_V),
                             batch=BATCH, q_off=0, tq_total=SEQ, tq=SEQ, heads=MLA_HEADS)
            o_l = _attention(q.reshape(N_TOK // DEC_SEQ, DEC_SEQ, qw), k_l.reshape(DEC_BATCH, lat_kv, qw),
                             v_l.reshape(DEC_BATCH, lat_kv, MLA_HEADS * MLA_V),
                             batch=DEC_BATCH, q_off=N_CTX // DEC_SEQ, tq_total=DEC_SEQ, tq=256, heads=2)
            mix_c, mix_l = o_c.reshape(N_CTX, D_MODEL), o_l.reshape(N_LAT, D_MODEL)
            w_out = mla_wo[j].astype(BF16)
        else:
            w_in = jnp.concatenate([ret_wq[j], ret_wk[j], ret_wv[j], ret_wg[j]], axis=1).astype(BF16)
            qkvg = _ret_proj(x, nw1, sh1, sc1, w_in, rc256, rs256)
            log_gamma = -jnp.exp(ret_decay[j].astype(F32))
            gn = ret_gn[j].reshape(RET_HEADS, 1, RET_DV)
            width = qkvg.shape[1]
            y_c, new_ret = _retention(log_gamma, qkvg.reshape(N_TOK // SEQ, SEQ, width), gn, None,
                                      batch=BATCH, b_off=0, t=SEQ, layer=j, states=new_ret)
            y_l = _retention(log_gamma, qkvg.reshape(N_TOK // DEC_SEQ, DEC_SEQ, width), gn, state_ret,
                             batch=DEC_BATCH, b_off=N_CTX // DEC_SEQ, t=DEC_SEQ, layer=j)
            mix_c, mix_l = y_c.reshape(N_CTX, RET_HEADS * RET_DV), y_l.reshape(N_LAT, RET_HEADS * RET_DV)
            w_out = ret_wo[j].astype(BF16)

        if i % 2 == 0:
            x1, h2 = _proj_residual(mix_c, mix_l, w_out, x, g1, nw2, sh2, sc2)
            x = _ffn(h2, x1, g2, ffn_w_gate[j].astype(BF16), ffn_w_up[j].astype(BF16), ffn_w_down[j].astype(BF16))
        else:
            wr = jnp.pad(moe_router[j], ((0, 0), (0, 128 - N_EXPERTS)))
            wr_hi = wr.astype(BF16)
            wr_lo = (wr - wr_hi.astype(F32)).astype(BF16)
            x1, h2, logits = _proj_residual(mix_c, mix_l, w_out, x, g1, nw2, sh2, sc2, router_w=(wr_hi, wr_lo))
            top_w, src, blk_e, nact = _route(logits[:, :N_EXPERTS])
            ya = _moe(blk_e, nact, src, h2, moe_wg, moe_wu, moe_wd, j)
            x = _combine(ya, x1, g2, top_w, split=(i == DEPTH - 1))

    y_prompt = x[0].reshape(BATCH, SEQ, D_MODEL)
    y_sample = x[1].reshape(DEC_BATCH, DEC_SEQ, D_MODEL)
    return (y_prompt, y_sample, jnp.stack(new_ckv, axis=1), jnp.stack(new_kpe, axis=1), new_ret)
```

```python
import functools

import jax
import jax.numpy as jnp
from jax import lax
from jax.experimental import pallas as pl
from jax.experimental.pallas import tpu as pltpu

F32 = jnp.float32
BF16 = jnp.bfloat16

D_MODEL = 1024
BATCH = 32
SEQ = 256
DEPTH = 4
DEC_BATCH = 4
DEC_SEQ = 2048
PAST_LEN = 512
GRID_W = 64
ROPE_BASE = 10000.0
NORM_EPS = 1e-6

MLA_HEADS = 16
MLA_NOPE = 64
MLA_ROPE = 32
MLA_QK = MLA_NOPE + MLA_ROPE
MLA_V = 64
MLA_Q_RANK = 384
MLA_KV_RANK = 256
HEAD_SLOT = 128

RET_HEADS = 4
RET_DK = 256
RET_DV = 512
RET_BLOCK = 256

FFN_DIM = 2816
N_EXPERTS = 8
TOP_K = 2
EXPERT_DIM = 3584

N_CTX = BATCH * SEQ
N_LAT = DEC_BATCH * DEC_SEQ
N_TOK = N_CTX + N_LAT
N_COND = 8

ROW_TILE = 512
CTX_TILES = N_CTX // ROW_TILE
TILES_PER_LAT_BATCH = DEC_SEQ // ROW_TILE

MOE_ROWS = 512
MOE_TILE = 1792
MOE_STEPS = EXPERT_DIM // MOE_TILE
MOE_STEP_ROWS = MOE_ROWS // MOE_STEPS
assert MOE_STEPS >= 2
N_ASSIGN = N_TOK * TOP_K
MOE_BLOCKS = N_ASSIGN // MOE_ROWS + N_EXPERTS

VMEM_LIMIT = 56 * 1024 * 1024
LOG2_E = 1.4426950408889634


def _params(*sem):
    return pltpu.CompilerParams(dimension_semantics=sem, vmem_limit_bytes=VMEM_LIMIT)


def _cond_of_tile(i):
    return jnp.where(i < CTX_TILES, 0, 1 + (i - CTX_TILES) // TILES_PER_LAT_BATCH)


def _dot(a, b):
    return jnp.dot(a, b, preferred_element_type=F32)


def _silu(x):
    return x * (1.0 / (1.0 + jnp.exp(-x)))


def _rms(x, w):
    return x * lax.rsqrt(jnp.mean(x * x, axis=-1, keepdims=True) + NORM_EPS) * w


def _modulate(x, nw, shift, scale):
    return _rms(x, nw) * (1.0 + scale) + shift


def _adaln_kernel(c_ref, w_ref, b_ref, o_ref):
    a = _silu(c_ref[...]).astype(BF16)
    o_ref[...] = _dot(a, w_ref[...].astype(BF16)) + b_ref[...]


def _adaln(cond, mod_w, mod_b):
    tn = 1024
    return pl.pallas_call(
        _adaln_kernel,
        grid=(DEPTH, 6 * D_MODEL // tn),
        in_specs=[
            pl.BlockSpec((N_COND, D_MODEL), lambda l, j: (0, 0)),
            pl.BlockSpec((None, D_MODEL, tn), lambda l, j: (l, 0, j)),
            pl.BlockSpec((None, 1, tn), lambda l, j: (l, 0, j)),
        ],
        out_specs=pl.BlockSpec((None, N_COND, tn), lambda l, j: (l, 0, j)),
        out_shape=jax.ShapeDtypeStruct((DEPTH, N_COND, 6 * D_MODEL), F32),
        compiler_params=_params("parallel", "parallel"),
        name="adaln",
    )(cond, mod_w, mod_b.reshape(DEPTH, 1, 6 * D_MODEL))


def _head_norm_rope(xh, ga, gb, valid):
    ss = jnp.sum(jnp.where(valid, xh * xh, 0.0), axis=-1, keepdims=True) * (1.0 / MLA_QK)
    r = lax.rsqrt(ss + NORM_EPS)
    return r * (xh * ga + pltpu.roll(xh, HEAD_SLOT - MLA_ROPE, 1) * gb)


def _mla_proj_kernel(x_ref, nw_ref, sh_ref, sc_ref, wqa_ref, qan_ref, wqb_ref, wkv_ref, kvn_ref,
                     ga_ref, gb_ref, ca_ref, sb_ref, q_ref, ckv_ref, kpe_ref):
    h = _modulate(x_ref[...], nw_ref[...], sh_ref[...], sc_ref[...]).astype(BF16)
    qa = _rms(_dot(h, wqa_ref[...]), qan_ref[...]).astype(BF16)
    q = _dot(qa, wqb_ref[...])
    kv = _dot(h, wkv_ref[...])
    ckv_ref[...] = _rms(kv[:, :MLA_KV_RANK], kvn_ref[...])
    kpe_ref[...] = kv[:, MLA_KV_RANK:]
    ga = ga_ref[...] * ca_ref[...]
    gb = gb_ref[...] * sb_ref[...]
    valid = lax.broadcasted_iota(jnp.int32, ga.shape, 1) < MLA_QK
    for hd in range(MLA_HEADS):
        sl = slice(hd * HEAD_SLOT, (hd + 1) * HEAD_SLOT)
        q_ref[:, sl] = _head_norm_rope(q[:, sl], ga, gb, valid).astype(BF16)


def _mla_proj(x, nw, shift, scale, wqa, qan, wqb, wkv, kvn, ga, gb, ca, sb):
    tm = ROW_TILE
    row = lambda i: (i, 0)
    fixed = lambda i: (0, 0)
    cond = lambda i: (_cond_of_tile(i), 0, 0)
    qw = MLA_HEADS * HEAD_SLOT
    kvw = MLA_KV_RANK + HEAD_SLOT
    return pl.pallas_call(
        _mla_proj_kernel,
        grid=(N_TOK // tm,),
        in_specs=[
            pl.BlockSpec((tm, D_MODEL), row),
            pl.BlockSpec((1, D_MODEL), fixed),
            pl.BlockSpec((None, 1, D_MODEL), cond),
            pl.BlockSpec((None, 1, D_MODEL), cond),
            pl.BlockSpec((D_MODEL, MLA_Q_RANK), fixed),
            pl.BlockSpec((1, MLA_Q_RANK), fixed),
            pl.BlockSpec((MLA_Q_RANK, qw), fixed),
            pl.BlockSpec((D_MODEL, kvw), fixed),
            pl.BlockSpec((1, MLA_KV_RANK), fixed),
            pl.BlockSpec((1, HEAD_SLOT), fixed),
            pl.BlockSpec((1, HEAD_SLOT), fixed),
            pl.BlockSpec((tm, HEAD_SLOT), row),
            pl.BlockSpec((tm, HEAD_SLOT), row),
        ],
        out_specs=[
            pl.BlockSpec((tm, qw), row),
            pl.BlockSpec((tm, MLA_KV_RANK), row),
            pl.BlockSpec((tm, HEAD_SLOT), row),
        ],
        out_shape=[
            jax.ShapeDtypeStruct((N_TOK, qw), BF16),
            jax.ShapeDtypeStruct((N_TOK, MLA_KV_RANK), F32),
            jax.ShapeDtypeStruct((N_TOK, HEAD_SLOT), F32),
        ],
        compiler_params=_params("parallel"),
        name="mla_proj",
    )(x, nw, shift, scale, wqa, qan, wqb, wkv, kvn, ga, gb, ca, sb)


def _kv_expand_kernel(ckv_ref, kpe_ref, wuk_ref, wuv_ref, ga_ref, gb_ref, ca_ref, sb_ref, k_ref, v_ref):
    c = ckv_ref[...].astype(BF16)
    kn = _dot(c, wuk_ref[...])
    v_ref[...] = _dot(c, wuv_ref[...]).astype(BF16)
    kpe = kpe_ref[...]
    ga = ga_ref[...] * ca_ref[...]
    gb = gb_ref[...] * sb_ref[...]
    valid = lax.broadcasted_iota(jnp.int32, ga.shape, 1) < MLA_QK
    for hd in range(MLA_HEADS):
        sl = slice(hd * HEAD_SLOT, (hd + 1) * HEAD_SLOT)
        k_ref[:, sl] = _head_norm_rope(kn[:, sl] + kpe, ga, gb, valid).astype(BF16)


def _kv_expand(ckv, kpe, wuk, wuv, ga, gb, ca, sb):
    n = ckv.shape[0]
    tm = ROW_TILE
    row = lambda i: (i, 0)
    fixed = lambda i: (0, 0)
    kw = MLA_HEADS * HEAD_SLOT
    vw = MLA_HEADS * MLA_V
    return pl.pallas_call(
        _kv_expand_kernel,
        grid=(n // tm,),
        in_specs=[
            pl.BlockSpec((tm, MLA_KV_RANK), row),
            pl.BlockSpec((tm, HEAD_SLOT), row),
            pl.BlockSpec((MLA_KV_RANK, kw), fixed),
            pl.BlockSpec((MLA_KV_RANK, vw), fixed),
            pl.BlockSpec((1, HEAD_SLOT), fixed),
            pl.BlockSpec((1, HEAD_SLOT), fixed),
            pl.BlockSpec((tm, HEAD_SLOT), row),
            pl.BlockSpec((tm, HEAD_SLOT), row),
        ],
        out_specs=[pl.BlockSpec((tm, kw), row), pl.BlockSpec((tm, vw), row)],
        out_shape=[jax.ShapeDtypeStruct((n, kw), BF16), jax.ShapeDtypeStruct((n, vw), BF16)],
        compiler_params=_params("parallel"),
        name="kv_expand",
    )(ckv, kpe, wuk, wuv, ga, gb, ca, sb)


def _attn_kernel(q_ref, k_ref, v_ref, o_ref, *, heads):
    tk = k_ref.shape[0]
    half = tk // 2 if tk % 512 == 0 else tk
    for hd in range(heads):
        qh = q_ref[:, hd * HEAD_SLOT:(hd + 1) * HEAD_SLOT]
        kh = k_ref[:, hd * HEAD_SLOT:(hd + 1) * HEAD_SLOT]
        s = lax.dot_general(qh, kh, (((1,), (1,)), ((), ())), preferred_element_type=F32)
        p = jnp.exp2(s - jnp.max(s, axis=-1, keepdims=True))
        inv = 1.0 / jnp.sum(p, axis=-1, keepdims=True)
        pb = p.astype(BF16)
        o = _dot(pb[:, :half], v_ref[:half, hd * MLA_V:(hd + 1) * MLA_V])
        if half < tk:
            o = o + _dot(pb[:, half:], v_ref[half:, hd * MLA_V:(hd + 1) * MLA_V])
        o_ref[:, hd * MLA_V:(hd + 1) * MLA_V] = (o * inv).astype(BF16)


def _attention(q, k, v, *, batch, q_off, tq_total, tq, heads):
    tk = k.shape[1]
    groups = MLA_HEADS // heads
    return pl.pallas_call(
        functools.partial(_attn_kernel, heads=heads),
        grid=(batch, groups, tq_total // tq),
        in_specs=[
            pl.BlockSpec((None, tq, heads * HEAD_SLOT), lambda b, g, i: (b + q_off, i, g)),
            pl.BlockSpec((None, tk, heads * HEAD_SLOT), lambda b, g, i: (b, 0, g)),
            pl.BlockSpec((None, tk, heads * MLA_V), lambda b, g, i: (b, 0, g)),
        ],
        out_specs=pl.BlockSpec((None, tq, heads * MLA_V), lambda b, g, i: (b, i, g)),
        out_shape=jax.ShapeDtypeStruct((batch, tq_total, MLA_HEADS * MLA_V), BF16),
        compiler_params=_params("parallel", "parallel", "parallel"),
        name="attention",
    )(q, k, v)


LANE_CHUNKS = D_MODEL // 128


def _store_row_major(ref, x):
    rows = x.shape[0]
    for c in range(LANE_CHUNKS):
        ref[pl.ds(c, rows, stride=LANE_CHUNKS), :] = x[:, c * 128:(c + 1) * 128]


def _load_row_major(ref, rows, c):
    return ref[pl.ds(c, rows, stride=LANE_CHUNKS), :]


def _proj_residual_kernel(ac_ref, al_ref, w_ref, x_ref, g_ref, nw_ref, sh_ref, sc_ref, *rest, router):
    if router:
        rhi_ref, rlo_ref, x1_ref, h2_ref, lg_ref, y_sc = rest
    else:
        x1_ref, h2_ref, y_sc = rest
    i = pl.program_id(0)

    @pl.when(i < CTX_TILES)
    def _():
        y_sc[...] = _dot(ac_ref[...], w_ref[...])

    @pl.when(i >= CTX_TILES)
    def _():
        y_sc[...] = _dot(al_ref[...], w_ref[...])

    x1 = x_ref[...] + g_ref[...] * y_sc[...]
    h2 = _modulate(x1, nw_ref[...], sh_ref[...], sc_ref[...])
    if router:
        hi = h2.astype(BF16)
        lo = (h2 - hi.astype(F32)).astype(BF16)
        lg_ref[...] = _dot(hi, rhi_ref[...]) + (_dot(lo, rhi_ref[...]) + _dot(hi, rlo_ref[...]))
        _store_row_major(h2_ref, h2)
    else:
        h2_ref[...] = h2.astype(h2_ref.dtype)
    x1_ref[...] = x1


def _proj_residual(a_ctx, a_lat, w, x, gate, nw, shift, scale, router_w=None):
    tm = ROW_TILE
    kdim = w.shape[0]
    row = lambda i: (i, 0)
    fixed = lambda i: (0, 0)
    cond = lambda i: (_cond_of_tile(i), 0, 0)
    router = router_w is not None
    in_specs = [
        pl.BlockSpec((tm, kdim), lambda i: (jnp.minimum(i, CTX_TILES - 1), 0)),
        pl.BlockSpec((tm, kdim), lambda i: (jnp.maximum(i - CTX_TILES, 0), 0)),
        pl.BlockSpec((kdim, D_MODEL), fixed),
        pl.BlockSpec((tm, D_MODEL), row),
        pl.BlockSpec((None, 1, D_MODEL), cond),
        pl.BlockSpec((1, D_MODEL), fixed),
        pl.BlockSpec((None, 1, D_MODEL), cond),
        pl.BlockSpec((None, 1, D_MODEL), cond),
    ]
    if router:
        h2_spec = pl.BlockSpec((tm * LANE_CHUNKS, 128), row)
        h2_shape = jax.ShapeDtypeStruct((N_TOK * LANE_CHUNKS, 128), F32)
    else:
        h2_spec = pl.BlockSpec((tm, D_MODEL), row)
        h2_shape = jax.ShapeDtypeStruct((N_TOK, D_MODEL), BF16)
    out_specs = [pl.BlockSpec((tm, D_MODEL), row), h2_spec]
    out_shape = [jax.ShapeDtypeStruct((N_TOK, D_MODEL), F32), h2_shape]
    args = [a_ctx, a_lat, w, x, gate, nw, shift, scale]
    if router:
        in_specs += [pl.BlockSpec((D_MODEL, 128), fixed), pl.BlockSpec((D_MODEL, 128), fixed)]
        out_specs.append(pl.BlockSpec((tm, 128), row))
        out_shape.append(jax.ShapeDtypeStruct((N_TOK, 128), F32))
        args += list(router_w)
    return pl.pallas_call(
        functools.partial(_proj_residual_kernel, router=router),
        grid=(N_TOK // tm,),
        in_specs=in_specs,
        out_specs=out_specs,
        out_shape=out_shape,
        scratch_shapes=[pltpu.VMEM((tm, D_MODEL), F32)],
        compiler_params=_params("arbitrary"),
        name="proj_residual",
    )(*args)


def _swiglu_partial(x, wg, wu, wd):
    g = _dot(x, wg)
    u = _dot(x, wu)
    return _dot((_silu(g) * u).astype(BF16), wd)


def _ffn_kernel(h_ref, x1_ref, g_ref, wg_ref, wu_ref, wd_ref, o_ref):
    part = _swiglu_partial(h_ref[...], wg_ref[...], wu_ref[...], wd_ref[...])
    o_ref[...] = x1_ref[...] + g_ref[...] * part


def _resident(shape):
    return pl.BlockSpec(shape, lambda *_: (0,) * len(shape), pipeline_mode=pl.Buffered(1))


def _ffn(h2, x1, gate, wg, wu, wd):
    tm = ROW_TILE
    row = lambda i: (i, 0)
    return pl.pallas_call(
        _ffn_kernel,
        grid=(N_TOK // tm,),
        in_specs=[
            pl.BlockSpec((tm, D_MODEL), row),
            pl.BlockSpec((tm, D_MODEL), row),
            pl.BlockSpec((None, 1, D_MODEL), lambda i: (_cond_of_tile(i), 0, 0)),
            _resident((D_MODEL, FFN_DIM)),
            _resident((D_MODEL, FFN_DIM)),
            _resident((FFN_DIM, D_MODEL)),
        ],
        out_specs=pl.BlockSpec((tm, D_MODEL), row),
        out_shape=jax.ShapeDtypeStruct((N_TOK, D_MODEL), F32),
        compiler_params=_params("parallel"),
        name="ffn",
    )(h2, x1, gate, wg, wu, wd)


def _moe_kernel(blk_e_ref, nact_ref, src_ref, x_hbm, wg_ref, wu_ref, wd_ref, ya_hbm,
                xrows0, xrows1, xb, acc, stage0, stage1, gsem, ssem):
    b = pl.program_id(0)
    f = pl.program_id(1)
    nact = nact_ref[0]
    xrows = (xrows0, xrows1)
    stage = (stage0, stage1)
    tile = lambda r: pl.ds(pl.multiple_of(r * LANE_CHUNKS, LANE_CHUNKS), LANE_CHUNKS)

    def gather_copy(par, step, r, a):
        tok = a & (N_TOK - 1)
        return pltpu.make_async_copy(x_hbm.at[tok], xrows[par].at[step, tile(r), :], gsem.at[par])

    def scatter_copy(par, step, r, a):
        return pltpu.make_async_copy(stage[par].at[step, tile(r), :], ya_hbm.at[a], ssem.at[par])

    def wait_gather(par):
        pltpu.make_async_copy(xrows[par], xrows[par], gsem.at[par]).wait()

    def wait_scatter(par):
        pltpu.make_async_copy(stage[par], stage[par], ssem.at[par]).wait()

    @pl.when((b == 0) & (f == 0))
    def _():
        stage1[...] = jnp.zeros_like(stage1)

        def issue(r, c):
            gather_copy(0, r // MOE_STEP_ROWS, r % MOE_STEP_ROWS, src_ref[MOE_ROWS + r]).start()
            return c

        lax.fori_loop(0, MOE_ROWS, issue, 0, unroll=8)

    def block_body(par):
        @pl.when(f == 0)
        def _():
            wait_gather(par)

            @pl.when(b >= 1)
            def _():
                wait_scatter(par)

            for step in range(MOE_STEPS):
                for c in range(LANE_CHUNKS):
                    xb[step * MOE_STEP_ROWS:(step + 1) * MOE_STEP_ROWS, c * 128:(c + 1) * 128] = (
                        _load_row_major(xrows[par].at[step], MOE_STEP_ROWS, c).astype(BF16))

        nxt = (b + 2) * MOE_ROWS + f * MOE_STEP_ROWS
        prv = b * MOE_ROWS + f * MOE_STEP_ROWS
        for r in range(MOE_STEP_ROWS):
            gather_copy(1 - par, f, r, src_ref[nxt + r]).start()
            scatter_copy(1 - par, f, r, src_ref[prv + r]).start()
        part = _swiglu_partial(xb[...], wg_ref[...], wu_ref[...], wd_ref[...])

        @pl.when(f == 0)
        def _():
            acc[...] = part

        @pl.when((f > 0) & (f < MOE_STEPS - 1))
        def _():
            acc[...] += part

        @pl.when(f == MOE_STEPS - 1)
        def _():
            total = acc[...] + part
            for step in range(MOE_STEPS):
                _store_row_major(stage[par].at[step], total[step * MOE_STEP_ROWS:(step + 1) * MOE_STEP_ROWS])

    def drain(par):
        wait_scatter(par)

        def issue(r, c):
            scatter_copy(1 - par, r // MOE_STEP_ROWS, r % MOE_STEP_ROWS, src_ref[b * MOE_ROWS + r]).start()
            return c

        lax.fori_loop(0, MOE_ROWS, issue, 0, unroll=8)
        wait_gather(par)
        wait_scatter(1 - par)

    for par in range(2):
        @pl.when((b < nact) & (b % 2 == par))
        def _():
            block_body(par)

        @pl.when((b == nact) & (f == 0) & (b % 2 == par))
        def _():
            drain(par)


def _moe(blk_e, nact, src, h2, wg, wu, wd, layer):
    tf = MOE_TILE
    wmap = lambda b, f, e, n, s: (layer, e[b], 0, f)
    rows_buf = pltpu.VMEM((MOE_STEPS, MOE_STEP_ROWS * LANE_CHUNKS, 128), F32)
    grid_spec = pltpu.PrefetchScalarGridSpec(
        num_scalar_prefetch=3,
        grid=(MOE_BLOCKS, MOE_STEPS),
        in_specs=[
            pl.BlockSpec(memory_space=pl.ANY),
            pl.BlockSpec((None, None, D_MODEL, tf), wmap),
            pl.BlockSpec((None, None, D_MODEL, tf), wmap),
            pl.BlockSpec((None, None, tf, D_MODEL), lambda b, f, e, n, s: (layer, e[b], f, 0)),
        ],
        out_specs=pl.BlockSpec(memory_space=pl.ANY),
        scratch_shapes=[
            rows_buf,
            rows_buf,
            pltpu.VMEM((MOE_ROWS, D_MODEL), BF16),
            pltpu.VMEM((MOE_ROWS, D_MODEL), F32),
            rows_buf,
            rows_buf,
            pltpu.SemaphoreType.DMA((2,)),
            pltpu.SemaphoreType.DMA((2,)),
        ],
    )
    return pl.pallas_call(
        _moe_kernel,
        grid_spec=grid_spec,
        out_shape=jax.ShapeDtypeStruct((N_ASSIGN + MOE_ROWS, LANE_CHUNKS, 128), F32),
        compiler_params=_params("arbitrary", "arbitrary"),
        name="moe",
    )(blk_e, nact, src, h2.reshape(N_TOK, LANE_CHUNKS, 128), wg, wu, wd)


def _combine_kernel(y0_ref, y1_ref, x1_ref, g_ref, w_ref, *o_refs):
    w = w_ref[...]
    rows = x1_ref.shape[0]

    def emit(o_ref):
        for c in range(LANE_CHUNKS):
            sl = slice(c * 128, (c + 1) * 128)
            y = w[:, 0:1] * _load_row_major(y0_ref, rows, c) + w[:, 1:2] * _load_row_major(y1_ref, rows, c)
            o_ref[:, sl] = x1_ref[:, sl] + g_ref[:, sl] * y

    if len(o_refs) == 1:
        emit(o_refs[0])
    else:
        i = pl.program_id(0)

        @pl.when(i < CTX_TILES)
        def _():
            emit(o_refs[0])

        @pl.when(i >= CTX_TILES)
        def _():
            emit(o_refs[1])


def _combine(ya, x1, gate, top_w, split):
    tm = ROW_TILE
    row = lambda i: (i, 0)
    ya2 = ya.reshape(ya.shape[0] * LANE_CHUNKS, 128)
    if split:
        out_specs = [pl.BlockSpec((tm, D_MODEL), lambda i: (jnp.minimum(i, CTX_TILES - 1), 0)),
                     pl.BlockSpec((tm, D_MODEL), lambda i: (jnp.maximum(i - CTX_TILES, 0), 0))]
        out_shape = [jax.ShapeDtypeStruct((N_CTX, D_MODEL), F32), jax.ShapeDtypeStruct((N_LAT, D_MODEL), F32)]
    else:
        out_specs = pl.BlockSpec((tm, D_MODEL), row)
        out_shape = jax.ShapeDtypeStruct((N_TOK, D_MODEL), F32)
    return pl.pallas_call(
        _combine_kernel,
        grid=(N_TOK // tm,),
        in_specs=[
            pl.BlockSpec((tm * LANE_CHUNKS, 128), row),
            pl.BlockSpec((tm * LANE_CHUNKS, 128), lambda i: (i + N_TOK // tm, 0)),
            pl.BlockSpec((tm, D_MODEL), row),
            pl.BlockSpec((None, 1, D_MODEL), lambda i: (_cond_of_tile(i), 0, 0)),
            pl.BlockSpec((tm, TOP_K), row),
        ],
        out_specs=out_specs,
        out_shape=out_shape,
        compiler_params=_params("arbitrary"),
        name="moe_combine",
    )(ya2, ya2, x1, gate, top_w)


def _route(logits):
    top_v, top_i = lax.top_k(logits, TOP_K)
    top_w = jax.nn.softmax(top_v, axis=-1)
    e = top_i.reshape(N_ASSIGN)
    onehot = (e[:, None] == jnp.arange(N_EXPERTS, dtype=e.dtype)[None, :]).astype(jnp.int32)
    csum = jnp.cumsum(onehot, axis=0)
    rank = jnp.sum(csum * onehot, axis=1) - 1
    counts = csum[-1]
    padded = (counts + MOE_ROWS - 1) // MOE_ROWS * MOE_ROWS
    pad_end = jnp.cumsum(padded)
    pad_start = pad_end - padded
    pos = (pad_start[e] + rank).astype(jnp.int32)
    n_src = (MOE_BLOCKS + 2) * MOE_ROWS
    dump = N_ASSIGN + jnp.arange(n_src, dtype=jnp.int32) % MOE_ROWS
    a = jnp.arange(N_ASSIGN, dtype=jnp.int32)
    src = dump.at[pos + MOE_ROWS].set((a % TOP_K) * N_TOK + a // TOP_K)
    blk_start = jnp.arange(MOE_BLOCKS, dtype=jnp.int32) * MOE_ROWS
    blk_e = jnp.minimum(jnp.searchsorted(pad_end, blk_start, side="right"), N_EXPERTS - 1).astype(jnp.int32)
    nact = (pad_end[-1:] // MOE_ROWS).astype(jnp.int32)
    return top_w, src, blk_e, nact


RET_QK_W = RET_HEADS * RET_DK
RET_V_W = RET_HEADS * RET_DV
RET_COL_TILE = 1024


def _ret_proj_kernel(x_ref, nw_ref, sh_ref, sc_ref, w_ref, c_ref, s_ref, o_ref):
    def body(rope):
        h = _modulate(x_ref[...], nw_ref[...], sh_ref[...], sc_ref[...]).astype(BF16)
        for c0 in range(0, o_ref.shape[1], RET_COL_TILE):
            acc = _dot(h, w_ref[:, c0:c0 + RET_COL_TILE])
            if c0 >= 2 * RET_QK_W + RET_V_W:
                o_ref[:, c0:c0 + RET_COL_TILE] = _silu(acc).astype(BF16)
            elif c0 >= 2 * RET_QK_W:
                o_ref[:, c0:c0 + RET_COL_TILE] = acc.astype(BF16)
            else:
                x = acc * (RET_DK ** -0.5) if c0 >= RET_QK_W else acc
                if not rope:
                    o_ref[:, c0:c0 + RET_COL_TILE] = x.astype(BF16)
                    continue
                for g in range(RET_COL_TILE // 128):
                    tl = slice((g % 2) * 128, (g % 2 + 1) * 128)
                    xs = x[:, g * 128:(g + 1) * 128]
                    y = xs * c_ref[:, tl] + pltpu.roll(xs, RET_DK // 4, 1) * s_ref[:, tl]
                    o_ref[:, c0 + g * 128:c0 + (g + 1) * 128] = y.astype(BF16)

    i = pl.program_id(0)

    @pl.when(i < CTX_TILES)
    def _():
        body(False)

    @pl.when(i >= CTX_TILES)
    def _():
        body(True)


def _ret_proj(x, nw, shift, scale, w, rope_c, rope_s):
    tm = ROW_TILE
    width = w.shape[1]
    row = lambda i: (i, 0)
    cond = lambda i: (_cond_of_tile(i), 0, 0)
    tab = lambda i: (jnp.where(i < CTX_TILES, 0, (i - CTX_TILES) % TILES_PER_LAT_BATCH), 0)
    return pl.pallas_call(
        _ret_proj_kernel,
        grid=(N_TOK // tm,),
        in_specs=[
            pl.BlockSpec((tm, D_MODEL), row),
            pl.BlockSpec((1, D_MODEL), lambda i: (0, 0)),
            pl.BlockSpec((None, 1, D_MODEL), cond),
            pl.BlockSpec((None, 1, D_MODEL), cond),
            _resident((D_MODEL, width)),
            pl.BlockSpec((tm, RET_DK), tab),
            pl.BlockSpec((tm, RET_DK), tab),
        ],
        out_specs=pl.BlockSpec((tm, width), row),
        out_shape=jax.ShapeDtypeStruct((N_TOK, width), BF16),
        compiler_params=_params("parallel"),
        name="ret_proj",
    )(x, nw, shift, scale, w, rope_c, rope_s)


def _retention_kernel(lg_ref, q_ref, k_ref, v_ref, sg_ref, gn_ref, *rest, zero_init, n_chunks, layer):
    if zero_init:
        y_ref, st_ref, o_acc, state, dmask = rest[-5:]
    else:
        s0f_ref, s0b_ref, y_ref, o_acc, state, dmask = rest
    hd = pl.program_id(1)
    c = RET_BLOCK
    ri = lax.broadcasted_iota(jnp.int32, (c, c), 0).astype(F32)
    ci = lax.broadcasted_iota(jnp.int32, (c, c), 1).astype(F32)
    pos = lax.broadcasted_iota(jnp.int32, (c, 1), 0).astype(F32)
    q_decay, k_decay, c_decay = [], [], []
    for d in range(2):
        lg = lg_ref[d, hd]
        rel = (ri - ci) if d == 0 else (ci - ri)
        dmask[d] = jnp.where(rel >= 0, jnp.exp(jnp.maximum(rel, 0.0) * lg), 0.0)
        q_decay.append(jnp.exp(((pos + 1.0) if d == 0 else (c - pos)) * lg))
        k_decay.append(jnp.exp(((c - 1.0 - pos) if d == 0 else pos) * lg))
        c_decay.append(jnp.exp(jnp.full((1, 1), c, F32) * lg))
        if zero_init:
            state[d] = jnp.zeros(state.shape[1:], F32)
        else:
            state[d] = (s0f_ref if d == 0 else s0b_ref)[...]

    def step(t, carry):
        for d in range(2):
            ch = t if d == 0 else n_chunks - 1 - t
            rows = pl.ds(pl.multiple_of(ch * c, c), c)
            qc = q_ref[rows, :]
            kc = k_ref[rows, :]
            vc = v_ref[rows, :]
            a = lax.dot_general(qc, kc, (((1,), (1,)), ((), ())), preferred_element_type=F32) * dmask[d]
            s = state[d]
            o_acc[d, rows, :] = (_dot(a.astype(BF16), vc)
                                 + _dot((qc.astype(F32) * q_decay[d]).astype(BF16), s.astype(BF16)))
            kd = (kc.astype(F32) * k_decay[d]).astype(BF16)
            state[d] = s * c_decay[d] + lax.dot_general(kd, vc, (((0,), (0,)), ((), ())),
                                                         preferred_element_type=F32)
        return carry

    lax.fori_loop(0, n_chunks, step, 0)

    if zero_init:
        for d in range(2):
            if st_ref.ndim == 3:
                st_ref[d] = state[d]
            else:
                st_ref[layer, d] = state[d]
                for other in range(st_ref.shape[0]):
                    if other != layer:
                        st_ref[other, d] = jnp.zeros(state.shape[1:], F32)

    y = _rms(o_acc[0] + o_acc[1], gn_ref[...])
    y_ref[...] = (sg_ref[...].astype(F32) * y).astype(BF16)


def _retention(log_gamma, qkvg, gn, state0, *, batch, b_off, t, layer, states=None):
    zero_init = state0 is None
    qblk = RET_HEADS * RET_DK // RET_DK
    vblk = 2 * RET_HEADS * RET_DK // RET_DV
    gblk = vblk + RET_HEADS
    in_specs = [
        pl.BlockSpec(memory_space=pltpu.SMEM),
        pl.BlockSpec((None, t, RET_DK), lambda b, h: (b + b_off, 0, h)),
        pl.BlockSpec((None, t, RET_DK), lambda b, h: (b + b_off, 0, qblk + h)),
        pl.BlockSpec((None, t, RET_DV), lambda b, h: (b + b_off, 0, vblk + h)),
        pl.BlockSpec((None, t, RET_DV), lambda b, h: (b + b_off, 0, gblk + h)),
        pl.BlockSpec((None, 1, RET_DV), lambda b, h: (h, 0, 0)),
    ]
    args = [log_gamma, qkvg, qkvg, qkvg, qkvg, gn]
    y_spec = pl.BlockSpec((None, t, RET_DV), lambda b, h: (b, 0, h))
    y_shape = jax.ShapeDtypeStruct((batch, t, RET_HEADS * RET_DV), BF16)
    aliases = {}
    if zero_init:
        n_ret = DEPTH // 2
        st_shape = jax.ShapeDtypeStruct((batch, n_ret, 2, RET_HEADS, RET_DK, RET_DV), F32)
        if states is None:
            st_spec = pl.BlockSpec((None, n_ret, 2, None, RET_DK, RET_DV), lambda b, h: (b, 0, 0, h, 0, 0))
        else:
            st_spec = pl.BlockSpec((None, None, 2, None, RET_DK, RET_DV), lambda b, h: (b, layer, 0, h, 0, 0))
            aliases = {len(args): 1}
            in_specs.append(pl.BlockSpec(memory_space=pl.ANY))
            args.append(states)
        out_specs = [y_spec, st_spec]
        out_shape = [y_shape, st_shape]
    else:
        in_specs += [
            pl.BlockSpec((None, None, None, None, RET_DK, RET_DV), lambda b, h: (b, layer, 0, h, 0, 0)),
            pl.BlockSpec((None, None, None, None, RET_DK, RET_DV), lambda b, h: (b, layer, 1, h, 0, 0))]
        args += [state0, state0]
        out_specs = y_spec
        out_shape = y_shape
    return pl.pallas_call(
        functools.partial(_retention_kernel, zero_init=zero_init, n_chunks=t // RET_BLOCK, layer=layer),
        grid=(batch, RET_HEADS),
        in_specs=in_specs,
        out_specs=out_specs,
        out_shape=out_shape,
        input_output_aliases=aliases,
        scratch_shapes=[pltpu.VMEM((2, t, RET_DV), F32), pltpu.VMEM((2, RET_DK, RET_DV), F32),
                        pltpu.VMEM((2, RET_BLOCK, RET_BLOCK), F32)],
        compiler_params=_params("parallel", "parallel"),
        name="retention",
    )(*args)


def _rope_tables(rot_dim):
    rows = DEC_SEQ // GRID_W
    row = jnp.repeat(jnp.arange(rows), GRID_W)
    col = jnp.tile(jnp.arange(GRID_W), rows)
    nf = rot_dim // 4
    inv = ROPE_BASE ** (-jnp.arange(nf, dtype=F32) / nf)
    ang = jnp.stack([row, col], axis=-1).astype(F32)[:, :, None] * inv
    cos, sin = jnp.cos(ang), jnp.sin(ang)
    c = jnp.stack([cos, cos], axis=2).reshape(DEC_SEQ, rot_dim)
    s = jnp.stack([-sin, sin], axis=2).reshape(DEC_SEQ, rot_dim)
    return c, s


def _swap_perm():
    nf = MLA_ROPE // 4
    return jnp.arange(MLA_ROPE) ^ nf


def _head_slot_tables(c32, s32):
    t = c32.shape[0]
    ca = jnp.concatenate([jnp.ones((t, MLA_NOPE), F32), c32, jnp.zeros((t, MLA_ROPE), F32)], axis=1)
    sb = jnp.concatenate([jnp.zeros((t, MLA_NOPE), F32), s32, jnp.zeros((t, MLA_ROPE), F32)], axis=1)
    return ca, sb


def _head_slot_gains(g, scale):
    perm = _swap_perm()
    zeros = jnp.zeros((MLA_ROPE,), F32)
    ga = jnp.concatenate([g, zeros]) * scale
    gb = jnp.concatenate([jnp.zeros((MLA_NOPE,), F32), g[MLA_NOPE:][perm], zeros]) * scale
    return ga.reshape(1, HEAD_SLOT), gb.reshape(1, HEAD_SLOT)


def _rows3(m):
    return m.reshape(N_COND, 1, D_MODEL)


def kernel(x_prompt, x_sample, cache_ckv, cache_kpe, state_ret, c, c_ctx, mod_w, mod_b, norm1_w, norm2_w, mla_wq_a, mla_q_a_norm, mla_wq_b, mla_wkv_a, mla_kv_norm, mla_w_uk, mla_w_uv, mla_q_norm, mla_k_norm, mla_wo, ret_wq, ret_wk, ret_wv, ret_wg, ret_wo, ret_decay, ret_gn, ffn_w_gate, ffn_w_up, ffn_w_down, moe_router, moe_w_gate, moe_w_up, moe_w_down):
    x = jnp.concatenate([x_prompt.reshape(N_CTX, D_MODEL), x_sample.reshape(N_LAT, D_MODEL)], axis=0)
    cond = jnp.concatenate([c_ctx[None, :], c, jnp.zeros((N_COND - 1 - DEC_BATCH, D_MODEL), F32)], axis=0)
    mods = _adaln(cond, mod_w, mod_b).reshape(DEPTH, N_COND, 6, D_MODEL)

    perm = _swap_perm()
    c32, s32 = _rope_tables(MLA_ROPE)
    ca_lat, sb_lat = _head_slot_tables(c32, s32)
    ca_id, sb_id = _head_slot_tables(jnp.ones((1, MLA_ROPE), F32), jnp.zeros((1, MLA_ROPE), F32))
    ca_q = jnp.concatenate([jnp.broadcast_to(ca_id, (N_CTX, HEAD_SLOT)), jnp.tile(ca_lat, (DEC_BATCH, 1))], axis=0)
    sb_q = jnp.concatenate([jnp.broadcast_to(sb_id, (N_CTX, HEAD_SLOT)), jnp.tile(sb_lat, (DEC_BATCH, 1))], axis=0)
    lat_kv = DEC_SEQ + PAST_LEN
    ca_kl = jnp.tile(jnp.concatenate([ca_lat, jnp.broadcast_to(ca_id, (PAST_LEN, HEAD_SLOT))], axis=0), (DEC_BATCH, 1))
    sb_kl = jnp.tile(jnp.concatenate([sb_lat, jnp.broadcast_to(sb_id, (PAST_LEN, HEAD_SLOT))], axis=0), (DEC_BATCH, 1))
    ca_kc = jnp.broadcast_to(ca_id, (N_CTX, HEAD_SLOT))
    sb_kc = jnp.broadcast_to(sb_id, (N_CTX, HEAD_SLOT))
    rc256, rs256 = _rope_tables(RET_DK)

    moe_wg, moe_wu, moe_wd = moe_w_gate.astype(BF16), moe_w_up.astype(BF16), moe_w_down.astype(BF16)

    new_ckv, new_kpe, new_ret = [], [], None
    for i in range(DEPTH):
        j = i // 2
        m = mods[i]
        sh1, sc1, g1, sh2, sc2, g2 = (_rows3(m[:, t]) for t in range(6))
        nw1 = norm1_w[i].reshape(1, D_MODEL)
        nw2 = norm2_w[i].reshape(1, D_MODEL)
        if i % 2 == 0:
            wqb = mla_wq_b[j].reshape(MLA_Q_RANK, MLA_HEADS, MLA_QK)
            wqb = jnp.concatenate([wqb, wqb[:, :, MLA_NOPE:][:, :, perm]], axis=2)
            wqb = wqb.reshape(MLA_Q_RANK, MLA_HEADS * HEAD_SLOT).astype(BF16)
            wkv = mla_wkv_a[j]
            wkv = jnp.concatenate([wkv[:, :MLA_KV_RANK], jnp.zeros((D_MODEL, MLA_NOPE), F32),
                                   wkv[:, MLA_KV_RANK:], wkv[:, MLA_KV_RANK:][:, perm]], axis=1).astype(BF16)
            wuk = mla_w_uk[j].reshape(MLA_KV_RANK, MLA_HEADS, MLA_NOPE)
            wuk = jnp.concatenate([wuk, jnp.zeros_like(wuk)], axis=2)
            wuk = wuk.reshape(MLA_KV_RANK, MLA_HEADS * HEAD_SLOT).astype(BF16)
            gaq, gbq = _head_slot_gains(mla_q_norm[j], MLA_QK ** -0.5 * LOG2_E)
            gak, gbk = _head_slot_gains(mla_k_norm[j], 1.0)

            q, ckv, kpe = _mla_proj(x, nw1, sh1, sc1, mla_wq_a[j].astype(BF16),
                                    mla_q_a_norm[j].reshape(1, MLA_Q_RANK), wqb, wkv,
                                    mla_kv_norm[j].reshape(1, MLA_KV_RANK), gaq, gbq, ca_q, sb_q)
            new_ckv.append(ckv[:N_CTX].reshape(BATCH, SEQ, MLA_KV_RANK))
            new_kpe.append(kpe[:N_CTX, MLA_NOPE:MLA_QK].reshape(BATCH, SEQ, MLA_ROPE))

            kpe_x = cache_kpe[:, j]
            kpe_x = jnp.concatenate([jnp.zeros((DEC_BATCH, PAST_LEN, MLA_NOPE), F32), kpe_x, kpe_x[:, :, perm]], axis=2)
            ckv_l = jnp.concatenate([ckv[N_CTX:].reshape(DEC_BATCH, DEC_SEQ, MLA_KV_RANK), cache_ckv[:, j]], axis=1)
            kpe_l = jnp.concatenate([kpe[N_CTX:].reshape(DEC_BATCH, DEC_SEQ, HEAD_SLOT), kpe_x], axis=1)
            wuv = mla_w_uv[j].astype(BF16)
            k_c, v_c = _kv_expand(ckv[:N_CTX], kpe[:N_CTX], wuk, wuv, gak, gbk, ca_kc, sb_kc)
            k_l, v_l = _kv_expand(ckv_l.reshape(DEC_BATCH * lat_kv, MLA_KV_RANK),
                                  kpe_l.reshape(DEC_BATCH * lat_kv, HEAD_SLOT), wuk, wuv, gak, gbk, ca_kl, sb_kl)

            qw = MLA_HEADS * HEAD_SLOT
            o_c = _attention(q.reshape(N_TOK // SEQ, SEQ, qw), k_c.reshape(BATCH, SEQ, qw),
                             v_c.reshape(BATCH, SEQ, MLA_HEADS * MLA_V),
                             batch=BATCH, q_off=0, tq_total=SEQ, tq=SEQ, heads=MLA_HEADS)
            o_l = _attention(q.reshape(N_TOK // DEC_SEQ, DEC_SEQ, qw), k_l.reshape(DEC_BATCH, lat_kv, qw),
                             v_l.reshape(DEC_BATCH, lat_kv, MLA_HEADS * MLA_V),
                             batch=DEC_BATCH, q_off=N_CTX // DEC_SEQ, tq_total=DEC_SEQ, tq=256, heads=2)
            mix_c, mix_l = o_c.reshape(N_CTX, D_MODEL), o_l.reshape(N_LAT, D_MODEL)
            w_out = mla_wo[j].astype(BF16)
        else:
            w_in = jnp.concatenate([ret_wq[j], ret_wk[j], ret_wv[j], ret_wg[j]], axis=1).astype(BF16)
            qkvg = _ret_proj(x, nw1, sh1, sc1, w_in, rc256, rs256)
            log_gamma = -jnp.exp(ret_decay[j].astype(F32))
            gn = ret_gn[j].reshape(RET_HEADS, 1, RET_DV)
            width = qkvg.shape[1]
            y_c, new_ret = _retention(log_gamma, qkvg.reshape(N_TOK // SEQ, SEQ, width), gn, None,
                                      batch=BATCH, b_off=0, t=SEQ, layer=j, states=new_ret)
            y_l = _retention(log_gamma, qkvg.reshape(N_TOK // DEC_SEQ, DEC_SEQ, width), gn, state_ret,
                             batch=DEC_BATCH, b_off=N_CTX // DEC_SEQ, t=DEC_SEQ, layer=j)
            mix_c, mix_l = y_c.reshape(N_CTX, RET_HEADS * RET_DV), y_l.reshape(N_LAT, RET_HEADS * RET_DV)
            w_out = ret_wo[j].astype(BF16)

        if i % 2 == 0:
            x1, h2 = _proj_residual(mix_c, mix_l, w_out, x, g1, nw2, sh2, sc2)
            x = _ffn(h2, x1, g2, ffn_w_gate[j].astype(BF16), ffn_w_up[j].astype(BF16), ffn_w_down[j].astype(BF16))
        else:
            wr = jnp.pad(moe_router[j], ((0, 0), (0, 128 - N_EXPERTS)))
            wr_hi = wr.astype(BF16)
            wr_lo = (wr - wr_hi.astype(F32)).astype(BF16)
            x1, h2, logits = _proj_residual(mix_c, mix_l, w_out, x, g1, nw2, sh2, sc2, router_w=(wr_hi, wr_lo))
            top_w, src, blk_e, nact = _route(logits[:, :N_EXPERTS])
            ya = _moe(blk_e, nact, src, h2, moe_wg, moe_wu, moe_wd, j)
            x = _combine(ya, x1, g2, top_w, split=(i == DEPTH - 1))

    y_prompt = x[0].reshape(BATCH, SEQ, D_MODEL)
    y_sample = x[1].reshape(DEC_BATCH, DEC_SEQ, D_MODEL)
    return (y_prompt, y_sample, jnp.stack(new_ckv, axis=1), jnp.stack(new_kpe, axis=1), new_ret)
```

```python
import functools

import jax
import jax.numpy as jnp
from jax import lax
from jax.experimental import pallas as pl
from jax.experimental.pallas import tpu as pltpu

F32 = jnp.float32
BF16 = jnp.bfloat16

D_MODEL = 1024
BATCH = 32
SEQ = 256
DEPTH = 4
DEC_BATCH = 4
DEC_SEQ = 2048
PAST_LEN = 512
GRID_W = 64
ROPE_BASE = 10000.0
NORM_EPS = 1e-6

MLA_HEADS = 16
MLA_NOPE = 64
MLA_ROPE = 32
MLA_QK = MLA_NOPE + MLA_ROPE
MLA_V = 64
MLA_Q_RANK = 384
MLA_KV_RANK = 256
HEAD_SLOT = 128

RET_HEADS = 4
RET_DK = 256
RET_DV = 512
RET_BLOCK = 256

FFN_DIM = 2816
N_EXPERTS = 8
TOP_K = 2
EXPERT_DIM = 3584

N_CTX = BATCH * SEQ
N_LAT = DEC_BATCH * DEC_SEQ
N_TOK = N_CTX + N_LAT
N_COND = 8

ROW_TILE = 512
CTX_TILES = N_CTX // ROW_TILE
TILES_PER_LAT_BATCH = DEC_SEQ // ROW_TILE

MOE_ROWS = 512
MOE_TILE = 1792
MOE_STEPS = EXPERT_DIM // MOE_TILE
MOE_STEP_ROWS = MOE_ROWS // MOE_STEPS
assert MOE_STEPS >= 2
N_ASSIGN = N_TOK * TOP_K
MOE_BLOCKS = N_ASSIGN // MOE_ROWS + N_EXPERTS

VMEM_LIMIT = 56 * 1024 * 1024
LOG2_E = 1.4426950408889634


def _params(*sem):
    return pltpu.CompilerParams(dimension_semantics=sem, vmem_limit_bytes=VMEM_LIMIT)


def _cond_of_tile(i):
    return jnp.where(i < CTX_TILES, 0, 1 + (i - CTX_TILES) // TILES_PER_LAT_BATCH)


def _dot(a, b):
    return jnp.dot(a, b, preferred_element_type=F32)


def _silu(x):
    return x * (1.0 / (1.0 + jnp.exp(-x)))


def _rms(x, w):
    return x * lax.rsqrt(jnp.mean(x * x, axis=-1, keepdims=True) + NORM_EPS) * w


def _modulate(x, nw, shift, scale):
    return _rms(x, nw) * (1.0 + scale) + shift


def _adaln_kernel(c_ref, w_ref, b_ref, o_ref):
    a = _silu(c_ref[...]).astype(BF16)
    o_ref[...] = _dot(a, w_ref[...].astype(BF16)) + b_ref[...]


def _adaln(cond, mod_w, mod_b):
    tn = 1024
    return pl.pallas_call(
        _adaln_kernel,
        grid=(DEPTH, 6 * D_MODEL // tn),
        in_specs=[
            pl.BlockSpec((N_COND, D_MODEL), lambda l, j: (0, 0)),
            pl.BlockSpec((None, D_MODEL, tn), lambda l, j: (l, 0, j)),
            pl.BlockSpec((None, 1, tn), lambda l, j: (l, 0, j)),
        ],
        out_specs=pl.BlockSpec((None, N_COND, tn), lambda l, j: (l, 0, j)),
        out_shape=jax.ShapeDtypeStruct((DEPTH, N_COND, 6 * D_MODEL), F32),
        compiler_params=_params("parallel", "parallel"),
        name="adaln",
    )(cond, mod_w, mod_b.reshape(DEPTH, 1, 6 * D_MODEL))


def _heads_norm_rope(xs, ga, gb):
    valid = lax.broadcasted_iota(jnp.int32, ga.shape, 1) < MLA_QK
    ss = [jnp.sum(jnp.where(valid, x * x, 0.0), axis=-1, keepdims=True) for x in xs]
    rs = [lax.rsqrt(s * (1.0 / MLA_QK) + NORM_EPS) for s in ss]
    return [r * (x * ga + pltpu.roll(x, HEAD_SLOT - MLA_ROPE, 1) * gb) for r, x in zip(rs, xs)]


def _mla_proj_kernel(x_ref, nw_ref, sh_ref, sc_ref, wqa_ref, qan_ref, wqb_ref, wkv_ref, kvn_ref,
                     ga_ref, gb_ref, ca_ref, sb_ref, q_ref, ckv_ref, kpe_ref):
    h = _modulate(x_ref[...], nw_ref[...], sh_ref[...], sc_ref[...]).astype(BF16)
    qa = _rms(_dot(h, wqa_ref[...]), qan_ref[...]).astype(BF16)
    q = _dot(qa, wqb_ref[...])
    kv = _dot(h, wkv_ref[...])
    ckv_ref[...] = _rms(kv[:, :MLA_KV_RANK], kvn_ref[...])
    kpe_ref[...] = kv[:, MLA_KV_RANK:]
    ga = ga_ref[...] * ca_ref[...]
    gb = gb_ref[...] * sb_ref[...]
    slots = [slice(hd * HEAD_SLOT, (hd + 1) * HEAD_SLOT) for hd in range(MLA_HEADS)]
    for sl, y in zip(slots, _heads_norm_rope([q[:, sl] for sl in slots], ga, gb)):
        q_ref[:, sl] = y.astype(BF16)


def _mla_proj(x, nw, shift, scale, wqa, qan, wqb, wkv, kvn, ga, gb, ca, sb):
    tm = ROW_TILE
    row = lambda i: (i, 0)
    fixed = lambda i: (0, 0)
    cond = lambda i: (_cond_of_tile(i), 0, 0)
    qw = MLA_HEADS * HEAD_SLOT
    kvw = MLA_KV_RANK + HEAD_SLOT
    return pl.pallas_call(
        _mla_proj_kernel,
        grid=(N_TOK // tm,),
        in_specs=[
            pl.BlockSpec((tm, D_MODEL), row),
            pl.BlockSpec((1, D_MODEL), fixed),
            pl.BlockSpec((None, 1, D_MODEL), cond),
            pl.BlockSpec((None, 1, D_MODEL), cond),
            pl.BlockSpec((D_MODEL, MLA_Q_RANK), fixed),
            pl.BlockSpec((1, MLA_Q_RANK), fixed),
            pl.BlockSpec((MLA_Q_RANK, qw), fixed),
            pl.BlockSpec((D_MODEL, kvw), fixed),
            pl.BlockSpec((1, MLA_KV_RANK), fixed),
            pl.BlockSpec((1, HEAD_SLOT), fixed),
            pl.BlockSpec((1, HEAD_SLOT), fixed),
            pl.BlockSpec((tm, HEAD_SLOT), row),
            pl.BlockSpec((tm, HEAD_SLOT), row),
        ],
        out_specs=[
            pl.BlockSpec((tm, qw), row),
            pl.BlockSpec((tm, MLA_KV_RANK), row),
            pl.BlockSpec((tm, HEAD_SLOT), row),
        ],
        out_shape=[
            jax.ShapeDtypeStruct((N_TOK, qw), BF16),
            jax.ShapeDtypeStruct((N_TOK, MLA_KV_RANK), F32),
            jax.ShapeDtypeStruct((N_TOK, HEAD_SLOT), F32),
        ],
        compiler_params=_params("parallel"),
        name="mla_proj",
    )(x, nw, shift, scale, wqa, qan, wqb, wkv, kvn, ga, gb, ca, sb)


def _kv_expand_kernel(ckv_ref, kpe_ref, wuk_ref, wuv_ref, ga_ref, gb_ref, ca_ref, sb_ref, k_ref, v_ref):
    c = ckv_ref[...].astype(BF16)
    kn = _dot(c, wuk_ref[...])
    v_ref[...] = _dot(c, wuv_ref[...]).astype(BF16)
    kpe = kpe_ref[...]
    ga = ga_ref[...] * ca_ref[...]
    gb = gb_ref[...] * sb_ref[...]
    slots = [slice(hd * HEAD_SLOT, (hd + 1) * HEAD_SLOT) for hd in range(MLA_HEADS)]
    for sl, y in zip(slots, _heads_norm_rope([kn[:, sl] + kpe for sl in slots], ga, gb)):
        k_ref[:, sl] = y.astype(BF16)


def _kv_expand(ckv, kpe, wuk, wuv, ga, gb, ca, sb):
    n = ckv.shape[0]
    tm = ROW_TILE
    row = lambda i: (i, 0)
    fixed = lambda i: (0, 0)
    kw = MLA_HEADS * HEAD_SLOT
    vw = MLA_HEADS * MLA_V
    return pl.pallas_call(
        _kv_expand_kernel,
        grid=(n // tm,),
        in_specs=[
            pl.BlockSpec((tm, MLA_KV_RANK), row),
            pl.BlockSpec((tm, HEAD_SLOT), row),
            pl.BlockSpec((MLA_KV_RANK, kw), fixed),
            pl.BlockSpec((MLA_KV_RANK, vw), fixed),
            pl.BlockSpec((1, HEAD_SLOT), fixed),
            pl.BlockSpec((1, HEAD_SLOT), fixed),
            pl.BlockSpec((tm, HEAD_SLOT), row),
            pl.BlockSpec((tm, HEAD_SLOT), row),
        ],
        out_specs=[pl.BlockSpec((tm, kw), row), pl.BlockSpec((tm, vw), row)],
        out_shape=[jax.ShapeDtypeStruct((n, kw), BF16), jax.ShapeDtypeStruct((n, vw), BF16)],
        compiler_params=_params("parallel"),
        name="kv_expand",
    )(ckv, kpe, wuk, wuv, ga, gb, ca, sb)


def _attn_kernel(q_ref, k_ref, v_ref, o_ref, *, heads):
    scores = []
    for hd in range(heads):
        qh = q_ref[:, hd * HEAD_SLOT:(hd + 1) * HEAD_SLOT]
        kh = k_ref[:, hd * HEAD_SLOT:(hd + 1) * HEAD_SLOT]
        scores.append(lax.dot_general(qh, kh, (((1,), (1,)), ((), ())), preferred_element_type=F32))
    probs, invs = [], []
    for s in scores:
        p = jnp.exp2(s - jnp.max(s, axis=-1, keepdims=True))
        invs.append(1.0 / jnp.sum(p, axis=-1, keepdims=True))
        probs.append(p.astype(BF16))
    for hd in range(heads):
        o = _dot(probs[hd], v_ref[:, hd * MLA_V:(hd + 1) * MLA_V])
        o_ref[:, hd * MLA_V:(hd + 1) * MLA_V] = (o * invs[hd]).astype(BF16)


def _attention(q, k, v, *, batch, q_off, tq_total, tq, heads):
    tk = k.shape[1]
    groups = MLA_HEADS // heads
    return pl.pallas_call(
        functools.partial(_attn_kernel, heads=heads),
        grid=(batch, groups, tq_total // tq),
        in_specs=[
            pl.BlockSpec((None, tq, heads * HEAD_SLOT), lambda b, g, i: (b + q_off, i, g)),
            pl.BlockSpec((None, tk, heads * HEAD_SLOT), lambda b, g, i: (b, 0, g)),
            pl.BlockSpec((None, tk, heads * MLA_V), lambda b, g, i: (b, 0, g)),
        ],
        out_specs=pl.BlockSpec((None, tq, heads * MLA_V), lambda b, g, i: (b, i, g)),
        out_shape=jax.ShapeDtypeStruct((batch, tq_total, MLA_HEADS * MLA_V), BF16),
        compiler_params=_params("parallel", "parallel", "parallel"),
        name="attention",
    )(q, k, v)


LANE_CHUNKS = D_MODEL // 128


def _store_row_major(ref, x):
    rows = x.shape[0]
    for c in range(LANE_CHUNKS):
        ref[pl.ds(c, rows, stride=LANE_CHUNKS), :] = x[:, c * 128:(c + 1) * 128]


def _load_row_major(ref, rows, c):
    return ref[pl.ds(c, rows, stride=LANE_CHUNKS), :]


def _proj_residual_kernel(ac_ref, al_ref, w_ref, x_ref, g_ref, nw_ref, sh_ref, sc_ref, *rest, router):
    if router:
        rhi_ref, rlo_ref, x1_ref, h2_ref, lg_ref, y_sc = rest
    else:
        x1_ref, h2_ref, y_sc = rest
    i = pl.program_id(0)

    @pl.when(i < CTX_TILES)
    def _():
        y_sc[...] = _dot(ac_ref[...], w_ref[...])

    @pl.when(i >= CTX_TILES)
    def _():
        y_sc[...] = _dot(al_ref[...], w_ref[...])

    x1 = x_ref[...] + g_ref[...] * y_sc[...]
    h2 = _modulate(x1, nw_ref[...], sh_ref[...], sc_ref[...])
    if router:
        hi = h2.astype(BF16)
        lo = (h2 - hi.astype(F32)).astype(BF16)
        lg_ref[...] = _dot(hi, rhi_ref[...]) + (_dot(lo, rhi_ref[...]) + _dot(hi, rlo_ref[...]))
        _store_row_major(h2_ref, h2)
    else:
        h2_ref[...] = h2.astype(h2_ref.dtype)
    x1_ref[...] = x1


def _proj_residual(a_ctx, a_lat, w, x, gate, nw, shift, scale, router_w=None):
    tm = ROW_TILE
    kdim = w.shape[0]
    row = lambda i: (i, 0)
    fixed = lambda i: (0, 0)
    cond = lambda i: (_cond_of_tile(i), 0, 0)
    router = router_w is not None
    in_specs = [
        pl.BlockSpec((tm, kdim), lambda i: (jnp.minimum(i, CTX_TILES - 1), 0)),
        pl.BlockSpec((tm, kdim), lambda i: (jnp.maximum(i - CTX_TILES, 0), 0)),
        pl.BlockSpec((kdim, D_MODEL), fixed),
        pl.BlockSpec((tm, D_MODEL), row),
        pl.BlockSpec((None, 1, D_MODEL), cond),
        pl.BlockSpec((1, D_MODEL), fixed),
        pl.BlockSpec((None, 1, D_MODEL), cond),
        pl.BlockSpec((None, 1, D_MODEL), cond),
    ]
    if router:
        h2_spec = pl.BlockSpec((tm * LANE_CHUNKS, 128), row)
        h2_shape = jax.ShapeDtypeStruct((N_TOK * LANE_CHUNKS, 128), F32)
    else:
        h2_spec = pl.BlockSpec((tm, D_MODEL), row)
        h2_shape = jax.ShapeDtypeStruct((N_TOK, D_MODEL), BF16)
    out_specs = [pl.BlockSpec((tm, D_MODEL), row), h2_spec]
    out_shape = [jax.ShapeDtypeStruct((N_TOK, D_MODEL), F32), h2_shape]
    args = [a_ctx, a_lat, w, x, gate, nw, shift, scale]
    if router:
        in_specs += [pl.BlockSpec((D_MODEL, 128), fixed), pl.BlockSpec((D_MODEL, 128), fixed)]
        out_specs.append(pl.BlockSpec((tm, 128), row))
        out_shape.append(jax.ShapeDtypeStruct((N_TOK, 128), F32))
        args += list(router_w)
    return pl.pallas_call(
        functools.partial(_proj_residual_kernel, router=router),
        grid=(N_TOK // tm,),
        in_specs=in_specs,
        out_specs=out_specs,
        out_shape=out_shape,
        scratch_shapes=[pltpu.VMEM((tm, D_MODEL), F32)],
        compiler_params=_params("arbitrary"),
        name="proj_residual",
    )(*args)


def _swiglu_partial(x, wg, wu, wd):
    g = _dot(x, wg)
    u = _dot(x, wu)
    return _dot((_silu(g) * u).astype(BF16), wd)


def _ffn_kernel(h_ref, x1_ref, g_ref, wg_ref, wu_ref, wd_ref, o_ref):
    part = _swiglu_partial(h_ref[...], wg_ref[...], wu_ref[...], wd_ref[...])
    o_ref[...] = x1_ref[...] + g_ref[...] * part


def _resident(shape):
    return pl.BlockSpec(shape, lambda *_: (0,) * len(shape), pipeline_mode=pl.Buffered(1))


def _ffn(h2, x1, gate, wg, wu, wd):
    tm = ROW_TILE
    row = lambda i: (i, 0)
    return pl.pallas_call(
        _ffn_kernel,
        grid=(N_TOK // tm,),
        in_specs=[
            pl.BlockSpec((tm, D_MODEL), row),
            pl.BlockSpec((tm, D_MODEL), row),
            pl.BlockSpec((None, 1, D_MODEL), lambda i: (_cond_of_tile(i), 0, 0)),
            _resident((D_MODEL, FFN_DIM)),
            _resident((D_MODEL, FFN_DIM)),
            _resident((FFN_DIM, D_MODEL)),
        ],
        out_specs=pl.BlockSpec((tm, D_MODEL), row),
        out_shape=jax.ShapeDtypeStruct((N_TOK, D_MODEL), F32),
        compiler_params=_params("parallel"),
        name="ffn",
    )(h2, x1, gate, wg, wu, wd)


def _moe_kernel(blk_e_ref, nact_ref, src_ref, x_hbm, wg_ref, wu_ref, wd_ref, ya_hbm,
                xrows0, xrows1, xb, acc, stage0, stage1, gsem, ssem):
    b = pl.program_id(0)
    f = pl.program_id(1)
    nact = nact_ref[0]
    xrows = (xrows0, xrows1)
    stage = (stage0, stage1)
    tile = lambda r: pl.ds(pl.multiple_of(r * LANE_CHUNKS, LANE_CHUNKS), LANE_CHUNKS)

    def gather_copy(par, step, r, a):
        tok = a & (N_TOK - 1)
        return pltpu.make_async_copy(x_hbm.at[tok], xrows[par].at[step, tile(r), :], gsem.at[par])

    def scatter_copy(par, step, r, a):
        return pltpu.make_async_copy(stage[par].at[step, tile(r), :], ya_hbm.at[a], ssem.at[par])

    def wait_gather(par):
        pltpu.make_async_copy(xrows[par], xrows[par], gsem.at[par]).wait()

    def wait_scatter(par):
        pltpu.make_async_copy(stage[par], stage[par], ssem.at[par]).wait()

    @pl.when((b == 0) & (f == 0))
    def _():
        stage1[...] = jnp.zeros_like(stage1)

        def issue(r, c):
            gather_copy(0, r // MOE_STEP_ROWS, r % MOE_STEP_ROWS, src_ref[MOE_ROWS + r]).start()
            return c

        lax.fori_loop(0, MOE_ROWS, issue, 0, unroll=8)

    def block_body(par):
        @pl.when(f == 0)
        def _():
            wait_gather(par)

            @pl.when(b >= 1)
            def _():
                wait_scatter(par)

            for step in range(MOE_STEPS):
                for c in range(LANE_CHUNKS):
                    xb[step * MOE_STEP_ROWS:(step + 1) * MOE_STEP_ROWS, c * 128:(c + 1) * 128] = (
                        _load_row_major(xrows[par].at[step], MOE_STEP_ROWS, c).astype(BF16))

        nxt = (b + 2) * MOE_ROWS + f * MOE_STEP_ROWS
        prv = b * MOE_ROWS + f * MOE_STEP_ROWS
        for r in range(MOE_STEP_ROWS):
            gather_copy(1 - par, f, r, src_ref[nxt + r]).start()
            scatter_copy(1 - par, f, r, src_ref[prv + r]).start()
        part = _swiglu_partial(xb[...], wg_ref[...], wu_ref[...], wd_ref[...])

        @pl.when(f == 0)
        def _():
            acc[...] = part

        @pl.when((f > 0) & (f < MOE_STEPS - 1))
        def _():
            acc[...] += part

        @pl.when(f == MOE_STEPS - 1)
        def _():
            total = acc[...] + part
            for step in range(MOE_STEPS):
                _store_row_major(stage[par].at[step], total[step * MOE_STEP_ROWS:(step + 1) * MOE_STEP_ROWS])

    def drain(par):
        wait_scatter(par)

        def issue(r, c):
            scatter_copy(1 - par, r // MOE_STEP_ROWS, r % MOE_STEP_ROWS, src_ref[b * MOE_ROWS + r]).start()
            return c

        lax.fori_loop(0, MOE_ROWS, issue, 0, unroll=8)
        wait_gather(par)
        wait_scatter(1 - par)

    for par in range(2):
        @pl.when((b < nact) & (b % 2 == par))
        def _():
            block_body(par)

        @pl.when((b == nact) & (f == 0) & (b % 2 == par))
        def _():
            drain(par)


def _moe(blk_e, nact, src, h2, wg, wu, wd, layer):
    tf = MOE_TILE
    wmap = lambda b, f, e, n, s: (layer, e[b], 0, f)
    rows_buf = pltpu.VMEM((MOE_STEPS, MOE_STEP_ROWS * LANE_CHUNKS, 128), F32)
    grid_spec = pltpu.PrefetchScalarGridSpec(
        num_scalar_prefetch=3,
        grid=(MOE_BLOCKS, MOE_STEPS),
        in_specs=[
            pl.BlockSpec(memory_space=pl.ANY),
            pl.BlockSpec((None, None, D_MODEL, tf), wmap),
            pl.BlockSpec((None, None, D_MODEL, tf), wmap),
            pl.BlockSpec((None, None, tf, D_MODEL), lambda b, f, e, n, s: (layer, e[b], f, 0)),
        ],
        out_specs=pl.BlockSpec(memory_space=pl.ANY),
        scratch_shapes=[
            rows_buf,
            rows_buf,
            pltpu.VMEM((MOE_ROWS, D_MODEL), BF16),
            pltpu.VMEM((MOE_ROWS, D_MODEL), F32),
            rows_buf,
            rows_buf,
            pltpu.SemaphoreType.DMA((2,)),
            pltpu.SemaphoreType.DMA((2,)),
        ],
    )
    return pl.pallas_call(
        _moe_kernel,
        grid_spec=grid_spec,
        out_shape=jax.ShapeDtypeStruct((N_ASSIGN + MOE_ROWS, LANE_CHUNKS, 128), F32),
        compiler_params=_params("arbitrary", "arbitrary"),
        name="moe",
    )(blk_e, nact, src, h2.reshape(N_TOK, LANE_CHUNKS, 128), wg, wu, wd)


def _combine_kernel(y0_ref, y1_ref, x1_ref, g_ref, w_ref, *o_refs):
    w = w_ref[...]
    rows = x1_ref.shape[0]

    def emit(o_ref):
        for c in range(LANE_CHUNKS):
            sl = slice(c * 128, (c + 1) * 128)
            y = w[:, 0:1] * _load_row_major(y0_ref, rows, c) + w[:, 1:2] * _load_row_major(y1_ref, rows, c)
            o_ref[:, sl] = x1_ref[:, sl] + g_ref[:, sl] * y

    if len(o_refs) == 1:
        emit(o_refs[0])
    else:
        i = pl.program_id(0)

        @pl.when(i < CTX_TILES)
        def _():
            emit(o_refs[0])

        @pl.when(i >= CTX_TILES)
        def _():
            emit(o_refs[1])


def _combine(ya, x1, gate, top_w, split):
    tm = ROW_TILE
    row = lambda i: (i, 0)
    ya2 = ya.reshape(ya.shape[0] * LANE_CHUNKS, 128)
    if split:
        out_specs = [pl.BlockSpec((tm, D_MODEL), lambda i: (jnp.minimum(i, CTX_TILES - 1), 0)),
                     pl.BlockSpec((tm, D_MODEL), lambda i: (jnp.maximum(i - CTX_TILES, 0), 0))]
        out_shape = [jax.ShapeDtypeStruct((N_CTX, D_MODEL), F32), jax.ShapeDtypeStruct((N_LAT, D_MODEL), F32)]
    else:
        out_specs = pl.BlockSpec((tm, D_MODEL), row)
        out_shape = jax.ShapeDtypeStruct((N_TOK, D_MODEL), F32)
    return pl.pallas_call(
        _combine_kernel,
        grid=(N_TOK // tm,),
        in_specs=[
            pl.BlockSpec((tm * LANE_CHUNKS, 128), row),
            pl.BlockSpec((tm * LANE_CHUNKS, 128), lambda i: (i + N_TOK // tm, 0)),
            pl.BlockSpec((tm, D_MODEL), row),
            pl.BlockSpec((None, 1, D_MODEL), lambda i: (_cond_of_tile(i), 0, 0)),
            pl.BlockSpec((tm, TOP_K), row),
        ],
        out_specs=out_specs,
        out_shape=out_shape,
        compiler_params=_params("arbitrary"),
        name="moe_combine",
    )(ya2, ya2, x1, gate, top_w)


def _route(logits):
    top_v, top_i = lax.top_k(logits, TOP_K)
    top_w = jax.nn.softmax(top_v, axis=-1)
    e = top_i.reshape(N_ASSIGN)
    onehot = (e[:, None] == jnp.arange(N_EXPERTS, dtype=e.dtype)[None, :]).astype(jnp.int32)
    csum = jnp.cumsum(onehot, axis=0)
    rank = jnp.sum(csum * onehot, axis=1) - 1
    counts = csum[-1]
    padded = (counts + MOE_ROWS - 1) // MOE_ROWS * MOE_ROWS
    pad_end = jnp.cumsum(padded)
    pad_start = pad_end - padded
    pos = (pad_start[e] + rank).astype(jnp.int32)
    n_src = (MOE_BLOCKS + 2) * MOE_ROWS
    dump = N_ASSIGN + jnp.arange(n_src, dtype=jnp.int32) % MOE_ROWS
    a = jnp.arange(N_ASSIGN, dtype=jnp.int32)
    src = dump.at[pos + MOE_ROWS].set((a % TOP_K) * N_TOK + a // TOP_K)
    blk_start = jnp.arange(MOE_BLOCKS, dtype=jnp.int32) * MOE_ROWS
    blk_e = jnp.minimum(jnp.searchsorted(pad_end, blk_start, side="right"), N_EXPERTS - 1).astype(jnp.int32)
    nact = (pad_end[-1:] // MOE_ROWS).astype(jnp.int32)
    return top_w, src, blk_e, nact


RET_QK_W = RET_HEADS * RET_DK
RET_V_W = RET_HEADS * RET_DV
RET_COL_TILE = 1024


def _ret_proj_kernel(x_ref, nw_ref, sh_ref, sc_ref, w_ref, c_ref, s_ref, o_ref):
    def body(rope):
        h = _modulate(x_ref[...], nw_ref[...], sh_ref[...], sc_ref[...]).astype(BF16)
        for c0 in range(0, o_ref.shape[1], RET_COL_TILE):
            acc = _dot(h, w_ref[:, c0:c0 + RET_COL_TILE])
            if c0 >= 2 * RET_QK_W + RET_V_W:
                o_ref[:, c0:c0 + RET_COL_TILE] = _silu(acc).astype(BF16)
            elif c0 >= 2 * RET_QK_W:
                o_ref[:, c0:c0 + RET_COL_TILE] = acc.astype(BF16)
            else:
                x = acc * (RET_DK ** -0.5) if c0 >= RET_QK_W else acc
                if not rope:
                    o_ref[:, c0:c0 + RET_COL_TILE] = x.astype(BF16)
                    continue
                for g in range(RET_COL_TILE // 128):
                    tl = slice((g % 2) * 128, (g % 2 + 1) * 128)
                    xs = x[:, g * 128:(g + 1) * 128]
                    y = xs * c_ref[:, tl] + pltpu.roll(xs, RET_DK // 4, 1) * s_ref[:, tl]
                    o_ref[:, c0 + g * 128:c0 + (g + 1) * 128] = y.astype(BF16)

    i = pl.program_id(0)

    @pl.when(i < CTX_TILES)
    def _():
        body(False)

    @pl.when(i >= CTX_TILES)
    def _():
        body(True)


def _ret_proj(x, nw, shift, scale, w, rope_c, rope_s):
    tm = ROW_TILE
    width = w.shape[1]
    row = lambda i: (i, 0)
    cond = lambda i: (_cond_of_tile(i), 0, 0)
    tab = lambda i: (jnp.where(i < CTX_TILES, 0, (i - CTX_TILES) % TILES_PER_LAT_BATCH), 0)
    return pl.pallas_call(
        _ret_proj_kernel,
        grid=(N_TOK // tm,),
        in_specs=[
            pl.BlockSpec((tm, D_MODEL), row),
            pl.BlockSpec((1, D_MODEL), lambda i: (0, 0)),
            pl.BlockSpec((None, 1, D_MODEL), cond),
            pl.BlockSpec((None, 1, D_MODEL), cond),
            _resident((D_MODEL, width)),
            pl.BlockSpec((tm, RET_DK), tab),
            pl.BlockSpec((tm, RET_DK), tab),
        ],
        out_specs=pl.BlockSpec((tm, width), row),
        out_shape=jax.ShapeDtypeStruct((N_TOK, width), BF16),
        compiler_params=_params("parallel"),
        name="ret_proj",
    )(x, nw, shift, scale, w, rope_c, rope_s)


def _retention_kernel(lg_ref, q_ref, k_ref, v_ref, sg_ref, gn_ref, *rest, zero_init, n_chunks, layer):
    if zero_init:
        y_ref, st_ref, o_acc, state, dmask = rest[-5:]
    else:
        s0f_ref, s0b_ref, y_ref, o_acc, state, dmask = rest
    hd = pl.program_id(1)
    c = RET_BLOCK
    ri = lax.broadcasted_iota(jnp.int32, (c, c), 0).astype(F32)
    ci = lax.broadcasted_iota(jnp.int32, (c, c), 1).astype(F32)
    pos = lax.broadcasted_iota(jnp.int32, (c, 1), 0).astype(F32)
    q_decay, k_decay, c_decay = [], [], []
    for d in range(2):
        lg = lg_ref[d, hd]
        rel = (ri - ci) if d == 0 else (ci - ri)
        dmask[d] = jnp.where(rel >= 0, jnp.exp(jnp.maximum(rel, 0.0) * lg), 0.0)
        q_decay.append(jnp.exp(((pos + 1.0) if d == 0 else (c - pos)) * lg))
        k_decay.append(jnp.exp(((c - 1.0 - pos) if d == 0 else pos) * lg))
        c_decay.append(jnp.exp(jnp.full((1, 1), c, F32) * lg))
        if zero_init:
            state[d] = jnp.zeros(state.shape[1:], F32)
        else:
            state[d] = (s0f_ref if d == 0 else s0b_ref)[...]

    def step(t, carry):
        for d in range(2):
            ch = t if d == 0 else n_chunks - 1 - t
            rows = pl.ds(pl.multiple_of(ch * c, c), c)
            qc = q_ref[rows, :]
            kc = k_ref[rows, :]
            vc = v_ref[rows, :]
            a = lax.dot_general(qc, kc, (((1,), (1,)), ((), ())), preferred_element_type=F32) * dmask[d]
            s = state[d]
            o_acc[d, rows, :] = (_dot(a.astype(BF16), vc)
                                 + _dot((qc.astype(F32) * q_decay[d]).astype(BF16), s.astype(BF16)))
            kd = (kc.astype(F32) * k_decay[d]).astype(BF16)
            state[d] = s * c_decay[d] + lax.dot_general(kd, vc, (((0,), (0,)), ((), ())),
                                                         preferred_element_type=F32)
        return carry

    lax.fori_loop(0, n_chunks, step, 0)

    if zero_init:
        for d in range(2):
            if st_ref.ndim == 3:
                st_ref[d] = state[d]
            else:
                st_ref[layer, d] = state[d]
                for other in range(st_ref.shape[0]):
                    if other != layer:
                        st_ref[other, d] = jnp.zeros(state.shape[1:], F32)

    y = _rms(o_acc[0] + o_acc[1], gn_ref[...])
    y_ref[...] = (sg_ref[...].astype(F32) * y).astype(BF16)


def _retention(log_gamma, qkvg, gn, state0, *, batch, b_off, t, layer, states=None):
    zero_init = state0 is None
    qblk = RET_HEADS * RET_DK // RET_DK
    vblk = 2 * RET_HEADS * RET_DK // RET_DV
    gblk = vblk + RET_HEADS
    in_specs = [
        pl.BlockSpec(memory_space=pltpu.SMEM),
        pl.BlockSpec((None, t, RET_DK), lambda b, h: (b + b_off, 0, h)),
        pl.BlockSpec((None, t, RET_DK), lambda b, h: (b + b_off, 0, qblk + h)),
        pl.BlockSpec((None, t, RET_DV), lambda b, h: (b + b_off, 0, vblk + h)),
        pl.BlockSpec((None, t, RET_DV), lambda b, h: (b + b_off, 0, gblk + h)),
        pl.BlockSpec((None, 1, RET_DV), lambda b, h: (h, 0, 0)),
    ]
    args = [log_gamma, qkvg, qkvg, qkvg, qkvg, gn]
    y_spec = pl.BlockSpec((None, t, RET_DV), lambda b, h: (b, 0, h))
    y_shape = jax.ShapeDtypeStruct((batch, t, RET_HEADS * RET_DV), BF16)
    aliases = {}
    if zero_init:
        n_ret = DEPTH // 2
        st_shape = jax.ShapeDtypeStruct((batch, n_ret, 2, RET_HEADS, RET_DK, RET_DV), F32)
        if states is None:
            st_spec = pl.BlockSpec((None, n_ret, 2, None, RET_DK, RET_DV), lambda b, h: (b, 0, 0, h, 0, 0))
        else:
            st_spec = pl.BlockSpec((None, None, 2, None, RET_DK, RET_DV), lambda b, h: (b, layer, 0, h, 0, 0))
            aliases = {len(args): 1}
            in_specs.append(pl.BlockSpec(memory_space=pl.ANY))
            args.append(states)
        out_specs = [y_spec, st_spec]
        out_shape = [y_shape, st_shape]
    else:
        in_specs += [
            pl.BlockSpec((None, None, None, None, RET_DK, RET_DV), lambda b, h: (b, layer, 0, h, 0, 0)),
            pl.BlockSpec((None, None, None, None, RET_DK, RET_DV), lambda b, h: (b, layer, 1, h, 0, 0))]
        args += [state0, state0]
        out_specs = y_spec
        out_shape = y_shape
    return pl.pallas_call(
        functools.partial(_retention_kernel, zero_init=zero_init, n_chunks=t // RET_BLOCK, layer=layer),
        grid=(batch, RET_HEADS),
        in_specs=in_specs,
        out_specs=out_specs,
        out_shape=out_shape,
        input_output_aliases=aliases,
        scratch_shapes=[pltpu.VMEM((2, t, RET_DV), F32), pltpu.VMEM((2, RET_DK, RET_DV), F32),
                        pltpu.VMEM((2, RET_BLOCK, RET_BLOCK), F32)],
        compiler_params=_params("parallel", "parallel"),
        name="retention",
    )(*args)


def _rope_tables(rot_dim):
    rows = DEC_SEQ // GRID_W
    row = jnp.repeat(jnp.arange(rows), GRID_W)
    col = jnp.tile(jnp.arange(GRID_W), rows)
    nf = rot_dim // 4
    inv = ROPE_BASE ** (-jnp.arange(nf, dtype=F32) / nf)
    ang = jnp.stack([row, col], axis=-1).astype(F32)[:, :, None] * inv
    cos, sin = jnp.cos(ang), jnp.sin(ang)
    c = jnp.stack([cos, cos], axis=2).reshape(DEC_SEQ, rot_dim)
    s = jnp.stack([-sin, sin], axis=2).reshape(DEC_SEQ, rot_dim)
    return c, s


def _swap_perm():
    nf = MLA_ROPE // 4
    return jnp.arange(MLA_ROPE) ^ nf


def _head_slot_tables(c32, s32):
    t = c32.shape[0]
    ca = jnp.concatenate([jnp.ones((t, MLA_NOPE), F32), c32, jnp.zeros((t, MLA_ROPE), F32)], axis=1)
    sb = jnp.concatenate([jnp.zeros((t, MLA_NOPE), F32), s32, jnp.zeros((t, MLA_ROPE), F32)], axis=1)
    return ca, sb


def _head_slot_gains(g, scale):
    perm = _swap_perm()
    zeros = jnp.zeros((MLA_ROPE,), F32)
    ga = jnp.concatenate([g, zeros]) * scale
    gb = jnp.concatenate([jnp.zeros((MLA_NOPE,), F32), g[MLA_NOPE:][perm], zeros]) * scale
    return ga.reshape(1, HEAD_SLOT), gb.reshape(1, HEAD_SLOT)


def _rows3(m):
    return m.reshape(N_COND, 1, D_MODEL)


def kernel(x_prompt, x_sample, cache_ckv, cache_kpe, state_ret, c, c_ctx, mod_w, mod_b, norm1_w, norm2_w, mla_wq_a, mla_q_a_norm, mla_wq_b, mla_wkv_a, mla_kv_norm, mla_w_uk, mla_w_uv, mla_q_norm, mla_k_norm, mla_wo, ret_wq, ret_wk, ret_wv, ret_wg, ret_wo, ret_decay, ret_gn, ffn_w_gate, ffn_w_up, ffn_w_down, moe_router, moe_w_gate, moe_w_up, moe_w_down):
    x = jnp.concatenate([x_prompt.reshape(N_CTX, D_MODEL), x_sample.reshape(N_LAT, D_MODEL)], axis=0)
    cond = jnp.concatenate([c_ctx[None, :], c, jnp.zeros((N_COND - 1 - DEC_BATCH, D_MODEL), F32)], axis=0)
    mods = _adaln(cond, mod_w, mod_b).reshape(DEPTH, N_COND, 6, D_MODEL)

    perm = _swap_perm()
    c32, s32 = _rope_tables(MLA_ROPE)
    ca_lat, sb_lat = _head_slot_tables(c32, s32)
    ca_id, sb_id = _head_slot_tables(jnp.ones((1, MLA_ROPE), F32), jnp.zeros((1, MLA_ROPE), F32))
    ca_q = jnp.concatenate([jnp.broadcast_to(ca_id, (N_CTX, HEAD_SLOT)), jnp.tile(ca_lat, (DEC_BATCH, 1))], axis=0)
    sb_q = jnp.concatenate([jnp.broadcast_to(sb_id, (N_CTX, HEAD_SLOT)), jnp.tile(sb_lat, (DEC_BATCH, 1))], axis=0)
    lat_kv = DEC_SEQ + PAST_LEN
    ca_kl = jnp.tile(jnp.concatenate([ca_lat, jnp.broadcast_to(ca_id, (PAST_LEN, HEAD_SLOT))], axis=0), (DEC_BATCH, 1))
    sb_kl = jnp.tile(jnp.concatenate([sb_lat, jnp.broadcast_to(sb_id, (PAST_LEN, HEAD_SLOT))], axis=0), (DEC_BATCH, 1))
    ca_kc = jnp.broadcast_to(ca_id, (N_CTX, HEAD_SLOT))
    sb_kc = jnp.broadcast_to(sb_id, (N_CTX, HEAD_SLOT))
    rc256, rs256 = _rope_tables(RET_DK)

    moe_wg, moe_wu, moe_wd = moe_w_gate.astype(BF16), moe_w_up.astype(BF16), moe_w_down.astype(BF16)

    new_ckv, new_kpe, new_ret = [], [], None
    for i in range(DEPTH):
        j = i // 2
        m = mods[i]
        sh1, sc1, g1, sh2, sc2, g2 = (_rows3(m[:, t]) for t in range(6))
        nw1 = norm1_w[i].reshape(1, D_MODEL)
        nw2 = norm2_w[i].reshape(1, D_MODEL)
        if i % 2 == 0:
            wqb = mla_wq_b[j].reshape(MLA_Q_RANK, MLA_HEADS, MLA_QK)
            wqb = jnp.concatenate([wqb, wqb[:, :, MLA_NOPE:][:, :, perm]], axis=2)
            wqb = wqb.reshape(MLA_Q_RANK, MLA_HEADS * HEAD_SLOT).astype(BF16)
            wkv = mla_wkv_a[j]
            wkv = jnp.concatenate([wkv[:, :MLA_KV_RANK], jnp.zeros((D_MODEL, MLA_NOPE), F32),
                                   wkv[:, MLA_KV_RANK:], wkv[:, MLA_KV_RANK:][:, perm]], axis=1).astype(BF16)
            wuk = mla_w_uk[j].reshape(MLA_KV_RANK, MLA_HEADS, MLA_NOPE)
            wuk = jnp.concatenate([wuk, jnp.zeros_like(wuk)], axis=2)
            wuk = wuk.reshape(MLA_KV_RANK, MLA_HEADS * HEAD_SLOT).astype(BF16)
            gaq, gbq = _head_slot_gains(mla_q_norm[j], MLA_QK ** -0.5 * LOG2_E)
            gak, gbk = _head_slot_gains(mla_k_norm[j], 1.0)

            q, ckv, kpe = _mla_proj(x, nw1, sh1, sc1, mla_wq_a[j].astype(BF16),
                                    mla_q_a_norm[j].reshape(1, MLA_Q_RANK), wqb, wkv,
                                    mla_kv_norm[j].reshape(1, MLA_KV_RANK), gaq, gbq, ca_q, sb_q)
            new_ckv.append(ckv[:N_CTX].reshape(BATCH, SEQ, MLA_KV_RANK))
            new_kpe.append(kpe[:N_CTX, MLA_NOPE:MLA_QK].reshape(BATCH, SEQ, MLA_ROPE))

            kpe_x = cache_kpe[:, j]
            kpe_x = jnp.concatenate([jnp.zeros((DEC_BATCH, PAST_LEN, MLA_NOPE), F32), kpe_x, kpe_x[:, :, perm]], axis=2)
            ckv_l = jnp.concatenate([ckv[N_CTX:].reshape(DEC_BATCH, DEC_SEQ, MLA_KV_RANK), cache_ckv[:, j]], axis=1)
            kpe_l = jnp.concatenate([kpe[N_CTX:].reshape(DEC_BATCH, DEC_SEQ, HEAD_SLOT), kpe_x], axis=1)
            wuv = mla_w_uv[j].astype(BF16)
            k_c, v_c = _kv_expand(ckv[:N_CTX], kpe[:N_CTX], wuk, wuv, gak, gbk, ca_kc, sb_kc)
            k_l, v_l = _kv_expand(ckv_l.reshape(DEC_BATCH * lat_kv, MLA_KV_RANK),
                                  kpe_l.reshape(DEC_BATCH * lat_kv, HEAD_SLOT), wuk, wuv, gak, gbk, ca_kl, sb_kl)

            qw = MLA_HEADS * HEAD_SLOT
            o_c = _attention(q.reshape(N_TOK // SEQ, SEQ, qw), k_c.reshape(BATCH, SEQ, qw),
                             v_c.reshape(BATCH, SEQ, MLA_HEADS * MLA_V),
                             batch=BATCH, q_off=0, tq_total=SEQ, tq=SEQ, heads=MLA_HEADS)
            o_l = _attention(q.reshape(N_TOK // DEC_SEQ, DEC_SEQ, qw), k_l.reshape(DEC_BATCH, lat_kv, qw),
                             v_l.reshape(DEC_BATCH, lat_kv, MLA_HEADS * MLA_V),
                             batch=DEC_BATCH, q_off=N_CTX // DEC_SEQ, tq_total=DEC_SEQ, tq=512, heads=2)
            mix_c, mix_l = o_c.reshape(N_CTX, D_MODEL), o_l.reshape(N_LAT, D_MODEL)
            w_out = mla_wo[j].astype(BF16)
        else:
            w_in = jnp.concatenate([ret_wq[j], ret_wk[j], ret_wv[j], ret_wg[j]], axis=1).astype(BF16)
            qkvg = _ret_proj(x, nw1, sh1, sc1, w_in, rc256, rs256)
            log_gamma = -jnp.exp(ret_decay[j].astype(F32))
            gn = ret_gn[j].reshape(RET_HEADS, 1, RET_DV)
            width = qkvg.shape[1]
            y_c, new_ret = _retention(log_gamma, qkvg.reshape(N_TOK // SEQ, SEQ, width), gn, None,
                                      batch=BATCH, b_off=0, t=SEQ, layer=j, states=new_ret)
            y_l = _retention(log_gamma, qkvg.reshape(N_TOK // DEC_SEQ, DEC_SEQ, width), gn, state_ret,
                             batch=DEC_BATCH, b_off=N_CTX // DEC_SEQ, t=DEC_SEQ, layer=j)
            mix_c, mix_l = y_c.reshape(N_CTX, RET_HEADS * RET_DV), y_l.reshape(N_LAT, RET_HEADS * RET_DV)
            w_out = ret_wo[j].astype(BF16)

        if i % 2 == 0:
            x1, h2 = _proj_residual(mix_c, mix_l, w_out, x, g1, nw2, sh2, sc2)
            x = _ffn(h2, x1, g2, ffn_w_gate[j].astype(BF16), ffn_w_up[j].astype(BF16), ffn_w_down[j].astype(BF16))
        else:
            wr = jnp.pad(moe_router[j], ((0, 0), (0, 128 - N_EXPERTS)))
            wr_hi = wr.astype(BF16)
            wr_lo = (wr - wr_hi.astype(F32)).astype(BF16)
            x1, h2, logits = _proj_residual(mix_c, mix_l, w_out, x, g1, nw2, sh2, sc2, router_w=(wr_hi, wr_lo))
            top_w, src, blk_e, nact = _route(logits[:, :N_EXPERTS])
            ya = _moe(blk_e, nact, src, h2, moe_wg, moe_wu, moe_wd, j)
            x = _combine(ya, x1, g2, top_w, split=(i == DEPTH - 1))

    y_prompt = x[0].reshape(BATCH, SEQ, D_MODEL)
    y_sample = x[1].reshape(DEC_BATCH, DEC_SEQ, D_MODEL)
    return (y_prompt, y_sample, jnp.stack(new_ckv, axis=1), jnp.stack(new_kpe, axis=1), new_ret)
```

```python
import functools

import jax
import jax.numpy as jnp
from jax import lax
from jax.experimental import pallas as pl
from jax.experimental.pallas import tpu as pltpu

F32 = jnp.float32
BF16 = jnp.bfloat16

D_MODEL = 1024
BATCH = 32
SEQ = 256
DEPTH = 4
DEC_BATCH = 4
DEC_SEQ = 2048
PAST_LEN = 512
GRID_W = 64
ROPE_BASE = 10000.0
NORM_EPS = 1e-6

MLA_HEADS = 16
MLA_NOPE = 64
MLA_ROPE = 32
MLA_QK = MLA_NOPE + MLA_ROPE
MLA_V = 64
MLA_Q_RANK = 384
MLA_KV_RANK = 256
HEAD_SLOT = 128

RET_HEADS = 4
RET_DK = 256
RET_DV = 512
RET_BLOCK = 256

FFN_DIM = 2816
N_EXPERTS = 8
TOP_K = 2
EXPERT_DIM = 3584

N_CTX = BATCH * SEQ
N_LAT = DEC_BATCH * DEC_SEQ
N_TOK = N_CTX + N_LAT
N_COND = 8

ROW_TILE = 512
CTX_TILES = N_CTX // ROW_TILE
TILES_PER_LAT_BATCH = DEC_SEQ // ROW_TILE

MOE_ROWS = 512
MOE_TILE = 1792
MOE_STEPS = EXPERT_DIM // MOE_TILE
MOE_STEP_ROWS = MOE_ROWS // MOE_STEPS
assert MOE_STEPS >= 2
N_ASSIGN = N_TOK * TOP_K
MOE_BLOCKS = N_ASSIGN // MOE_ROWS + N_EXPERTS

VMEM_LIMIT = 56 * 1024 * 1024
LOG2_E = 1.4426950408889634


def _params(*sem):
    return pltpu.CompilerParams(dimension_semantics=sem, vmem_limit_bytes=VMEM_LIMIT)


def _cond_of_tile(i):
    return jnp.where(i < CTX_TILES, 0, 1 + (i - CTX_TILES) // TILES_PER_LAT_BATCH)


def _dot(a, b):
    return jnp.dot(a, b, preferred_element_type=F32)


def _silu(x):
    return x * (1.0 / (1.0 + jnp.exp(-x)))


def _rms(x, w):
    return x * lax.rsqrt(jnp.mean(x * x, axis=-1, keepdims=True) + NORM_EPS) * w


def _modulate(x, nw, shift, scale):
    return _rms(x, nw) * (1.0 + scale) + shift


def _adaln_kernel(c_ref, w_ref, b_ref, o_ref):
    a = _silu(c_ref[...]).astype(BF16)
    o_ref[...] = _dot(a, w_ref[...].astype(BF16)) + b_ref[...]


def _adaln(cond, mod_w, mod_b):
    tn = 1024
    return pl.pallas_call(
        _adaln_kernel,
        grid=(DEPTH, 6 * D_MODEL // tn),
        in_specs=[
            pl.BlockSpec((N_COND, D_MODEL), lambda l, j: (0, 0)),
            pl.BlockSpec((None, D_MODEL, tn), lambda l, j: (l, 0, j)),
            pl.BlockSpec((None, 1, tn), lambda l, j: (l, 0, j)),
        ],
        out_specs=pl.BlockSpec((None, N_COND, tn), lambda l, j: (l, 0, j)),
        out_shape=jax.ShapeDtypeStruct((DEPTH, N_COND, 6 * D_MODEL), F32),
        compiler_params=_params("parallel", "parallel"),
        name="adaln",
    )(cond, mod_w, mod_b.reshape(DEPTH, 1, 6 * D_MODEL))


def _heads_norm_rope(xs, ga, gb):
    valid = lax.broadcasted_iota(jnp.int32, ga.shape, 1) < MLA_QK
    ss = [jnp.sum(jnp.where(valid, x * x, 0.0), axis=-1, keepdims=True) for x in xs]
    rs = [lax.rsqrt(s * (1.0 / MLA_QK) + NORM_EPS) for s in ss]
    return [r * (x * ga + pltpu.roll(x, HEAD_SLOT - MLA_ROPE, 1) * gb) for r, x in zip(rs, xs)]


def _mla_proj_kernel(x_ref, nw_ref, sh_ref, sc_ref, wqa_ref, qan_ref, wqb_ref, wkv_ref, kvn_ref,
                     ga_ref, gb_ref, ca_ref, sb_ref, q_ref, ckv_ref, kpe_ref):
    h = _modulate(x_ref[...], nw_ref[...], sh_ref[...], sc_ref[...]).astype(BF16)
    qa = _rms(_dot(h, wqa_ref[...]), qan_ref[...]).astype(BF16)
    q = _dot(qa, wqb_ref[...])
    kv = _dot(h, wkv_ref[...])
    ckv_ref[...] = _rms(kv[:, :MLA_KV_RANK], kvn_ref[...])
    kpe_ref[...] = kv[:, MLA_KV_RANK:]
    ga = ga_ref[...] * ca_ref[...]
    gb = gb_ref[...] * sb_ref[...]
    slots = [slice(hd * HEAD_SLOT, (hd + 1) * HEAD_SLOT) for hd in range(MLA_HEADS)]
    for sl, y in zip(slots, _heads_norm_rope([q[:, sl] for sl in slots], ga, gb)):
        q_ref[:, sl] = y.astype(BF16)


def _mla_proj(x, nw, shift, scale, wqa, qan, wqb, wkv, kvn, ga, gb, ca, sb):
    tm = ROW_TILE
    row = lambda i: (i, 0)
    fixed = lambda i: (0, 0)
    cond = lambda i: (_cond_of_tile(i), 0, 0)
    qw = MLA_HEADS * HEAD_SLOT
    kvw = MLA_KV_RANK + HEAD_SLOT
    return pl.pallas_call(
        _mla_proj_kernel,
        grid=(N_TOK // tm,),
        in_specs=[
            pl.BlockSpec((tm, D_MODEL), row),
            pl.BlockSpec((1, D_MODEL), fixed),
            pl.BlockSpec((None, 1, D_MODEL), cond),
            pl.BlockSpec((None, 1, D_MODEL), cond),
            pl.BlockSpec((D_MODEL, MLA_Q_RANK), fixed),
            pl.BlockSpec((1, MLA_Q_RANK), fixed),
            pl.BlockSpec((MLA_Q_RANK, qw), fixed),
            pl.BlockSpec((D_MODEL, kvw), fixed),
            pl.BlockSpec((1, MLA_KV_RANK), fixed),
            pl.BlockSpec((1, HEAD_SLOT), fixed),
            pl.BlockSpec((1, HEAD_SLOT), fixed),
            pl.BlockSpec((tm, HEAD_SLOT), row),
            pl.BlockSpec((tm, HEAD_SLOT), row),
        ],
        out_specs=[
            pl.BlockSpec((tm, qw), row),
            pl.BlockSpec((tm, MLA_KV_RANK), row),
            pl.BlockSpec((tm, HEAD_SLOT), row),
        ],
        out_shape=[
            jax.ShapeDtypeStruct((N_TOK, qw), BF16),
            jax.ShapeDtypeStruct((N_TOK, MLA_KV_RANK), F32),
            jax.ShapeDtypeStruct((N_TOK, HEAD_SLOT), F32),
        ],
        compiler_params=_params("parallel"),
        name="mla_proj",
    )(x, nw, shift, scale, wqa, qan, wqb, wkv, kvn, ga, gb, ca, sb)


def _kv_expand_kernel(ckv_ref, kpe_ref, wuk_ref, wuv_ref, ga_ref, gb_ref, ca_ref, sb_ref, k_ref, v_ref):
    c = ckv_ref[...].astype(BF16)
    kn = _dot(c, wuk_ref[...])
    v_ref[...] = _dot(c, wuv_ref[...]).astype(BF16)
    kpe = kpe_ref[...]
    ga = ga_ref[...] * ca_ref[...]
    gb = gb_ref[...] * sb_ref[...]
    slots = [slice(hd * HEAD_SLOT, (hd + 1) * HEAD_SLOT) for hd in range(MLA_HEADS)]
    for sl, y in zip(slots, _heads_norm_rope([kn[:, sl] + kpe for sl in slots], ga, gb)):
        k_ref[:, sl] = y.astype(BF16)


def _kv_expand(ckv, kpe, wuk, wuv, ga, gb, ca, sb):
    n = ckv.shape[0]
    tm = ROW_TILE
    row = lambda i: (i, 0)
    fixed = lambda i: (0, 0)
    kw = MLA_HEADS * HEAD_SLOT
    vw = MLA_HEADS * MLA_V
    return pl.pallas_call(
        _kv_expand_kernel,
        grid=(n // tm,),
        in_specs=[
            pl.BlockSpec((tm, MLA_KV_RANK), row),
            pl.BlockSpec((tm, HEAD_SLOT), row),
            pl.BlockSpec((MLA_KV_RANK, kw), fixed),
            pl.BlockSpec((MLA_KV_RANK, vw), fixed),
            pl.BlockSpec((1, HEAD_SLOT), fixed),
            pl.BlockSpec((1, HEAD_SLOT), fixed),
            pl.BlockSpec((tm, HEAD_SLOT), row),
            pl.BlockSpec((tm, HEAD_SLOT), row),
        ],
        out_specs=[pl.BlockSpec((tm, kw), row), pl.BlockSpec((tm, vw), row)],
        out_shape=[jax.ShapeDtypeStruct((n, kw), BF16), jax.ShapeDtypeStruct((n, vw), BF16)],
        compiler_params=_params("parallel"),
        name="kv_expand",
    )(ckv, kpe, wuk, wuv, ga, gb, ca, sb)


def _attn_kernel(q_ref, k_ref, v_ref, o_ref, *, heads):
    scores = []
    for hd in range(heads):
        qh = q_ref[:, hd * HEAD_SLOT:(hd + 1) * HEAD_SLOT]
        kh = k_ref[:, hd * HEAD_SLOT:(hd + 1) * HEAD_SLOT]
        scores.append(lax.dot_general(qh, kh, (((1,), (1,)), ((), ())), preferred_element_type=F32))
    probs, invs = [], []
    for s in scores:
        p = jnp.exp2(s - jnp.max(s, axis=-1, keepdims=True))
        invs.append(1.0 / jnp.sum(p, axis=-1, keepdims=True))
        probs.append(p.astype(BF16))
    for hd in range(heads):
        o = _dot(probs[hd], v_ref[:, hd * MLA_V:(hd + 1) * MLA_V])
        o_ref[:, hd * MLA_V:(hd + 1) * MLA_V] = (o * invs[hd]).astype(BF16)


def _attention(q, k, v, *, batch, q_off, tq_total, tq, heads):
    tk = k.shape[1]
    groups = MLA_HEADS // heads
    return pl.pallas_call(
        functools.partial(_attn_kernel, heads=heads),
        grid=(batch, groups, tq_total // tq),
        in_specs=[
            pl.BlockSpec((None, tq, heads * HEAD_SLOT), lambda b, g, i: (b + q_off, i, g)),
            pl.BlockSpec((None, tk, heads * HEAD_SLOT), lambda b, g, i: (b, 0, g)),
            pl.BlockSpec((None, tk, heads * MLA_V), lambda b, g, i: (b, 0, g)),
        ],
        out_specs=pl.BlockSpec((None, tq, heads * MLA_V), lambda b, g, i: (b, i, g)),
        out_shape=jax.ShapeDtypeStruct((batch, tq_total, MLA_HEADS * MLA_V), BF16),
        compiler_params=_params("parallel", "parallel", "parallel"),
        name="attention",
    )(q, k, v)


LANE_CHUNKS = D_MODEL // 128


def _store_row_major(ref, x):
    rows = x.shape[0]
    for c in range(LANE_CHUNKS):
        ref[pl.ds(c, rows, stride=LANE_CHUNKS), :] = x[:, c * 128:(c + 1) * 128]


def _load_row_major(ref, rows, c):
    return ref[pl.ds(c, rows, stride=LANE_CHUNKS), :]


def _proj_residual_kernel(ac_ref, al_ref, w_ref, x_ref, g_ref, nw_ref, sh_ref, sc_ref, *rest, router):
    if router:
        rhi_ref, rlo_ref, x1_ref, h2_ref, lg_ref, y_sc = rest
    else:
        x1_ref, h2_ref, y_sc = rest
    i = pl.program_id(0)

    @pl.when(i < CTX_TILES)
    def _():
        y_sc[...] = _dot(ac_ref[...], w_ref[...])

    @pl.when(i >= CTX_TILES)
    def _():
        y_sc[...] = _dot(al_ref[...], w_ref[...])

    x1 = x_ref[...] + g_ref[...] * y_sc[...]
    h2 = _modulate(x1, nw_ref[...], sh_ref[...], sc_ref[...])
    if router:
        hi = h2.astype(BF16)
        lo = (h2 - hi.astype(F32)).astype(BF16)
        lg_ref[...] = _dot(hi, rhi_ref[...]) + (_dot(lo, rhi_ref[...]) + _dot(hi, rlo_ref[...]))
        _store_row_major(h2_ref, h2)
    else:
        h2_ref[...] = h2.astype(h2_ref.dtype)
    x1_ref[...] = x1


def _proj_residual(a_ctx, a_lat, w, x, gate, nw, shift, scale, router_w=None):
    tm = ROW_TILE
    kdim = w.shape[0]
    row = lambda i: (i, 0)
    fixed = lambda i: (0, 0)
    cond = lambda i: (_cond_of_tile(i), 0, 0)
    router = router_w is not None
    in_specs = [
        pl.BlockSpec((tm, kdim), lambda i: (jnp.minimum(i, CTX_TILES - 1), 0)),
        pl.BlockSpec((tm, kdim), lambda i: (jnp.maximum(i - CTX_TILES, 0), 0)),
        pl.BlockSpec((kdim, D_MODEL), fixed),
        pl.BlockSpec((tm, D_MODEL), row),
        pl.BlockSpec((None, 1, D_MODEL), cond),
        pl.BlockSpec((1, D_MODEL), fixed),
        pl.BlockSpec((None, 1, D_MODEL), cond),
        pl.BlockSpec((None, 1, D_MODEL), cond),
    ]
    if router:
        h2_spec = pl.BlockSpec((tm * LANE_CHUNKS, 128), row)
        h2_shape = jax.ShapeDtypeStruct((N_TOK * LANE_CHUNKS, 128), F32)
    else:
        h2_spec = pl.BlockSpec((tm, D_MODEL), row)
        h2_shape = jax.ShapeDtypeStruct((N_TOK, D_MODEL), BF16)
    out_specs = [pl.BlockSpec((tm, D_MODEL), row), h2_spec]
    out_shape = [jax.ShapeDtypeStruct((N_TOK, D_MODEL), F32), h2_shape]
    args = [a_ctx, a_lat, w, x, gate, nw, shift, scale]
    if router:
        in_specs += [pl.BlockSpec((D_MODEL, 128), fixed), pl.BlockSpec((D_MODEL, 128), fixed)]
        out_specs.append(pl.BlockSpec((tm, 128), row))
        out_shape.append(jax.ShapeDtypeStruct((N_TOK, 128), F32))
        args += list(router_w)
    return pl.pallas_call(
        functools.partial(_proj_residual_kernel, router=router),
        grid=(N_TOK // tm,),
        in_specs=in_specs,
        out_specs=out_specs,
        out_shape=out_shape,
        scratch_shapes=[pltpu.VMEM((tm, D_MODEL), F32)],
        compiler_params=_params("arbitrary"),
        name="proj_residual",
    )(*args)


def _swiglu_partial(x, wg, wu, wd):
    g = _dot(x, wg)
    u = _dot(x, wu)
    return _dot((_silu(g) * u).astype(BF16), wd)


def _ffn_kernel(h_ref, x1_ref, g_ref, wg_ref, wu_ref, wd_ref, o_ref):
    part = _swiglu_partial(h_ref[...], wg_ref[...], wu_ref[...], wd_ref[...])
    o_ref[...] = x1_ref[...] + g_ref[...] * part


def _resident(shape):
    return pl.BlockSpec(shape, lambda *_: (0,) * len(shape), pipeline_mode=pl.Buffered(1))


def _ffn(h2, x1, gate, wg, wu, wd):
    tm = ROW_TILE
    row = lambda i: (i, 0)
    return pl.pallas_call(
        _ffn_kernel,
        grid=(N_TOK // tm,),
        in_specs=[
            pl.BlockSpec((tm, D_MODEL), row),
            pl.BlockSpec((tm, D_MODEL), row),
            pl.BlockSpec((None, 1, D_MODEL), lambda i: (_cond_of_tile(i), 0, 0)),
            _resident((D_MODEL, FFN_DIM)),
            _resident((D_MODEL, FFN_DIM)),
            _resident((FFN_DIM, D_MODEL)),
        ],
        out_specs=pl.BlockSpec((tm, D_MODEL), row),
        out_shape=jax.ShapeDtypeStruct((N_TOK, D_MODEL), F32),
        compiler_params=_params("parallel"),
        name="ffn",
    )(h2, x1, gate, wg, wu, wd)


def _moe_kernel(blk_e_ref, nact_ref, order_ref, base_ref, x_hbm, wg_ref, wu_ref, wd_ref, ya_hbm,
                xrows0, xrows1, xb, acc, stage0, stage1, gsem, ssem):
    b = pl.program_id(0)
    f = pl.program_id(1)
    nact = nact_ref[0]
    xrows = (xrows0, xrows1)
    stage = (stage0, stage1)
    tile = lambda r: pl.ds(pl.multiple_of(r * LANE_CHUNKS, LANE_CHUNKS), LANE_CHUNKS)

    def row_ids(blk):
        base = base_ref[blk + 1]
        return lambda r: order_ref[base + r]

    def gather_copy(par, step, r, a):
        tok = a & (N_TOK - 1)
        return pltpu.make_async_copy(x_hbm.at[tok], xrows[par].at[step, tile(r), :], gsem.at[par])

    def scatter_copy(par, step, r, a):
        return pltpu.make_async_copy(stage[par].at[step, tile(r), :], ya_hbm.at[a], ssem.at[par])

    def wait_gather(par):
        pltpu.make_async_copy(xrows[par], xrows[par], gsem.at[par]).wait()

    def wait_scatter(par):
        pltpu.make_async_copy(stage[par], stage[par], ssem.at[par]).wait()

    @pl.when((b == 0) & (f == 0))
    def _():
        stage1[...] = jnp.zeros_like(stage1)
        ids = row_ids(0)

        def issue(r, c):
            gather_copy(0, r // MOE_STEP_ROWS, r % MOE_STEP_ROWS, ids(r)).start()
            return c

        lax.fori_loop(0, MOE_ROWS, issue, 0, unroll=8)

    def block_body(par):
        @pl.when(f == 0)
        def _():
            wait_gather(par)

            @pl.when(b >= 1)
            def _():
                wait_scatter(par)

            for step in range(MOE_STEPS):
                for c in range(LANE_CHUNKS):
                    xb[step * MOE_STEP_ROWS:(step + 1) * MOE_STEP_ROWS, c * 128:(c + 1) * 128] = (
                        _load_row_major(xrows[par].at[step], MOE_STEP_ROWS, c).astype(BF16))

        nxt, prv = row_ids(b + 1), row_ids(b - 1)
        row0 = f * MOE_STEP_ROWS
        for r in range(MOE_STEP_ROWS):
            gather_copy(1 - par, f, r, nxt(row0 + r)).start()
            scatter_copy(1 - par, f, r, prv(row0 + r)).start()
        part = _swiglu_partial(xb[...], wg_ref[...], wu_ref[...], wd_ref[...])

        @pl.when(f == 0)
        def _():
            acc[...] = part

        @pl.when((f > 0) & (f < MOE_STEPS - 1))
        def _():
            acc[...] += part

        @pl.when(f == MOE_STEPS - 1)
        def _():
            total = acc[...] + part
            for step in range(MOE_STEPS):
                _store_row_major(stage[par].at[step], total[step * MOE_STEP_ROWS:(step + 1) * MOE_STEP_ROWS])

    def drain(par):
        wait_scatter(par)
        ids = row_ids(b - 1)

        def issue(r, c):
            scatter_copy(1 - par, r // MOE_STEP_ROWS, r % MOE_STEP_ROWS, ids(r)).start()
            return c

        lax.fori_loop(0, MOE_ROWS, issue, 0, unroll=8)
        wait_gather(par)
        wait_scatter(1 - par)

    for par in range(2):
        @pl.when((b < nact) & (b % 2 == par))
        def _():
            block_body(par)

        @pl.when((b == nact) & (f == 0) & (b % 2 == par))
        def _():
            drain(par)


def _moe(routing, h2, wg, wu, wd, layer):
    tf = MOE_TILE
    wmap = lambda b, f, e, *_: (layer, e[b], 0, f)
    rows_buf = pltpu.VMEM((MOE_STEPS, MOE_STEP_ROWS * LANE_CHUNKS, 128), F32)
    grid_spec = pltpu.PrefetchScalarGridSpec(
        num_scalar_prefetch=len(routing),
        grid=(MOE_BLOCKS, MOE_STEPS),
        in_specs=[
            pl.BlockSpec(memory_space=pl.ANY),
            pl.BlockSpec((None, None, D_MODEL, tf), wmap),
            pl.BlockSpec((None, None, D_MODEL, tf), wmap),
            pl.BlockSpec((None, None, tf, D_MODEL), lambda b, f, e, *_: (layer, e[b], f, 0)),
        ],
        out_specs=pl.BlockSpec(memory_space=pl.ANY),
        scratch_shapes=[
            rows_buf,
            rows_buf,
            pltpu.VMEM((MOE_ROWS, D_MODEL), BF16),
            pltpu.VMEM((MOE_ROWS, D_MODEL), F32),
            rows_buf,
            rows_buf,
            pltpu.SemaphoreType.DMA((2,)),
            pltpu.SemaphoreType.DMA((2,)),
        ],
    )
    return pl.pallas_call(
        _moe_kernel,
        grid_spec=grid_spec,
        out_shape=jax.ShapeDtypeStruct((N_ASSIGN + MOE_ROWS, LANE_CHUNKS, 128), F32),
        compiler_params=_params("arbitrary", "arbitrary"),
        name="moe",
    )(*routing, h2.reshape(N_TOK, LANE_CHUNKS, 128), wg, wu, wd)


def _combine_kernel(y0_ref, y1_ref, x1_ref, g_ref, w_ref, *o_refs):
    w = w_ref[...]
    rows = x1_ref.shape[0]

    def emit(o_ref):
        for c in range(LANE_CHUNKS):
            sl = slice(c * 128, (c + 1) * 128)
            y = w[:, 0:1] * _load_row_major(y0_ref, rows, c) + w[:, 1:2] * _load_row_major(y1_ref, rows, c)
            o_ref[:, sl] = x1_ref[:, sl] + g_ref[:, sl] * y

    if len(o_refs) == 1:
        emit(o_refs[0])
    else:
        i = pl.program_id(0)

        @pl.when(i < CTX_TILES)
        def _():
            emit(o_refs[0])

        @pl.when(i >= CTX_TILES)
        def _():
            emit(o_refs[1])


def _combine(ya, x1, gate, top_w, split):
    tm = ROW_TILE
    row = lambda i: (i, 0)
    ya2 = ya.reshape(ya.shape[0] * LANE_CHUNKS, 128)
    if split:
        out_specs = [pl.BlockSpec((tm, D_MODEL), lambda i: (jnp.minimum(i, CTX_TILES - 1), 0)),
                     pl.BlockSpec((tm, D_MODEL), lambda i: (jnp.maximum(i - CTX_TILES, 0), 0))]
        out_shape = [jax.ShapeDtypeStruct((N_CTX, D_MODEL), F32), jax.ShapeDtypeStruct((N_LAT, D_MODEL), F32)]
    else:
        out_specs = pl.BlockSpec((tm, D_MODEL), row)
        out_shape = jax.ShapeDtypeStruct((N_TOK, D_MODEL), F32)
    return pl.pallas_call(
        _combine_kernel,
        grid=(N_TOK // tm,),
        in_specs=[
            pl.BlockSpec((tm * LANE_CHUNKS, 128), row),
            pl.BlockSpec((tm * LANE_CHUNKS, 128), lambda i: (i + N_TOK // tm, 0)),
            pl.BlockSpec((tm, D_MODEL), row),
            pl.BlockSpec((None, 1, D_MODEL), lambda i: (_cond_of_tile(i), 0, 0)),
            pl.BlockSpec((tm, TOP_K), row),
        ],
        out_specs=out_specs,
        out_shape=out_shape,
        compiler_params=_params("arbitrary"),
        name="moe_combine",
    )(ya2, ya2, x1, gate, top_w)


def _route(logits):
    top_v, top_i = lax.top_k(logits, TOP_K)
    top_w = jax.nn.softmax(top_v, axis=-1)
    e = top_i.reshape(N_ASSIGN).astype(jnp.int32)
    a = jnp.arange(N_ASSIGN, dtype=jnp.int32)
    row_id = (a % TOP_K) * N_TOK + a // TOP_K
    order = jnp.sort(e * N_ASSIGN + row_id) & (N_ASSIGN - 1)
    order = jnp.concatenate([order, N_ASSIGN + jnp.arange(MOE_ROWS, dtype=jnp.int32)])
    counts = jnp.sum((e[:, None] == jnp.arange(N_EXPERTS, dtype=jnp.int32)[None, :]).astype(jnp.int32), axis=0)
    start = jnp.cumsum(counts) - counts
    nblk = (counts + MOE_ROWS - 1) // MOE_ROWS
    blk_end = jnp.cumsum(nblk)
    blk = jnp.arange(-1, MOE_BLOCKS + 1, dtype=jnp.int32)
    blk_e = jnp.minimum(jnp.searchsorted(blk_end, blk, side="right"), N_EXPERTS - 1).astype(jnp.int32)
    within = (blk - (blk_end - nblk)[blk_e]) * MOE_ROWS
    real = (blk >= 0) & (blk < blk_end[-1])
    blk_base = jnp.where(real, start[blk_e] + within, N_ASSIGN).astype(jnp.int32)
    nact = blk_end[-1:].astype(jnp.int32)
    return top_w, (blk_e[1:-1], nact, order, blk_base)


RET_QK_W = RET_HEADS * RET_DK
RET_V_W = RET_HEADS * RET_DV
RET_COL_TILE = 1024


def _ret_proj_kernel(x_ref, nw_ref, sh_ref, sc_ref, w_ref, c_ref, s_ref, o_ref):
    def body(rope):
        h = _modulate(x_ref[...], nw_ref[...], sh_ref[...], sc_ref[...]).astype(BF16)
        for c0 in range(0, o_ref.shape[1], RET_COL_TILE):
            acc = _dot(h, w_ref[:, c0:c0 + RET_COL_TILE])
            if c0 >= 2 * RET_QK_W + RET_V_W:
                o_ref[:, c0:c0 + RET_COL_TILE] = _silu(acc).astype(BF16)
            elif c0 >= 2 * RET_QK_W:
                o_ref[:, c0:c0 + RET_COL_TILE] = acc.astype(BF16)
            else:
                x = acc * (RET_DK ** -0.5) if c0 >= RET_QK_W else acc
                if not rope:
                    o_ref[:, c0:c0 + RET_COL_TILE] = x.astype(BF16)
                    continue
                for g in range(RET_COL_TILE // 128):
                    tl = slice((g % 2) * 128, (g % 2 + 1) * 128)
                    xs = x[:, g * 128:(g + 1) * 128]
                    y = xs * c_ref[:, tl] + pltpu.roll(xs, RET_DK // 4, 1) * s_ref[:, tl]
                    o_ref[:, c0 + g * 128:c0 + (g + 1) * 128] = y.astype(BF16)

    i = pl.program_id(0)

    @pl.when(i < CTX_TILES)
    def _():
        body(False)

    @pl.when(i >= CTX_TILES)
    def _():
        body(True)


def _ret_proj(x, nw, shift, scale, w, rope_c, rope_s):
    tm = ROW_TILE
    width = w.shape[1]
    row = lambda i: (i, 0)
    cond = lambda i: (_cond_of_tile(i), 0, 0)
    tab = lambda i: (jnp.where(i < CTX_TILES, 0, (i - CTX_TILES) % TILES_PER_LAT_BATCH), 0)
    return pl.pallas_call(
        _ret_proj_kernel,
        grid=(N_TOK // tm,),
        in_specs=[
            pl.BlockSpec((tm, D_MODEL), row),
            pl.BlockSpec((1, D_MODEL), lambda i: (0, 0)),
            pl.BlockSpec((None, 1, D_MODEL), cond),
            pl.BlockSpec((None, 1, D_MODEL), cond),
            _resident((D_MODEL, width)),
            pl.BlockSpec((tm, RET_DK), tab),
            pl.BlockSpec((tm, RET_DK), tab),
        ],
        out_specs=pl.BlockSpec((tm, width), row),
        out_shape=jax.ShapeDtypeStruct((N_TOK, width), BF16),
        compiler_params=_params("parallel"),
        name="ret_proj",
    )(x, nw, shift, scale, w, rope_c, rope_s)


def _retention_kernel(lg_ref, q_ref, k_ref, v_ref, sg_ref, gn_ref, *rest, zero_init, n_chunks, layer):
    if zero_init:
        y_ref, st_ref, o_acc, state, dmask = rest[-5:]
    else:
        s0f_ref, s0b_ref, y_ref, o_acc, state, dmask = rest
    hd = pl.program_id(1)
    c = RET_BLOCK
    ri = lax.broadcasted_iota(jnp.int32, (c, c), 0).astype(F32)
    ci = lax.broadcasted_iota(jnp.int32, (c, c), 1).astype(F32)
    pos = lax.broadcasted_iota(jnp.int32, (c, 1), 0).astype(F32)
    q_decay, k_decay, c_decay = [], [], []
    for d in range(2):
        lg = lg_ref[d, hd]
        rel = (ri - ci) if d == 0 else (ci - ri)
        dmask[d] = jnp.where(rel >= 0, jnp.exp(jnp.maximum(rel, 0.0) * lg), 0.0)
        q_decay.append(jnp.exp(((pos + 1.0) if d == 0 else (c - pos)) * lg))
        k_decay.append(jnp.exp(((c - 1.0 - pos) if d == 0 else pos) * lg))
        c_decay.append(jnp.exp(jnp.full((1, 1), c, F32) * lg))
        if zero_init:
            state[d] = jnp.zeros(state.shape[1:], F32)
        else:
            state[d] = (s0f_ref if d == 0 else s0b_ref)[...]

    def step(t, carry):
        for d in range(2):
            ch = t if d == 0 else n_chunks - 1 - t
            rows = pl.ds(pl.multiple_of(ch * c, c), c)
            qc = q_ref[rows, :]
            kc = k_ref[rows, :]
            vc = v_ref[rows, :]
            a = lax.dot_general(qc, kc, (((1,), (1,)), ((), ())), preferred_element_type=F32) * dmask[d]
            s = state[d]
            o_acc[d, rows, :] = (_dot(a.astype(BF16), vc)
                                 + _dot((qc.astype(F32) * q_decay[d]).astype(BF16), s.astype(BF16)))
            kd = (kc.astype(F32) * k_decay[d]).astype(BF16)
            state[d] = s * c_decay[d] + lax.dot_general(kd, vc, (((0,), (0,)), ((), ())),
                                                         preferred_element_type=F32)
        return carry

    lax.fori_loop(0, n_chunks, step, 0)

    if zero_init:
        for d in range(2):
            if st_ref.ndim == 3:
                st_ref[d] = state[d]
            else:
                st_ref[layer, d] = state[d]
                for other in range(st_ref.shape[0]):
                    if other != layer:
                        st_ref[other, d] = jnp.zeros(state.shape[1:], F32)

    y = _rms(o_acc[0] + o_acc[1], gn_ref[...])
    y_ref[...] = (sg_ref[...].astype(F32) * y).astype(BF16)


def _retention(log_gamma, qkvg, gn, state0, *, batch, b_off, t, layer, states=None):
    zero_init = state0 is None
    qblk = RET_HEADS * RET_DK // RET_DK
    vblk = 2 * RET_HEADS * RET_DK // RET_DV
    gblk = vblk + RET_HEADS
    in_specs = [
        pl.BlockSpec(memory_space=pltpu.SMEM),
        pl.BlockSpec((None, t, RET_DK), lambda b, h: (b + b_off, 0, h)),
        pl.BlockSpec((None, t, RET_DK), lambda b, h: (b + b_off, 0, qblk + h)),
        pl.BlockSpec((None, t, RET_DV), lambda b, h: (b + b_off, 0, vblk + h)),
        pl.BlockSpec((None, t, RET_DV), lambda b, h: (b + b_off, 0, gblk + h)),
        pl.BlockSpec((None, 1, RET_DV), lambda b, h: (h, 0, 0)),
    ]
    args = [log_gamma, qkvg, qkvg, qkvg, qkvg, gn]
    y_spec = pl.BlockSpec((None, t, RET_DV), lambda b, h: (b, 0, h))
    y_shape = jax.ShapeDtypeStruct((batch, t, RET_HEADS * RET_DV), BF16)
    aliases = {}
    if zero_init:
        n_ret = DEPTH // 2
        st_shape = jax.ShapeDtypeStruct((batch, n_ret, 2, RET_HEADS, RET_DK, RET_DV), F32)
        if states is None:
            st_spec = pl.BlockSpec((None, n_ret, 2, None, RET_DK, RET_DV), lambda b, h: (b, 0, 0, h, 0, 0))
        else:
            st_spec = pl.BlockSpec((None, None, 2, None, RET_DK, RET_DV), lambda b, h: (b, layer, 0, h, 0, 0))
            aliases = {len(args): 1}
            in_specs.append(pl.BlockSpec(memory_space=pl.ANY))
            args.append(states)
        out_specs = [y_spec, st_spec]
        out_shape = [y_shape, st_shape]
    else:
        in_specs += [
            pl.BlockSpec((None, None, None, None, RET_DK, RET_DV), lambda b, h: (b, layer, 0, h, 0, 0)),
            pl.BlockSpec((None, None, None, None, RET_DK, RET_DV), lambda b, h: (b, layer, 1, h, 0, 0))]
        args += [state0, state0]
        out_specs = y_spec
        out_shape = y_shape
    return pl.pallas_call(
        functools.partial(_retention_kernel, zero_init=zero_init, n_chunks=t // RET_BLOCK, layer=layer),
        grid=(batch, RET_HEADS),
        in_specs=in_specs,
        out_specs=out_specs,
        out_shape=out_shape,
        input_output_aliases=aliases,
        scratch_shapes=[pltpu.VMEM((2, t, RET_DV), F32), pltpu.VMEM((2, RET_DK, RET_DV), F32),
                        pltpu.VMEM((2, RET_BLOCK, RET_BLOCK), F32)],
        compiler_params=_params("parallel", "parallel"),
        name="retention",
    )(*args)


def _rope_tables(rot_dim):
    rows = DEC_SEQ // GRID_W
    row = jnp.repeat(jnp.arange(rows), GRID_W)
    col = jnp.tile(jnp.arange(GRID_W), rows)
    nf = rot_dim // 4
    inv = ROPE_BASE ** (-jnp.arange(nf, dtype=F32) / nf)
    ang = jnp.stack([row, col], axis=-1).astype(F32)[:, :, None] * inv
    cos, sin = jnp.cos(ang), jnp.sin(ang)
    c = jnp.stack([cos, cos], axis=2).reshape(DEC_SEQ, rot_dim)
    s = jnp.stack([-sin, sin], axis=2).reshape(DEC_SEQ, rot_dim)
    return c, s


def _swap_perm():
    nf = MLA_ROPE // 4
    return jnp.arange(MLA_ROPE) ^ nf


def _head_slot_tables(c32, s32):
    t = c32.shape[0]
    ca = jnp.concatenate([jnp.ones((t, MLA_NOPE), F32), c32, jnp.zeros((t, MLA_ROPE), F32)], axis=1)
    sb = jnp.concatenate([jnp.zeros((t, MLA_NOPE), F32), s32, jnp.zeros((t, MLA_ROPE), F32)], axis=1)
    return ca, sb


def _head_slot_gains(g, scale):
    perm = _swap_perm()
    zeros = jnp.zeros((MLA_ROPE,), F32)
    ga = jnp.concatenate([g, zeros]) * scale
    gb = jnp.concatenate([jnp.zeros((MLA_NOPE,), F32), g[MLA_NOPE:][perm], zeros]) * scale
    return ga.reshape(1, HEAD_SLOT), gb.reshape(1, HEAD_SLOT)


def _rows3(m):
    return m.reshape(N_COND, 1, D_MODEL)


def kernel(x_prompt, x_sample, cache_ckv, cache_kpe, state_ret, c, c_ctx, mod_w, mod_b, norm1_w, norm2_w, mla_wq_a, mla_q_a_norm, mla_wq_b, mla_wkv_a, mla_kv_norm, mla_w_uk, mla_w_uv, mla_q_norm, mla_k_norm, mla_wo, ret_wq, ret_wk, ret_wv, ret_wg, ret_wo, ret_decay, ret_gn, ffn_w_gate, ffn_w_up, ffn_w_down, moe_router, moe_w_gate, moe_w_up, moe_w_down):
    x = jnp.concatenate([x_prompt.reshape(N_CTX, D_MODEL), x_sample.reshape(N_LAT, D_MODEL)], axis=0)
    cond = jnp.concatenate([c_ctx[None, :], c, jnp.zeros((N_COND - 1 - DEC_BATCH, D_MODEL), F32)], axis=0)
    mods = _adaln(cond, mod_w, mod_b).reshape(DEPTH, N_COND, 6, D_MODEL)

    perm = _swap_perm()
    c32, s32 = _rope_tables(MLA_ROPE)
    ca_lat, sb_lat = _head_slot_tables(c32, s32)
    ca_id, sb_id = _head_slot_tables(jnp.ones((1, MLA_ROPE), F32), jnp.zeros((1, MLA_ROPE), F32))
    ca_q = jnp.concatenate([jnp.broadcast_to(ca_id, (N_CTX, HEAD_SLOT)), jnp.tile(ca_lat, (DEC_BATCH, 1))], axis=0)
    sb_q = jnp.concatenate([jnp.broadcast_to(sb_id, (N_CTX, HEAD_SLOT)), jnp.tile(sb_lat, (DEC_BATCH, 1))], axis=0)
    lat_kv = DEC_SEQ + PAST_LEN
    ca_kl = jnp.tile(jnp.concatenate([ca_lat, jnp.broadcast_to(ca_id, (PAST_LEN, HEAD_SLOT))], axis=0), (DEC_BATCH, 1))
    sb_kl = jnp.tile(jnp.concatenate([sb_lat, jnp.broadcast_to(sb_id, (PAST_LEN, HEAD_SLOT))], axis=0), (DEC_BATCH, 1))
    ca_kc = jnp.broadcast_to(ca_id, (N_CTX, HEAD_SLOT))
    sb_kc = jnp.broadcast_to(sb_id, (N_CTX, HEAD_SLOT))
    rc256, rs256 = _rope_tables(RET_DK)

    moe_wg, moe_wu, moe_wd = moe_w_gate.astype(BF16), moe_w_up.astype(BF16), moe_w_down.astype(BF16)

    new_ckv, new_kpe, new_ret = [], [], None
    for i in range(DEPTH):
        j = i // 2
        m = mods[i]
        sh1, sc1, g1, sh2, sc2, g2 = (_rows3(m[:, t]) for t in range(6))
        nw1 = norm1_w[i].reshape(1, D_MODEL)
        nw2 = norm2_w[i].reshape(1, D_MODEL)
        if i % 2 == 0:
            wqb = mla_wq_b[j].reshape(MLA_Q_RANK, MLA_HEADS, MLA_QK)
            wqb = jnp.concatenate([wqb, wqb[:, :, MLA_NOPE:][:, :, perm]], axis=2)
            wqb = wqb.reshape(MLA_Q_RANK, MLA_HEADS * HEAD_SLOT).astype(BF16)
            wkv = mla_wkv_a[j]
            wkv = jnp.concatenate([wkv[:, :MLA_KV_RANK], jnp.zeros((D_MODEL, MLA_NOPE), F32),
                                   wkv[:, MLA_KV_RANK:], wkv[:, MLA_KV_RANK:][:, perm]], axis=1).astype(BF16)
            wuk = mla_w_uk[j].reshape(MLA_KV_RANK, MLA_HEADS, MLA_NOPE)
            wuk = jnp.concatenate([wuk, jnp.zeros_like(wuk)], axis=2)
            wuk = wuk.reshape(MLA_KV_RANK, MLA_HEADS * HEAD_SLOT).astype(BF16)
            gaq, gbq = _head_slot_gains(mla_q_norm[j], MLA_QK ** -0.5 * LOG2_E)
            gak, gbk = _head_slot_gains(mla_k_norm[j], 1.0)

            q, ckv, kpe = _mla_proj(x, nw1, sh1, sc1, mla_wq_a[j].astype(BF16),
                                    mla_q_a_norm[j].reshape(1, MLA_Q_RANK), wqb, wkv,
                                    mla_kv_norm[j].reshape(1, MLA_KV_RANK), gaq, gbq, ca_q, sb_q)
            new_ckv.append(ckv[:N_CTX].reshape(BATCH, SEQ, MLA_KV_RANK))
            new_kpe.append(kpe[:N_CTX, MLA_NOPE:MLA_QK].reshape(BATCH, SEQ, MLA_ROPE))

            kpe_x = cache_kpe[:, j]
            kpe_x = jnp.concatenate([jnp.zeros((DEC_BATCH, PAST_LEN, MLA_NOPE), F32), kpe_x, kpe_x[:, :, perm]], axis=2)
            ckv_l = jnp.concatenate([ckv[N_CTX:].reshape(DEC_BATCH, DEC_SEQ, MLA_KV_RANK), cache_ckv[:, j]], axis=1)
            kpe_l = jnp.concatenate([kpe[N_CTX:].reshape(DEC_BATCH, DEC_SEQ, HEAD_SLOT), kpe_x], axis=1)
            wuv = mla_w_uv[j].astype(BF16)
            k_c, v_c = _kv_expand(ckv[:N_CTX], kpe[:N_CTX], wuk, wuv, gak, gbk, ca_kc, sb_kc)
            k_l, v_l = _kv_expand(ckv_l.reshape(DEC_BATCH * lat_kv, MLA_KV_RANK),
                                  kpe_l.reshape(DEC_BATCH * lat_kv, HEAD_SLOT), wuk, wuv, gak, gbk, ca_kl, sb_kl)

            qw = MLA_HEADS * HEAD_SLOT
            o_c = _attention(q.reshape(N_TOK // SEQ, SEQ, qw), k_c.reshape(BATCH, SEQ, qw),
                             v_c.reshape(BATCH, SEQ, MLA_HEADS * MLA_V),
                             batch=BATCH, q_off=0, tq_total=SEQ, tq=SEQ, heads=MLA_HEADS)
            o_l = _attention(q.reshape(N_TOK // DEC_SEQ, DEC_SEQ, qw), k_l.reshape(DEC_BATCH, lat_kv, qw),
                             v_l.reshape(DEC_BATCH, lat_kv, MLA_HEADS * MLA_V),
                             batch=DEC_BATCH, q_off=N_CTX // DEC_SEQ, tq_total=DEC_SEQ, tq=512, heads=2)
            mix_c, mix_l = o_c.reshape(N_CTX, D_MODEL), o_l.reshape(N_LAT, D_MODEL)
            w_out = mla_wo[j].astype(BF16)
        else:
            w_in = jnp.concatenate([ret_wq[j], ret_wk[j], ret_wv[j], ret_wg[j]], axis=1).astype(BF16)
            qkvg = _ret_proj(x, nw1, sh1, sc1, w_in, rc256, rs256)
            log_gamma = -jnp.exp(ret_decay[j].astype(F32))
            gn = ret_gn[j].reshape(RET_HEADS, 1, RET_DV)
            width = qkvg.shape[1]
            y_c, new_ret = _retention(log_gamma, qkvg.reshape(N_TOK // SEQ, SEQ, width), gn, None,
                                      batch=BATCH, b_off=0, t=SEQ, layer=j, states=new_ret)
            y_l = _retention(log_gamma, qkvg.reshape(N_TOK // DEC_SEQ, DEC_SEQ, width), gn, state_ret,
                             batch=DEC_BATCH, b_off=N_CTX // DEC_SEQ, t=DEC_SEQ, layer=j)
            mix_c, mix_l = y_c.reshape(N_CTX, RET_HEADS * RET_DV), y_l.reshape(N_LAT, RET_HEADS * RET_DV)
            w_out = ret_wo[j].astype(BF16)

        if i % 2 == 0:
            x1, h2 = _proj_residual(mix_c, mix_l, w_out, x, g1, nw2, sh2, sc2)
            x = _ffn(h2, x1, g2, ffn_w_gate[j].astype(BF16), ffn_w_up[j].astype(BF16), ffn_w_down[j].astype(BF16))
        else:
            wr = jnp.pad(moe_router[j], ((0, 0), (0, 128 - N_EXPERTS)))
            wr_hi = wr.astype(BF16)
            wr_lo = (wr - wr_hi.astype(F32)).astype(BF16)
            x1, h2, logits = _proj_residual(mix_c, mix_l, w_out, x, g1, nw2, sh2, sc2, router_w=(wr_hi, wr_lo))
            top_w, routing = _route(logits[:, :N_EXPERTS])
            ya = _moe(routing, h2, moe_wg, moe_wu, moe_wd, j)
            x = _combine(ya, x1, g2, top_w, split=(i == DEPTH - 1))

    y_prompt = x[0].reshape(BATCH, SEQ, D_MODEL)
    y_sample = x[1].reshape(DEC_BATCH, DEC_SEQ, D_MODEL)
    return (y_prompt, y_sample, jnp.stack(new_ckv, axis=1), jnp.stack(new_kpe, axis=1), new_ret)
```

```python
import functools

import jax
import jax.numpy as jnp
import numpy as np
from jax import lax
from jax.experimental import pallas as pl
from jax.experimental.pallas import tpu as pltpu

F32 = jnp.float32
BF16 = jnp.bfloat16

D_MODEL = 1024
BATCH = 32
SEQ = 256
DEPTH = 4
DEC_BATCH = 4
DEC_SEQ = 2048
PAST_LEN = 512
GRID_W = 64
ROPE_BASE = 10000.0
NORM_EPS = 1e-6

MLA_HEADS = 16
MLA_NOPE = 64
MLA_ROPE = 32
MLA_QK = MLA_NOPE + MLA_ROPE
MLA_V = 64
MLA_Q_RANK = 384
MLA_KV_RANK = 256
HEAD_SLOT = 128

RET_HEADS = 4
RET_DK = 256
RET_DV = 512
RET_BLOCK = 256

FFN_DIM = 2816
N_EXPERTS = 8
TOP_K = 2
EXPERT_DIM = 3584

N_CTX = BATCH * SEQ
N_LAT = DEC_BATCH * DEC_SEQ
N_TOK = N_CTX + N_LAT
N_COND = 8

ROW_TILE = 512
CTX_TILES = N_CTX // ROW_TILE
TILES_PER_LAT_BATCH = DEC_SEQ // ROW_TILE

MOE_ROWS = 512
MOE_TILE = 1792
MOE_STEPS = EXPERT_DIM // MOE_TILE
MOE_STEP_ROWS = MOE_ROWS // MOE_STEPS
assert MOE_STEPS >= 2
N_ASSIGN = N_TOK * TOP_K
MOE_BLOCKS = N_ASSIGN // MOE_ROWS + N_EXPERTS

VMEM_LIMIT = 56 * 1024 * 1024
LOG2_E = 1.4426950408889634


def _params(*sem):
    return pltpu.CompilerParams(dimension_semantics=sem, vmem_limit_bytes=VMEM_LIMIT)


def _cond_of_tile(i):
    return jnp.where(i < CTX_TILES, 0, 1 + (i - CTX_TILES) // TILES_PER_LAT_BATCH)


def _dot(a, b):
    return jnp.dot(a, b, preferred_element_type=F32)


def _silu(x):
    return x * (1.0 / (1.0 + jnp.exp(-x)))


def _rms(x, w):
    return x * lax.rsqrt(jnp.mean(x * x, axis=-1, keepdims=True) + NORM_EPS) * w


def _modulate(x, nw, shift, scale):
    return _rms(x, nw) * (1.0 + scale) + shift


def _adaln_kernel(c_ref, w_ref, b_ref, o_ref):
    a = _silu(c_ref[...]).astype(BF16)
    o_ref[...] = _dot(a, w_ref[...].astype(BF16)) + b_ref[...]


def _adaln(cond, mod_w, mod_b):
    tn = 1024
    return pl.pallas_call(
        _adaln_kernel,
        grid=(DEPTH, 6 * D_MODEL // tn),
        in_specs=[
            pl.BlockSpec((N_COND, D_MODEL), lambda l, j: (0, 0)),
            pl.BlockSpec((None, D_MODEL, tn), lambda l, j: (l, 0, j)),
            pl.BlockSpec((None, 1, tn), lambda l, j: (l, 0, j)),
        ],
        out_specs=pl.BlockSpec((None, N_COND, tn), lambda l, j: (l, 0, j)),
        out_shape=jax.ShapeDtypeStruct((DEPTH, N_COND, 6 * D_MODEL), F32),
        compiler_params=_params("parallel", "parallel"),
        name="adaln",
    )(cond, mod_w, mod_b.reshape(DEPTH, 1, 6 * D_MODEL))


SLOT_GROUP = HEAD_SLOT // MLA_ROPE


def _heads_norm_rope(xs, swaps, ga, gb):
    ss = [jnp.sum(x * x, axis=-1, keepdims=True) for x in xs]
    rs = [lax.rsqrt(s * (1.0 / MLA_QK) + NORM_EPS) for s in ss]
    out = []
    for hd, (r, x, sw) in enumerate(zip(rs, xs, swaps)):
        lanes = slice((hd % SLOT_GROUP) * HEAD_SLOT, (hd % SLOT_GROUP + 1) * HEAD_SLOT)
        out.append(r * (x * ga[:, lanes] + sw * gb[:, lanes]))
    return out


def _split_rows(x):
    if isinstance(x, tuple):
        return x[0], x[1], 0
    return x, x, CTX_TILES


def _split_specs(width, lat_tile0, tm=ROW_TILE):
    return [pl.BlockSpec((tm, width), lambda i: (jnp.minimum(i, CTX_TILES - 1), 0)),
            pl.BlockSpec((tm, width), lambda i: (jnp.maximum(i - CTX_TILES, 0) + lat_tile0, 0))]


def _pick_rows(ctx_ref, lat_ref):
    return jnp.where(pl.program_id(0) < CTX_TILES, ctx_ref[...], lat_ref[...])


def _mla_proj_kernel(xc_ref, xl_ref, nw_ref, sh_ref, sc_ref, wqa_ref, qan_ref, wqb_ref, wkv_ref, kvn_ref,
                     ga_ref, gb_ref, ca_ref, sb_ref, q_ref, ckv_ref, kpe_ref):
    h = _modulate(_pick_rows(xc_ref, xl_ref), nw_ref[...], sh_ref[...], sc_ref[...]).astype(BF16)
    qa = _rms(_dot(h, wqa_ref[...]), qan_ref[...]).astype(BF16)
    q = _dot(qa, wqb_ref[...])
    kv = _dot(h, wkv_ref[...])
    ckv_ref[...] = _rms(kv[:, :MLA_KV_RANK], kvn_ref[...])
    kpe_ref[...] = kv[:, MLA_KV_RANK:]
    ga = ga_ref[...] * ca_ref[...]
    gb = gb_ref[...] * sb_ref[...]
    slots = [slice(hd * HEAD_SLOT, (hd + 1) * HEAD_SLOT) for hd in range(MLA_HEADS)]
    swap0 = MLA_HEADS * HEAD_SLOT
    swaps = [q[:, swap0 + (hd // SLOT_GROUP) * HEAD_SLOT:swap0 + (hd // SLOT_GROUP + 1) * HEAD_SLOT]
             for hd in range(MLA_HEADS)]
    for sl, y in zip(slots, _heads_norm_rope([q[:, sl] for sl in slots], swaps, ga, gb)):
        q_ref[:, sl] = y.astype(BF16)


def _mla_proj(x, nw, shift, scale, wqa, qan, wqb, wkv, kvn, ga, gb, ca, sb, tab_index):
    tm = ROW_TILE
    row = lambda i: (i, 0)
    fixed = lambda i: (0, 0)
    cond = lambda i: (_cond_of_tile(i), 0, 0)
    qw = MLA_HEADS * HEAD_SLOT
    tabw = SLOT_GROUP * HEAD_SLOT
    kpew = 2 * HEAD_SLOT
    tab = lambda i: (tab_index(i), 0)
    x_ctx, x_lat, lat_tile0 = _split_rows(x)
    return pl.pallas_call(
        _mla_proj_kernel,
        grid=(N_TOK // tm,),
        in_specs=_split_specs(D_MODEL, lat_tile0) + [
            pl.BlockSpec((1, D_MODEL), fixed),
            pl.BlockSpec((None, 1, D_MODEL), cond),
            pl.BlockSpec((None, 1, D_MODEL), cond),
            pl.BlockSpec((D_MODEL, MLA_Q_RANK), fixed),
            pl.BlockSpec((1, MLA_Q_RANK), fixed),
            pl.BlockSpec((MLA_Q_RANK, wqb.shape[1]), fixed),
            pl.BlockSpec((D_MODEL, MLA_KV_RANK + kpew), fixed),
            pl.BlockSpec((1, MLA_KV_RANK), fixed),
            pl.BlockSpec((1, tabw), fixed),
            pl.BlockSpec((1, tabw), fixed),
            pl.BlockSpec((tm, tabw), tab),
            pl.BlockSpec((tm, tabw), tab),
        ],
        out_specs=[
            pl.BlockSpec((tm, qw), row),
            pl.BlockSpec((tm, MLA_KV_RANK), row),
            pl.BlockSpec((tm, kpew), row),
        ],
        out_shape=[
            jax.ShapeDtypeStruct((N_TOK, qw), BF16),
            jax.ShapeDtypeStruct((N_TOK, MLA_KV_RANK), F32),
            jax.ShapeDtypeStruct((N_TOK, kpew), F32),
        ],
        compiler_params=_params("arbitrary"),
        name="mla_proj",
    )(x_ctx, x_lat, nw, shift, scale, wqa, qan, wqb, wkv, kvn, ga, gb, ca, sb)


def _kv_expand_kernel(ckv_ref, kpe_ref, wuk_ref, wuv_ref, ga_ref, gb_ref, ca_ref, sb_ref, k_ref, v_ref):
    c = ckv_ref[...].astype(BF16)
    kn = _dot(c, wuk_ref[...])
    v_ref[...] = _dot(c, wuv_ref[...]).astype(BF16)
    kpe = kpe_ref[:, :HEAD_SLOT]
    swap = kpe_ref[:, HEAD_SLOT:]
    ga = ga_ref[...] * ca_ref[...]
    gb = gb_ref[...] * sb_ref[...]
    group = lax.broadcasted_iota(jnp.int32, kpe.shape, 1) // MLA_ROPE
    kpe_at = [jnp.where(group == g, kpe, 0.0) for g in range(SLOT_GROUP)]
    slots = [slice(hd * HEAD_SLOT, (hd + 1) * HEAD_SLOT) for hd in range(MLA_HEADS)]
    xs = [kn[:, sl] + kpe_at[hd % SLOT_GROUP] for hd, sl in enumerate(slots)]
    for sl, y in zip(slots, _heads_norm_rope(xs, [swap] * MLA_HEADS, ga, gb)):
        k_ref[:, sl] = y.astype(BF16)


def _kv_expand(ckv, kpe, wuk, wuv, ga, gb, ca, sb, tab_index):
    n = ckv.shape[0]
    tm = ROW_TILE
    row = lambda i: (i, 0)
    fixed = lambda i: (0, 0)
    tab = lambda i: (tab_index(i), 0)
    kw = MLA_HEADS * HEAD_SLOT
    vw = MLA_HEADS * MLA_V
    tabw = SLOT_GROUP * HEAD_SLOT
    return pl.pallas_call(
        _kv_expand_kernel,
        grid=(n // tm,),
        in_specs=[
            pl.BlockSpec((tm, MLA_KV_RANK), row),
            pl.BlockSpec((tm, 2 * HEAD_SLOT), row),
            pl.BlockSpec((MLA_KV_RANK, kw), fixed),
            pl.BlockSpec((MLA_KV_RANK, vw), fixed),
            pl.BlockSpec((1, tabw), fixed),
            pl.BlockSpec((1, tabw), fixed),
            pl.BlockSpec((tm, tabw), tab),
            pl.BlockSpec((tm, tabw), tab),
        ],
        out_specs=[pl.BlockSpec((tm, kw), row), pl.BlockSpec((tm, vw), row)],
        out_shape=[jax.ShapeDtypeStruct((n, kw), BF16), jax.ShapeDtypeStruct((n, vw), BF16)],
        compiler_params=_params("parallel"),
        name="kv_expand",
    )(ckv, kpe, wuk, wuv, ga, gb, ca, sb)


def _attn_kernel(q_ref, k_ref, v_ref, o_ref, *, heads):
    scores = []
    for hd in range(heads):
        qh = q_ref[:, hd * HEAD_SLOT:(hd + 1) * HEAD_SLOT]
        kh = k_ref[:, hd * HEAD_SLOT:(hd + 1) * HEAD_SLOT]
        scores.append(lax.dot_general(qh, kh, (((1,), (1,)), ((), ())), preferred_element_type=F32))
    probs, invs = [], []
    for s in scores:
        p = jnp.exp2(s - jnp.max(s, axis=-1, keepdims=True))
        invs.append(1.0 / jnp.sum(p, axis=-1, keepdims=True))
        probs.append(p.astype(BF16))
    for hd in range(heads):
        o = _dot(probs[hd], v_ref[:, hd * MLA_V:(hd + 1) * MLA_V])
        o_ref[:, hd * MLA_V:(hd + 1) * MLA_V] = (o * invs[hd]).astype(BF16)


def _attention(q, k, v, *, batch, q_off, tq_total, tq, heads):
    tk = k.shape[1]
    groups = MLA_HEADS // heads
    return pl.pallas_call(
        functools.partial(_attn_kernel, heads=heads),
        grid=(batch, groups, tq_total // tq),
        in_specs=[
            pl.BlockSpec((None, tq, heads * HEAD_SLOT), lambda b, g, i: (b + q_off, i, g)),
            pl.BlockSpec((None, tk, heads * HEAD_SLOT), lambda b, g, i: (b, 0, g)),
            pl.BlockSpec((None, tk, heads * MLA_V), lambda b, g, i: (b, 0, g)),
        ],
        out_specs=pl.BlockSpec((None, tq, heads * MLA_V), lambda b, g, i: (b, i, g)),
        out_shape=jax.ShapeDtypeStruct((batch, tq_total, MLA_HEADS * MLA_V), BF16),
        compiler_params=_params("parallel", "parallel", "parallel"),
        name="attention",
    )(q, k, v)


LANE_CHUNKS = D_MODEL // 128


def _store_row_major(ref, x):
    rows = x.shape[0]
    for c in range(LANE_CHUNKS):
        ref[pl.ds(c, rows, stride=LANE_CHUNKS), :] = x[:, c * 128:(c + 1) * 128]


def _load_row_major(ref, rows, c):
    return ref[pl.ds(c, rows, stride=LANE_CHUNKS), :]


def _proj_residual_kernel(ac_ref, al_ref, w_ref, xc_ref, xl_ref, g_ref, nw_ref, sh_ref, sc_ref, *rest, router):
    if router:
        rhi_ref, rlo_ref, x1_ref, h2_ref, lg_ref, y_sc = rest
    else:
        x1_ref, h2_ref, y_sc = rest
    i = pl.program_id(0)

    @pl.when(i < CTX_TILES)
    def _():
        y_sc[...] = _dot(ac_ref[...], w_ref[...])

    @pl.when(i >= CTX_TILES)
    def _():
        y_sc[...] = _dot(al_ref[...], w_ref[...])

    x1 = _pick_rows(xc_ref, xl_ref) + g_ref[...] * y_sc[...]
    h2 = _modulate(x1, nw_ref[...], sh_ref[...], sc_ref[...])
    if router:
        hi = h2.astype(BF16)
        lo = (h2 - hi.astype(F32)).astype(BF16)
        lg_ref[...] = _dot(hi, rhi_ref[...]) + (_dot(lo, rhi_ref[...]) + _dot(hi, rlo_ref[...]))
        _store_row_major(h2_ref, h2)
    else:
        h2_ref[...] = h2.astype(h2_ref.dtype)
    x1_ref[...] = x1


def _proj_residual(a_ctx, a_lat, w, x, gate, nw, shift, scale, router_w=None):
    tm = ROW_TILE
    kdim = w.shape[0]
    row = lambda i: (i, 0)
    fixed = lambda i: (0, 0)
    cond = lambda i: (_cond_of_tile(i), 0, 0)
    router = router_w is not None
    x_ctx, x_lat, lat_tile0 = _split_rows(x)
    in_specs = _split_specs(kdim, 0) + [pl.BlockSpec((kdim, D_MODEL), fixed)] + _split_specs(D_MODEL, lat_tile0) + [
        pl.BlockSpec((None, 1, D_MODEL), cond),
        pl.BlockSpec((1, D_MODEL), fixed),
        pl.BlockSpec((None, 1, D_MODEL), cond),
        pl.BlockSpec((None, 1, D_MODEL), cond),
    ]
    if router:
        h2_spec = pl.BlockSpec((tm * LANE_CHUNKS, 128), row)
        h2_shape = jax.ShapeDtypeStruct((N_TOK * LANE_CHUNKS, 128), F32)
    else:
        h2_spec = pl.BlockSpec((tm, D_MODEL), row)
        h2_shape = jax.ShapeDtypeStruct((N_TOK, D_MODEL), BF16)
    out_specs = [pl.BlockSpec((tm, D_MODEL), row), h2_spec]
    out_shape = [jax.ShapeDtypeStruct((N_TOK, D_MODEL), F32), h2_shape]
    args = [a_ctx, a_lat, w, x_ctx, x_lat, gate, nw, shift, scale]
    if router:
        in_specs += [pl.BlockSpec((D_MODEL, 128), fixed), pl.BlockSpec((D_MODEL, 128), fixed)]
        out_specs.append(pl.BlockSpec((tm, 128), row))
        out_shape.append(jax.ShapeDtypeStruct((N_TOK, 128), F32))
        args += list(router_w)
    return pl.pallas_call(
        functools.partial(_proj_residual_kernel, router=router),
        grid=(N_TOK // tm,),
        in_specs=in_specs,
        out_specs=out_specs,
        out_shape=out_shape,
        scratch_shapes=[pltpu.VMEM((tm, D_MODEL), F32)],
        compiler_params=_params("arbitrary"),
        name="proj_residual",
    )(*args)


def _swiglu_partial(x, wg, wu, wd):
    g = _dot(x, wg)
    u = _dot(x, wu)
    return _dot((_silu(g) * u).astype(BF16), wd)


def _ffn_kernel(h_ref, x1_ref, g_ref, wg_ref, wu_ref, wd_ref, o_ref):
    part = _swiglu_partial(h_ref[...], wg_ref[...], wu_ref[...], wd_ref[...])
    o_ref[...] = x1_ref[...] + g_ref[...] * part


def _resident(shape):
    return pl.BlockSpec(shape, lambda *_: (0,) * len(shape), pipeline_mode=pl.Buffered(1))


def _ffn(h2, x1, gate, wg, wu, wd):
    tm = ROW_TILE
    row = lambda i: (i, 0)
    return pl.pallas_call(
        _ffn_kernel,
        grid=(N_TOK // tm,),
        in_specs=[
            pl.BlockSpec((tm, D_MODEL), row),
            pl.BlockSpec((tm, D_MODEL), row),
            pl.BlockSpec((None, 1, D_MODEL), lambda i: (_cond_of_tile(i), 0, 0)),
            _resident((D_MODEL, FFN_DIM)),
            _resident((D_MODEL, FFN_DIM)),
            _resident((FFN_DIM, D_MODEL)),
        ],
        out_specs=pl.BlockSpec((tm, D_MODEL), row),
        out_shape=jax.ShapeDtypeStruct((N_TOK, D_MODEL), F32),
        compiler_params=_params("parallel"),
        name="ffn",
    )(h2, x1, gate, wg, wu, wd)


def _moe_kernel(blk_e_ref, nact_ref, order_ref, base_ref, x_hbm, wg_ref, wu_ref, wd_ref, ya_hbm,
                xrows0, xrows1, xb, acc, stage0, stage1, gsem, ssem):
    b = pl.program_id(0)
    f = pl.program_id(1)
    nact = nact_ref[0]
    xrows = (xrows0, xrows1)
    stage = (stage0, stage1)
    tile = lambda r: pl.ds(pl.multiple_of(r * LANE_CHUNKS, LANE_CHUNKS), LANE_CHUNKS)

    def row_ids(blk):
        base = base_ref[blk + 1]
        return lambda r: order_ref[base + r]

    def gather_copy(par, step, r, a):
        tok = a & (N_TOK - 1)
        return pltpu.make_async_copy(x_hbm.at[tok], xrows[par].at[step, tile(r), :], gsem.at[par])

    def scatter_copy(par, step, r, a):
        return pltpu.make_async_copy(stage[par].at[step, tile(r), :], ya_hbm.at[a], ssem.at[par])

    def wait_gather(par):
        pltpu.make_async_copy(xrows[par], xrows[par], gsem.at[par]).wait()

    def wait_scatter(par):
        pltpu.make_async_copy(stage[par], stage[par], ssem.at[par]).wait()

    @pl.when((b == 0) & (f == 0))
    def _():
        stage1[...] = jnp.zeros_like(stage1)
        ids = row_ids(0)

        def issue(r, c):
            gather_copy(0, r // MOE_STEP_ROWS, r % MOE_STEP_ROWS, ids(r)).start()
            return c

        lax.fori_loop(0, MOE_ROWS, issue, 0, unroll=8)

    def block_body(par):
        @pl.when(f == 0)
        def _():
            wait_gather(par)

            @pl.when(b >= 1)
            def _():
                wait_scatter(par)

            for step in range(MOE_STEPS):
                for c in range(LANE_CHUNKS):
                    xb[step * MOE_STEP_ROWS:(step + 1) * MOE_STEP_ROWS, c * 128:(c + 1) * 128] = (
                        _load_row_major(xrows[par].at[step], MOE_STEP_ROWS, c).astype(BF16))

        nxt, prv = row_ids(b + 1), row_ids(b - 1)
        row0 = f * MOE_STEP_ROWS
        for r in range(MOE_STEP_ROWS):
            gather_copy(1 - par, f, r, nxt(row0 + r)).start()
            scatter_copy(1 - par, f, r, prv(row0 + r)).start()
        part = _swiglu_partial(xb[...], wg_ref[...], wu_ref[...], wd_ref[...])

        @pl.when(f == 0)
        def _():
            acc[...] = part

        @pl.when((f > 0) & (f < MOE_STEPS - 1))
        def _():
            acc[...] += part

        @pl.when(f == MOE_STEPS - 1)
        def _():
            total = acc[...] + part
            for step in range(MOE_STEPS):
                _store_row_major(stage[par].at[step], total[step * MOE_STEP_ROWS:(step + 1) * MOE_STEP_ROWS])

    def drain(par):
        wait_scatter(par)
        ids = row_ids(b - 1)

        def issue(r, c):
            scatter_copy(1 - par, r // MOE_STEP_ROWS, r % MOE_STEP_ROWS, ids(r)).start()
            return c

        lax.fori_loop(0, MOE_ROWS, issue, 0, unroll=8)
        wait_gather(par)
        wait_scatter(1 - par)

    for par in range(2):
        @pl.when((b < nact) & (b % 2 == par))
        def _():
            block_body(par)

        @pl.when((b == nact) & (f == 0) & (b % 2 == par))
        def _():
            drain(par)


def _moe(routing, h2, wg, wu, wd, layer):
    tf = MOE_TILE
    wmap = lambda b, f, e, *_: (layer, e[b], 0, f)
    rows_buf = pltpu.VMEM((MOE_STEPS, MOE_STEP_ROWS * LANE_CHUNKS, 128), F32)
    grid_spec = pltpu.PrefetchScalarGridSpec(
        num_scalar_prefetch=len(routing),
        grid=(MOE_BLOCKS, MOE_STEPS),
        in_specs=[
            pl.BlockSpec(memory_space=pl.ANY),
            pl.BlockSpec((None, None, D_MODEL, tf), wmap),
            pl.BlockSpec((None, None, D_MODEL, tf), wmap),
            pl.BlockSpec((None, None, tf, D_MODEL), lambda b, f, e, *_: (layer, e[b], f, 0)),
        ],
        out_specs=pl.BlockSpec(memory_space=pl.ANY),
        scratch_shapes=[
            rows_buf,
            rows_buf,
            pltpu.VMEM((MOE_ROWS, D_MODEL), BF16),
            pltpu.VMEM((MOE_ROWS, D_MODEL), F32),
            rows_buf,
            rows_buf,
            pltpu.SemaphoreType.DMA((2,)),
            pltpu.SemaphoreType.DMA((2,)),
        ],
    )
    return pl.pallas_call(
        _moe_kernel,
        grid_spec=grid_spec,
        out_shape=jax.ShapeDtypeStruct((N_ASSIGN + MOE_ROWS, LANE_CHUNKS, 128), F32),
        compiler_params=_params("arbitrary", "arbitrary"),
        name="moe",
    )(*routing, h2.reshape(N_TOK, LANE_CHUNKS, 128), wg, wu, wd)


def _combine_kernel(y0_ref, y1_ref, x1_ref, g_ref, w_ref, *o_refs):
    w = w_ref[...]
    rows = x1_ref.shape[0]

    def emit(o_ref):
        for c in range(LANE_CHUNKS):
            sl = slice(c * 128, (c + 1) * 128)
            y = w[:, 0:1] * _load_row_major(y0_ref, rows, c) + w[:, 1:2] * _load_row_major(y1_ref, rows, c)
            o_ref[:, sl] = x1_ref[:, sl] + g_ref[:, sl] * y

    if len(o_refs) == 1:
        emit(o_refs[0])
    else:
        i = pl.program_id(0)

        @pl.when(i < CTX_TILES)
        def _():
            emit(o_refs[0])

        @pl.when(i >= CTX_TILES)
        def _():
            emit(o_refs[1])


def _combine(ya, x1, gate, top_w, split):
    tm = ROW_TILE
    row = lambda i: (i, 0)
    ya2 = ya.reshape(ya.shape[0] * LANE_CHUNKS, 128)
    if split:
        out_specs = [pl.BlockSpec((tm, D_MODEL), lambda i: (jnp.minimum(i, CTX_TILES - 1), 0)),
                     pl.BlockSpec((tm, D_MODEL), lambda i: (jnp.maximum(i - CTX_TILES, 0), 0))]
        out_shape = [jax.ShapeDtypeStruct((N_CTX, D_MODEL), F32), jax.ShapeDtypeStruct((N_LAT, D_MODEL), F32)]
    else:
        out_specs = pl.BlockSpec((tm, D_MODEL), row)
        out_shape = jax.ShapeDtypeStruct((N_TOK, D_MODEL), F32)
    return pl.pallas_call(
        _combine_kernel,
        grid=(N_TOK // tm,),
        in_specs=[
            pl.BlockSpec((tm * LANE_CHUNKS, 128), row),
            pl.BlockSpec((tm * LANE_CHUNKS, 128), lambda i: (i + N_TOK // tm, 0)),
            pl.BlockSpec((tm, D_MODEL), row),
            pl.BlockSpec((None, 1, D_MODEL), lambda i: (_cond_of_tile(i), 0, 0)),
            pl.BlockSpec((tm, TOP_K), row),
        ],
        out_specs=out_specs,
        out_shape=out_shape,
        compiler_params=_params("arbitrary"),
        name="moe_combine",
    )(ya2, ya2, x1, gate, top_w)


def _route(logits):
    top_v, top_i = lax.top_k(logits, TOP_K)
    top_w = jax.nn.softmax(top_v, axis=-1)
    e = top_i.reshape(N_ASSIGN).astype(jnp.int32)
    a = jnp.arange(N_ASSIGN, dtype=jnp.int32)
    row_id = (a % TOP_K) * N_TOK + a // TOP_K
    order = jnp.sort(e * N_ASSIGN + row_id) & (N_ASSIGN - 1)
    order = jnp.concatenate([order, N_ASSIGN + jnp.arange(MOE_ROWS, dtype=jnp.int32)])
    counts = jnp.sum((e[:, None] == jnp.arange(N_EXPERTS, dtype=jnp.int32)[None, :]).astype(jnp.int32), axis=0)
    start = jnp.cumsum(counts) - counts
    nblk = (counts + MOE_ROWS - 1) // MOE_ROWS
    blk_end = jnp.cumsum(nblk)
    blk = jnp.arange(-1, MOE_BLOCKS + 1, dtype=jnp.int32)
    blk_e = jnp.sum((blk[:, None] >= blk_end[None, :]).astype(jnp.int32), axis=1)
    blk_e = jnp.minimum(blk_e, N_EXPERTS - 1)
    within = (blk - (blk_end - nblk)[blk_e]) * MOE_ROWS
    real = (blk >= 0) & (blk < blk_end[-1])
    blk_base = jnp.where(real, start[blk_e] + within, N_ASSIGN).astype(jnp.int32)
    nact = blk_end[-1:].astype(jnp.int32)
    return top_w, (blk_e[1:-1], nact, order, blk_base)


RET_QK_W = RET_HEADS * RET_DK
RET_V_W = RET_HEADS * RET_DV
RET_COL_TILE = 1024


def _ret_proj_kernel(x_ref, nw_ref, sh_ref, sc_ref, w_ref, c_ref, s_ref, o_ref):
    def body(rope):
        h = _modulate(x_ref[...], nw_ref[...], sh_ref[...], sc_ref[...]).astype(BF16)
        for c0 in range(0, o_ref.shape[1], RET_COL_TILE):
            acc = _dot(h, w_ref[:, c0:c0 + RET_COL_TILE])
            if c0 >= 2 * RET_QK_W + RET_V_W:
                o_ref[:, c0:c0 + RET_COL_TILE] = _silu(acc).astype(BF16)
            elif c0 >= 2 * RET_QK_W:
                o_ref[:, c0:c0 + RET_COL_TILE] = acc.astype(BF16)
            else:
                x = acc * (RET_DK ** -0.5) if c0 >= RET_QK_W else acc
                if not rope:
                    o_ref[:, c0:c0 + RET_COL_TILE] = x.astype(BF16)
                    continue
                for g in range(RET_COL_TILE // 128):
                    tl = slice((g % 2) * 128, (g % 2 + 1) * 128)
                    xs = x[:, g * 128:(g + 1) * 128]
                    y = xs * c_ref[:, tl] + pltpu.roll(xs, RET_DK // 4, 1) * s_ref[:, tl]
                    o_ref[:, c0 + g * 128:c0 + (g + 1) * 128] = y.astype(BF16)

    i = pl.program_id(0)

    @pl.when(i < CTX_TILES)
    def _():
        body(False)

    @pl.when(i >= CTX_TILES)
    def _():
        body(True)


def _ret_proj(x, nw, shift, scale, w, rope_c, rope_s):
    tm = ROW_TILE
    width = w.shape[1]
    row = lambda i: (i, 0)
    cond = lambda i: (_cond_of_tile(i), 0, 0)
    tab = lambda i: (jnp.where(i < CTX_TILES, 0, (i - CTX_TILES) % TILES_PER_LAT_BATCH), 0)
    return pl.pallas_call(
        _ret_proj_kernel,
        grid=(N_TOK // tm,),
        in_specs=[
            pl.BlockSpec((tm, D_MODEL), row),
            pl.BlockSpec((1, D_MODEL), lambda i: (0, 0)),
            pl.BlockSpec((None, 1, D_MODEL), cond),
            pl.BlockSpec((None, 1, D_MODEL), cond),
            _resident((D_MODEL, width)),
            pl.BlockSpec((tm, RET_DK), tab),
            pl.BlockSpec((tm, RET_DK), tab),
        ],
        out_specs=pl.BlockSpec((tm, width), row),
        out_shape=jax.ShapeDtypeStruct((N_TOK, width), BF16),
        compiler_params=_params("parallel"),
        name="ret_proj",
    )(x, nw, shift, scale, w, rope_c, rope_s)


def _retention_kernel(lg_ref, q_ref, k_ref, v_ref, sg_ref, gn_ref, *rest, zero_init, n_chunks, layer):
    if zero_init:
        y_ref, st_ref, o_acc, state, dmask = rest[-5:]
    else:
        s0f_ref, s0b_ref, y_ref, o_acc, state, dmask = rest
    hd = pl.program_id(1)
    c = RET_BLOCK
    ri = lax.broadcasted_iota(jnp.int32, (c, c), 0).astype(F32)
    ci = lax.broadcasted_iota(jnp.int32, (c, c), 1).astype(F32)
    pos = lax.broadcasted_iota(jnp.int32, (c, 1), 0).astype(F32)
    q_decay, k_decay, c_decay = [], [], []
    for d in range(2):
        lg = lg_ref[d, hd]
        rel = (ri - ci) if d == 0 else (ci - ri)
        dmask[d] = jnp.where(rel >= 0, jnp.exp(jnp.maximum(rel, 0.0) * lg), 0.0)
        q_decay.append(jnp.exp(((pos + 1.0) if d == 0 else (c - pos)) * lg))
        k_decay.append(jnp.exp(((c - 1.0 - pos) if d == 0 else pos) * lg))
        c_decay.append(jnp.exp(jnp.full((1, 1), c, F32) * lg))
        if zero_init:
            state[d] = jnp.zeros(state.shape[1:], F32)
        else:
            state[d] = (s0f_ref if d == 0 else s0b_ref)[...]

    def step(t, carry):
        for d in range(2):
            ch = t if d == 0 else n_chunks - 1 - t
            rows = pl.ds(pl.multiple_of(ch * c, c), c)
            qc = q_ref[rows, :]
            kc = k_ref[rows, :]
            vc = v_ref[rows, :]
            a = lax.dot_general(qc, kc, (((1,), (1,)), ((), ())), preferred_element_type=F32) * dmask[d]
            s = state[d]
            o_acc[d, rows, :] = (_dot(a.astype(BF16), vc)
                                 + _dot((qc.astype(F32) * q_decay[d]).astype(BF16), s.astype(BF16)))
            kd = (kc.astype(F32) * k_decay[d]).astype(BF16)
            state[d] = s * c_decay[d] + lax.dot_general(kd, vc, (((0,), (0,)), ((), ())),
                                                         preferred_element_type=F32)
        return carry

    lax.fori_loop(0, n_chunks, step, 0)

    if zero_init:
        for d in range(2):
            if st_ref.ndim == 3:
                st_ref[d] = state[d]
            else:
                st_ref[layer, d] = state[d]
                for other in range(st_ref.shape[0]):
                    if other != layer:
                        st_ref[other, d] = jnp.zeros(state.shape[1:], F32)

    y = _rms(o_acc[0] + o_acc[1], gn_ref[...])
    y_ref[...] = (sg_ref[...].astype(F32) * y).astype(BF16)


def _retention(log_gamma, qkvg, gn, state0, *, batch, b_off, t, layer, states=None):
    zero_init = state0 is None
    qblk = RET_HEADS * RET_DK // RET_DK
    vblk = 2 * RET_HEADS * RET_DK // RET_DV
    gblk = vblk + RET_HEADS
    in_specs = [
        pl.BlockSpec(memory_space=pltpu.SMEM),
        pl.BlockSpec((None, t, RET_DK), lambda b, h: (b + b_off, 0, h)),
        pl.BlockSpec((None, t, RET_DK), lambda b, h: (b + b_off, 0, qblk + h)),
        pl.BlockSpec((None, t, RET_DV), lambda b, h: (b + b_off, 0, vblk + h)),
        pl.BlockSpec((None, t, RET_DV), lambda b, h: (b + b_off, 0, gblk + h)),
        pl.BlockSpec((None, 1, RET_DV), lambda b, h: (h, 0, 0)),
    ]
    args = [log_gamma, qkvg, qkvg, qkvg, qkvg, gn]
    y_spec = pl.BlockSpec((None, t, RET_DV), lambda b, h: (b, 0, h))
    y_shape = jax.ShapeDtypeStruct((batch, t, RET_HEADS * RET_DV), BF16)
    aliases = {}
    if zero_init:
        n_ret = DEPTH // 2
        st_shape = jax.ShapeDtypeStruct((batch, n_ret, 2, RET_HEADS, RET_DK, RET_DV), F32)
        if states is None:
            st_spec = pl.BlockSpec((None, n_ret, 2, None, RET_DK, RET_DV), lambda b, h: (b, 0, 0, h, 0, 0))
        else:
            st_spec = pl.BlockSpec((None, None, 2, None, RET_DK, RET_DV), lambda b, h: (b, layer, 0, h, 0, 0))
            aliases = {len(args): 1}
            in_specs.append(pl.BlockSpec(memory_space=pl.ANY))
            args.append(states)
        out_specs = [y_spec, st_spec]
        out_shape = [y_shape, st_shape]
    else:
        in_specs += [
            pl.BlockSpec((None, None, None, None, RET_DK, RET_DV), lambda b, h: (b, layer, 0, h, 0, 0)),
            pl.BlockSpec((None, None, None, None, RET_DK, RET_DV), lambda b, h: (b, layer, 1, h, 0, 0))]
        args += [state0, state0]
        out_specs = y_spec
        out_shape = y_shape
    return pl.pallas_call(
        functools.partial(_retention_kernel, zero_init=zero_init, n_chunks=t // RET_BLOCK, layer=layer),
        grid=(batch, RET_HEADS),
        in_specs=in_specs,
        out_specs=out_specs,
        out_shape=out_shape,
        input_output_aliases=aliases,
        scratch_shapes=[pltpu.VMEM((2, t, RET_DV), F32), pltpu.VMEM((2, RET_DK, RET_DV), F32),
                        pltpu.VMEM((2, RET_BLOCK, RET_BLOCK), F32)],
        compiler_params=_params("parallel", "parallel"),
        name="retention",
    )(*args)


def _rope_tables(rot_dim):
    rows = DEC_SEQ // GRID_W
    row = jnp.repeat(jnp.arange(rows), GRID_W)
    col = jnp.tile(jnp.arange(GRID_W), rows)
    nf = rot_dim // 4
    inv = ROPE_BASE ** (-jnp.arange(nf, dtype=F32) / nf)
    ang = jnp.stack([row, col], axis=-1).astype(F32)[:, :, None] * inv
    cos, sin = jnp.cos(ang), jnp.sin(ang)
    c = jnp.stack([cos, cos], axis=2).reshape(DEC_SEQ, rot_dim)
    s = jnp.stack([-sin, sin], axis=2).reshape(DEC_SEQ, rot_dim)
    return c, s


def _swap_perm():
    nf = MLA_ROPE // 4
    return jnp.arange(MLA_ROPE) ^ nf


def _slot_sources():
    src = np.full((SLOT_GROUP, HEAD_SLOT), -1, np.int32)
    for g in range(SLOT_GROUP):
        rope_lanes = MLA_ROPE * g + np.arange(MLA_ROPE)
        other = np.setdiff1d(np.arange(HEAD_SLOT), rope_lanes)
        src[g, other[:MLA_NOPE]] = np.arange(MLA_NOPE)
        src[g, rope_lanes] = MLA_NOPE + np.arange(MLA_ROPE)
    return src


def _to_head_slots(w):
    src = np.tile(_slot_sources(), (MLA_HEADS // SLOT_GROUP, 1))
    idx = jnp.broadcast_to(jnp.asarray(np.maximum(src, 0)), w.shape[:-1] + (HEAD_SLOT,))
    return jnp.take_along_axis(w, idx, axis=-1) * jnp.asarray(src >= 0, w.dtype)


def _head_slot_tables(c32, s32):
    t = c32.shape[0]
    ca = jnp.ones((t, SLOT_GROUP, SLOT_GROUP, MLA_ROPE), F32)
    sb = jnp.zeros((t, SLOT_GROUP, SLOT_GROUP, MLA_ROPE), F32)
    for g in range(SLOT_GROUP):
        ca = ca.at[:, g, g].set(c32)
        sb = sb.at[:, g, g].set(s32)
    return ca.reshape(t, SLOT_GROUP * HEAD_SLOT), sb.reshape(t, SLOT_GROUP * HEAD_SLOT)


def _head_slot_gains(g, scale):
    src = _slot_sources()
    ga = g[np.maximum(src, 0)] * jnp.asarray(src >= 0, F32) * scale
    gb = jnp.zeros((SLOT_GROUP, SLOT_GROUP, MLA_ROPE), F32)
    for s in range(SLOT_GROUP):
        gb = gb.at[s, s].set(g[MLA_NOPE:][_swap_perm()] * scale)
    return ga.reshape(1, SLOT_GROUP * HEAD_SLOT), gb.reshape(1, SLOT_GROUP * HEAD_SLOT)


def _rows3(m):
    return m.reshape(N_COND, 1, D_MODEL)


def kernel(x_prompt, x_sample, cache_ckv, cache_kpe, state_ret, c, c_ctx, mod_w, mod_b, norm1_w, norm2_w, mla_wq_a, mla_q_a_norm, mla_wq_b, mla_wkv_a, mla_kv_norm, mla_w_uk, mla_w_uv, mla_q_norm, mla_k_norm, mla_wo, ret_wq, ret_wk, ret_wv, ret_wg, ret_wo, ret_decay, ret_gn, ffn_w_gate, ffn_w_up, ffn_w_down, moe_router, moe_w_gate, moe_w_up, moe_w_down):
    x = (x_prompt.reshape(N_CTX, D_MODEL), x_sample.reshape(N_LAT, D_MODEL))
    cond = jnp.concatenate([c_ctx[None, :], c, jnp.zeros((N_COND - 1 - DEC_BATCH, D_MODEL), F32)], axis=0)
    mods = _adaln(cond, mod_w, mod_b).reshape(DEPTH, N_COND, 6, D_MODEL)

    perm = _swap_perm()
    c32, s32 = _rope_tables(MLA_ROPE)
    c32 = jnp.concatenate([jnp.ones((ROW_TILE, MLA_ROPE), F32), c32], axis=0)
    s32 = jnp.concatenate([jnp.zeros((ROW_TILE, MLA_ROPE), F32), s32], axis=0)
    rope_ca, rope_sb = _head_slot_tables(c32, s32)
    lat_kv = DEC_SEQ + PAST_LEN
    kv_tiles = lat_kv // ROW_TILE
    tab_q = lambda i: jnp.where(i < CTX_TILES, 0, 1 + (i - CTX_TILES) % TILES_PER_LAT_BATCH)
    tab_kc = lambda i: 0
    tab_kl = lambda i: jnp.where(i % kv_tiles < TILES_PER_LAT_BATCH, 1 + i % kv_tiles, 0)
    rc256, rs256 = _rope_tables(RET_DK)

    moe_wg, moe_wu, moe_wd = moe_w_gate.astype(BF16), moe_w_up.astype(BF16), moe_w_down.astype(BF16)

    new_ckv, new_kpe, new_ret = [], [], None
    for i in range(DEPTH):
        j = i // 2
        m = mods[i]
        sh1, sc1, g1, sh2, sc2, g2 = (_rows3(m[:, t]) for t in range(6))
        nw1 = norm1_w[i].reshape(1, D_MODEL)
        nw2 = norm2_w[i].reshape(1, D_MODEL)
        if i % 2 == 0:
            wqb = mla_wq_b[j].reshape(MLA_Q_RANK, MLA_HEADS, MLA_QK)
            wqb_swap = wqb[:, :, MLA_NOPE:][:, :, perm].reshape(MLA_Q_RANK, MLA_HEADS * MLA_ROPE)
            wqb = jnp.concatenate([_to_head_slots(wqb).reshape(MLA_Q_RANK, MLA_HEADS * HEAD_SLOT), wqb_swap],
                                  axis=1).astype(BF16)
            wkv = mla_wkv_a[j]
            w_pe = wkv[:, MLA_KV_RANK:]
            wkv = jnp.concatenate([wkv[:, :MLA_KV_RANK], jnp.tile(w_pe, (1, SLOT_GROUP)),
                                   jnp.tile(w_pe[:, perm], (1, SLOT_GROUP))], axis=1).astype(BF16)
            wuk = mla_w_uk[j].reshape(MLA_KV_RANK, MLA_HEADS, MLA_NOPE)
            wuk = jnp.concatenate([wuk, jnp.zeros((MLA_KV_RANK, MLA_HEADS, MLA_ROPE), F32)], axis=2)
            wuk = _to_head_slots(wuk).reshape(MLA_KV_RANK, MLA_HEADS * HEAD_SLOT).astype(BF16)
            gaq, gbq = _head_slot_gains(mla_q_norm[j], MLA_QK ** -0.5 * LOG2_E)
            gak, gbk = _head_slot_gains(mla_k_norm[j], 1.0)

            q, ckv, kpe = _mla_proj(x, nw1, sh1, sc1, mla_wq_a[j].astype(BF16),
                                    mla_q_a_norm[j].reshape(1, MLA_Q_RANK), wqb, wkv,
                                    mla_kv_norm[j].reshape(1, MLA_KV_RANK), gaq, gbq, rope_ca, rope_sb, tab_q)
            new_ckv.append(ckv[:N_CTX].reshape(BATCH, SEQ, MLA_KV_RANK))
            new_kpe.append(kpe[:N_CTX, :MLA_ROPE].reshape(BATCH, SEQ, MLA_ROPE))

            kpe_x = cache_kpe[:, j]
            kpe_x = jnp.concatenate([jnp.tile(kpe_x, (1, 1, SLOT_GROUP)), jnp.tile(kpe_x[:, :, perm], (1, 1, SLOT_GROUP))],
                                    axis=2)
            ckv_l = jnp.concatenate([ckv[N_CTX:].reshape(DEC_BATCH, DEC_SEQ, MLA_KV_RANK), cache_ckv[:, j]], axis=1)
            kpe_l = jnp.concatenate([kpe[N_CTX:].reshape(DEC_BATCH, DEC_SEQ, 2 * HEAD_SLOT), kpe_x], axis=1)
            wuv = mla_w_uv[j].astype(BF16)
            k_c, v_c = _kv_expand(ckv[:N_CTX], kpe[:N_CTX], wuk, wuv, gak, gbk, rope_ca, rope_sb, tab_kc)
            k_l, v_l = _kv_expand(ckv_l.reshape(DEC_BATCH * lat_kv, MLA_KV_RANK),
                                  kpe_l.reshape(DEC_BATCH * lat_kv, 2 * HEAD_SLOT), wuk, wuv, gak, gbk,
                                  rope_ca, rope_sb, tab_kl)

            qw = MLA_HEADS * HEAD_SLOT
            o_c = _attention(q.reshape(N_TOK // SEQ, SEQ, qw), k_c.reshape(BATCH, SEQ, qw),
                             v_c.reshape(BATCH, SEQ, MLA_HEADS * MLA_V),
                             batch=BATCH, q_off=0, tq_total=SEQ, tq=SEQ, heads=MLA_HEADS)
            o_l = _attention(q.reshape(N_TOK // DEC_SEQ, DEC_SEQ, qw), k_l.reshape(DEC_BATCH, lat_kv, qw),
                             v_l.reshape(DEC_BATCH, lat_kv, MLA_HEADS * MLA_V),
                             batch=DEC_BATCH, q_off=N_CTX // DEC_SEQ, tq_total=DEC_SEQ, tq=512, heads=2)
            mix_c, mix_l = o_c.reshape(N_CTX, D_MODEL), o_l.reshape(N_LAT, D_MODEL)
            w_out = mla_wo[j].astype(BF16)
        else:
            w_in = jnp.concatenate([ret_wq[j], ret_wk[j], ret_wv[j], ret_wg[j]], axis=1).astype(BF16)
            qkvg = _ret_proj(x, nw1, sh1, sc1, w_in, rc256, rs256)
            log_gamma = -jnp.exp(ret_decay[j].astype(F32))
            gn = ret_gn[j].reshape(RET_HEADS, 1, RET_DV)
            width = qkvg.shape[1]
            y_c, new_ret = _retention(log_gamma, qkvg.reshape(N_TOK // SEQ, SEQ, width), gn, None,
                                      batch=BATCH, b_off=0, t=SEQ, layer=j, states=new_ret)
            y_l = _retention(log_gamma, qkvg.reshape(N_TOK // DEC_SEQ, DEC_SEQ, width), gn, state_ret,
                             batch=DEC_BATCH, b_off=N_CTX // DEC_SEQ, t=DEC_SEQ, layer=j)
            mix_c, mix_l = y_c.reshape(N_CTX, RET_HEADS * RET_DV), y_l.reshape(N_LAT, RET_HEADS * RET_DV)
            w_out = ret_wo[j].astype(BF16)

        if i % 2 == 0:
            x1, h2 = _proj_residual(mix_c, mix_l, w_out, x, g1, nw2, sh2, sc2)
            x = _ffn(h2, x1, g2, ffn_w_gate[j].astype(BF16), ffn_w_up[j].astype(BF16), ffn_w_down[j].astype(BF16))
        else:
            wr = jnp.pad(moe_router[j], ((0, 0), (0, 128 - N_EXPERTS)))
            wr_hi = wr.astype(BF16)
            wr_lo = (wr - wr_hi.astype(F32)).astype(BF16)
            x1, h2, logits = _proj_residual(mix_c, mix_l, w_out, x, g1, nw2, sh2, sc2, router_w=(wr_hi, wr_lo))
            top_w, routing = _route(logits[:, :N_EXPERTS])
            ya = _moe(routing, h2, moe_wg, moe_wu, moe_wd, j)
            x = _combine(ya, x1, g2, top_w, split=(i == DEPTH - 1))

    y_prompt = x[0].reshape(BATCH, SEQ, D_MODEL)
    y_sample = x[1].reshape(DEC_BATCH, DEC_SEQ, D_MODEL)
    return (y_prompt, y_sample, jnp.stack(new_ckv, axis=1), jnp.stack(new_kpe, axis=1), new_ret)
```

```python
import functools

import jax
import jax.numpy as jnp
from jax import lax
from jax.experimental import pallas as pl
from jax.experimental.pallas import tpu as pltpu

F32 = jnp.float32
BF16 = jnp.bfloat16

D_MODEL = 1024
BATCH = 32
SEQ = 256
DEPTH = 4
DEC_BATCH = 4
DEC_SEQ = 2048
PAST_LEN = 512
GRID_W = 64
ROPE_BASE = 10000.0
NORM_EPS = 1e-6

MLA_HEADS = 16
MLA_NOPE = 64
MLA_ROPE = 32
MLA_QK = MLA_NOPE + MLA_ROPE
MLA_V = 64
MLA_Q_RANK = 384
MLA_KV_RANK = 256
HEAD_SLOT = 128

RET_HEADS = 4
RET_DK = 256
RET_DV = 512
RET_BLOCK = 256

FFN_DIM = 2816
N_EXPERTS = 8
TOP_K = 2
EXPERT_DIM = 3584

N_CTX = BATCH * SEQ
N_LAT = DEC_BATCH * DEC_SEQ
N_TOK = N_CTX + N_LAT
N_COND = 8

ROW_TILE = 512
CTX_TILES = N_CTX // ROW_TILE
TILES_PER_LAT_BATCH = DEC_SEQ // ROW_TILE

MOE_ROWS = 512
MOE_TILE = 1792
MOE_STEPS = EXPERT_DIM // MOE_TILE
MOE_STEP_ROWS = MOE_ROWS // MOE_STEPS
assert MOE_STEPS >= 2
N_ASSIGN = N_TOK * TOP_K
MOE_BLOCKS = N_ASSIGN // MOE_ROWS + N_EXPERTS

VMEM_LIMIT = 56 * 1024 * 1024
LOG2_E = 1.4426950408889634


def _params(*sem):
    return pltpu.CompilerParams(dimension_semantics=sem, vmem_limit_bytes=VMEM_LIMIT)


def _cond_of_tile(i):
    return jnp.where(i < CTX_TILES, 0, 1 + (i - CTX_TILES) // TILES_PER_LAT_BATCH)


def _dot(a, b):
    return jnp.dot(a, b, preferred_element_type=F32)


def _silu(x):
    return x * (1.0 / (1.0 + jnp.exp(-x)))


def _rms(x, w):
    return x * lax.rsqrt(jnp.mean(x * x, axis=-1, keepdims=True) + NORM_EPS) * w


def _modulate(x, nw, shift, scale):
    return _rms(x, nw) * (1.0 + scale) + shift


def _adaln_kernel(c_ref, w_ref, b_ref, o_ref):
    a = _silu(c_ref[...]).astype(BF16)
    o_ref[...] = _dot(a, w_ref[...].astype(BF16)) + b_ref[...]


def _adaln(cond, mod_w, mod_b):
    tn = 1024
    return pl.pallas_call(
        _adaln_kernel,
        grid=(DEPTH, 6 * D_MODEL // tn),
        in_specs=[
            pl.BlockSpec((N_COND, D_MODEL), lambda l, j: (0, 0)),
            pl.BlockSpec((None, D_MODEL, tn), lambda l, j: (l, 0, j)),
            pl.BlockSpec((None, 1, tn), lambda l, j: (l, 0, j)),
        ],
        out_specs=pl.BlockSpec((None, N_COND, tn), lambda l, j: (l, 0, j)),
        out_shape=jax.ShapeDtypeStruct((DEPTH, N_COND, 6 * D_MODEL), F32),
        compiler_params=_params("parallel", "parallel"),
        name="adaln",
    )(cond, mod_w, mod_b.reshape(DEPTH, 1, 6 * D_MODEL))


SLOT_GROUP = HEAD_SLOT // MLA_ROPE


def _heads_norm_rope(xs, swaps, ga, gb):
    ss = [jnp.sum(x * x, axis=-1, keepdims=True) for x in xs]
    rs = [lax.rsqrt(s * (1.0 / MLA_QK) + NORM_EPS) for s in ss]
    out = []
    for hd, (r, x, sw) in enumerate(zip(rs, xs, swaps)):
        lanes = slice((hd % SLOT_GROUP) * HEAD_SLOT, (hd % SLOT_GROUP + 1) * HEAD_SLOT)
        out.append(r * (x * ga[:, lanes] + sw * gb[:, lanes]))
    return out


def _split_rows(x):
    if isinstance(x, tuple):
        return x[0], x[1], 0
    return x, x, CTX_TILES


def _split_specs(width, lat_tile0, tm=ROW_TILE):
    return [pl.BlockSpec((tm, width), lambda i: (jnp.minimum(i, CTX_TILES - 1), 0)),
            pl.BlockSpec((tm, width), lambda i: (jnp.maximum(i - CTX_TILES, 0) + lat_tile0, 0))]


def _pick_rows(ctx_ref, lat_ref):
    return jnp.where(pl.program_id(0) < CTX_TILES, ctx_ref[...], lat_ref[...])


def _mla_proj_kernel(xc_ref, xl_ref, nw_ref, sh_ref, sc_ref, wqa_ref, qan_ref, wqb_ref, wkv_ref, kvn_ref,
                     ga_ref, gb_ref, ca_ref, sb_ref, q_ref, ckv_ref, kpe_ref):
    h = _modulate(_pick_rows(xc_ref, xl_ref), nw_ref[...], sh_ref[...], sc_ref[...]).astype(BF16)
    qa = _rms(_dot(h, wqa_ref[...]), qan_ref[...]).astype(BF16)
    q = _dot(qa, wqb_ref[...])
    kv = _dot(h, wkv_ref[...])
    ckv_ref[...] = _rms(kv[:, :MLA_KV_RANK], kvn_ref[...])
    kpe_ref[...] = kv[:, MLA_KV_RANK:]
    ga = ga_ref[...] * ca_ref[...]
    gb = gb_ref[...] * sb_ref[...]
    slots = [slice(hd * HEAD_SLOT, (hd + 1) * HEAD_SLOT) for hd in range(MLA_HEADS)]
    swap0 = MLA_HEADS * HEAD_SLOT
    swaps = [q[:, swap0 + (hd // SLOT_GROUP) * HEAD_SLOT:swap0 + (hd // SLOT_GROUP + 1) * HEAD_SLOT]
             for hd in range(MLA_HEADS)]
    for sl, y in zip(slots, _heads_norm_rope([q[:, sl] for sl in slots], swaps, ga, gb)):
        q_ref[:, sl] = y.astype(BF16)


def _mla_proj(x, nw, shift, scale, wqa, qan, wqb, wkv, kvn, ga, gb, ca, sb, tab_index):
    tm = ROW_TILE
    row = lambda i: (i, 0)
    fixed = lambda i: (0, 0)
    cond = lambda i: (_cond_of_tile(i), 0, 0)
    qw = MLA_HEADS * HEAD_SLOT
    tabw = SLOT_GROUP * HEAD_SLOT
    kpew = 2 * HEAD_SLOT
    tab = lambda i: (tab_index(i), 0)
    x_ctx, x_lat, lat_tile0 = _split_rows(x)
    return pl.pallas_call(
        _mla_proj_kernel,
        grid=(N_TOK // tm,),
        in_specs=_split_specs(D_MODEL, lat_tile0) + [
            pl.BlockSpec((1, D_MODEL), fixed),
            pl.BlockSpec((None, 1, D_MODEL), cond),
            pl.BlockSpec((None, 1, D_MODEL), cond),
            pl.BlockSpec((D_MODEL, MLA_Q_RANK), fixed),
            pl.BlockSpec((1, MLA_Q_RANK), fixed),
            pl.BlockSpec((MLA_Q_RANK, wqb.shape[1]), fixed),
            pl.BlockSpec((D_MODEL, MLA_KV_RANK + kpew), fixed),
            pl.BlockSpec((1, MLA_KV_RANK), fixed),
            pl.BlockSpec((1, tabw), fixed),
            pl.BlockSpec((1, tabw), fixed),
            pl.BlockSpec((tm, tabw), tab),
            pl.BlockSpec((tm, tabw), tab),
        ],
        out_specs=[
            pl.BlockSpec((tm, qw), row),
            pl.BlockSpec((tm, MLA_KV_RANK), row),
            pl.BlockSpec((tm, kpew), row),
        ],
        out_shape=[
            jax.ShapeDtypeStruct((N_TOK, qw), BF16),
            jax.ShapeDtypeStruct((N_TOK, MLA_KV_RANK), F32),
            jax.ShapeDtypeStruct((N_TOK, kpew), F32),
        ],
        compiler_params=_params("arbitrary"),
        name="mla_proj",
    )(x_ctx, x_lat, nw, shift, scale, wqa, qan, wqb, wkv, kvn, ga, gb, ca, sb)


def _kv_expand_kernel(ckv_ref, kpe_ref, wuk_ref, wuv_ref, ga_ref, gb_ref, ca_ref, sb_ref, k_ref, v_ref):
    c = ckv_ref[...].astype(BF16)
    kn = _dot(c, wuk_ref[...])
    v_ref[...] = _dot(c, wuv_ref[...]).astype(BF16)
    kpe = kpe_ref[:, :HEAD_SLOT]
    swap = kpe_ref[:, HEAD_SLOT:]
    ga = ga_ref[...] * ca_ref[...]
    gb = gb_ref[...] * sb_ref[...]
    group = lax.broadcasted_iota(jnp.int32, kpe.shape, 1) // MLA_ROPE
    kpe_at = [jnp.where(group == g, kpe, 0.0) for g in range(SLOT_GROUP)]
    slots = [slice(hd * HEAD_SLOT, (hd + 1) * HEAD_SLOT) for hd in range(MLA_HEADS)]
    xs = [kn[:, sl] + kpe_at[hd % SLOT_GROUP] for hd, sl in enumerate(slots)]
    for sl, y in zip(slots, _heads_norm_rope(xs, [swap] * MLA_HEADS, ga, gb)):
        k_ref[:, sl] = y.astype(BF16)


def _kv_expand(ckv, kpe, wuk, wuv, ga, gb, ca, sb, tab_index):
    n = ckv.shape[0]
    tm = ROW_TILE
    row = lambda i: (i, 0)
    fixed = lambda i: (0, 0)
    tab = lambda i: (tab_index(i), 0)
    kw = MLA_HEADS * HEAD_SLOT
    vw = MLA_HEADS * MLA_V
    tabw = SLOT_GROUP * HEAD_SLOT
    return pl.pallas_call(
        _kv_expand_kernel,
        grid=(n // tm,),
        in_specs=[
            pl.BlockSpec((tm, MLA_KV_RANK), row),
            pl.BlockSpec((tm, 2 * HEAD_SLOT), row),
            pl.BlockSpec((MLA_KV_RANK, kw), fixed),
            pl.BlockSpec((MLA_KV_RANK, vw), fixed),
            pl.BlockSpec((1, tabw), fixed),
            pl.BlockSpec((1, tabw), fixed),
            pl.BlockSpec((tm, tabw), tab),
            pl.BlockSpec((tm, tabw), tab),
        ],
        out_specs=[pl.BlockSpec((tm, kw), row), pl.BlockSpec((tm, vw), row)],
        out_shape=[jax.ShapeDtypeStruct((n, kw), BF16), jax.ShapeDtypeStruct((n, vw), BF16)],
        compiler_params=_params("parallel"),
        name="kv_expand",
    )(ckv, kpe, wuk, wuv, ga, gb, ca, sb)


def _attn_kernel(q_ref, k_ref, v_ref, o_ref, *, heads):
    scores = []
    for hd in range(heads):
        qh = q_ref[:, hd * HEAD_SLOT:(hd + 1) * HEAD_SLOT]
        kh = k_ref[:, hd * HEAD_SLOT:(hd + 1) * HEAD_SLOT]
        scores.append(lax.dot_general(qh, kh, (((1,), (1,)), ((), ())), preferred_element_type=F32))
    probs, invs = [], []
    for s in scores:
        p = jnp.exp2(s - jnp.max(s, axis=-1, keepdims=True))
        invs.append(1.0 / jnp.sum(p, axis=-1, keepdims=True))
        probs.append(p.astype(BF16))
    for hd in range(heads):
        o = _dot(probs[hd], v_ref[:, hd * MLA_V:(hd + 1) * MLA_V])
        o_ref[:, hd * MLA_V:(hd + 1) * MLA_V] = (o * invs[hd]).astype(BF16)


def _attention(q, k, v, *, batch, q_off, tq_total, tq, heads):
    tk = k.shape[1]
    groups = MLA_HEADS // heads
    return pl.pallas_call(
        functools.partial(_attn_kernel, heads=heads),
        grid=(batch, groups, tq_total // tq),
        in_specs=[
            pl.BlockSpec((None, tq, heads * HEAD_SLOT), lambda b, g, i: (b + q_off, i, g)),
            pl.BlockSpec((None, tk, heads * HEAD_SLOT), lambda b, g, i: (b, 0, g)),
            pl.BlockSpec((None, tk, heads * MLA_V), lambda b, g, i: (b, 0, g)),
        ],
        out_specs=pl.BlockSpec((None, tq, heads * MLA_V), lambda b, g, i: (b, i, g)),
        out_shape=jax.ShapeDtypeStruct((batch, tq_total, MLA_HEADS * MLA_V), BF16),
        compiler_params=_params("parallel", "parallel", "parallel"),
        name="attention",
    )(q, k, v)


LANE_CHUNKS = D_MODEL // 128


def _store_row_major(ref, x):
    rows = x.shape[0]
    for c in range(LANE_CHUNKS):
        ref[pl.ds(c, rows, stride=LANE_CHUNKS), :] = x[:, c * 128:(c + 1) * 128]


def _load_row_major(ref, rows, c):
    return ref[pl.ds(c, rows, stride=LANE_CHUNKS), :]


def _proj_residual_kernel(ac_ref, al_ref, w_ref, xc_ref, xl_ref, g_ref, nw_ref, sh_ref, sc_ref, *rest, router):
    if router:
        rhi_ref, rlo_ref, x1_ref, h2_ref, lg_ref, y_sc = rest
    else:
        x1_ref, h2_ref, y_sc = rest
    i = pl.program_id(0)

    @pl.when(i < CTX_TILES)
    def _():
        y_sc[...] = _dot(ac_ref[...], w_ref[...])

    @pl.when(i >= CTX_TILES)
    def _():
        y_sc[...] = _dot(al_ref[...], w_ref[...])

    x1 = _pick_rows(xc_ref, xl_ref) + g_ref[...] * y_sc[...]
    h2 = _modulate(x1, nw_ref[...], sh_ref[...], sc_ref[...])
    if router:
        hi = h2.astype(BF16)
        lo = (h2 - hi.astype(F32)).astype(BF16)
        lg_ref[...] = _dot(hi, rhi_ref[...]) + (_dot(lo, rhi_ref[...]) + _dot(hi, rlo_ref[...]))
        _store_row_major(h2_ref, h2)
    else:
        h2_ref[...] = h2.astype(h2_ref.dtype)
    x1_ref[...] = x1


def _proj_residual(a_ctx, a_lat, w, x, gate, nw, shift, scale, router_w=None):
    tm = ROW_TILE
    kdim = w.shape[0]
    row = lambda i: (i, 0)
    fixed = lambda i: (0, 0)
    cond = lambda i: (_cond_of_tile(i), 0, 0)
    router = router_w is not None
    x_ctx, x_lat, lat_tile0 = _split_rows(x)
    in_specs = _split_specs(kdim, 0) + [pl.BlockSpec((kdim, D_MODEL), fixed)] + _split_specs(D_MODEL, lat_tile0) + [
        pl.BlockSpec((None, 1, D_MODEL), cond),
        pl.BlockSpec((1, D_MODEL), fixed),
        pl.BlockSpec((None, 1, D_MODEL), cond),
        pl.BlockSpec((None, 1, D_MODEL), cond),
    ]
    if router:
        h2_spec = pl.BlockSpec((tm * LANE_CHUNKS, 128), row)
        h2_shape = jax.ShapeDtypeStruct((N_TOK * LANE_CHUNKS, 128), F32)
    else:
        h2_spec = pl.BlockSpec((tm, D_MODEL), row)
        h2_shape = jax.ShapeDtypeStruct((N_TOK, D_MODEL), BF16)
    out_specs = [pl.BlockSpec((tm, D_MODEL), row), h2_spec]
    out_shape = [jax.ShapeDtypeStruct((N_TOK, D_MODEL), F32), h2_shape]
    args = [a_ctx, a_lat, w, x_ctx, x_lat, gate, nw, shift, scale]
    if router:
        in_specs += [pl.BlockSpec((D_MODEL, 128), fixed), pl.BlockSpec((D_MODEL, 128), fixed)]
        out_specs.append(pl.BlockSpec((tm, 128), row))
        out_shape.append(jax.ShapeDtypeStruct((N_TOK, 128), F32))
        args += list(router_w)
    return pl.pallas_call(
        functools.partial(_proj_residual_kernel, router=router),
        grid=(N_TOK // tm,),
        in_specs=in_specs,
        out_specs=out_specs,
        out_shape=out_shape,
        scratch_shapes=[pltpu.VMEM((tm, D_MODEL), F32)],
        compiler_params=_params("arbitrary"),
        name="proj_residual",
    )(*args)


def _swiglu_partial(x, wg, wu, wd):
    g = _dot(x, wg)
    u = _dot(x, wu)
    return _dot((_silu(g) * u).astype(BF16), wd)


def _ffn_kernel(h_ref, x1_ref, g_ref, wg_ref, wu_ref, wd_ref, o_ref):
    part = _swiglu_partial(h_ref[...], wg_ref[...], wu_ref[...], wd_ref[...])
    o_ref[...] = x1_ref[...] + g_ref[...] * part


def _resident(shape):
    return pl.BlockSpec(shape, lambda *_: (0,) * len(shape), pipeline_mode=pl.Buffered(1))


def _ffn(h2, x1, gate, wg, wu, wd):
    tm = ROW_TILE
    row = lambda i: (i, 0)
    return pl.pallas_call(
        _ffn_kernel,
        grid=(N_TOK // tm,),
        in_specs=[
            pl.BlockSpec((tm, D_MODEL), row),
            pl.BlockSpec((tm, D_MODEL), row),
            pl.BlockSpec((None, 1, D_MODEL), lambda i: (_cond_of_tile(i), 0, 0)),
            _resident((D_MODEL, FFN_DIM)),
            _resident((D_MODEL, FFN_DIM)),
            _resident((FFN_DIM, D_MODEL)),
        ],
        out_specs=pl.BlockSpec((tm, D_MODEL), row),
        out_shape=jax.ShapeDtypeStruct((N_TOK, D_MODEL), F32),
        compiler_params=_params("parallel"),
        name="ffn",
    )(h2, x1, gate, wg, wu, wd)


def _moe_kernel(blk_e_ref, nact_ref, order_ref, base_ref, x_hbm, wg_ref, wu_ref, wd_ref, ya_hbm,
                xrows0, xrows1, xb, acc, stage0, stage1, gsem, ssem):
    b = pl.program_id(0)
    f = pl.program_id(1)
    nact = nact_ref[0]
    xrows = (xrows0, xrows1)
    stage = (stage0, stage1)
    tile = lambda r: pl.ds(pl.multiple_of(r * LANE_CHUNKS, LANE_CHUNKS), LANE_CHUNKS)

    def row_ids(blk):
        base = base_ref[blk + 1]
        return lambda r: order_ref[base + r]

    def gather_copy(par, step, r, a):
        tok = a & (N_TOK - 1)
        return pltpu.make_async_copy(x_hbm.at[tok], xrows[par].at[step, tile(r), :], gsem.at[par])

    def scatter_copy(par, step, r, a):
        return pltpu.make_async_copy(stage[par].at[step, tile(r), :], ya_hbm.at[a], ssem.at[par])

    def wait_gather(par):
        pltpu.make_async_copy(xrows[par], xrows[par], gsem.at[par]).wait()

    def wait_scatter(par):
        pltpu.make_async_copy(stage[par], stage[par], ssem.at[par]).wait()

    @pl.when((b == 0) & (f == 0))
    def _():
        stage1[...] = jnp.zeros_like(stage1)
        ids = row_ids(0)

        def issue(r, c):
            gather_copy(0, r // MOE_STEP_ROWS, r % MOE_STEP_ROWS, ids(r)).start()
            return c

        lax.fori_loop(0, MOE_ROWS, issue, 0, unroll=8)

    def block_body(par):
        @pl.when(f == 0)
        def _():
            wait_gather(par)

            @pl.when(b >= 1)
            def _():
                wait_scatter(par)

            for step in range(MOE_STEPS):
                for c in range(LANE_CHUNKS):
                    xb[step * MOE_STEP_ROWS:(step + 1) * MOE_STEP_ROWS, c * 128:(c + 1) * 128] = (
                        _load_row_major(xrows[par].at[step], MOE_STEP_ROWS, c).astype(BF16))

        nxt, prv = row_ids(b + 1), row_ids(b - 1)
        row0 = f * MOE_STEP_ROWS
        for r in range(MOE_STEP_ROWS):
            gather_copy(1 - par, f, r, nxt(row0 + r)).start()
            scatter_copy(1 - par, f, r, prv(row0 + r)).start()
        part = _swiglu_partial(xb[...], wg_ref[...], wu_ref[...], wd_ref[...])

        @pl.when(f == 0)
        def _():
            acc[...] = part

        @pl.when((f > 0) & (f < MOE_STEPS - 1))
        def _():
            acc[...] += part

        @pl.when(f == MOE_STEPS - 1)
        def _():
            total = acc[...] + part
            for step in range(MOE_STEPS):
                _store_row_major(stage[par].at[step], total[step * MOE_STEP_ROWS:(step + 1) * MOE_STEP_ROWS])

    def drain(par):
        wait_scatter(par)
        ids = row_ids(b - 1)

        def issue(r, c):
            scatter_copy(1 - par, r // MOE_STEP_ROWS, r % MOE_STEP_ROWS, ids(r)).start()
            return c

        lax.fori_loop(0, MOE_ROWS, issue, 0, unroll=8)
        wait_gather(par)
        wait_scatter(1 - par)

    for par in range(2):
        @pl.when((b < nact) & (b % 2 == par))
        def _():
            block_body(par)

        @pl.when((b == nact) & (f == 0) & (b % 2 == par))
        def _():
            drain(par)


def _moe(routing, h2, wg, wu, wd, layer):
    tf = MOE_TILE
    wmap = lambda b, f, e, *_: (layer, e[b], 0, f)
    rows_buf = pltpu.VMEM((MOE_STEPS, MOE_STEP_ROWS * LANE_CHUNKS, 128), F32)
    grid_spec = pltpu.PrefetchScalarGridSpec(
        num_scalar_prefetch=len(routing),
        grid=(MOE_BLOCKS, MOE_STEPS),
        in_specs=[
            pl.BlockSpec(memory_space=pl.ANY),
            pl.BlockSpec((None, None, D_MODEL, tf), wmap),
            pl.BlockSpec((None, None, D_MODEL, tf), wmap),
            pl.BlockSpec((None, None, tf, D_MODEL), lambda b, f, e, *_: (layer, e[b], f, 0)),
        ],
        out_specs=pl.BlockSpec(memory_space=pl.ANY),
        scratch_shapes=[
            rows_buf,
            rows_buf,
            pltpu.VMEM((MOE_ROWS, D_MODEL), BF16),
            pltpu.VMEM((MOE_ROWS, D_MODEL), F32),
            rows_buf,
            rows_buf,
            pltpu.SemaphoreType.DMA((2,)),
            pltpu.SemaphoreType.DMA((2,)),
        ],
    )
    return pl.pallas_call(
        _moe_kernel,
        grid_spec=grid_spec,
        out_shape=jax.ShapeDtypeStruct((N_ASSIGN + MOE_ROWS, LANE_CHUNKS, 128), F32),
        compiler_params=_params("arbitrary", "arbitrary"),
        name="moe",
    )(*routing, h2.reshape(N_TOK, LANE_CHUNKS, 128), wg, wu, wd)


def _combine_kernel(y0_ref, y1_ref, x1_ref, g_ref, w_ref, *o_refs):
    w = w_ref[...]
    rows = x1_ref.shape[0]

    def emit(o_ref):
        for c in range(LANE_CHUNKS):
            sl = slice(c * 128, (c + 1) * 128)
            y = w[:, 0:1] * _load_row_major(y0_ref, rows, c) + w[:, 1:2] * _load_row_major(y1_ref, rows, c)
            o_ref[:, sl] = x1_ref[:, sl] + g_ref[:, sl] * y

    if len(o_refs) == 1:
        emit(o_refs[0])
    else:
        i = pl.program_id(0)

        @pl.when(i < CTX_TILES)
        def _():
            emit(o_refs[0])

        @pl.when(i >= CTX_TILES)
        def _():
            emit(o_refs[1])


def _combine(ya, x1, gate, top_w, split):
    tm = ROW_TILE
    row = lambda i: (i, 0)
    ya2 = ya.reshape(ya.shape[0] * LANE_CHUNKS, 128)
    if split:
        out_specs = [pl.BlockSpec((tm, D_MODEL), lambda i: (jnp.minimum(i, CTX_TILES - 1), 0)),
                     pl.BlockSpec((tm, D_MODEL), lambda i: (jnp.maximum(i - CTX_TILES, 0), 0))]
        out_shape = [jax.ShapeDtypeStruct((N_CTX, D_MODEL), F32), jax.ShapeDtypeStruct((N_LAT, D_MODEL), F32)]
    else:
        out_specs = pl.BlockSpec((tm, D_MODEL), row)
        out_shape = jax.ShapeDtypeStruct((N_TOK, D_MODEL), F32)
    return pl.pallas_call(
        _combine_kernel,
        grid=(N_TOK // tm,),
        in_specs=[
            pl.BlockSpec((tm * LANE_CHUNKS, 128), row),
            pl.BlockSpec((tm * LANE_CHUNKS, 128), lambda i: (i + N_TOK // tm, 0)),
            pl.BlockSpec((tm, D_MODEL), row),
            pl.BlockSpec((None, 1, D_MODEL), lambda i: (_cond_of_tile(i), 0, 0)),
            pl.BlockSpec((tm, TOP_K), row),
        ],
        out_specs=out_specs,
        out_shape=out_shape,
        compiler_params=_params("arbitrary"),
        name="moe_combine",
    )(ya2, ya2, x1, gate, top_w)


def _route(logits):
    top_v, top_i = lax.top_k(logits, TOP_K)
    top_w = jax.nn.softmax(top_v, axis=-1)
    e = top_i.reshape(N_ASSIGN).astype(jnp.int32)
    a = jnp.arange(N_ASSIGN, dtype=jnp.int32)
    row_id = (a % TOP_K) * N_TOK + a // TOP_K
    order = jnp.sort(e * N_ASSIGN + row_id) & (N_ASSIGN - 1)
    order = jnp.concatenate([order, N_ASSIGN + jnp.arange(MOE_ROWS, dtype=jnp.int32)])
    counts = jnp.sum((e[:, None] == jnp.arange(N_EXPERTS, dtype=jnp.int32)[None, :]).astype(jnp.int32), axis=0)
    start = jnp.cumsum(counts) - counts
    nblk = (counts + MOE_ROWS - 1) // MOE_ROWS
    blk_end = jnp.cumsum(nblk)
    blk = jnp.arange(-1, MOE_BLOCKS + 1, dtype=jnp.int32)
    blk_e = jnp.sum((blk[:, None] >= blk_end[None, :]).astype(jnp.int32), axis=1)
    blk_e = jnp.minimum(blk_e, N_EXPERTS - 1)
    within = (blk - (blk_end - nblk)[blk_e]) * MOE_ROWS
    real = (blk >= 0) & (blk < blk_end[-1])
    blk_base = jnp.where(real, start[blk_e] + within, N_ASSIGN).astype(jnp.int32)
    nact = blk_end[-1:].astype(jnp.int32)
    return top_w, (blk_e[1:-1], nact, order, blk_base)


RET_QK_W = RET_HEADS * RET_DK
RET_V_W = RET_HEADS * RET_DV
RET_COL_TILE = 1024


def _ret_proj_kernel(x_ref, nw_ref, sh_ref, sc_ref, w_ref, c_ref, s_ref, o_ref):
    def body(rope):
        h = _modulate(x_ref[...], nw_ref[...], sh_ref[...], sc_ref[...]).astype(BF16)
        for c0 in range(0, o_ref.shape[1], RET_COL_TILE):
            acc = _dot(h, w_ref[:, c0:c0 + RET_COL_TILE])
            if c0 >= 2 * RET_QK_W + RET_V_W:
                o_ref[:, c0:c0 + RET_COL_TILE] = _silu(acc).astype(BF16)
            elif c0 >= 2 * RET_QK_W:
                o_ref[:, c0:c0 + RET_COL_TILE] = acc.astype(BF16)
            else:
                x = acc * (RET_DK ** -0.5) if c0 >= RET_QK_W else acc
                if not rope:
                    o_ref[:, c0:c0 + RET_COL_TILE] = x.astype(BF16)
                    continue
                for g in range(RET_COL_TILE // 128):
                    tl = slice((g % 2) * 128, (g % 2 + 1) * 128)
                    xs = x[:, g * 128:(g + 1) * 128]
                    y = xs * c_ref[:, tl] + pltpu.roll(xs, RET_DK // 4, 1) * s_ref[:, tl]
                    o_ref[:, c0 + g * 128:c0 + (g + 1) * 128] = y.astype(BF16)

    i = pl.program_id(0)

    @pl.when(i < CTX_TILES)
    def _():
        body(False)

    @pl.when(i >= CTX_TILES)
    def _():
        body(True)


def _ret_proj(x, nw, shift, scale, w, rope_c, rope_s):
    tm = ROW_TILE
    width = w.shape[1]
    row = lambda i: (i, 0)
    cond = lambda i: (_cond_of_tile(i), 0, 0)
    tab = lambda i: (jnp.where(i < CTX_TILES, 0, (i - CTX_TILES) % TILES_PER_LAT_BATCH), 0)
    return pl.pallas_call(
        _ret_proj_kernel,
        grid=(N_TOK // tm,),
        in_specs=[
            pl.BlockSpec((tm, D_MODEL), row),
            pl.BlockSpec((1, D_MODEL), lambda i: (0, 0)),
            pl.BlockSpec((None, 1, D_MODEL), cond),
            pl.BlockSpec((None, 1, D_MODEL), cond),
            _resident((D_MODEL, width)),
            pl.BlockSpec((tm, RET_DK), tab),
            pl.BlockSpec((tm, RET_DK), tab),
        ],
        out_specs=pl.BlockSpec((tm, width), row),
        out_shape=jax.ShapeDtypeStruct((N_TOK, width), BF16),
        compiler_params=_params("parallel"),
        name="ret_proj",
    )(x, nw, shift, scale, w, rope_c, rope_s)


def _retention_kernel(lg_ref, q_ref, k_ref, v_ref, sg_ref, gn_ref, *rest, zero_init, n_chunks, layer):
    if zero_init:
        y_ref, st_ref, o_acc, state, dmask = rest[-5:]
    else:
        s0f_ref, s0b_ref, y_ref, o_acc, state, dmask = rest
    hd = pl.program_id(1)
    c = RET_BLOCK
    ri = lax.broadcasted_iota(jnp.int32, (c, c), 0).astype(F32)
    ci = lax.broadcasted_iota(jnp.int32, (c, c), 1).astype(F32)
    pos = lax.broadcasted_iota(jnp.int32, (c, 1), 0).astype(F32)
    q_decay, k_decay, c_decay = [], [], []
    for d in range(2):
        lg = lg_ref[d, hd]
        rel = (ri - ci) if d == 0 else (ci - ri)
        dmask[d] = jnp.where(rel >= 0, jnp.exp(jnp.maximum(rel, 0.0) * lg), 0.0)
        q_decay.append(jnp.exp(((pos + 1.0) if d == 0 else (c - pos)) * lg))
        k_decay.append(jnp.exp(((c - 1.0 - pos) if d == 0 else pos) * lg))
        c_decay.append(jnp.exp(jnp.full((1, 1), c, F32) * lg))
        if zero_init:
            state[d] = jnp.zeros(state.shape[1:], F32)
        else:
            state[d] = (s0f_ref if d == 0 else s0b_ref)[...]

    def step(t, carry):
        for d in range(2):
            ch = t if d == 0 else n_chunks - 1 - t
            rows = pl.ds(pl.multiple_of(ch * c, c), c)
            qc = q_ref[rows, :]
            kc = k_ref[rows, :]
            vc = v_ref[rows, :]
            a = lax.dot_general(qc, kc, (((1,), (1,)), ((), ())), preferred_element_type=F32) * dmask[d]
            s = state[d]
            o_acc[d, rows, :] = (_dot(a.astype(BF16), vc)
                                 + _dot((qc.astype(F32) * q_decay[d]).astype(BF16), s.astype(BF16)))
            kd = (kc.astype(F32) * k_decay[d]).astype(BF16)
            state[d] = s * c_decay[d] + lax.dot_general(kd, vc, (((0,), (0,)), ((), ())),
                                                         preferred_element_type=F32)
        return carry

    lax.fori_loop(0, n_chunks, step, 0)

    if zero_init:
        for d in range(2):
            if st_ref.ndim == 3:
                st_ref[d] = state[d]
            else:
                st_ref[layer, d] = state[d]
                for other in range(st_ref.shape[0]):
                    if other != layer:
                        st_ref[other, d] = jnp.zeros(state.shape[1:], F32)

    y = _rms(o_acc[0] + o_acc[1], gn_ref[...])
    y_ref[...] = (sg_ref[...].astype(F32) * y).astype(BF16)


def _retention(log_gamma, qkvg, gn, state0, *, batch, b_off, t, layer, states=None):
    zero_init = state0 is None
    qblk = RET_HEADS * RET_DK // RET_DK
    vblk = 2 * RET_HEADS * RET_DK // RET_DV
    gblk = vblk + RET_HEADS
    in_specs = [
        pl.BlockSpec(memory_space=pltpu.SMEM),
        pl.BlockSpec((None, t, RET_DK), lambda b, h: (b + b_off, 0, h)),
        pl.BlockSpec((None, t, RET_DK), lambda b, h: (b + b_off, 0, qblk + h)),
        pl.BlockSpec((None, t, RET_DV), lambda b, h: (b + b_off, 0, vblk + h)),
        pl.BlockSpec((None, t, RET_DV), lambda b, h: (b + b_off, 0, gblk + h)),
        pl.BlockSpec((None, 1, RET_DV), lambda b, h: (h, 0, 0)),
    ]
    args = [log_gamma, qkvg, qkvg, qkvg, qkvg, gn]
    y_spec = pl.BlockSpec((None, t, RET_DV), lambda b, h: (b, 0, h))
    y_shape = jax.ShapeDtypeStruct((batch, t, RET_HEADS * RET_DV), BF16)
    aliases = {}
    if zero_init:
        n_ret = DEPTH // 2
        st_shape = jax.ShapeDtypeStruct((batch, n_ret, 2, RET_HEADS, RET_DK, RET_DV), F32)
        if states is None:
            st_spec = pl.BlockSpec((None, n_ret, 2, None, RET_DK, RET_DV), lambda b, h: (b, 0, 0, h, 0, 0))
        else:
            st_spec = pl.BlockSpec((None, None, 2, None, RET_DK, RET_DV), lambda b, h: (b, layer, 0, h, 0, 0))
            aliases = {len(args): 1}
            in_specs.append(pl.BlockSpec(memory_space=pl.ANY))
            args.append(states)
        out_specs = [y_spec, st_spec]
        out_shape = [y_shape, st_shape]
    else:
        in_specs += [
            pl.BlockSpec((None, None, None, None, RET_DK, RET_DV), lambda b, h: (b, layer, 0, h, 0, 0)),
            pl.BlockSpec((None, None, None, None, RET_DK, RET_DV), lambda b, h: (b, layer, 1, h, 0, 0))]
        args += [state0, state0]
        out_specs = y_spec
        out_shape = y_shape
    return pl.pallas_call(
        functools.partial(_retention_kernel, zero_init=zero_init, n_chunks=t // RET_BLOCK, layer=layer),
        grid=(batch, RET_HEADS),
        in_specs=in_specs,
        out_specs=out_specs,
        out_shape=out_shape,
        input_output_aliases=aliases,
        scratch_shapes=[pltpu.VMEM((2, t, RET_DV), F32), pltpu.VMEM((2, RET_DK, RET_DV), F32),
                        pltpu.VMEM((2, RET_BLOCK, RET_BLOCK), F32)],
        compiler_params=_params("parallel", "parallel"),
        name="retention",
    )(*args)


def _rope_tables(rot_dim):
    rows = DEC_SEQ // GRID_W
    row = jnp.repeat(jnp.arange(rows), GRID_W)
    col = jnp.tile(jnp.arange(GRID_W), rows)
    nf = rot_dim // 4
    inv = ROPE_BASE ** (-jnp.arange(nf, dtype=F32) / nf)
    ang = jnp.stack([row, col], axis=-1).astype(F32)[:, :, None] * inv
    cos, sin = jnp.cos(ang), jnp.sin(ang)
    c = jnp.stack([cos, cos], axis=2).reshape(DEC_SEQ, rot_dim)
    s = jnp.stack([-sin, sin], axis=2).reshape(DEC_SEQ, rot_dim)
    return c, s


def _swap_rope(x):
    blocks = x.reshape(x.shape[:-1] + (2, 2, MLA_ROPE // 4))
    return blocks[..., ::-1, :].reshape(x.shape)


def _slot_orders(x):
    n0, n1, rope = x[..., :MLA_ROPE], x[..., MLA_ROPE:MLA_NOPE], x[..., MLA_NOPE:]
    zero = jnp.zeros_like(rope)
    orders = ((rope, n0, n1, zero), (n0, rope, n1, zero), (n0, n1, rope, zero), (n0, n1, zero, rope))
    return [jnp.concatenate(parts, axis=-1) for parts in orders]


def _to_head_slots(w):
    w = w.reshape(w.shape[:-2] + (MLA_HEADS // SLOT_GROUP, SLOT_GROUP, MLA_QK))
    slots = [_slot_orders(w[..., g, :])[g] for g in range(SLOT_GROUP)]
    return jnp.stack(slots, axis=-2).reshape(w.shape[:-3] + (MLA_HEADS, HEAD_SLOT))


def _head_slot_tables(c32, s32):
    t = c32.shape[0]
    ca = jnp.ones((t, SLOT_GROUP, SLOT_GROUP, MLA_ROPE), F32)
    sb = jnp.zeros((t, SLOT_GROUP, SLOT_GROUP, MLA_ROPE), F32)
    for g in range(SLOT_GROUP):
        ca = ca.at[:, g, g].set(c32)
        sb = sb.at[:, g, g].set(s32)
    return ca.reshape(t, SLOT_GROUP * HEAD_SLOT), sb.reshape(t, SLOT_GROUP * HEAD_SLOT)


def _head_slot_gains(g, scale):
    ga = jnp.stack(_slot_orders(g)) * scale
    gb = jnp.zeros((SLOT_GROUP, SLOT_GROUP, MLA_ROPE), F32)
    for s in range(SLOT_GROUP):
        gb = gb.at[s, s].set(_swap_rope(g[MLA_NOPE:]) * scale)
    return ga.reshape(1, SLOT_GROUP * HEAD_SLOT), gb.reshape(1, SLOT_GROUP * HEAD_SLOT)


def _rows3(m):
    return m.reshape(N_COND, 1, D_MODEL)


def kernel(x_prompt, x_sample, cache_ckv, cache_kpe, state_ret, c, c_ctx, mod_w, mod_b, norm1_w, norm2_w, mla_wq_a, mla_q_a_norm, mla_wq_b, mla_wkv_a, mla_kv_norm, mla_w_uk, mla_w_uv, mla_q_norm, mla_k_norm, mla_wo, ret_wq, ret_wk, ret_wv, ret_wg, ret_wo, ret_decay, ret_gn, ffn_w_gate, ffn_w_up, ffn_w_down, moe_router, moe_w_gate, moe_w_up, moe_w_down):
    x = (x_prompt.reshape(N_CTX, D_MODEL), x_sample.reshape(N_LAT, D_MODEL))
    cond = jnp.concatenate([c_ctx[None, :], c, jnp.zeros((N_COND - 1 - DEC_BATCH, D_MODEL), F32)], axis=0)
    mods = _adaln(cond, mod_w, mod_b).reshape(DEPTH, N_COND, 6, D_MODEL)

    c32, s32 = _rope_tables(MLA_ROPE)
    c32 = jnp.concatenate([jnp.ones((ROW_TILE, MLA_ROPE), F32), c32], axis=0)
    s32 = jnp.concatenate([jnp.zeros((ROW_TILE, MLA_ROPE), F32), s32], axis=0)
    rope_ca, rope_sb = _head_slot_tables(c32, s32)
    lat_kv = DEC_SEQ + PAST_LEN
    kv_tiles = lat_kv // ROW_TILE
    tab_q = lambda i: jnp.where(i < CTX_TILES, 0, 1 + (i - CTX_TILES) % TILES_PER_LAT_BATCH)
    tab_kc = lambda i: 0
    tab_kl = lambda i: jnp.where(i % kv_tiles < TILES_PER_LAT_BATCH, 1 + i % kv_tiles, 0)
    rc256, rs256 = _rope_tables(RET_DK)

    moe_wg, moe_wu, moe_wd = moe_w_gate.astype(BF16), moe_w_up.astype(BF16), moe_w_down.astype(BF16)

    new_ckv, new_kpe, new_ret = [], [], None
    for i in range(DEPTH):
        j = i // 2
        m = mods[i]
        sh1, sc1, g1, sh2, sc2, g2 = (_rows3(m[:, t]) for t in range(6))
        nw1 = norm1_w[i].reshape(1, D_MODEL)
        nw2 = norm2_w[i].reshape(1, D_MODEL)
        if i % 2 == 0:
            wqb = mla_wq_b[j].reshape(MLA_Q_RANK, MLA_HEADS, MLA_QK)
            wqb_swap = _swap_rope(wqb[:, :, MLA_NOPE:]).reshape(MLA_Q_RANK, MLA_HEADS * MLA_ROPE)
            wqb = jnp.concatenate([_to_head_slots(wqb).reshape(MLA_Q_RANK, MLA_HEADS * HEAD_SLOT), wqb_swap],
                                  axis=1).astype(BF16)
            wkv = mla_wkv_a[j]
            w_pe = wkv[:, MLA_KV_RANK:]
            wkv = jnp.concatenate([wkv[:, :MLA_KV_RANK], jnp.tile(w_pe, (1, SLOT_GROUP)),
                                   jnp.tile(_swap_rope(w_pe), (1, SLOT_GROUP))], axis=1).astype(BF16)
            wuk = mla_w_uk[j].reshape(MLA_KV_RANK, MLA_HEADS, MLA_NOPE)
            wuk = jnp.concatenate([wuk, jnp.zeros((MLA_KV_RANK, MLA_HEADS, MLA_ROPE), F32)], axis=2)
            wuk = _to_head_slots(wuk).reshape(MLA_KV_RANK, MLA_HEADS * HEAD_SLOT).astype(BF16)
            gaq, gbq = _head_slot_gains(mla_q_norm[j], MLA_QK ** -0.5 * LOG2_E)
            gak, gbk = _head_slot_gains(mla_k_norm[j], 1.0)

            q, ckv, kpe = _mla_proj(x, nw1, sh1, sc1, mla_wq_a[j].astype(BF16),
                                    mla_q_a_norm[j].reshape(1, MLA_Q_RANK), wqb, wkv,
                                    mla_kv_norm[j].reshape(1, MLA_KV_RANK), gaq, gbq, rope_ca, rope_sb, tab_q)
            new_ckv.append(ckv[:N_CTX].reshape(BATCH, SEQ, MLA_KV_RANK))
            new_kpe.append(kpe[:N_CTX, :MLA_ROPE].reshape(BATCH, SEQ, MLA_ROPE))

            kpe_x = cache_kpe[:, j]
            kpe_x = jnp.concatenate([jnp.tile(kpe_x, (1, 1, SLOT_GROUP)), jnp.tile(_swap_rope(kpe_x), (1, 1, SLOT_GROUP))],
                                    axis=2)
            ckv_l = jnp.concatenate([ckv[N_CTX:].reshape(DEC_BATCH, DEC_SEQ, MLA_KV_RANK), cache_ckv[:, j]], axis=1)
            kpe_l = jnp.concatenate([kpe[N_CTX:].reshape(DEC_BATCH, DEC_SEQ, 2 * HEAD_SLOT), kpe_x], axis=1)
            wuv = mla_w_uv[j].astype(BF16)
            k_c, v_c = _kv_expand(ckv[:N_CTX], kpe[:N_CTX], wuk, wuv, gak, gbk, rope_ca, rope_sb, tab_kc)
            k_l, v_l = _kv_expand(ckv_l.reshape(DEC_BATCH * lat_kv, MLA_KV_RANK),
                                  kpe_l.reshape(DEC_BATCH * lat_kv, 2 * HEAD_SLOT), wuk, wuv, gak, gbk,
                                  rope_ca, rope_sb, tab_kl)

            qw = MLA_HEADS * HEAD_SLOT
            o_c = _attention(q.reshape(N_TOK // SEQ, SEQ, qw), k_c.reshape(BATCH, SEQ, qw),
                             v_c.reshape(BATCH, SEQ, MLA_HEADS * MLA_V),
                             batch=BATCH, q_off=0, tq_total=SEQ, tq=SEQ, heads=MLA_HEADS)
            o_l = _attention(q.reshape(N_TOK // DEC_SEQ, DEC_SEQ, qw), k_l.reshape(DEC_BATCH, lat_kv, qw),
                             v_l.reshape(DEC_BATCH, lat_kv, MLA_HEADS * MLA_V),
                             batch=DEC_BATCH, q_off=N_CTX // DEC_SEQ, tq_total=DEC_SEQ, tq=512, heads=2)
            mix_c, mix_l = o_c.reshape(N_CTX, D_MODEL), o_l.reshape(N_LAT, D_MODEL)
            w_out = mla_wo[j].astype(BF16)
        else:
            w_in = jnp.concatenate([ret_wq[j], ret_wk[j], ret_wv[j], ret_wg[j]], axis=1).astype(BF16)
            qkvg = _ret_proj(x, nw1, sh1, sc1, w_in, rc256, rs256)
            log_gamma = -jnp.exp(ret_decay[j].astype(F32))
            gn = ret_gn[j].reshape(RET_HEADS, 1, RET_DV)
            width = qkvg.shape[1]
            y_c, new_ret = _retention(log_gamma, qkvg.reshape(N_TOK // SEQ, SEQ, width), gn, None,
                                      batch=BATCH, b_off=0, t=SEQ, layer=j, states=new_ret)
            y_l = _retention(log_gamma, qkvg.reshape(N_TOK // DEC_SEQ, DEC_SEQ, width), gn, state_ret,
                             batch=DEC_BATCH, b_off=N_CTX // DEC_SEQ, t=DEC_SEQ, layer=j)
            mix_c, mix_l = y_c.reshape(N_CTX, RET_HEADS * RET_DV), y_l.reshape(N_LAT, RET_HEADS * RET_DV)
            w_out = ret_wo[j].astype(BF16)

        if i % 2 == 0:
            x1, h2 = _proj_residual(mix_c, mix_l, w_out, x, g1, nw2, sh2, sc2)
            x = _ffn(h2, x1, g2, ffn_w_gate[j].astype(BF16), ffn_w_up[j].astype(BF16), ffn_w_down[j].astype(BF16))
        else:
            wr = jnp.pad(moe_router[j], ((0, 0), (0, 128 - N_EXPERTS)))
            wr_hi = wr.astype(BF16)
            wr_lo = (wr - wr_hi.astype(F32)).astype(BF16)
            x1, h2, logits = _proj_residual(mix_c, mix_l, w_out, x, g1, nw2, sh2, sc2, router_w=(wr_hi, wr_lo))
            top_w, routing = _route(logits[:, :N_EXPERTS])
            ya = _moe(routing, h2, moe_wg, moe_wu, moe_wd, j)
            x = _combine(ya, x1, g2, top_w, split=(i == DEPTH - 1))

    y_prompt = x[0].reshape(BATCH, SEQ, D_MODEL)
    y_sample = x[1].reshape(DEC_BATCH, DEC_SEQ, D_MODEL)
    return (y_prompt, y_sample, jnp.stack(new_ckv, axis=1), jnp.stack(new_kpe, axis=1), new_ret)
```

```python
import functools

import jax
import jax.numpy as jnp
from jax import lax
from jax.experimental import pallas as pl
from jax.experimental.pallas import tpu as pltpu

F32 = jnp.float32
BF16 = jnp.bfloat16

D_MODEL = 1024
BATCH = 32
SEQ = 256
DEPTH = 4
DEC_BATCH = 4
DEC_SEQ = 2048
PAST_LEN = 512
GRID_W = 64
ROPE_BASE = 10000.0
NORM_EPS = 1e-6

MLA_HEADS = 16
MLA_NOPE = 64
MLA_ROPE = 32
MLA_QK = MLA_NOPE + MLA_ROPE
MLA_V = 64
MLA_Q_RANK = 384
MLA_KV_RANK = 256
HEAD_SLOT = 128

RET_HEADS = 4
RET_DK = 256
RET_DV = 512
RET_BLOCK = 256

FFN_DIM = 2816
N_EXPERTS = 8
TOP_K = 2
EXPERT_DIM = 3584

N_CTX = BATCH * SEQ
N_LAT = DEC_BATCH * DEC_SEQ
N_TOK = N_CTX + N_LAT
N_COND = 8

ROW_TILE = 512
CTX_TILES = N_CTX // ROW_TILE
TILES_PER_LAT_BATCH = DEC_SEQ // ROW_TILE

MOE_ROWS = 512
MOE_TILE = 1792
MOE_STEPS = EXPERT_DIM // MOE_TILE
MOE_STEP_ROWS = MOE_ROWS // MOE_STEPS
assert MOE_STEPS >= 2
N_ASSIGN = N_TOK * TOP_K
MOE_BLOCKS = N_ASSIGN // MOE_ROWS + N_EXPERTS

VMEM_LIMIT = 56 * 1024 * 1024
LOG2_E = 1.4426950408889634


def _params(*sem):
    return pltpu.CompilerParams(dimension_semantics=sem, vmem_limit_bytes=VMEM_LIMIT)


def _cond_of_tile(i):
    return jnp.where(i < CTX_TILES, 0, 1 + (i - CTX_TILES) // TILES_PER_LAT_BATCH)


def _dot(a, b):
    return jnp.dot(a, b, preferred_element_type=F32)


def _silu(x):
    return x * (1.0 / (1.0 + jnp.exp(-x)))


def _rms(x, w):
    return x * lax.rsqrt(jnp.mean(x * x, axis=-1, keepdims=True) + NORM_EPS) * w


def _modulate(x, nw, shift, scale):
    return _rms(x, nw) * (1.0 + scale) + shift


def _adaln_kernel(c_ref, w_ref, b_ref, o_ref):
    a = _silu(c_ref[...]).astype(BF16)
    o_ref[...] = _dot(a, w_ref[...].astype(BF16)) + b_ref[...]


def _adaln(cond, mod_w, mod_b):
    tn = 1024
    return pl.pallas_call(
        _adaln_kernel,
        grid=(DEPTH, 6 * D_MODEL // tn),
        in_specs=[
            pl.BlockSpec((N_COND, D_MODEL), lambda l, j: (0, 0)),
            pl.BlockSpec((None, D_MODEL, tn), lambda l, j: (l, 0, j)),
            pl.BlockSpec((None, 1, tn), lambda l, j: (l, 0, j)),
        ],
        out_specs=pl.BlockSpec((None, N_COND, tn), lambda l, j: (l, 0, j)),
        out_shape=jax.ShapeDtypeStruct((DEPTH, N_COND, 6 * D_MODEL), F32),
        compiler_params=_params("parallel", "parallel"),
        name="adaln",
    )(cond, mod_w, mod_b.reshape(DEPTH, 1, 6 * D_MODEL))


SLOT_GROUP = HEAD_SLOT // MLA_ROPE


def _heads_norm_rope(xs, swaps, ga, gb):
    ss = [jnp.sum(x * x, axis=-1, keepdims=True) for x in xs]
    rs = [lax.rsqrt(s * (1.0 / MLA_QK) + NORM_EPS) for s in ss]
    out = []
    for hd, (r, x, sw) in enumerate(zip(rs, xs, swaps)):
        lanes = slice((hd % SLOT_GROUP) * HEAD_SLOT, (hd % SLOT_GROUP + 1) * HEAD_SLOT)
        out.append(r * (x * ga[:, lanes] + sw * gb[:, lanes]))
    return out


def _split_rows(x):
    if isinstance(x, tuple):
        return x[0], x[1], 0
    return x, x, CTX_TILES


def _split_specs(width, lat_tile0, tm=ROW_TILE):
    return [pl.BlockSpec((tm, width), lambda i: (jnp.minimum(i, CTX_TILES - 1), 0)),
            pl.BlockSpec((tm, width), lambda i: (jnp.maximum(i - CTX_TILES, 0) + lat_tile0, 0))]


def _pick_rows(ctx_ref, lat_ref):
    return jnp.where(pl.program_id(0) < CTX_TILES, ctx_ref[...], lat_ref[...])


def _mla_proj_kernel(xc_ref, xl_ref, nw_ref, sh_ref, sc_ref, wqa_ref, qan_ref, wqb_ref, wkv_ref, kvn_ref,
                     ga_ref, gb_ref, ca_ref, sb_ref, q_ref, ckv_ref, kpe_ref):
    h = _modulate(_pick_rows(xc_ref, xl_ref), nw_ref[...], sh_ref[...], sc_ref[...]).astype(BF16)
    qa = _rms(_dot(h, wqa_ref[...]), qan_ref[...]).astype(BF16)
    q = _dot(qa, wqb_ref[...])
    kv = _dot(h, wkv_ref[...])
    ckv_ref[...] = _rms(kv[:, :MLA_KV_RANK], kvn_ref[...])
    kpe_ref[...] = kv[:, MLA_KV_RANK:]
    ga = ga_ref[...] * ca_ref[...]
    gb = gb_ref[...] * sb_ref[...]
    slots = [slice(hd * HEAD_SLOT, (hd + 1) * HEAD_SLOT) for hd in range(MLA_HEADS)]
    swap0 = MLA_HEADS * HEAD_SLOT
    swaps = [q[:, swap0 + (hd // SLOT_GROUP) * HEAD_SLOT:swap0 + (hd // SLOT_GROUP + 1) * HEAD_SLOT]
             for hd in range(MLA_HEADS)]
    for sl, y in zip(slots, _heads_norm_rope([q[:, sl] for sl in slots], swaps, ga, gb)):
        q_ref[:, sl] = y.astype(BF16)


def _mla_proj(x, nw, shift, scale, wqa, qan, wqb, wkv, kvn, ga, gb, ca, sb, tab_index):
    tm = ROW_TILE
    row = lambda i: (i, 0)
    fixed = lambda i: (0, 0)
    cond = lambda i: (_cond_of_tile(i), 0, 0)
    qw = MLA_HEADS * HEAD_SLOT
    tabw = SLOT_GROUP * HEAD_SLOT
    kpew = 2 * HEAD_SLOT
    tab = lambda i: (tab_index(i), 0)
    x_ctx, x_lat, lat_tile0 = _split_rows(x)
    return pl.pallas_call(
        _mla_proj_kernel,
        grid=(N_TOK // tm,),
        in_specs=_split_specs(D_MODEL, lat_tile0) + [
            pl.BlockSpec((1, D_MODEL), fixed),
            pl.BlockSpec((None, 1, D_MODEL), cond),
            pl.BlockSpec((None, 1, D_MODEL), cond),
            pl.BlockSpec((D_MODEL, MLA_Q_RANK), fixed),
            pl.BlockSpec((1, MLA_Q_RANK), fixed),
            pl.BlockSpec((MLA_Q_RANK, wqb.shape[1]), fixed),
            pl.BlockSpec((D_MODEL, MLA_KV_RANK + kpew), fixed),
            pl.BlockSpec((1, MLA_KV_RANK), fixed),
            pl.BlockSpec((1, tabw), fixed),
            pl.BlockSpec((1, tabw), fixed),
            pl.BlockSpec((tm, tabw), tab),
            pl.BlockSpec((tm, tabw), tab),
        ],
        out_specs=[
            pl.BlockSpec((tm, qw), row),
            pl.BlockSpec((tm, MLA_KV_RANK), row),
            pl.BlockSpec((tm, kpew), row),
        ],
        out_shape=[
            jax.ShapeDtypeStruct((N_TOK, qw), BF16),
            jax.ShapeDtypeStruct((N_TOK, MLA_KV_RANK), F32),
            jax.ShapeDtypeStruct((N_TOK, kpew), F32),
        ],
        compiler_params=_params("arbitrary"),
        name="mla_proj",
    )(x_ctx, x_lat, nw, shift, scale, wqa, qan, wqb, wkv, kvn, ga, gb, ca, sb)


def _kv_expand_kernel(ckv_ref, kpe_ref, wuk_ref, wuv_ref, ga_ref, gb_ref, ca_ref, sb_ref, k_ref, v_ref):
    c = ckv_ref[...].astype(BF16)
    kn = _dot(c, wuk_ref[...])
    v_ref[...] = _dot(c, wuv_ref[...]).astype(BF16)
    kpe = kpe_ref[:, :HEAD_SLOT]
    swap = kpe_ref[:, HEAD_SLOT:]
    ga = ga_ref[...] * ca_ref[...]
    gb = gb_ref[...] * sb_ref[...]
    group = lax.broadcasted_iota(jnp.int32, kpe.shape, 1) // MLA_ROPE
    kpe_at = [jnp.where(group == g, kpe, 0.0) for g in range(SLOT_GROUP)]
    slots = [slice(hd * HEAD_SLOT, (hd + 1) * HEAD_SLOT) for hd in range(MLA_HEADS)]
    xs = [kn[:, sl] + kpe_at[hd % SLOT_GROUP] for hd, sl in enumerate(slots)]
    for sl, y in zip(slots, _heads_norm_rope(xs, [swap] * MLA_HEADS, ga, gb)):
        k_ref[:, sl] = y.astype(BF16)


def _kv_expand(ckv, kpe, wuk, wuv, ga, gb, ca, sb, tab_index):
    n = ckv.shape[0]
    tm = ROW_TILE
    row = lambda i: (i, 0)
    fixed = lambda i: (0, 0)
    tab = lambda i: (tab_index(i), 0)
    kw = MLA_HEADS * HEAD_SLOT
    vw = MLA_HEADS * MLA_V
    tabw = SLOT_GROUP * HEAD_SLOT
    return pl.pallas_call(
        _kv_expand_kernel,
        grid=(n // tm,),
        in_specs=[
            pl.BlockSpec((tm, MLA_KV_RANK), row),
            pl.BlockSpec((tm, 2 * HEAD_SLOT), row),
            pl.BlockSpec((MLA_KV_RANK, kw), fixed),
            pl.BlockSpec((MLA_KV_RANK, vw), fixed),
            pl.BlockSpec((1, tabw), fixed),
            pl.BlockSpec((1, tabw), fixed),
            pl.BlockSpec((tm, tabw), tab),
            pl.BlockSpec((tm, tabw), tab),
        ],
        out_specs=[pl.BlockSpec((tm, kw), row), pl.BlockSpec((tm, vw), row)],
        out_shape=[jax.ShapeDtypeStruct((n, kw), BF16), jax.ShapeDtypeStruct((n, vw), BF16)],
        compiler_params=_params("parallel"),
        name="kv_expand",
    )(ckv, kpe, wuk, wuv, ga, gb, ca, sb)


def _attn_kernel(q_ref, k_ref, v_ref, o_ref, *, heads):
    scores = []
    for hd in range(heads):
        qh = q_ref[:, hd * HEAD_SLOT:(hd + 1) * HEAD_SLOT]
        kh = k_ref[:, hd * HEAD_SLOT:(hd + 1) * HEAD_SLOT]
        scores.append(lax.dot_general(qh, kh, (((1,), (1,)), ((), ())), preferred_element_type=F32))
    probs, invs = [], []
    for s in scores:
        p = jnp.exp2(s - jnp.max(s, axis=-1, keepdims=True))
        invs.append(1.0 / jnp.sum(p, axis=-1, keepdims=True))
        probs.append(p.astype(BF16))
    for hd in range(heads):
        o = _dot(probs[hd], v_ref[:, hd * MLA_V:(hd + 1) * MLA_V])
        o_ref[:, hd * MLA_V:(hd + 1) * MLA_V] = (o * invs[hd]).astype(BF16)


def _attention(q, k, v, *, batch, q_off, tq_total, tq, heads):
    tk = k.shape[1]
    groups = MLA_HEADS // heads
    return pl.pallas_call(
        functools.partial(_attn_kernel, heads=heads),
        grid=(batch, groups, tq_total // tq),
        in_specs=[
            pl.BlockSpec((None, tq, heads * HEAD_SLOT), lambda b, g, i: (b + q_off, i, g)),
            pl.BlockSpec((None, tk, heads * HEAD_SLOT), lambda b, g, i: (b, 0, g)),
            pl.BlockSpec((None, tk, heads * MLA_V), lambda b, g, i: (b, 0, g)),
        ],
        out_specs=pl.BlockSpec((None, tq, heads * MLA_V), lambda b, g, i: (b, i, g)),
        out_shape=jax.ShapeDtypeStruct((batch, tq_total, MLA_HEADS * MLA_V), BF16),
        compiler_params=_params("parallel", "parallel", "parallel"),
        name="attention",
    )(q, k, v)


LANE_CHUNKS = D_MODEL // 128


def _store_row_major(ref, x):
    rows = x.shape[0]
    for c in range(LANE_CHUNKS):
        ref[pl.ds(c, rows, stride=LANE_CHUNKS), :] = x[:, c * 128:(c + 1) * 128]


def _load_row_major(ref, rows, c):
    return ref[pl.ds(c, rows, stride=LANE_CHUNKS), :]


def _proj_residual_kernel(ac_ref, al_ref, w_ref, xc_ref, xl_ref, g_ref, nw_ref, sh_ref, sc_ref, *rest, router):
    if router:
        rhi_ref, rlo_ref, x1_ref, h2_ref, lg_ref, y_sc = rest
    else:
        x1_ref, h2_ref, y_sc = rest
    i = pl.program_id(0)

    @pl.when(i < CTX_TILES)
    def _():
        y_sc[...] = _dot(ac_ref[...], w_ref[...])

    @pl.when(i >= CTX_TILES)
    def _():
        y_sc[...] = _dot(al_ref[...], w_ref[...])

    x1 = _pick_rows(xc_ref, xl_ref) + g_ref[...] * y_sc[...]
    h2 = _modulate(x1, nw_ref[...], sh_ref[...], sc_ref[...])
    if router:
        hi = h2.astype(BF16)
        lo = (h2 - hi.astype(F32)).astype(BF16)
        lg_ref[...] = _dot(hi, rhi_ref[...]) + (_dot(lo, rhi_ref[...]) + _dot(hi, rlo_ref[...]))
        _store_row_major(h2_ref, h2)
    else:
        h2_ref[...] = h2.astype(h2_ref.dtype)
    x1_ref[...] = x1


def _proj_residual(a_ctx, a_lat, w, x, gate, nw, shift, scale, router_w=None):
    tm = ROW_TILE
    kdim = w.shape[0]
    row = lambda i: (i, 0)
    fixed = lambda i: (0, 0)
    cond = lambda i: (_cond_of_tile(i), 0, 0)
    router = router_w is not None
    x_ctx, x_lat, lat_tile0 = _split_rows(x)
    in_specs = _split_specs(kdim, 0) + [pl.BlockSpec((kdim, D_MODEL), fixed)] + _split_specs(D_MODEL, lat_tile0) + [
        pl.BlockSpec((None, 1, D_MODEL), cond),
        pl.BlockSpec((1, D_MODEL), fixed),
        pl.BlockSpec((None, 1, D_MODEL), cond),
        pl.BlockSpec((None, 1, D_MODEL), cond),
    ]
    if router:
        h2_spec = pl.BlockSpec((tm * LANE_CHUNKS, 128), row)
        h2_shape = jax.ShapeDtypeStruct((N_TOK * LANE_CHUNKS, 128), F32)
    else:
        h2_spec = pl.BlockSpec((tm, D_MODEL), row)
        h2_shape = jax.ShapeDtypeStruct((N_TOK, D_MODEL), BF16)
    out_specs = [pl.BlockSpec((tm, D_MODEL), row), h2_spec]
    out_shape = [jax.ShapeDtypeStruct((N_TOK, D_MODEL), F32), h2_shape]
    args = [a_ctx, a_lat, w, x_ctx, x_lat, gate, nw, shift, scale]
    if router:
        in_specs += [pl.BlockSpec((D_MODEL, 128), fixed), pl.BlockSpec((D_MODEL, 128), fixed)]
        out_specs.append(pl.BlockSpec((tm, 128), row))
        out_shape.append(jax.ShapeDtypeStruct((N_TOK, 128), F32))
        args += list(router_w)
    return pl.pallas_call(
        functools.partial(_proj_residual_kernel, router=router),
        grid=(N_TOK // tm,),
        in_specs=in_specs,
        out_specs=out_specs,
        out_shape=out_shape,
        scratch_shapes=[pltpu.VMEM((tm, D_MODEL), F32)],
        compiler_params=_params("arbitrary"),
        name="proj_residual",
    )(*args)


def _swiglu_partial(x, wg, wu, wd):
    g = _dot(x, wg)
    u = _dot(x, wu)
    return _dot((_silu(g) * u).astype(BF16), wd)


def _ffn_kernel(h_ref, x1_ref, g_ref, wg_ref, wu_ref, wd_ref, o_ref):
    part = _swiglu_partial(h_ref[...], wg_ref[...], wu_ref[...], wd_ref[...])
    o_ref[...] = x1_ref[...] + g_ref[...] * part


def _resident(shape):
    return pl.BlockSpec(shape, lambda *_: (0,) * len(shape), pipeline_mode=pl.Buffered(1))


def _ffn(h2, x1, gate, wg, wu, wd):
    tm = ROW_TILE
    row = lambda i: (i, 0)
    return pl.pallas_call(
        _ffn_kernel,
        grid=(N_TOK // tm,),
        in_specs=[
            pl.BlockSpec((tm, D_MODEL), row),
            pl.BlockSpec((tm, D_MODEL), row),
            pl.BlockSpec((None, 1, D_MODEL), lambda i: (_cond_of_tile(i), 0, 0)),
            _resident((D_MODEL, FFN_DIM)),
            _resident((D_MODEL, FFN_DIM)),
            _resident((FFN_DIM, D_MODEL)),
        ],
        out_specs=pl.BlockSpec((tm, D_MODEL), row),
        out_shape=jax.ShapeDtypeStruct((N_TOK, D_MODEL), F32),
        compiler_params=_params("parallel"),
        name="ffn",
    )(h2, x1, gate, wg, wu, wd)


def _moe_kernel(blk_e_ref, nact_ref, order_ref, base_ref, x_hbm, wg_ref, wu_ref, wd_ref, ya_hbm,
                xrows0, xrows1, xb, acc, stage0, stage1, gsem, ssem):
    b = pl.program_id(0)
    f = pl.program_id(1)
    nact = nact_ref[0]
    xrows = (xrows0, xrows1)
    stage = (stage0, stage1)
    tile = lambda r: pl.ds(pl.multiple_of(r * LANE_CHUNKS, LANE_CHUNKS), LANE_CHUNKS)

    def row_ids(blk):
        base = base_ref[blk + 1]
        return lambda r: order_ref[base + r]

    def gather_copy(par, step, r, a):
        tok = a & (N_TOK - 1)
        return pltpu.make_async_copy(x_hbm.at[tok], xrows[par].at[step, tile(r), :], gsem.at[par])

    def scatter_copy(par, step, r, a):
        return pltpu.make_async_copy(stage[par].at[step, tile(r), :], ya_hbm.at[a], ssem.at[par])

    def wait_gather(par):
        pltpu.make_async_copy(xrows[par], xrows[par], gsem.at[par]).wait()

    def wait_scatter(par):
        pltpu.make_async_copy(stage[par], stage[par], ssem.at[par]).wait()

    @pl.when((b == 0) & (f == 0))
    def _():
        stage1[...] = jnp.zeros_like(stage1)
        ids = row_ids(0)

        def issue(r, c):
            gather_copy(0, r // MOE_STEP_ROWS, r % MOE_STEP_ROWS, ids(r)).start()
            return c

        lax.fori_loop(0, MOE_ROWS, issue, 0, unroll=8)

    def block_body(par):
        @pl.when(f == 0)
        def _():
            wait_gather(par)

            @pl.when(b >= 1)
            def _():
                wait_scatter(par)

            for step in range(MOE_STEPS):
                for c in range(LANE_CHUNKS):
                    xb[step * MOE_STEP_ROWS:(step + 1) * MOE_STEP_ROWS, c * 128:(c + 1) * 128] = (
                        _load_row_major(xrows[par].at[step], MOE_STEP_ROWS, c).astype(BF16))

        nxt, prv = row_ids(b + 1), row_ids(b - 1)
        row0 = f * MOE_STEP_ROWS
        for r in range(MOE_STEP_ROWS):
            gather_copy(1 - par, f, r, nxt(row0 + r)).start()
            scatter_copy(1 - par, f, r, prv(row0 + r)).start()
        part = _swiglu_partial(xb[...], wg_ref[...], wu_ref[...], wd_ref[...])

        @pl.when(f == 0)
        def _():
            acc[...] = part

        @pl.when((f > 0) & (f < MOE_STEPS - 1))
        def _():
            acc[...] += part

        @pl.when(f == MOE_STEPS - 1)
        def _():
            total = acc[...] + part
            for step in range(MOE_STEPS):
                _store_row_major(stage[par].at[step], total[step * MOE_STEP_ROWS:(step + 1) * MOE_STEP_ROWS])

    def drain(par):
        wait_scatter(par)
        ids = row_ids(b - 1)

        def issue(r, c):
            scatter_copy(1 - par, r // MOE_STEP_ROWS, r % MOE_STEP_ROWS, ids(r)).start()
            return c

        lax.fori_loop(0, MOE_ROWS, issue, 0, unroll=8)
        wait_gather(par)
        wait_scatter(1 - par)

    for par in range(2):
        @pl.when((b < nact) & (b % 2 == par))
        def _():
            block_body(par)

        @pl.when((b == nact) & (f == 0) & (b % 2 == par))
        def _():
            drain(par)


def _moe(routing, h2, wg, wu, wd, layer):
    tf = MOE_TILE
    wmap = lambda b, f, e, *_: (layer, e[b], 0, f)
    rows_buf = pltpu.VMEM((MOE_STEPS, MOE_STEP_ROWS * LANE_CHUNKS, 128), F32)
    grid_spec = pltpu.PrefetchScalarGridSpec(
        num_scalar_prefetch=len(routing),
        grid=(MOE_BLOCKS, MOE_STEPS),
        in_specs=[
            pl.BlockSpec(memory_space=pl.ANY),
            pl.BlockSpec((None, None, D_MODEL, tf), wmap),
            pl.BlockSpec((None, None, D_MODEL, tf), wmap),
            pl.BlockSpec((None, None, tf, D_MODEL), lambda b, f, e, *_: (layer, e[b], f, 0)),
        ],
        out_specs=pl.BlockSpec(memory_space=pl.ANY),
        scratch_shapes=[
            rows_buf,
            rows_buf,
            pltpu.VMEM((MOE_ROWS, D_MODEL), BF16),
            pltpu.VMEM((MOE_ROWS, D_MODEL), F32),
            rows_buf,
            rows_buf,
            pltpu.SemaphoreType.DMA((2,)),
            pltpu.SemaphoreType.DMA((2,)),
        ],
    )
    return pl.pallas_call(
        _moe_kernel,
        grid_spec=grid_spec,
        out_shape=jax.ShapeDtypeStruct((N_ASSIGN + MOE_ROWS, LANE_CHUNKS, 128), F32),
        compiler_params=_params("arbitrary", "arbitrary"),
        name="moe",
    )(*routing, h2.reshape(N_TOK, LANE_CHUNKS, 128), wg, wu, wd)


def _combine_kernel(y0_ref, y1_ref, x1_ref, g_ref, w_ref, *o_refs):
    w = w_ref[...]
    rows = x1_ref.shape[0]

    def emit(o_ref):
        for c in range(LANE_CHUNKS):
            sl = slice(c * 128, (c + 1) * 128)
            y = w[:, 0:1] * _load_row_major(y0_ref, rows, c) + w[:, 1:2] * _load_row_major(y1_ref, rows, c)
            o_ref[:, sl] = x1_ref[:, sl] + g_ref[:, sl] * y

    if len(o_refs) == 1:
        emit(o_refs[0])
    else:
        i = pl.program_id(0)

        @pl.when(i < CTX_TILES)
        def _():
            emit(o_refs[0])

        @pl.when(i >= CTX_TILES)
        def _():
            emit(o_refs[1])


def _combine(ya, x1, gate, top_w, split):
    tm = ROW_TILE
    row = lambda i: (i, 0)
    ya2 = ya.reshape(ya.shape[0] * LANE_CHUNKS, 128)
    if split:
        out_specs = [pl.BlockSpec((tm, D_MODEL), lambda i: (jnp.minimum(i, CTX_TILES - 1), 0)),
                     pl.BlockSpec((tm, D_MODEL), lambda i: (jnp.maximum(i - CTX_TILES, 0), 0))]
        out_shape = [jax.ShapeDtypeStruct((N_CTX, D_MODEL), F32), jax.ShapeDtypeStruct((N_LAT, D_MODEL), F32)]
    else:
        out_specs = pl.BlockSpec((tm, D_MODEL), row)
        out_shape = jax.ShapeDtypeStruct((N_TOK, D_MODEL), F32)
    return pl.pallas_call(
        _combine_kernel,
        grid=(N_TOK // tm,),
        in_specs=[
            pl.BlockSpec((tm * LANE_CHUNKS, 128), row),
            pl.BlockSpec((tm * LANE_CHUNKS, 128), lambda i: (i + N_TOK // tm, 0)),
            pl.BlockSpec((tm, D_MODEL), row),
            pl.BlockSpec((None, 1, D_MODEL), lambda i: (_cond_of_tile(i), 0, 0)),
            pl.BlockSpec((tm, TOP_K), row),
        ],
        out_specs=out_specs,
        out_shape=out_shape,
        compiler_params=_params("arbitrary"),
        name="moe_combine",
    )(ya2, ya2, x1, gate, top_w)


def _route(logits):
    top_v, top_i = lax.top_k(logits, TOP_K)
    top_w = jax.nn.softmax(top_v, axis=-1)
    e = top_i.reshape(N_ASSIGN).astype(jnp.int32)
    a = jnp.arange(N_ASSIGN, dtype=jnp.int32)
    row_id = (a % TOP_K) * N_TOK + a // TOP_K
    order = jnp.sort(e * N_ASSIGN + row_id) & (N_ASSIGN - 1)
    order = jnp.concatenate([order, N_ASSIGN + jnp.arange(MOE_ROWS, dtype=jnp.int32)])
    counts = jnp.sum((e[:, None] == jnp.arange(N_EXPERTS, dtype=jnp.int32)[None, :]).astype(jnp.int32), axis=0)
    start = jnp.cumsum(counts) - counts
    nblk = (counts + MOE_ROWS - 1) // MOE_ROWS
    blk_end = jnp.cumsum(nblk)
    blk = jnp.arange(-1, MOE_BLOCKS + 1, dtype=jnp.int32)
    blk_e = jnp.sum((blk[:, None] >= blk_end[None, :]).astype(jnp.int32), axis=1)
    blk_e = jnp.minimum(blk_e, N_EXPERTS - 1)
    within = (blk - (blk_end - nblk)[blk_e]) * MOE_ROWS
    real = (blk >= 0) & (blk < blk_end[-1])
    blk_base = jnp.where(real, start[blk_e] + within, N_ASSIGN).astype(jnp.int32)
    nact = blk_end[-1:].astype(jnp.int32)
    return top_w, (blk_e[1:-1], nact, order, blk_base)


RET_QK_W = RET_HEADS * RET_DK
RET_V_W = RET_HEADS * RET_DV
RET_COL_TILE = 1024


def _ret_proj_kernel(x_ref, nw_ref, sh_ref, sc_ref, w_ref, c_ref, s_ref, o_ref):
    def body(rope):
        h = _modulate(x_ref[...], nw_ref[...], sh_ref[...], sc_ref[...]).astype(BF16)
        for c0 in range(0, o_ref.shape[1], RET_COL_TILE):
            acc = _dot(h, w_ref[:, c0:c0 + RET_COL_TILE])
            if c0 >= 2 * RET_QK_W + RET_V_W:
                o_ref[:, c0:c0 + RET_COL_TILE] = _silu(acc).astype(BF16)
            elif c0 >= 2 * RET_QK_W:
                o_ref[:, c0:c0 + RET_COL_TILE] = acc.astype(BF16)
            else:
                x = acc * (RET_DK ** -0.5) if c0 >= RET_QK_W else acc
                if not rope:
                    o_ref[:, c0:c0 + RET_COL_TILE] = x.astype(BF16)
                    continue
                for g in range(RET_COL_TILE // 128):
                    tl = slice((g % 2) * 128, (g % 2 + 1) * 128)
                    xs = x[:, g * 128:(g + 1) * 128]
                    y = xs * c_ref[:, tl] + pltpu.roll(xs, RET_DK // 4, 1) * s_ref[:, tl]
                    o_ref[:, c0 + g * 128:c0 + (g + 1) * 128] = y.astype(BF16)

    i = pl.program_id(0)

    @pl.when(i < CTX_TILES)
    def _():
        body(False)

    @pl.when(i >= CTX_TILES)
    def _():
        body(True)


def _ret_proj(x, nw, shift, scale, w, rope_c, rope_s):
    tm = ROW_TILE
    width = w.shape[1]
    row = lambda i: (i, 0)
    cond = lambda i: (_cond_of_tile(i), 0, 0)
    tab = lambda i: (jnp.where(i < CTX_TILES, 0, (i - CTX_TILES) % TILES_PER_LAT_BATCH), 0)
    return pl.pallas_call(
        _ret_proj_kernel,
        grid=(N_TOK // tm,),
        in_specs=[
            pl.BlockSpec((tm, D_MODEL), row),
            pl.BlockSpec((1, D_MODEL), lambda i: (0, 0)),
            pl.BlockSpec((None, 1, D_MODEL), cond),
            pl.BlockSpec((None, 1, D_MODEL), cond),
            _resident((D_MODEL, width)),
            pl.BlockSpec((tm, RET_DK), tab),
            pl.BlockSpec((tm, RET_DK), tab),
        ],
        out_specs=pl.BlockSpec((tm, width), row),
        out_shape=jax.ShapeDtypeStruct((N_TOK, width), BF16),
        compiler_params=_params("parallel"),
        name="ret_proj",
    )(x, nw, shift, scale, w, rope_c, rope_s)


def _retention_kernel(lg_ref, q_ref, k_ref, v_ref, sg_ref, gn_ref, *rest, zero_init, n_chunks, layer, heads):
    if zero_init:
        y_ref, st_ref, o_acc, state, dmask = rest[-5:]
    else:
        s0f_ref, s0b_ref, y_ref, o_acc, state, dmask = rest
    hd0 = pl.program_id(1) * heads
    c = RET_BLOCK
    ri = lax.broadcasted_iota(jnp.int32, (c, c), 0).astype(F32)
    ci = lax.broadcasted_iota(jnp.int32, (c, c), 1).astype(F32)
    pos = lax.broadcasted_iota(jnp.int32, (c, 1), 0).astype(F32)
    chains = [(h, d) for h in range(heads) for d in range(2)]
    q_decay, k_decay, c_decay = {}, {}, {}
    for h, d in chains:
        lg = lg_ref[d, hd0 + h]
        rel = (ri - ci) if d == 0 else (ci - ri)
        dmask[h, d] = jnp.where(rel >= 0, jnp.exp(jnp.maximum(rel, 0.0) * lg), 0.0)
        q_decay[h, d] = jnp.exp(((pos + 1.0) if d == 0 else (c - pos)) * lg)
        k_decay[h, d] = jnp.exp(((c - 1.0 - pos) if d == 0 else pos) * lg)
        c_decay[h, d] = jnp.exp(jnp.full((1, 1), c, F32) * lg)
        if zero_init:
            state[h, d] = jnp.zeros(state.shape[2:], F32)
        else:
            state[h, d] = (s0f_ref if d == 0 else s0b_ref)[h]

    def step(t, carry):
        rows = [pl.ds(pl.multiple_of((t if d == 0 else n_chunks - 1 - t) * c, c), c) for _, d in chains]
        qs = [q_ref[r, h * RET_DK:(h + 1) * RET_DK] for r, (h, _) in zip(rows, chains)]
        ks = [k_ref[r, h * RET_DK:(h + 1) * RET_DK] for r, (h, _) in zip(rows, chains)]
        vs = [v_ref[r, h * RET_DV:(h + 1) * RET_DV] for r, (h, _) in zip(rows, chains)]
        att = [lax.dot_general(q, k, (((1,), (1,)), ((), ())), preferred_element_type=F32) * dmask[hd]
               for q, k, hd in zip(qs, ks, chains)]
        old = [state[hd] for hd in chains]
        for r, hd, a, q, v, s in zip(rows, chains, att, qs, vs, old):
            o_acc[hd[0], hd[1], r, :] = (_dot(a.astype(BF16), v)
                                         + _dot((q.astype(F32) * q_decay[hd]).astype(BF16), s.astype(BF16)))
        for hd, k, v, s in zip(chains, ks, vs, old):
            kd = (k.astype(F32) * k_decay[hd]).astype(BF16)
            state[hd] = s * c_decay[hd] + lax.dot_general(kd, v, (((0,), (0,)), ((), ())),
                                                           preferred_element_type=F32)
        return carry

    lax.fori_loop(0, n_chunks, step, 0)

    if zero_init:
        for h, d in chains:
            if st_ref.ndim == 4:
                st_ref[d, h] = state[h, d]
            else:
                st_ref[layer, d, h] = state[h, d]
                for other in range(st_ref.shape[0]):
                    if other != layer:
                        st_ref[other, d, h] = jnp.zeros(state.shape[2:], F32)

    for h in range(heads):
        cols = slice(h * RET_DV, (h + 1) * RET_DV)
        y = _rms(o_acc[h, 0] + o_acc[h, 1], gn_ref[h])
        y_ref[:, cols] = (sg_ref[:, cols].astype(F32) * y).astype(BF16)


def _retention(log_gamma, qkvg, gn, state0, *, batch, b_off, t, layer, states=None):
    zero_init = state0 is None
    heads = RET_HEADS if t == RET_BLOCK else 1
    qw, vw = heads * RET_DK, heads * RET_DV
    kblk = RET_HEADS * RET_DK // qw
    vblk = 2 * RET_HEADS * RET_DK // vw
    gblk = vblk + RET_HEADS * RET_DV // vw
    in_specs = [
        pl.BlockSpec(memory_space=pltpu.SMEM),
        pl.BlockSpec((None, t, qw), lambda b, h: (b + b_off, 0, h)),
        pl.BlockSpec((None, t, qw), lambda b, h: (b + b_off, 0, kblk + h)),
        pl.BlockSpec((None, t, vw), lambda b, h: (b + b_off, 0, vblk + h)),
        pl.BlockSpec((None, t, vw), lambda b, h: (b + b_off, 0, gblk + h)),
        pl.BlockSpec((heads, 1, RET_DV), lambda b, h: (h, 0, 0)),
    ]
    args = [log_gamma, qkvg, qkvg, qkvg, qkvg, gn]
    y_spec = pl.BlockSpec((None, t, vw), lambda b, h: (b, 0, h))
    y_shape = jax.ShapeDtypeStruct((batch, t, RET_HEADS * RET_DV), BF16)
    aliases = {}
    if zero_init:
        n_ret = DEPTH // 2
        st_shape = jax.ShapeDtypeStruct((batch, n_ret, 2, RET_HEADS, RET_DK, RET_DV), F32)
        if states is None:
            st_spec = pl.BlockSpec((None, n_ret, 2, heads, RET_DK, RET_DV), lambda b, h: (b, 0, 0, h, 0, 0))
        else:
            st_spec = pl.BlockSpec((None, None, 2, heads, RET_DK, RET_DV), lambda b, h: (b, layer, 0, h, 0, 0))
            aliases = {len(args): 1}
            in_specs.append(pl.BlockSpec(memory_space=pl.ANY))
            args.append(states)
        out_specs = [y_spec, st_spec]
        out_shape = [y_shape, st_shape]
    else:
        in_specs += [
            pl.BlockSpec((None, None, None, heads, RET_DK, RET_DV), lambda b, h: (b, layer, 0, h, 0, 0)),
            pl.BlockSpec((None, None, None, heads, RET_DK, RET_DV), lambda b, h: (b, layer, 1, h, 0, 0))]
        args += [state0, state0]
        out_specs = y_spec
        out_shape = y_shape
    return pl.pallas_call(
        functools.partial(_retention_kernel, zero_init=zero_init, n_chunks=t // RET_BLOCK, layer=layer,
                          heads=heads),
        grid=(batch, RET_HEADS // heads),
        in_specs=in_specs,
        out_specs=out_specs,
        out_shape=out_shape,
        input_output_aliases=aliases,
        scratch_shapes=[pltpu.VMEM((heads, 2, t, RET_DV), F32), pltpu.VMEM((heads, 2, RET_DK, RET_DV), F32),
                        pltpu.VMEM((heads, 2, RET_BLOCK, RET_BLOCK), F32)],
        compiler_params=_params("parallel", "parallel"),
        name="retention",
    )(*args)


def _rope_tables(rot_dim):
    rows = DEC_SEQ // GRID_W
    row = jnp.repeat(jnp.arange(rows), GRID_W)
    col = jnp.tile(jnp.arange(GRID_W), rows)
    nf = rot_dim // 4
    inv = ROPE_BASE ** (-jnp.arange(nf, dtype=F32) / nf)
    ang = jnp.stack([row, col], axis=-1).astype(F32)[:, :, None] * inv
    cos, sin = jnp.cos(ang), jnp.sin(ang)
    c = jnp.stack([cos, cos], axis=2).reshape(DEC_SEQ, rot_dim)
    s = jnp.stack([-sin, sin], axis=2).reshape(DEC_SEQ, rot_dim)
    return c, s


def _swap_rope(x):
    blocks = x.reshape(x.shape[:-1] + (2, 2, MLA_ROPE // 4))
    return blocks[..., ::-1, :].reshape(x.shape)


def _slot_orders(x):
    n0, n1, rope = x[..., :MLA_ROPE], x[..., MLA_ROPE:MLA_NOPE], x[..., MLA_NOPE:]
    zero = jnp.zeros_like(rope)
    orders = ((rope, n0, n1, zero), (n0, rope, n1, zero), (n0, n1, rope, zero), (n0, n1, zero, rope))
    return [jnp.concatenate(parts, axis=-1) for parts in orders]


def _to_head_slots(w):
    w = w.reshape(w.shape[:-2] + (MLA_HEADS // SLOT_GROUP, SLOT_GROUP, MLA_QK))
    slots = [_slot_orders(w[..., g, :])[g] for g in range(SLOT_GROUP)]
    return jnp.stack(slots, axis=-2).reshape(w.shape[:-3] + (MLA_HEADS, HEAD_SLOT))


def _head_slot_tables(c32, s32):
    t = c32.shape[0]
    ca = jnp.ones((t, SLOT_GROUP, SLOT_GROUP, MLA_ROPE), F32)
    sb = jnp.zeros((t, SLOT_GROUP, SLOT_GROUP, MLA_ROPE), F32)
    for g in range(SLOT_GROUP):
        ca = ca.at[:, g, g].set(c32)
        sb = sb.at[:, g, g].set(s32)
    return ca.reshape(t, SLOT_GROUP * HEAD_SLOT), sb.reshape(t, SLOT_GROUP * HEAD_SLOT)


def _head_slot_gains(g, scale):
    ga = jnp.stack(_slot_orders(g)) * scale
    gb = jnp.zeros((SLOT_GROUP, SLOT_GROUP, MLA_ROPE), F32)
    for s in range(SLOT_GROUP):
        gb = gb.at[s, s].set(_swap_rope(g[MLA_NOPE:]) * scale)
    return ga.reshape(1, SLOT_GROUP * HEAD_SLOT), gb.reshape(1, SLOT_GROUP * HEAD_SLOT)


def _rows3(m):
    return m.reshape(N_COND, 1, D_MODEL)


def kernel(x_prompt, x_sample, cache_ckv, cache_kpe, state_ret, c, c_ctx, mod_w, mod_b, norm1_w, norm2_w, mla_wq_a, mla_q_a_norm, mla_wq_b, mla_wkv_a, mla_kv_norm, mla_w_uk, mla_w_uv, mla_q_norm, mla_k_norm, mla_wo, ret_wq, ret_wk, ret_wv, ret_wg, ret_wo, ret_decay, ret_gn, ffn_w_gate, ffn_w_up, ffn_w_down, moe_router, moe_w_gate, moe_w_up, moe_w_down):
    x = (x_prompt.reshape(N_CTX, D_MODEL), x_sample.reshape(N_LAT, D_MODEL))
    cond = jnp.concatenate([c_ctx[None, :], c, jnp.zeros((N_COND - 1 - DEC_BATCH, D_MODEL), F32)], axis=0)
    mods = _adaln(cond, mod_w, mod_b).reshape(DEPTH, N_COND, 6, D_MODEL)

    c32, s32 = _rope_tables(MLA_ROPE)
    c32 = jnp.concatenate([jnp.ones((ROW_TILE, MLA_ROPE), F32), c32], axis=0)
    s32 = jnp.concatenate([jnp.zeros((ROW_TILE, MLA_ROPE), F32), s32], axis=0)
    rope_ca, rope_sb = _head_slot_tables(c32, s32)
    lat_kv = DEC_SEQ + PAST_LEN
    kv_tiles = lat_kv // ROW_TILE
    tab_q = lambda i: jnp.where(i < CTX_TILES, 0, 1 + (i - CTX_TILES) % TILES_PER_LAT_BATCH)
    tab_kc = lambda i: 0
    tab_kl = lambda i: jnp.where(i % kv_tiles < TILES_PER_LAT_BATCH, 1 + i % kv_tiles, 0)
    rc256, rs256 = _rope_tables(RET_DK)

    moe_wg, moe_wu, moe_wd = moe_w_gate.astype(BF16), moe_w_up.astype(BF16), moe_w_down.astype(BF16)

    new_ckv, new_kpe, new_ret = [], [], None
    for i in range(DEPTH):
        j = i // 2
        m = mods[i]
        sh1, sc1, g1, sh2, sc2, g2 = (_rows3(m[:, t]) for t in range(6))
        nw1 = norm1_w[i].reshape(1, D_MODEL)
        nw2 = norm2_w[i].reshape(1, D_MODEL)
        if i % 2 == 0:
            wqb = mla_wq_b[j].reshape(MLA_Q_RANK, MLA_HEADS, MLA_QK)
            wqb_swap = _swap_rope(wqb[:, :, MLA_NOPE:]).reshape(MLA_Q_RANK, MLA_HEADS * MLA_ROPE)
            wqb = jnp.concatenate([_to_head_slots(wqb).reshape(MLA_Q_RANK, MLA_HEADS * HEAD_SLOT), wqb_swap],
                                  axis=1).astype(BF16)
            wkv = mla_wkv_a[j]
            w_pe = wkv[:, MLA_KV_RANK:]
            wkv = jnp.concatenate([wkv[:, :MLA_KV_RANK], jnp.tile(w_pe, (1, SLOT_GROUP)),
                                   jnp.tile(_swap_rope(w_pe), (1, SLOT_GROUP))], axis=1).astype(BF16)
            wuk = mla_w_uk[j].reshape(MLA_KV_RANK, MLA_HEADS, MLA_NOPE)
            wuk = jnp.concatenate([wuk, jnp.zeros((MLA_KV_RANK, MLA_HEADS, MLA_ROPE), F32)], axis=2)
            wuk = _to_head_slots(wuk).reshape(MLA_KV_RANK, MLA_HEADS * HEAD_SLOT).astype(BF16)
            gaq, gbq = _head_slot_gains(mla_q_norm[j], MLA_QK ** -0.5 * LOG2_E)
            gak, gbk = _head_slot_gains(mla_k_norm[j], 1.0)

            q, ckv, kpe = _mla_proj(x, nw1, sh1, sc1, mla_wq_a[j].astype(BF16),
                                    mla_q_a_norm[j].reshape(1, MLA_Q_RANK), wqb, wkv,
                                    mla_kv_norm[j].reshape(1, MLA_KV_RANK), gaq, gbq, rope_ca, rope_sb, tab_q)
            new_ckv.append(ckv[:N_CTX].reshape(BATCH, SEQ, MLA_KV_RANK))
            new_kpe.append(kpe[:N_CTX, :MLA_ROPE].reshape(BATCH, SEQ, MLA_ROPE))

            kpe_x = cache_kpe[:, j]
            kpe_x = jnp.concatenate([jnp.tile(kpe_x, (1, 1, SLOT_GROUP)), jnp.tile(_swap_rope(kpe_x), (1, 1, SLOT_GROUP))],
                                    axis=2)
            ckv_l = jnp.concatenate([ckv[N_CTX:].reshape(DEC_BATCH, DEC_SEQ, MLA_KV_RANK), cache_ckv[:, j]], axis=1)
            kpe_l = jnp.concatenate([kpe[N_CTX:].reshape(DEC_BATCH, DEC_SEQ, 2 * HEAD_SLOT), kpe_x], axis=1)
            wuv = mla_w_uv[j].astype(BF16)
            k_c, v_c = _kv_expand(ckv[:N_CTX], kpe[:N_CTX], wuk, wuv, gak, gbk, rope_ca, rope_sb, tab_kc)
            k_l, v_l = _kv_expand(ckv_l.reshape(DEC_BATCH * lat_kv, MLA_KV_RANK),
                                  kpe_l.reshape(DEC_BATCH * lat_kv, 2 * HEAD_SLOT), wuk, wuv, gak, gbk,
                                  rope_ca, rope_sb, tab_kl)

            qw = MLA_HEADS * HEAD_SLOT
            o_c = _attention(q.reshape(N_TOK // SEQ, SEQ, qw), k_c.reshape(BATCH, SEQ, qw),
                             v_c.reshape(BATCH, SEQ, MLA_HEADS * MLA_V),
                             batch=BATCH, q_off=0, tq_total=SEQ, tq=SEQ, heads=MLA_HEADS)
            o_l = _attention(q.reshape(N_TOK // DEC_SEQ, DEC_SEQ, qw), k_l.reshape(DEC_BATCH, lat_kv, qw),
                             v_l.reshape(DEC_BATCH, lat_kv, MLA_HEADS * MLA_V),
                             batch=DEC_BATCH, q_off=N_CTX // DEC_SEQ, tq_total=DEC_SEQ, tq=512, heads=2)
            mix_c, mix_l = o_c.reshape(N_CTX, D_MODEL), o_l.reshape(N_LAT, D_MODEL)
            w_out = mla_wo[j].astype(BF16)
        else:
            w_in = jnp.concatenate([ret_wq[j], ret_wk[j], ret_wv[j], ret_wg[j]], axis=1).astype(BF16)
            qkvg = _ret_proj(x, nw1, sh1, sc1, w_in, rc256, rs256)
            log_gamma = -jnp.exp(ret_decay[j].astype(F32))
            gn = ret_gn[j].reshape(RET_HEADS, 1, RET_DV)
            width = qkvg.shape[1]
            y_c, new_ret = _retention(log_gamma, qkvg.reshape(N_TOK // SEQ, SEQ, width), gn, None,
                                      batch=BATCH, b_off=0, t=SEQ, layer=j, states=new_ret)
            y_l = _retention(log_gamma, qkvg.reshape(N_TOK // DEC_SEQ, DEC_SEQ, width), gn, state_ret,
                             batch=DEC_BATCH, b_off=N_CTX // DEC_SEQ, t=DEC_SEQ, layer=j)
            mix_c, mix_l = y_c.reshape(N_CTX, RET_HEADS * RET_DV), y_l.reshape(N_LAT, RET_HEADS * RET_DV)
            w_out = ret_wo[j].astype(BF16)

        if i % 2 == 0:
            x1, h2 = _proj_residual(mix_c, mix_l, w_out, x, g1, nw2, sh2, sc2)
            x = _ffn(h2, x1, g2, ffn_w_gate[j].astype(BF16), ffn_w_up[j].astype(BF16), ffn_w_down[j].astype(BF16))
        else:
            wr = jnp.pad(moe_router[j], ((0, 0), (0, 128 - N_EXPERTS)))
            wr_hi = wr.astype(BF16)
            wr_lo = (wr - wr_hi.astype(F32)).astype(BF16)
            x1, h2, logits = _proj_residual(mix_c, mix_l, w_out, x, g1, nw2, sh2, sc2, router_w=(wr_hi, wr_lo))
            top_w, routing = _route(logits[:, :N_EXPERTS])
            ya = _moe(routing, h2, moe_wg, moe_wu, moe_wd, j)
            x = _combine(ya, x1, g2, top_w, split=(i == DEPTH - 1))

    y_prompt = x[0].reshape(BATCH, SEQ, D_MODEL)
    y_sample = x[1].reshape(DEC_BATCH, DEC_SEQ, D_MODEL)
    return (y_prompt, y_sample, jnp.stack(new_ckv, axis=1), jnp.stack(new_kpe, axis=1), new_ret)
```

```python
import functools

import jax
import jax.numpy as jnp
from jax import lax
from jax.experimental import pallas as pl
from jax.experimental.pallas import tpu as pltpu

F32 = jnp.float32
BF16 = jnp.bfloat16

D_MODEL = 1024
BATCH = 32
SEQ = 256
DEPTH = 4
DEC_BATCH = 4
DEC_SEQ = 2048
PAST_LEN = 512
GRID_W = 64
ROPE_BASE = 10000.0
NORM_EPS = 1e-6

MLA_HEADS = 16
MLA_NOPE = 64
MLA_ROPE = 32
MLA_QK = MLA_NOPE + MLA_ROPE
MLA_V = 64
MLA_Q_RANK = 384
MLA_KV_RANK = 256
HEAD_SLOT = 128

RET_HEADS = 4
RET_DK = 256
RET_DV = 512
RET_BLOCK = 256

FFN_DIM = 2816
N_EXPERTS = 8
TOP_K = 2
EXPERT_DIM = 3584

N_CTX = BATCH * SEQ
N_LAT = DEC_BATCH * DEC_SEQ
N_TOK = N_CTX + N_LAT
N_COND = 8

ROW_TILE = 512
CTX_TILES = N_CTX // ROW_TILE
TILES_PER_LAT_BATCH = DEC_SEQ // ROW_TILE

MOE_ROWS = 512
MOE_TILE = 1792
MOE_STEPS = EXPERT_DIM // MOE_TILE
MOE_STEP_ROWS = MOE_ROWS // MOE_STEPS
assert MOE_STEPS >= 2
N_ASSIGN = N_TOK * TOP_K
MOE_BLOCKS = N_ASSIGN // MOE_ROWS + N_EXPERTS

VMEM_LIMIT = 56 * 1024 * 1024
LOG2_E = 1.4426950408889634


def _params(*sem):
    return pltpu.CompilerParams(dimension_semantics=sem, vmem_limit_bytes=VMEM_LIMIT)


def _cond_of_tile(i):
    return jnp.where(i < CTX_TILES, 0, 1 + (i - CTX_TILES) // TILES_PER_LAT_BATCH)


def _dot(a, b):
    return jnp.dot(a, b, preferred_element_type=F32)


def _silu(x):
    return x * (1.0 / (1.0 + jnp.exp(-x)))


def _rms(x, w):
    return x * lax.rsqrt(jnp.mean(x * x, axis=-1, keepdims=True) + NORM_EPS) * w


def _modulate(x, nw, shift, scale):
    return _rms(x, nw) * (1.0 + scale) + shift


def _adaln_kernel(c_ref, w_ref, b_ref, o_ref):
    a = _silu(c_ref[...]).astype(BF16)
    o_ref[...] = _dot(a, w_ref[...].astype(BF16)) + b_ref[...]


def _adaln(cond, mod_w, mod_b):
    tn = 1024
    return pl.pallas_call(
        _adaln_kernel,
        grid=(DEPTH, 6 * D_MODEL // tn),
        in_specs=[
            pl.BlockSpec((N_COND, D_MODEL), lambda l, j: (0, 0)),
            pl.BlockSpec((None, D_MODEL, tn), lambda l, j: (l, 0, j)),
            pl.BlockSpec((None, 1, tn), lambda l, j: (l, 0, j)),
        ],
        out_specs=pl.BlockSpec((None, N_COND, tn), lambda l, j: (l, 0, j)),
        out_shape=jax.ShapeDtypeStruct((DEPTH, N_COND, 6 * D_MODEL), F32),
        compiler_params=_params("parallel", "parallel"),
        name="adaln",
    )(cond, mod_w, mod_b.reshape(DEPTH, 1, 6 * D_MODEL))


SLOT_GROUP = HEAD_SLOT // MLA_ROPE


def _heads_norm_rope(xs, swaps, ga, gb):
    ss = [jnp.sum(x * x, axis=-1, keepdims=True) for x in xs]
    rs = [lax.rsqrt(s * (1.0 / MLA_QK) + NORM_EPS) for s in ss]
    out = []
    for hd, (r, x, sw) in enumerate(zip(rs, xs, swaps)):
        lanes = slice((hd % SLOT_GROUP) * HEAD_SLOT, (hd % SLOT_GROUP + 1) * HEAD_SLOT)
        out.append(r * (x * ga[:, lanes] + sw * gb[:, lanes]))
    return out


def _split_rows(x):
    if isinstance(x, tuple):
        return x[0], x[1], 0
    return x, x, CTX_TILES


def _split_specs(width, lat_tile0, tm=ROW_TILE):
    return [pl.BlockSpec((tm, width), lambda i: (jnp.minimum(i, CTX_TILES - 1), 0)),
            pl.BlockSpec((tm, width), lambda i: (jnp.maximum(i - CTX_TILES, 0) + lat_tile0, 0))]


def _pick_rows(ctx_ref, lat_ref):
    return jnp.where(pl.program_id(0) < CTX_TILES, ctx_ref[...], lat_ref[...])


def _mla_proj_kernel(xc_ref, xl_ref, nw_ref, sh_ref, sc_ref, wqa_ref, qan_ref, wqb_ref, wkv_ref, kvn_ref,
                     ga_ref, gb_ref, ca_ref, sb_ref, q_ref, ckv_ref, kpe_ref):
    h = _modulate(_pick_rows(xc_ref, xl_ref), nw_ref[...], sh_ref[...], sc_ref[...]).astype(BF16)
    qa = _rms(_dot(h, wqa_ref[...]), qan_ref[...]).astype(BF16)
    q = _dot(qa, wqb_ref[...])
    kv = _dot(h, wkv_ref[...])
    ckv_ref[...] = _rms(kv[:, :MLA_KV_RANK], kvn_ref[...])
    kpe_ref[...] = kv[:, MLA_KV_RANK:]
    ga = ga_ref[...] * ca_ref[...]
    gb = gb_ref[...] * sb_ref[...]
    slots = [slice(hd * HEAD_SLOT, (hd + 1) * HEAD_SLOT) for hd in range(MLA_HEADS)]
    swap0 = MLA_HEADS * HEAD_SLOT
    swaps = [q[:, swap0 + (hd // SLOT_GROUP) * HEAD_SLOT:swap0 + (hd // SLOT_GROUP + 1) * HEAD_SLOT]
             for hd in range(MLA_HEADS)]
    for sl, y in zip(slots, _heads_norm_rope([q[:, sl] for sl in slots], swaps, ga, gb)):
        q_ref[:, sl] = y.astype(BF16)


def _mla_proj(x, nw, shift, scale, wqa, qan, wqb, wkv, kvn, ga, gb, ca, sb, tab_index):
    tm = ROW_TILE
    row = lambda i: (i, 0)
    fixed = lambda i: (0, 0)
    cond = lambda i: (_cond_of_tile(i), 0, 0)
    qw = MLA_HEADS * HEAD_SLOT
    tabw = SLOT_GROUP * HEAD_SLOT
    kpew = 2 * HEAD_SLOT
    tab = lambda i: (tab_index(i), 0)
    x_ctx, x_lat, lat_tile0 = _split_rows(x)
    return pl.pallas_call(
        _mla_proj_kernel,
        grid=(N_TOK // tm,),
        in_specs=_split_specs(D_MODEL, lat_tile0) + [
            pl.BlockSpec((1, D_MODEL), fixed),
            pl.BlockSpec((None, 1, D_MODEL), cond),
            pl.BlockSpec((None, 1, D_MODEL), cond),
            pl.BlockSpec((D_MODEL, MLA_Q_RANK), fixed),
            pl.BlockSpec((1, MLA_Q_RANK), fixed),
            pl.BlockSpec((MLA_Q_RANK, wqb.shape[1]), fixed),
            pl.BlockSpec((D_MODEL, MLA_KV_RANK + kpew), fixed),
            pl.BlockSpec((1, MLA_KV_RANK), fixed),
            pl.BlockSpec((1, tabw), fixed),
            pl.BlockSpec((1, tabw), fixed),
            pl.BlockSpec((tm, tabw), tab),
            pl.BlockSpec((tm, tabw), tab),
        ],
        out_specs=[
            pl.BlockSpec((tm, qw), row),
            pl.BlockSpec((tm, MLA_KV_RANK), row),
            pl.BlockSpec((tm, kpew), row),
        ],
        out_shape=[
            jax.ShapeDtypeStruct((N_TOK, qw), BF16),
            jax.ShapeDtypeStruct((N_TOK, MLA_KV_RANK), F32),
            jax.ShapeDtypeStruct((N_TOK, kpew), F32),
        ],
        compiler_params=_params("arbitrary"),
        name="mla_proj",
    )(x_ctx, x_lat, nw, shift, scale, wqa, qan, wqb, wkv, kvn, ga, gb, ca, sb)


def _kv_expand_kernel(ckv_ref, kpe_ref, *rest, cached):
    if cached:
        cckv_ref, ckpe_ref, wuk_ref, wuv_ref, ga_ref, gb_ref, ca_ref, sb_ref, k_ref, v_ref = rest
        from_cache = pl.program_id(0) % LAT_KV_TILES == LAT_KV_TILES - 1
        c = jnp.where(from_cache, cckv_ref[...], ckv_ref[...]).astype(BF16)
        kpe2 = jnp.where(from_cache, ckpe_ref[...], kpe_ref[...])
    else:
        wuk_ref, wuv_ref, ga_ref, gb_ref, ca_ref, sb_ref, k_ref, v_ref = rest
        c = ckv_ref[...].astype(BF16)
        kpe2 = kpe_ref[...]
    kn = _dot(c, wuk_ref[...])
    v_ref[...] = _dot(c, wuv_ref[...]).astype(BF16)
    kpe = kpe2[:, :HEAD_SLOT]
    swap = kpe2[:, HEAD_SLOT:]
    ga = ga_ref[...] * ca_ref[...]
    gb = gb_ref[...] * sb_ref[...]
    group = lax.broadcasted_iota(jnp.int32, kpe.shape, 1) // MLA_ROPE
    kpe_at = [jnp.where(group == g, kpe, 0.0) for g in range(SLOT_GROUP)]
    slots = [slice(hd * HEAD_SLOT, (hd + 1) * HEAD_SLOT) for hd in range(MLA_HEADS)]
    xs = [kn[:, sl] + kpe_at[hd % SLOT_GROUP] for hd, sl in enumerate(slots)]
    for sl, y in zip(slots, _heads_norm_rope(xs, [swap] * MLA_HEADS, ga, gb)):
        k_ref[:, sl] = y.astype(BF16)


LAT_KV_TILES = (DEC_SEQ + PAST_LEN) // ROW_TILE
assert PAST_LEN == ROW_TILE


def _kv_expand(ckv, kpe, wuk, wuv, ga, gb, ca, sb, tab_index, cache=None):
    tm = ROW_TILE
    row = lambda i: (i, 0)
    fixed = lambda i: (0, 0)
    tab = lambda i: (tab_index(i), 0)
    kw = MLA_HEADS * HEAD_SLOT
    vw = MLA_HEADS * MLA_V
    tabw = SLOT_GROUP * HEAD_SLOT
    if cache is None:
        n = N_CTX
        src = row
        cache_specs, cache_args = [], []
    else:
        n = DEC_BATCH * (DEC_SEQ + PAST_LEN)
        cache_ckv, layer, cache_kpe2 = cache
        src = lambda i: (CTX_TILES + (i // LAT_KV_TILES) * TILES_PER_LAT_BATCH
                         + jnp.minimum(i % LAT_KV_TILES, TILES_PER_LAT_BATCH - 1), 0)
        cache_specs = [pl.BlockSpec((None, None, PAST_LEN, MLA_KV_RANK), lambda i: (i // LAT_KV_TILES, layer, 0, 0)),
                       pl.BlockSpec((None, PAST_LEN, 2 * HEAD_SLOT), lambda i: (i // LAT_KV_TILES, 0, 0))]
        cache_args = [cache_ckv, cache_kpe2]
    return pl.pallas_call(
        functools.partial(_kv_expand_kernel, cached=cache is not None),
        grid=(n // tm,),
        in_specs=[
            pl.BlockSpec((tm, MLA_KV_RANK), src),
            pl.BlockSpec((tm, 2 * HEAD_SLOT), src),
        ] + cache_specs + [
            pl.BlockSpec((MLA_KV_RANK, kw), fixed),
            pl.BlockSpec((MLA_KV_RANK, vw), fixed),
            pl.BlockSpec((1, tabw), fixed),
            pl.BlockSpec((1, tabw), fixed),
            pl.BlockSpec((tm, tabw), tab),
            pl.BlockSpec((tm, tabw), tab),
        ],
        out_specs=[pl.BlockSpec((tm, kw), row), pl.BlockSpec((tm, vw), row)],
        out_shape=[jax.ShapeDtypeStruct((n, kw), BF16), jax.ShapeDtypeStruct((n, vw), BF16)],
        compiler_params=_params("arbitrary"),
        name="kv_expand",
    )(ckv, kpe, *cache_args, wuk, wuv, ga, gb, ca, sb)


def _attn_kernel(q_ref, k_ref, v_ref, o_ref, *, heads):
    scores = []
    for hd in range(heads):
        qh = q_ref[:, hd * HEAD_SLOT:(hd + 1) * HEAD_SLOT]
        kh = k_ref[:, hd * HEAD_SLOT:(hd + 1) * HEAD_SLOT]
        scores.append(lax.dot_general(qh, kh, (((1,), (1,)), ((), ())), preferred_element_type=F32))
    probs, invs = [], []
    for s in scores:
        p = jnp.exp2(s - jnp.max(s, axis=-1, keepdims=True))
        invs.append(1.0 / jnp.sum(p, axis=-1, keepdims=True))
        probs.append(p.astype(BF16))
    for hd in range(heads):
        o = _dot(probs[hd], v_ref[:, hd * MLA_V:(hd + 1) * MLA_V])
        o_ref[:, hd * MLA_V:(hd + 1) * MLA_V] = (o * invs[hd]).astype(BF16)


def _attention(q, k, v, *, batch, q_off, tq_total, tq, heads):
    tk = k.shape[1]
    groups = MLA_HEADS // heads
    return pl.pallas_call(
        functools.partial(_attn_kernel, heads=heads),
        grid=(batch, groups, tq_total // tq),
        in_specs=[
            pl.BlockSpec((None, tq, heads * HEAD_SLOT), lambda b, g, i: (b + q_off, i, g)),
            pl.BlockSpec((None, tk, heads * HEAD_SLOT), lambda b, g, i: (b, 0, g)),
            pl.BlockSpec((None, tk, heads * MLA_V), lambda b, g, i: (b, 0, g)),
        ],
        out_specs=pl.BlockSpec((None, tq, heads * MLA_V), lambda b, g, i: (b, i, g)),
        out_shape=jax.ShapeDtypeStruct((batch, tq_total, MLA_HEADS * MLA_V), BF16),
        compiler_params=_params("parallel", "parallel", "parallel"),
        name="attention",
    )(q, k, v)


LANE_CHUNKS = D_MODEL // 128


def _store_row_major(ref, x):
    rows = x.shape[0]
    for c in range(LANE_CHUNKS):
        ref[pl.ds(c, rows, stride=LANE_CHUNKS), :] = x[:, c * 128:(c + 1) * 128]


def _load_row_major(ref, rows, c):
    return ref[pl.ds(c, rows, stride=LANE_CHUNKS), :]


def _proj_residual_kernel(ac_ref, al_ref, w_ref, xc_ref, xl_ref, g_ref, nw_ref, sh_ref, sc_ref, *rest, router):
    if router:
        rhi_ref, rlo_ref, x1_ref, h2_ref, lg_ref, y_sc = rest
    else:
        x1_ref, h2_ref, y_sc = rest
    i = pl.program_id(0)

    @pl.when(i < CTX_TILES)
    def _():
        y_sc[...] = _dot(ac_ref[...], w_ref[...])

    @pl.when(i >= CTX_TILES)
    def _():
        y_sc[...] = _dot(al_ref[...], w_ref[...])

    x1 = _pick_rows(xc_ref, xl_ref) + g_ref[...] * y_sc[...]
    h2 = _modulate(x1, nw_ref[...], sh_ref[...], sc_ref[...])
    if router:
        hi = h2.astype(BF16)
        lo = (h2 - hi.astype(F32)).astype(BF16)
        lg_ref[...] = _dot(hi, rhi_ref[...]) + (_dot(lo, rhi_ref[...]) + _dot(hi, rlo_ref[...]))
        _store_row_major(h2_ref, h2)
    else:
        h2_ref[...] = h2.astype(h2_ref.dtype)
    x1_ref[...] = x1


def _proj_residual(a_ctx, a_lat, w, x, gate, nw, shift, scale, router_w=None):
    tm = ROW_TILE
    kdim = w.shape[0]
    row = lambda i: (i, 0)
    fixed = lambda i: (0, 0)
    cond = lambda i: (_cond_of_tile(i), 0, 0)
    router = router_w is not None
    x_ctx, x_lat, lat_tile0 = _split_rows(x)
    in_specs = _split_specs(kdim, 0) + [pl.BlockSpec((kdim, D_MODEL), fixed)] + _split_specs(D_MODEL, lat_tile0) + [
        pl.BlockSpec((None, 1, D_MODEL), cond),
        pl.BlockSpec((1, D_MODEL), fixed),
        pl.BlockSpec((None, 1, D_MODEL), cond),
        pl.BlockSpec((None, 1, D_MODEL), cond),
    ]
    if router:
        h2_spec = pl.BlockSpec((tm * LANE_CHUNKS, 128), row)
        h2_shape = jax.ShapeDtypeStruct((N_TOK * LANE_CHUNKS, 128), F32)
    else:
        h2_spec = pl.BlockSpec((tm, D_MODEL), row)
        h2_shape = jax.ShapeDtypeStruct((N_TOK, D_MODEL), BF16)
    out_specs = [pl.BlockSpec((tm, D_MODEL), row), h2_spec]
    out_shape = [jax.ShapeDtypeStruct((N_TOK, D_MODEL), F32), h2_shape]
    args = [a_ctx, a_lat, w, x_ctx, x_lat, gate, nw, shift, scale]
    if router:
        in_specs += [pl.BlockSpec((D_MODEL, 128), fixed), pl.BlockSpec((D_MODEL, 128), fixed)]
        out_specs.append(pl.BlockSpec((tm, 128), row))
        out_shape.append(jax.ShapeDtypeStruct((N_TOK, 128), F32))
        args += list(router_w)
    return pl.pallas_call(
        functools.partial(_proj_residual_kernel, router=router),
        grid=(N_TOK // tm,),
        in_specs=in_specs,
        out_specs=out_specs,
        out_shape=out_shape,
        scratch_shapes=[pltpu.VMEM((tm, D_MODEL), F32)],
        compiler_params=_params("arbitrary"),
        name="proj_residual",
    )(*args)


def _swiglu_partial(x, wg, wu, wd):
    g = _dot(x, wg)
    u = _dot(x, wu)
    return _dot((_silu(g) * u).astype(BF16), wd)


def _ffn_kernel(h_ref, x1_ref, g_ref, wg_ref, wu_ref, wd_ref, o_ref):
    part = _swiglu_partial(h_ref[...], wg_ref[...], wu_ref[...], wd_ref[...])
    o_ref[...] = x1_ref[...] + g_ref[...] * part


def _resident(shape):
    return pl.BlockSpec(shape, lambda *_: (0,) * len(shape), pipeline_mode=pl.Buffered(1))


def _ffn(h2, x1, gate, wg, wu, wd):
    tm = ROW_TILE
    row = lambda i: (i, 0)
    return pl.pallas_call(
        _ffn_kernel,
        grid=(N_TOK // tm,),
        in_specs=[
            pl.BlockSpec((tm, D_MODEL), row),
            pl.BlockSpec((tm, D_MODEL), row),
            pl.BlockSpec((None, 1, D_MODEL), lambda i: (_cond_of_tile(i), 0, 0)),
            _resident((D_MODEL, FFN_DIM)),
            _resident((D_MODEL, FFN_DIM)),
            _resident((FFN_DIM, D_MODEL)),
        ],
        out_specs=pl.BlockSpec((tm, D_MODEL), row),
        out_shape=jax.ShapeDtypeStruct((N_TOK, D_MODEL), F32),
        compiler_params=_params("parallel"),
        name="ffn",
    )(h2, x1, gate, wg, wu, wd)


def _moe_kernel(blk_e_ref, nact_ref, order_ref, base_ref, x_hbm, wg_ref, wu_ref, wd_ref, ya_hbm,
                xrows0, xrows1, xb, acc, stage0, stage1, gsem, ssem):
    b = pl.program_id(0)
    f = pl.program_id(1)
    nact = nact_ref[0]
    xrows = (xrows0, xrows1)
    stage = (stage0, stage1)
    tile = lambda r: pl.ds(pl.multiple_of(r * LANE_CHUNKS, LANE_CHUNKS), LANE_CHUNKS)

    def row_ids(blk):
        base = base_ref[blk + 1]
        return lambda r: order_ref[base + r]

    def gather_copy(par, step, r, a):
        tok = a & (N_TOK - 1)
        return pltpu.make_async_copy(x_hbm.at[tok], xrows[par].at[step, tile(r), :], gsem.at[par])

    def scatter_copy(par, step, r, a):
        return pltpu.make_async_copy(stage[par].at[step, tile(r), :], ya_hbm.at[a], ssem.at[par])

    def wait_gather(par):
        pltpu.make_async_copy(xrows[par], xrows[par], gsem.at[par]).wait()

    def wait_scatter(par):
        pltpu.make_async_copy(stage[par], stage[par], ssem.at[par]).wait()

    @pl.when((b == 0) & (f == 0))
    def _():
        stage1[...] = jnp.zeros_like(stage1)
        ids = row_ids(0)

        def issue(r, c):
            gather_copy(0, r // MOE_STEP_ROWS, r % MOE_STEP_ROWS, ids(r)).start()
            return c

        lax.fori_loop(0, MOE_ROWS, issue, 0, unroll=8)

    def block_body(par):
        @pl.when(f == 0)
        def _():
            wait_gather(par)

            @pl.when(b >= 1)
            def _():
                wait_scatter(par)

            for step in range(MOE_STEPS):
                for c in range(LANE_CHUNKS):
                    xb[step * MOE_STEP_ROWS:(step + 1) * MOE_STEP_ROWS, c * 128:(c + 1) * 128] = (
                        _load_row_major(xrows[par].at[step], MOE_STEP_ROWS, c).astype(BF16))

        nxt, prv = row_ids(b + 1), row_ids(b - 1)
        row0 = f * MOE_STEP_ROWS
        for r in range(MOE_STEP_ROWS):
            gather_copy(1 - par, f, r, nxt(row0 + r)).start()
            scatter_copy(1 - par, f, r, prv(row0 + r)).start()
        part = _swiglu_partial(xb[...], wg_ref[...], wu_ref[...], wd_ref[...])

        @pl.when(f == 0)
        def _():
            acc[...] = part

        @pl.when((f > 0) & (f < MOE_STEPS - 1))
        def _():
            acc[...] += part

        @pl.when(f == MOE_STEPS - 1)
        def _():
            total = acc[...] + part
            for step in range(MOE_STEPS):
                _store_row_major(stage[par].at[step], total[step * MOE_STEP_ROWS:(step + 1) * MOE_STEP_ROWS])

    def drain(par):
        wait_scatter(par)
        ids = row_ids(b - 1)

        def issue(r, c):
            scatter_copy(1 - par, r // MOE_STEP_ROWS, r % MOE_STEP_ROWS, ids(r)).start()
            return c

        lax.fori_loop(0, MOE_ROWS, issue, 0, unroll=8)
        wait_gather(par)
        wait_scatter(1 - par)

    for par in range(2):
        @pl.when((b < nact) & (b % 2 == par))
        def _():
            block_body(par)

        @pl.when((b == nact) & (f == 0) & (b % 2 == par))
        def _():
            drain(par)


def _moe(routing, h2, wg, wu, wd, layer):
    tf = MOE_TILE
    wmap = lambda b, f, e, *_: (layer, e[b], 0, f)
    rows_buf = pltpu.VMEM((MOE_STEPS, MOE_STEP_ROWS * LANE_CHUNKS, 128), F32)
    grid_spec = pltpu.PrefetchScalarGridSpec(
        num_scalar_prefetch=len(routing),
        grid=(MOE_BLOCKS, MOE_STEPS),
        in_specs=[
            pl.BlockSpec(memory_space=pl.ANY),
            pl.BlockSpec((None, None, D_MODEL, tf), wmap),
            pl.BlockSpec((None, None, D_MODEL, tf), wmap),
            pl.BlockSpec((None, None, tf, D_MODEL), lambda b, f, e, *_: (layer, e[b], f, 0)),
        ],
        out_specs=pl.BlockSpec(memory_space=pl.ANY),
        scratch_shapes=[
            rows_buf,
            rows_buf,
            pltpu.VMEM((MOE_ROWS, D_MODEL), BF16),
            pltpu.VMEM((MOE_ROWS, D_MODEL), F32),
            rows_buf,
            rows_buf,
            pltpu.SemaphoreType.DMA((2,)),
            pltpu.SemaphoreType.DMA((2,)),
        ],
    )
    return pl.pallas_call(
        _moe_kernel,
        grid_spec=grid_spec,
        out_shape=jax.ShapeDtypeStruct((N_ASSIGN + MOE_ROWS, LANE_CHUNKS, 128), F32),
        compiler_params=_params("arbitrary", "arbitrary"),
        name="moe",
    )(*routing, h2.reshape(N_TOK, LANE_CHUNKS, 128), wg, wu, wd)


def _combine_kernel(y0_ref, y1_ref, x1_ref, g_ref, w_ref, *o_refs):
    w = w_ref[...]
    rows = x1_ref.shape[0]

    def emit(o_ref):
        for c in range(LANE_CHUNKS):
            sl = slice(c * 128, (c + 1) * 128)
            y = w[:, 0:1] * _load_row_major(y0_ref, rows, c) + w[:, 1:2] * _load_row_major(y1_ref, rows, c)
            o_ref[:, sl] = x1_ref[:, sl] + g_ref[:, sl] * y

    if len(o_refs) == 1:
        emit(o_refs[0])
    else:
        i = pl.program_id(0)

        @pl.when(i < CTX_TILES)
        def _():
            emit(o_refs[0])

        @pl.when(i >= CTX_TILES)
        def _():
            emit(o_refs[1])


def _combine(ya, x1, gate, top_w, split):
    tm = ROW_TILE
    row = lambda i: (i, 0)
    ya2 = ya.reshape(ya.shape[0] * LANE_CHUNKS, 128)
    if split:
        out_specs = [pl.BlockSpec((tm, D_MODEL), lambda i: (jnp.minimum(i, CTX_TILES - 1), 0)),
                     pl.BlockSpec((tm, D_MODEL), lambda i: (jnp.maximum(i - CTX_TILES, 0), 0))]
        out_shape = [jax.ShapeDtypeStruct((N_CTX, D_MODEL), F32), jax.ShapeDtypeStruct((N_LAT, D_MODEL), F32)]
    else:
        out_specs = pl.BlockSpec((tm, D_MODEL), row)
        out_shape = jax.ShapeDtypeStruct((N_TOK, D_MODEL), F32)
    return pl.pallas_call(
        _combine_kernel,
        grid=(N_TOK // tm,),
        in_specs=[
            pl.BlockSpec((tm * LANE_CHUNKS, 128), row),
            pl.BlockSpec((tm * LANE_CHUNKS, 128), lambda i: (i + N_TOK // tm, 0)),
            pl.BlockSpec((tm, D_MODEL), row),
            pl.BlockSpec((None, 1, D_MODEL), lambda i: (_cond_of_tile(i), 0, 0)),
            pl.BlockSpec((tm, TOP_K), row),
        ],
        out_specs=out_specs,
        out_shape=out_shape,
        compiler_params=_params("arbitrary"),
        name="moe_combine",
    )(ya2, ya2, x1, gate, top_w)


def _route(logits):
    top_v, top_i = lax.top_k(logits, TOP_K)
    top_w = jax.nn.softmax(top_v, axis=-1)
    e = top_i.reshape(N_ASSIGN).astype(jnp.int32)
    a = jnp.arange(N_ASSIGN, dtype=jnp.int32)
    row_id = (a % TOP_K) * N_TOK + a // TOP_K
    order = jnp.sort(e * N_ASSIGN + row_id) & (N_ASSIGN - 1)
    order = jnp.concatenate([order, N_ASSIGN + jnp.arange(MOE_ROWS, dtype=jnp.int32)])
    counts = jnp.sum((e[:, None] == jnp.arange(N_EXPERTS, dtype=jnp.int32)[None, :]).astype(jnp.int32), axis=0)
    start = jnp.cumsum(counts) - counts
    nblk = (counts + MOE_ROWS - 1) // MOE_ROWS
    blk_end = jnp.cumsum(nblk)
    blk = jnp.arange(-1, MOE_BLOCKS + 1, dtype=jnp.int32)
    blk_e = jnp.sum((blk[:, None] >= blk_end[None, :]).astype(jnp.int32), axis=1)
    blk_e = jnp.minimum(blk_e, N_EXPERTS - 1)
    within = (blk - (blk_end - nblk)[blk_e]) * MOE_ROWS
    real = (blk >= 0) & (blk < blk_end[-1])
    blk_base = jnp.where(real, start[blk_e] + within, N_ASSIGN).astype(jnp.int32)
    nact = blk_end[-1:].astype(jnp.int32)
    return top_w, (blk_e[1:-1], nact, order, blk_base)


RET_QK_W = RET_HEADS * RET_DK
RET_V_W = RET_HEADS * RET_DV
RET_COL_TILE = 1024


def _ret_proj_kernel(x_ref, nw_ref, sh_ref, sc_ref, w_ref, c_ref, s_ref, o_ref):
    def body(rope):
        h = _modulate(x_ref[...], nw_ref[...], sh_ref[...], sc_ref[...]).astype(BF16)
        for c0 in range(0, o_ref.shape[1], RET_COL_TILE):
            acc = _dot(h, w_ref[:, c0:c0 + RET_COL_TILE])
            if c0 >= 2 * RET_QK_W + RET_V_W:
                o_ref[:, c0:c0 + RET_COL_TILE] = _silu(acc).astype(BF16)
            elif c0 >= 2 * RET_QK_W:
                o_ref[:, c0:c0 + RET_COL_TILE] = acc.astype(BF16)
            else:
                x = acc * (RET_DK ** -0.5) if c0 >= RET_QK_W else acc
                if not rope:
                    o_ref[:, c0:c0 + RET_COL_TILE] = x.astype(BF16)
                    continue
                for g in range(RET_COL_TILE // 128):
                    tl = slice((g % 2) * 128, (g % 2 + 1) * 128)
                    xs = x[:, g * 128:(g + 1) * 128]
                    y = xs * c_ref[:, tl] + pltpu.roll(xs, RET_DK // 4, 1) * s_ref[:, tl]
                    o_ref[:, c0 + g * 128:c0 + (g + 1) * 128] = y.astype(BF16)

    i = pl.program_id(0)

    @pl.when(i < CTX_TILES)
    def _():
        body(False)

    @pl.when(i >= CTX_TILES)
    def _():
        body(True)


def _ret_proj(x, nw, shift, scale, w, rope_c, rope_s):
    tm = ROW_TILE
    width = w.shape[1]
    row = lambda i: (i, 0)
    cond = lambda i: (_cond_of_tile(i), 0, 0)
    tab = lambda i: (jnp.where(i < CTX_TILES, 0, (i - CTX_TILES) % TILES_PER_LAT_BATCH), 0)
    return pl.pallas_call(
        _ret_proj_kernel,
        grid=(N_TOK // tm,),
        in_specs=[
            pl.BlockSpec((tm, D_MODEL), row),
            pl.BlockSpec((1, D_MODEL), lambda i: (0, 0)),
            pl.BlockSpec((None, 1, D_MODEL), cond),
            pl.BlockSpec((None, 1, D_MODEL), cond),
            _resident((D_MODEL, width)),
            pl.BlockSpec((tm, RET_DK), tab),
            pl.BlockSpec((tm, RET_DK), tab),
        ],
        out_specs=pl.BlockSpec((tm, width), row),
        out_shape=jax.ShapeDtypeStruct((N_TOK, width), BF16),
        compiler_params=_params("parallel"),
        name="ret_proj",
    )(x, nw, shift, scale, w, rope_c, rope_s)


def _retention_kernel(lg_ref, q_ref, k_ref, v_ref, sg_ref, gn_ref, *rest, zero_init, n_chunks, layer, heads):
    if zero_init:
        y_ref, st_ref, o_acc, state, dmask = rest[-5:]
    else:
        s0f_ref, s0b_ref, y_ref, o_acc, state, dmask = rest
    hd0 = pl.program_id(1) * heads
    c = RET_BLOCK
    ri = lax.broadcasted_iota(jnp.int32, (c, c), 0).astype(F32)
    ci = lax.broadcasted_iota(jnp.int32, (c, c), 1).astype(F32)
    pos = lax.broadcasted_iota(jnp.int32, (c, 1), 0).astype(F32)
    chains = [(h, d) for h in range(heads) for d in range(2)]
    q_decay, k_decay, c_decay = {}, {}, {}
    for h, d in chains:
        lg = lg_ref[d, hd0 + h]
        rel = (ri - ci) if d == 0 else (ci - ri)
        dmask[h, d] = jnp.where(rel >= 0, jnp.exp(jnp.maximum(rel, 0.0) * lg), 0.0)
        q_decay[h, d] = jnp.exp(((pos + 1.0) if d == 0 else (c - pos)) * lg)
        k_decay[h, d] = jnp.exp(((c - 1.0 - pos) if d == 0 else pos) * lg)
        c_decay[h, d] = jnp.exp(jnp.full((1, 1), c, F32) * lg)
        if zero_init:
            state[h, d] = jnp.zeros(state.shape[2:], F32)
        else:
            state[h, d] = (s0f_ref if d == 0 else s0b_ref)[h]

    def step(t, carry):
        rows = [pl.ds(pl.multiple_of((t if d == 0 else n_chunks - 1 - t) * c, c), c) for _, d in chains]
        qs = [q_ref[r, h * RET_DK:(h + 1) * RET_DK] for r, (h, _) in zip(rows, chains)]
        ks = [k_ref[r, h * RET_DK:(h + 1) * RET_DK] for r, (h, _) in zip(rows, chains)]
        vs = [v_ref[r, h * RET_DV:(h + 1) * RET_DV] for r, (h, _) in zip(rows, chains)]
        att = [lax.dot_general(q, k, (((1,), (1,)), ((), ())), preferred_element_type=F32) * dmask[hd]
               for q, k, hd in zip(qs, ks, chains)]
        old = [state[hd] for hd in chains]
        for r, hd, a, q, v, s in zip(rows, chains, att, qs, vs, old):
            o_acc[hd[0], hd[1], r, :] = (_dot(a.astype(BF16), v)
                                         + _dot((q.astype(F32) * q_decay[hd]).astype(BF16), s.astype(BF16)))
        for hd, k, v, s in zip(chains, ks, vs, old):
            kd = (k.astype(F32) * k_decay[hd]).astype(BF16)
            state[hd] = s * c_decay[hd] + lax.dot_general(kd, v, (((0,), (0,)), ((), ())),
                                                           preferred_element_type=F32)
        return carry

    lax.fori_loop(0, n_chunks, step, 0)

    if zero_init:
        for h, d in chains:
            if st_ref.ndim == 4:
                st_ref[d, h] = state[h, d]
            else:
                st_ref[layer, d, h] = state[h, d]
                for other in range(st_ref.shape[0]):
                    if other != layer:
                        st_ref[other, d, h] = jnp.zeros(state.shape[2:], F32)

    for h in range(heads):
        cols = slice(h * RET_DV, (h + 1) * RET_DV)
        y = _rms(o_acc[h, 0] + o_acc[h, 1], gn_ref[h])
        y_ref[:, cols] = (sg_ref[:, cols].astype(F32) * y).astype(BF16)


def _retention(log_gamma, qkvg, gn, state0, *, batch, b_off, t, layer, states=None):
    zero_init = state0 is None
    heads = RET_HEADS if t == RET_BLOCK else 1
    qw, vw = heads * RET_DK, heads * RET_DV
    kblk = RET_HEADS * RET_DK // qw
    vblk = 2 * RET_HEADS * RET_DK // vw
    gblk = vblk + RET_HEADS * RET_DV // vw
    in_specs = [
        pl.BlockSpec(memory_space=pltpu.SMEM),
        pl.BlockSpec((None, t, qw), lambda b, h: (b + b_off, 0, h)),
        pl.BlockSpec((None, t, qw), lambda b, h: (b + b_off, 0, kblk + h)),
        pl.BlockSpec((None, t, vw), lambda b, h: (b + b_off, 0, vblk + h)),
        pl.BlockSpec((None, t, vw), lambda b, h: (b + b_off, 0, gblk + h)),
        pl.BlockSpec((heads, 1, RET_DV), lambda b, h: (h, 0, 0)),
    ]
    args = [log_gamma, qkvg, qkvg, qkvg, qkvg, gn]
    y_spec = pl.BlockSpec((None, t, vw), lambda b, h: (b, 0, h))
    y_shape = jax.ShapeDtypeStruct((batch, t, RET_HEADS * RET_DV), BF16)
    aliases = {}
    if zero_init:
        n_ret = DEPTH // 2
        st_shape = jax.ShapeDtypeStruct((batch, n_ret, 2, RET_HEADS, RET_DK, RET_DV), F32)
        if states is None:
            st_spec = pl.BlockSpec((None, n_ret, 2, heads, RET_DK, RET_DV), lambda b, h: (b, 0, 0, h, 0, 0))
        else:
            st_spec = pl.BlockSpec((None, None, 2, heads, RET_DK, RET_DV), lambda b, h: (b, layer, 0, h, 0, 0))
            aliases = {len(args): 1}
            in_specs.append(pl.BlockSpec(memory_space=pl.ANY))
            args.append(states)
        out_specs = [y_spec, st_spec]
        out_shape = [y_shape, st_shape]
    else:
        in_specs += [
            pl.BlockSpec((None, None, None, heads, RET_DK, RET_DV), lambda b, h: (b, layer, 0, h, 0, 0)),
            pl.BlockSpec((None, None, None, heads, RET_DK, RET_DV), lambda b, h: (b, layer, 1, h, 0, 0))]
        args += [state0, state0]
        out_specs = y_spec
        out_shape = y_shape
    return pl.pallas_call(
        functools.partial(_retention_kernel, zero_init=zero_init, n_chunks=t // RET_BLOCK, layer=layer,
                          heads=heads),
        grid=(batch, RET_HEADS // heads),
        in_specs=in_specs,
        out_specs=out_specs,
        out_shape=out_shape,
        input_output_aliases=aliases,
        scratch_shapes=[pltpu.VMEM((heads, 2, t, RET_DV), F32), pltpu.VMEM((heads, 2, RET_DK, RET_DV), F32),
                        pltpu.VMEM((heads, 2, RET_BLOCK, RET_BLOCK), F32)],
        compiler_params=_params("parallel", "parallel"),
        name="retention",
    )(*args)


def _rope_tables(rot_dim):
    rows = DEC_SEQ // GRID_W
    row = jnp.repeat(jnp.arange(rows), GRID_W)
    col = jnp.tile(jnp.arange(GRID_W), rows)
    nf = rot_dim // 4
    inv = ROPE_BASE ** (-jnp.arange(nf, dtype=F32) / nf)
    ang = jnp.stack([row, col], axis=-1).astype(F32)[:, :, None] * inv
    cos, sin = jnp.cos(ang), jnp.sin(ang)
    c = jnp.stack([cos, cos], axis=2).reshape(DEC_SEQ, rot_dim)
    s = jnp.stack([-sin, sin], axis=2).reshape(DEC_SEQ, rot_dim)
    return c, s


def _swap_rope(x):
    blocks = x.reshape(x.shape[:-1] + (2, 2, MLA_ROPE // 4))
    return blocks[..., ::-1, :].reshape(x.shape)


def _slot_orders(x):
    n0, n1, rope = x[..., :MLA_ROPE], x[..., MLA_ROPE:MLA_NOPE], x[..., MLA_NOPE:]
    zero = jnp.zeros_like(rope)
    orders = ((rope, n0, n1, zero), (n0, rope, n1, zero), (n0, n1, rope, zero), (n0, n1, zero, rope))
    return [jnp.concatenate(parts, axis=-1) for parts in orders]


def _to_head_slots(w):
    w = w.reshape(w.shape[:-2] + (MLA_HEADS // SLOT_GROUP, SLOT_GROUP, MLA_QK))
    slots = [_slot_orders(w[..., g, :])[g] for g in range(SLOT_GROUP)]
    return jnp.stack(slots, axis=-2).reshape(w.shape[:-3] + (MLA_HEADS, HEAD_SLOT))


def _head_slot_tables(c32, s32):
    t = c32.shape[0]
    own = (jnp.arange(SLOT_GROUP)[:, None] == jnp.arange(SLOT_GROUP)[None, :])[None, :, :, None]
    ca = jnp.where(own, c32[:, None, None, :], 1.0)
    sb = jnp.where(own, s32[:, None, None, :], 0.0)
    return ca.reshape(t, SLOT_GROUP * HEAD_SLOT), sb.reshape(t, SLOT_GROUP * HEAD_SLOT)


def _head_slot_gains(g, scale):
    ga = jnp.stack(_slot_orders(g)) * scale
    gb = jnp.zeros((SLOT_GROUP, SLOT_GROUP, MLA_ROPE), F32)
    for s in range(SLOT_GROUP):
        gb = gb.at[s, s].set(_swap_rope(g[MLA_NOPE:]) * scale)
    return ga.reshape(1, SLOT_GROUP * HEAD_SLOT), gb.reshape(1, SLOT_GROUP * HEAD_SLOT)


def _rows3(m):
    return m.reshape(N_COND, 1, D_MODEL)


def kernel(x_prompt, x_sample, cache_ckv, cache_kpe, state_ret, c, c_ctx, mod_w, mod_b, norm1_w, norm2_w, mla_wq_a, mla_q_a_norm, mla_wq_b, mla_wkv_a, mla_kv_norm, mla_w_uk, mla_w_uv, mla_q_norm, mla_k_norm, mla_wo, ret_wq, ret_wk, ret_wv, ret_wg, ret_wo, ret_decay, ret_gn, ffn_w_gate, ffn_w_up, ffn_w_down, moe_router, moe_w_gate, moe_w_up, moe_w_down):
    x = (x_prompt.reshape(N_CTX, D_MODEL), x_sample.reshape(N_LAT, D_MODEL))
    cond = jnp.concatenate([c_ctx[None, :], c, jnp.zeros((N_COND - 1 - DEC_BATCH, D_MODEL), F32)], axis=0)
    mods = _adaln(cond, mod_w, mod_b).reshape(DEPTH, N_COND, 6, D_MODEL)

    c32, s32 = _rope_tables(MLA_ROPE)
    c32 = jnp.concatenate([jnp.ones((ROW_TILE, MLA_ROPE), F32), c32], axis=0)
    s32 = jnp.concatenate([jnp.zeros((ROW_TILE, MLA_ROPE), F32), s32], axis=0)
    rope_ca, rope_sb = _head_slot_tables(c32, s32)
    lat_kv = DEC_SEQ + PAST_LEN
    kv_tiles = lat_kv // ROW_TILE
    tab_q = lambda i: jnp.where(i < CTX_TILES, 0, 1 + (i - CTX_TILES) % TILES_PER_LAT_BATCH)
    tab_kc = lambda i: 0
    tab_kl = lambda i: jnp.where(i % kv_tiles < TILES_PER_LAT_BATCH, 1 + i % kv_tiles, 0)
    rc256, rs256 = _rope_tables(RET_DK)

    moe_wg, moe_wu, moe_wd = moe_w_gate.astype(BF16), moe_w_up.astype(BF16), moe_w_down.astype(BF16)

    new_ckv, new_kpe, new_ret = [], [], None
    for i in range(DEPTH):
        j = i // 2
        m = mods[i]
        sh1, sc1, g1, sh2, sc2, g2 = (_rows3(m[:, t]) for t in range(6))
        nw1 = norm1_w[i].reshape(1, D_MODEL)
        nw2 = norm2_w[i].reshape(1, D_MODEL)
        if i % 2 == 0:
            wqb = mla_wq_b[j].reshape(MLA_Q_RANK, MLA_HEADS, MLA_QK)
            wqb_swap = _swap_rope(wqb[:, :, MLA_NOPE:]).reshape(MLA_Q_RANK, MLA_HEADS * MLA_ROPE)
            wqb = jnp.concatenate([_to_head_slots(wqb).reshape(MLA_Q_RANK, MLA_HEADS * HEAD_SLOT), wqb_swap],
                                  axis=1).astype(BF16)
            wkv = mla_wkv_a[j]
            w_pe = wkv[:, MLA_KV_RANK:]
            wkv = jnp.concatenate([wkv[:, :MLA_KV_RANK], jnp.tile(w_pe, (1, SLOT_GROUP)),
                                   jnp.tile(_swap_rope(w_pe), (1, SLOT_GROUP))], axis=1).astype(BF16)
            wuk = mla_w_uk[j].reshape(MLA_KV_RANK, MLA_HEADS, MLA_NOPE)
            wuk = jnp.concatenate([wuk, jnp.zeros((MLA_KV_RANK, MLA_HEADS, MLA_ROPE), F32)], axis=2)
            wuk = _to_head_slots(wuk).reshape(MLA_KV_RANK, MLA_HEADS * HEAD_SLOT).astype(BF16)
            gaq, gbq = _head_slot_gains(mla_q_norm[j], MLA_QK ** -0.5 * LOG2_E)
            gak, gbk = _head_slot_gains(mla_k_norm[j], 1.0)

            q, ckv, kpe = _mla_proj(x, nw1, sh1, sc1, mla_wq_a[j].astype(BF16),
                                    mla_q_a_norm[j].reshape(1, MLA_Q_RANK), wqb, wkv,
                                    mla_kv_norm[j].reshape(1, MLA_KV_RANK), gaq, gbq, rope_ca, rope_sb, tab_q)
            new_ckv.append(ckv[:N_CTX].reshape(BATCH, SEQ, MLA_KV_RANK))
            new_kpe.append(kpe[:N_CTX, :MLA_ROPE].reshape(BATCH, SEQ, MLA_ROPE))

            kpe_x = cache_kpe[:, j]
            kpe_x = jnp.concatenate([jnp.tile(kpe_x, (1, 1, SLOT_GROUP)), jnp.tile(_swap_rope(kpe_x), (1, 1, SLOT_GROUP))],
                                    axis=2)
            wuv = mla_w_uv[j].astype(BF16)
            k_c, v_c = _kv_expand(ckv, kpe, wuk, wuv, gak, gbk, rope_ca, rope_sb, tab_kc)
            k_l, v_l = _kv_expand(ckv, kpe, wuk, wuv, gak, gbk, rope_ca, rope_sb, tab_kl,
                                  cache=(cache_ckv, j, kpe_x))

            qw = MLA_HEADS * HEAD_SLOT
            o_c = _attention(q.reshape(N_TOK // SEQ, SEQ, qw), k_c.reshape(BATCH, SEQ, qw),
                             v_c.reshape(BATCH, SEQ, MLA_HEADS * MLA_V),
                             batch=BATCH, q_off=0, tq_total=SEQ, tq=SEQ, heads=MLA_HEADS)
            o_l = _attention(q.reshape(N_TOK // DEC_SEQ, DEC_SEQ, qw), k_l.reshape(DEC_BATCH, lat_kv, qw),
                             v_l.reshape(DEC_BATCH, lat_kv, MLA_HEADS * MLA_V),
                             batch=DEC_BATCH, q_off=N_CTX // DEC_SEQ, tq_total=DEC_SEQ, tq=1024, heads=2)
            mix_c, mix_l = o_c.reshape(N_CTX, D_MODEL), o_l.reshape(N_LAT, D_MODEL)
            w_out = mla_wo[j].astype(BF16)
        else:
            w_in = jnp.concatenate([ret_wq[j], ret_wk[j], ret_wv[j], ret_wg[j]], axis=1).astype(BF16)
            qkvg = _ret_proj(x, nw1, sh1, sc1, w_in, rc256, rs256)
            log_gamma = -jnp.exp(ret_decay[j].astype(F32))
            gn = ret_gn[j].reshape(RET_HEADS, 1, RET_DV)
            width = qkvg.shape[1]
            y_c, new_ret = _retention(log_gamma, qkvg.reshape(N_TOK // SEQ, SEQ, width), gn, None,
                                      batch=BATCH, b_off=0, t=SEQ, layer=j, states=new_ret)
            y_l = _retention(log_gamma, qkvg.reshape(N_TOK // DEC_SEQ, DEC_SEQ, width), gn, state_ret,
                             batch=DEC_BATCH, b_off=N_CTX // DEC_SEQ, t=DEC_SEQ, layer=j)
            mix_c, mix_l = y_c.reshape(N_CTX, RET_HEADS * RET_DV), y_l.reshape(N_LAT, RET_HEADS * RET_DV)
            w_out = ret_wo[j].astype(BF16)

        if i % 2 == 0:
            x1, h2 = _proj_residual(mix_c, mix_l, w_out, x, g1, nw2, sh2, sc2)
            x = _ffn(h2, x1, g2, ffn_w_gate[j].astype(BF16), ffn_w_up[j].astype(BF16), ffn_w_down[j].astype(BF16))
        else:
            wr = jnp.pad(moe_router[j], ((0, 0), (0, 128 - N_EXPERTS)))
            wr_hi = wr.astype(BF16)
            wr_lo = (wr - wr_hi.astype(F32)).astype(BF16)
            x1, h2, logits = _proj_residual(mix_c, mix_l, w_out, x, g1, nw2, sh2, sc2, router_w=(wr_hi, wr_lo))
            top_w, routing = _route(logits[:, :N_EXPERTS])
            ya = _moe(routing, h2, moe_wg, moe_wu, moe_wd, j)
            x = _combine(ya, x1, g2, top_w, split=(i == DEPTH - 1))

    y_prompt = x[0].reshape(BATCH, SEQ, D_MODEL)
    y_sample = x[1].reshape(DEC_BATCH, DEC_SEQ, D_MODEL)
    return (y_prompt, y_sample, jnp.stack(new_ckv, axis=1), jnp.stack(new_kpe, axis=1), new_ret)
```

```python
import functools

import jax
import jax.numpy as jnp
from jax import lax
from jax.experimental import pallas as pl
from jax.experimental.pallas import tpu as pltpu

F32 = jnp.float32
BF16 = jnp.bfloat16

D_MODEL = 1024
BATCH = 32
SEQ = 256
DEPTH = 4
DEC_BATCH = 4
DEC_SEQ = 2048
PAST_LEN = 512
GRID_W = 64
ROPE_BASE = 10000.0
NORM_EPS = 1e-6

MLA_HEADS = 16
MLA_NOPE = 64
MLA_ROPE = 32
MLA_QK = MLA_NOPE + MLA_ROPE
MLA_V = 64
MLA_Q_RANK = 384
MLA_KV_RANK = 256
LANES = 128
HEAD_SLOT = LANES

RET_HEADS = 4
RET_DK = 256
RET_DV = 512
RET_BLOCK = 256

FFN_DIM = 2816
N_EXPERTS = 8
TOP_K = 2
EXPERT_DIM = 3584

N_CTX = BATCH * SEQ
N_LAT = DEC_BATCH * DEC_SEQ
N_TOK = N_CTX + N_LAT
N_COND = 8

ROW_TILE = 512
CTX_TILES = N_CTX // ROW_TILE
TILES_PER_LAT_BATCH = DEC_SEQ // ROW_TILE

MOE_ROWS = 512
MOE_TILE = 1792
MOE_STEPS = EXPERT_DIM // MOE_TILE
MOE_STEP_ROWS = MOE_ROWS // MOE_STEPS
assert MOE_STEPS >= 2
N_ASSIGN = N_TOK * TOP_K
MOE_BLOCKS = N_ASSIGN // MOE_ROWS + N_EXPERTS

VMEM_LIMIT = 56 * 1024 * 1024
LOG2_E = 1.4426950408889634


def _params(*sem):
    return pltpu.CompilerParams(dimension_semantics=sem, vmem_limit_bytes=VMEM_LIMIT)


def _cond_of_tile(i):
    return jnp.where(i < CTX_TILES, 0, 1 + (i - CTX_TILES) // TILES_PER_LAT_BATCH)


def _dot(a, b):
    return jnp.dot(a, b, preferred_element_type=F32)


def _silu(x):
    return x * (1.0 / (1.0 + jnp.exp(-x)))


def _rms(x, w):
    return x * lax.rsqrt(jnp.mean(x * x, axis=-1, keepdims=True) + NORM_EPS) * w


def _modulate(x, nw, shift, scale):
    return _rms(x, nw) * (1.0 + scale) + shift


def _adaln_kernel(c_ref, w_ref, b_ref, o_ref):
    a = _silu(c_ref[...]).astype(BF16)
    o_ref[...] = _dot(a, w_ref[...].astype(BF16)) + b_ref[...]


def _adaln(cond, mod_w, mod_b):
    tn = 1024
    return pl.pallas_call(
        _adaln_kernel,
        grid=(DEPTH, 6 * D_MODEL // tn),
        in_specs=[
            pl.BlockSpec((N_COND, D_MODEL), lambda l, j: (0, 0)),
            pl.BlockSpec((None, D_MODEL, tn), lambda l, j: (l, 0, j)),
            pl.BlockSpec((None, 1, tn), lambda l, j: (l, 0, j)),
        ],
        out_specs=pl.BlockSpec((None, N_COND, tn), lambda l, j: (l, 0, j)),
        out_shape=jax.ShapeDtypeStruct((DEPTH, N_COND, 6 * D_MODEL), F32),
        compiler_params=_params("parallel", "parallel"),
        name="adaln",
    )(cond, mod_w, mod_b.reshape(DEPTH, 1, 6 * D_MODEL))


SLOT_GROUP = HEAD_SLOT // MLA_ROPE


def _heads_norm_rope(xs, swaps, ga, gb):
    ss = [jnp.sum(x * x, axis=-1, keepdims=True) for x in xs]
    rs = [lax.rsqrt(s * (1.0 / MLA_QK) + NORM_EPS) for s in ss]
    out = []
    for hd, (r, x, sw) in enumerate(zip(rs, xs, swaps)):
        lanes = slice((hd % SLOT_GROUP) * HEAD_SLOT, (hd % SLOT_GROUP + 1) * HEAD_SLOT)
        out.append(r * (x * ga[:, lanes] + sw * gb[:, lanes]))
    return out


def _split_rows(x):
    if isinstance(x, tuple):
        return x[0], x[1], 0
    return x, x, CTX_TILES


def _split_specs(width, lat_tile0, tm=ROW_TILE):
    return [pl.BlockSpec((tm, width), lambda i: (jnp.minimum(i, CTX_TILES - 1), 0)),
            pl.BlockSpec((tm, width), lambda i: (jnp.maximum(i - CTX_TILES, 0) + lat_tile0, 0))]


def _pick_rows(ctx_ref, lat_ref):
    return jnp.where(pl.program_id(0) < CTX_TILES, ctx_ref[...], lat_ref[...])


def _mla_proj_kernel(xc_ref, xl_ref, nw_ref, sh_ref, sc_ref, wqa_ref, qan_ref, wqb_ref, wkv_ref, kvn_ref,
                     ga_ref, gb_ref, ca_ref, sb_ref, q_ref, ckv_ref, kpe_ref):
    h = _modulate(_pick_rows(xc_ref, xl_ref), nw_ref[...], sh_ref[...], sc_ref[...]).astype(BF16)
    qa = _rms(_dot(h, wqa_ref[...]), qan_ref[...]).astype(BF16)
    q = _dot(qa, wqb_ref[...])
    kv = _dot(h, wkv_ref[...])
    ckv_ref[...] = _rms(kv[:, :MLA_KV_RANK], kvn_ref[...])
    kpe_ref[...] = kv[:, MLA_KV_RANK:]
    ga = ga_ref[...] * ca_ref[...]
    gb = gb_ref[...] * sb_ref[...]
    slots = [slice(hd * HEAD_SLOT, (hd + 1) * HEAD_SLOT) for hd in range(MLA_HEADS)]
    swap0 = MLA_HEADS * HEAD_SLOT
    swaps = [q[:, swap0 + (hd // SLOT_GROUP) * HEAD_SLOT:swap0 + (hd // SLOT_GROUP + 1) * HEAD_SLOT]
             for hd in range(MLA_HEADS)]
    for sl, y in zip(slots, _heads_norm_rope([q[:, sl] for sl in slots], swaps, ga, gb)):
        q_ref[:, sl] = y.astype(BF16)


def _mla_proj(x, nw, shift, scale, wqa, qan, wqb, wkv, kvn, ga, gb, ca, sb, tab_index):
    tm = ROW_TILE
    row = lambda i: (i, 0)
    fixed = lambda i: (0, 0)
    cond = lambda i: (_cond_of_tile(i), 0, 0)
    qw = MLA_HEADS * HEAD_SLOT
    tabw = SLOT_GROUP * HEAD_SLOT
    kpew = 2 * HEAD_SLOT
    tab = lambda i: (tab_index(i), 0)
    x_ctx, x_lat, lat_tile0 = _split_rows(x)
    return pl.pallas_call(
        _mla_proj_kernel,
        grid=(N_TOK // tm,),
        in_specs=_split_specs(D_MODEL, lat_tile0) + [
            pl.BlockSpec((1, D_MODEL), fixed),
            pl.BlockSpec((None, 1, D_MODEL), cond),
            pl.BlockSpec((None, 1, D_MODEL), cond),
            pl.BlockSpec((D_MODEL, MLA_Q_RANK), fixed),
            pl.BlockSpec((1, MLA_Q_RANK), fixed),
            pl.BlockSpec((MLA_Q_RANK, wqb.shape[1]), fixed),
            pl.BlockSpec((D_MODEL, MLA_KV_RANK + kpew), fixed),
            pl.BlockSpec((1, MLA_KV_RANK), fixed),
            pl.BlockSpec((1, tabw), fixed),
            pl.BlockSpec((1, tabw), fixed),
            pl.BlockSpec((tm, tabw), tab),
            pl.BlockSpec((tm, tabw), tab),
        ],
        out_specs=[
            pl.BlockSpec((tm, qw), row),
            pl.BlockSpec((tm, MLA_KV_RANK), row),
            pl.BlockSpec((tm, kpew), row),
        ],
        out_shape=[
            jax.ShapeDtypeStruct((N_TOK, qw), BF16),
            jax.ShapeDtypeStruct((N_TOK, MLA_KV_RANK), F32),
            jax.ShapeDtypeStruct((N_TOK, kpew), F32),
        ],
        compiler_params=_params("arbitrary"),
        name="mla_proj",
    )(x_ctx, x_lat, nw, shift, scale, wqa, qan, wqb, wkv, kvn, ga, gb, ca, sb)


def _kv_expand_kernel(ckv_ref, kpe_ref, *rest, cached):
    if cached:
        cckv_ref, ckpe_ref, wuk_ref, wuv_ref, ga_ref, gb_ref, ca_ref, sb_ref, k_ref, v_ref = rest
        from_cache = pl.program_id(0) % LAT_KV_TILES == LAT_KV_TILES - 1
        c = jnp.where(from_cache, cckv_ref[...], ckv_ref[...]).astype(BF16)
        kpe2 = jnp.where(from_cache, ckpe_ref[...], kpe_ref[...])
    else:
        wuk_ref, wuv_ref, ga_ref, gb_ref, ca_ref, sb_ref, k_ref, v_ref = rest
        c = ckv_ref[...].astype(BF16)
        kpe2 = kpe_ref[...]
    kn = _dot(c, wuk_ref[...])
    v_ref[...] = _dot(c, wuv_ref[...]).astype(BF16)
    kpe = kpe2[:, :HEAD_SLOT]
    swap = kpe2[:, HEAD_SLOT:]
    ga = ga_ref[...] * ca_ref[...]
    gb = gb_ref[...] * sb_ref[...]
    group = lax.broadcasted_iota(jnp.int32, kpe.shape, 1) // MLA_ROPE
    kpe_at = [jnp.where(group == g, kpe, 0.0) for g in range(SLOT_GROUP)]
    slots = [slice(hd * HEAD_SLOT, (hd + 1) * HEAD_SLOT) for hd in range(MLA_HEADS)]
    xs = [kn[:, sl] + kpe_at[hd % SLOT_GROUP] for hd, sl in enumerate(slots)]
    for sl, y in zip(slots, _heads_norm_rope(xs, [swap] * MLA_HEADS, ga, gb)):
        k_ref[:, sl] = y.astype(BF16)


LAT_KV_TILES = (DEC_SEQ + PAST_LEN) // ROW_TILE
assert PAST_LEN == ROW_TILE


def _kv_expand(ckv, kpe, wuk, wuv, ga, gb, ca, sb, tab_index, cache=None):
    tm = ROW_TILE
    row = lambda i: (i, 0)
    fixed = lambda i: (0, 0)
    tab = lambda i: (tab_index(i), 0)
    kw = MLA_HEADS * HEAD_SLOT
    vw = MLA_HEADS * MLA_V
    tabw = SLOT_GROUP * HEAD_SLOT
    if cache is None:
        n = N_CTX
        src = row
        cache_specs, cache_args = [], []
    else:
        n = DEC_BATCH * (DEC_SEQ + PAST_LEN)
        cache_ckv, layer, cache_kpe2 = cache
        src = lambda i: (CTX_TILES + (i // LAT_KV_TILES) * TILES_PER_LAT_BATCH
                         + jnp.minimum(i % LAT_KV_TILES, TILES_PER_LAT_BATCH - 1), 0)
        cache_specs = [pl.BlockSpec((None, None, PAST_LEN, MLA_KV_RANK), lambda i: (i // LAT_KV_TILES, layer, 0, 0)),
                       pl.BlockSpec((None, PAST_LEN, 2 * HEAD_SLOT), lambda i: (i // LAT_KV_TILES, 0, 0))]
        cache_args = [cache_ckv, cache_kpe2]
    return pl.pallas_call(
        functools.partial(_kv_expand_kernel, cached=cache is not None),
        grid=(n // tm,),
        in_specs=[
            pl.BlockSpec((tm, MLA_KV_RANK), src),
            pl.BlockSpec((tm, 2 * HEAD_SLOT), src),
        ] + cache_specs + [
            pl.BlockSpec((MLA_KV_RANK, kw), fixed),
            pl.BlockSpec((MLA_KV_RANK, vw), fixed),
            pl.BlockSpec((1, tabw), fixed),
            pl.BlockSpec((1, tabw), fixed),
            pl.BlockSpec((tm, tabw), tab),
            pl.BlockSpec((tm, tabw), tab),
        ],
        out_specs=[pl.BlockSpec((tm, kw), row), pl.BlockSpec((tm, vw), row)],
        out_shape=[jax.ShapeDtypeStruct((n, kw), BF16), jax.ShapeDtypeStruct((n, vw), BF16)],
        compiler_params=_params("arbitrary"),
        name="kv_expand",
    )(ckv, kpe, *cache_args, wuk, wuv, ga, gb, ca, sb)


def _attn_kernel(q_ref, k_ref, v_ref, o_ref, *, heads):
    scores = []
    for hd in range(heads):
        qh = q_ref[:, hd * HEAD_SLOT:(hd + 1) * HEAD_SLOT]
        kh = k_ref[:, hd * HEAD_SLOT:(hd + 1) * HEAD_SLOT]
        scores.append(lax.dot_general(qh, kh, (((1,), (1,)), ((), ())), preferred_element_type=F32))
    probs, invs = [], []
    for s in scores:
        p = jnp.exp2(s - jnp.max(s, axis=-1, keepdims=True))
        invs.append(1.0 / jnp.sum(p, axis=-1, keepdims=True))
        probs.append(p.astype(BF16))
    for hd in range(heads):
        o = _dot(probs[hd], v_ref[:, hd * MLA_V:(hd + 1) * MLA_V])
        o_ref[:, hd * MLA_V:(hd + 1) * MLA_V] = (o * invs[hd]).astype(BF16)


def _attention(q, k, v, *, batch, q_off, tq_total, tq, heads):
    tk = k.shape[1]
    groups = MLA_HEADS // heads
    return pl.pallas_call(
        functools.partial(_attn_kernel, heads=heads),
        grid=(batch, groups, tq_total // tq),
        in_specs=[
            pl.BlockSpec((None, tq, heads * HEAD_SLOT), lambda b, g, i: (b + q_off, i, g)),
            pl.BlockSpec((None, tk, heads * HEAD_SLOT), lambda b, g, i: (b, 0, g)),
            pl.BlockSpec((None, tk, heads * MLA_V), lambda b, g, i: (b, 0, g)),
        ],
        out_specs=pl.BlockSpec((None, tq, heads * MLA_V), lambda b, g, i: (b, i, g)),
        out_shape=jax.ShapeDtypeStruct((batch, tq_total, MLA_HEADS * MLA_V), BF16),
        compiler_params=_params("parallel", "parallel", "parallel"),
        name="attention",
    )(q, k, v)


LANE_CHUNKS = D_MODEL // LANES


def _store_row_major(ref, x):
    rows = x.shape[0]
    for c in range(LANE_CHUNKS):
        ref[pl.ds(c, rows, stride=LANE_CHUNKS), :] = x[:, c * LANES:(c + 1) * LANES]


def _load_row_major(ref, rows, c):
    return ref[pl.ds(c, rows, stride=LANE_CHUNKS), :]


def _proj_residual_kernel(ac_ref, al_ref, w_ref, xc_ref, xl_ref, g_ref, nw_ref, sh_ref, sc_ref, *rest, router):
    if router:
        rhi_ref, rlo_ref, x1_ref, h2_ref, lg_ref, y_sc = rest
    else:
        x1_ref, h2_ref, y_sc = rest
    i = pl.program_id(0)

    @pl.when(i < CTX_TILES)
    def _():
        y_sc[...] = _dot(ac_ref[...], w_ref[...])

    @pl.when(i >= CTX_TILES)
    def _():
        y_sc[...] = _dot(al_ref[...], w_ref[...])

    x1 = _pick_rows(xc_ref, xl_ref) + g_ref[...] * y_sc[...]
    h2 = _modulate(x1, nw_ref[...], sh_ref[...], sc_ref[...])
    if router:
        hi = h2.astype(BF16)
        lo = (h2 - hi.astype(F32)).astype(BF16)
        lg_ref[...] = _dot(hi, rhi_ref[...]) + (_dot(lo, rhi_ref[...]) + _dot(hi, rlo_ref[...]))
        _store_row_major(h2_ref, h2)
    else:
        h2_ref[...] = h2.astype(h2_ref.dtype)
    x1_ref[...] = x1


def _proj_residual(a_ctx, a_lat, w, x, gate, nw, shift, scale, router_w=None):
    tm = ROW_TILE
    kdim = w.shape[0]
    row = lambda i: (i, 0)
    fixed = lambda i: (0, 0)
    cond = lambda i: (_cond_of_tile(i), 0, 0)
    router = router_w is not None
    x_ctx, x_lat, lat_tile0 = _split_rows(x)
    in_specs = _split_specs(kdim, 0) + [pl.BlockSpec((kdim, D_MODEL), fixed)] + _split_specs(D_MODEL, lat_tile0) + [
        pl.BlockSpec((None, 1, D_MODEL), cond),
        pl.BlockSpec((1, D_MODEL), fixed),
        pl.BlockSpec((None, 1, D_MODEL), cond),
        pl.BlockSpec((None, 1, D_MODEL), cond),
    ]
    if router:
        h2_spec = pl.BlockSpec((tm * LANE_CHUNKS, LANES), row)
        h2_shape = jax.ShapeDtypeStruct((N_TOK * LANE_CHUNKS, LANES), F32)
    else:
        h2_spec = pl.BlockSpec((tm, D_MODEL), row)
        h2_shape = jax.ShapeDtypeStruct((N_TOK, D_MODEL), BF16)
    out_specs = [pl.BlockSpec((tm, D_MODEL), row), h2_spec]
    out_shape = [jax.ShapeDtypeStruct((N_TOK, D_MODEL), F32), h2_shape]
    args = [a_ctx, a_lat, w, x_ctx, x_lat, gate, nw, shift, scale]
    if router:
        in_specs += [pl.BlockSpec((D_MODEL, LANES), fixed), pl.BlockSpec((D_MODEL, LANES), fixed)]
        out_specs.append(pl.BlockSpec((tm, LANES), row))
        out_shape.append(jax.ShapeDtypeStruct((N_TOK, LANES), F32))
        args += list(router_w)
    return pl.pallas_call(
        functools.partial(_proj_residual_kernel, router=router),
        grid=(N_TOK // tm,),
        in_specs=in_specs,
        out_specs=out_specs,
        out_shape=out_shape,
        scratch_shapes=[pltpu.VMEM((tm, D_MODEL), F32)],
        compiler_params=_params("arbitrary"),
        name="proj_residual",
    )(*args)


def _swiglu_partial(x, wg, wu, wd):
    g = _dot(x, wg)
    u = _dot(x, wu)
    return _dot((_silu(g) * u).astype(BF16), wd)


def _proj_ffn_kernel(ac_ref, al_ref, wo_ref, xc_ref, xl_ref, g1_ref, nw_ref, sh_ref, sc_ref, g2_ref,
                     wg_ref, wu_ref, wd_ref, o_ref):
    i = pl.program_id(0)
    a = jnp.where(i < CTX_TILES, ac_ref[...], al_ref[...])
    x1 = _pick_rows(xc_ref, xl_ref) + g1_ref[...] * _dot(a, wo_ref[...])
    h2 = _modulate(x1, nw_ref[...], sh_ref[...], sc_ref[...]).astype(BF16)
    o_ref[...] = x1 + g2_ref[...] * _swiglu_partial(h2, wg_ref[...], wu_ref[...], wd_ref[...])


def _resident(shape):
    return pl.BlockSpec(shape, lambda *_: (0,) * len(shape), pipeline_mode=pl.Buffered(1))


def _proj_ffn(a_ctx, a_lat, wo, x, gate1, nw, shift, scale, gate2, wg, wu, wd):
    tm = ROW_TILE
    kdim = wo.shape[0]
    cond = lambda i: (_cond_of_tile(i), 0, 0)
    x_ctx, x_lat, lat_tile0 = _split_rows(x)
    return pl.pallas_call(
        _proj_ffn_kernel,
        grid=(N_TOK // tm,),
        in_specs=_split_specs(kdim, 0) + [_resident((kdim, D_MODEL))] + _split_specs(D_MODEL, lat_tile0) + [
            pl.BlockSpec((None, 1, D_MODEL), cond),
            pl.BlockSpec((1, D_MODEL), lambda i: (0, 0)),
            pl.BlockSpec((None, 1, D_MODEL), cond),
            pl.BlockSpec((None, 1, D_MODEL), cond),
            pl.BlockSpec((None, 1, D_MODEL), cond),
            _resident((D_MODEL, FFN_DIM)),
            _resident((D_MODEL, FFN_DIM)),
            _resident((FFN_DIM, D_MODEL)),
        ],
        out_specs=pl.BlockSpec((tm, D_MODEL), lambda i: (i, 0)),
        out_shape=jax.ShapeDtypeStruct((N_TOK, D_MODEL), F32),
        compiler_params=_params("arbitrary"),
        name="proj_ffn",
    )(a_ctx, a_lat, wo, x_ctx, x_lat, gate1, nw, shift, scale, gate2, wg, wu, wd)


def _moe_kernel(blk_e_ref, nact_ref, order_ref, base_ref, x_hbm, wg_ref, wu_ref, wd_ref, ya_hbm,
                xrows0, xrows1, xb, acc, stage0, stage1, gsem, ssem):
    b = pl.program_id(0)
    f = pl.program_id(1)
    nact = nact_ref[0]
    xrows = (xrows0, xrows1)
    stage = (stage0, stage1)
    tile = lambda r: pl.ds(pl.multiple_of(r * LANE_CHUNKS, LANE_CHUNKS), LANE_CHUNKS)

    def row_ids(blk):
        base = base_ref[blk + 1]
        return lambda r: order_ref[base + r]

    def gather_copy(par, step, r, a):
        tok = a & (N_TOK - 1)
        return pltpu.make_async_copy(x_hbm.at[tok], xrows[par].at[step, tile(r), :], gsem.at[par])

    def scatter_copy(par, step, r, a):
        return pltpu.make_async_copy(stage[par].at[step, tile(r), :], ya_hbm.at[a], ssem.at[par])

    def wait_gather(par):
        pltpu.make_async_copy(xrows[par], xrows[par], gsem.at[par]).wait()

    def wait_scatter(par):
        pltpu.make_async_copy(stage[par], stage[par], ssem.at[par]).wait()

    @pl.when((b == 0) & (f == 0))
    def _():
        stage1[...] = jnp.zeros_like(stage1)
        ids = row_ids(0)

        def issue(r, c):
            gather_copy(0, r // MOE_STEP_ROWS, r % MOE_STEP_ROWS, ids(r)).start()
            return c

        lax.fori_loop(0, MOE_ROWS, issue, 0, unroll=8)

    def block_body(par):
        @pl.when(f == 0)
        def _():
            wait_gather(par)

            @pl.when(b >= 1)
            def _():
                wait_scatter(par)

            for step in range(MOE_STEPS):
                for c in range(LANE_CHUNKS):
                    xb[step * MOE_STEP_ROWS:(step + 1) * MOE_STEP_ROWS, c * LANES:(c + 1) * LANES] = (
                        _load_row_major(xrows[par].at[step], MOE_STEP_ROWS, c).astype(BF16))

        nxt, prv = row_ids(b + 1), row_ids(b - 1)
        row0 = f * MOE_STEP_ROWS
        for r in range(MOE_STEP_ROWS):
            gather_copy(1 - par, f, r, nxt(row0 + r)).start()
            scatter_copy(1 - par, f, r, prv(row0 + r)).start()
        part = _swiglu_partial(xb[...], wg_ref[...], wu_ref[...], wd_ref[...])

        @pl.when(f == 0)
        def _():
            acc[...] = part

        @pl.when((f > 0) & (f < MOE_STEPS - 1))
        def _():
            acc[...] += part

        @pl.when(f == MOE_STEPS - 1)
        def _():
            total = acc[...] + part
            for step in range(MOE_STEPS):
                _store_row_major(stage[par].at[step], total[step * MOE_STEP_ROWS:(step + 1) * MOE_STEP_ROWS])

    def drain(par):
        wait_scatter(par)
        ids = row_ids(b - 1)

        def issue(r, c):
            scatter_copy(1 - par, r // MOE_STEP_ROWS, r % MOE_STEP_ROWS, ids(r)).start()
            return c

        lax.fori_loop(0, MOE_ROWS, issue, 0, unroll=8)
        wait_gather(par)
        wait_scatter(1 - par)

    for par in range(2):
        @pl.when((b < nact) & (b % 2 == par))
        def _():
            block_body(par)

        @pl.when((b == nact) & (f == 0) & (b % 2 == par))
        def _():
            drain(par)


def _moe(routing, h2, wg, wu, wd, layer):
    tf = MOE_TILE
    wmap = lambda b, f, e, *_: (layer, e[b], 0, f)
    rows_buf = pltpu.VMEM((MOE_STEPS, MOE_STEP_ROWS * LANE_CHUNKS, LANES), F32)
    grid_spec = pltpu.PrefetchScalarGridSpec(
        num_scalar_prefetch=len(routing),
        grid=(MOE_BLOCKS, MOE_STEPS),
        in_specs=[
            pl.BlockSpec(memory_space=pl.ANY),
            pl.BlockSpec((None, None, D_MODEL, tf), wmap),
            pl.BlockSpec((None, None, D_MODEL, tf), wmap),
            pl.BlockSpec((None, None, tf, D_MODEL), lambda b, f, e, *_: (layer, e[b], f, 0)),
        ],
        out_specs=pl.BlockSpec(memory_space=pl.ANY),
        scratch_shapes=[
            rows_buf,
            rows_buf,
            pltpu.VMEM((MOE_ROWS, D_MODEL), BF16),
            pltpu.VMEM((MOE_ROWS, D_MODEL), F32),
            rows_buf,
            rows_buf,
            pltpu.SemaphoreType.DMA((2,)),
            pltpu.SemaphoreType.DMA((2,)),
        ],
    )
    return pl.pallas_call(
        _moe_kernel,
        grid_spec=grid_spec,
        out_shape=jax.ShapeDtypeStruct((N_ASSIGN + MOE_ROWS, LANE_CHUNKS, LANES), F32),
        compiler_params=_params("arbitrary", "arbitrary"),
        name="moe",
    )(*routing, h2.reshape(N_TOK, LANE_CHUNKS, LANES), wg, wu, wd)


def _combine_kernel(y0_ref, y1_ref, x1_ref, g_ref, w_ref, *o_refs):
    w = w_ref[...]
    rows = x1_ref.shape[0]

    def emit(o_ref):
        for c in range(LANE_CHUNKS):
            sl = slice(c * LANES, (c + 1) * LANES)
            y = w[:, 0:1] * _load_row_major(y0_ref, rows, c) + w[:, 1:2] * _load_row_major(y1_ref, rows, c)
            o_ref[:, sl] = x1_ref[:, sl] + g_ref[:, sl] * y

    if len(o_refs) == 1:
        emit(o_refs[0])
    else:
        i = pl.program_id(0)

        @pl.when(i < CTX_TILES)
        def _():
            emit(o_refs[0])

        @pl.when(i >= CTX_TILES)
        def _():
            emit(o_refs[1])


def _combine(ya, x1, gate, top_w, split):
    tm = ROW_TILE
    row = lambda i: (i, 0)
    ya2 = ya.reshape(ya.shape[0] * LANE_CHUNKS, LANES)
    if split:
        out_specs = [pl.BlockSpec((tm, D_MODEL), lambda i: (jnp.minimum(i, CTX_TILES - 1), 0)),
                     pl.BlockSpec((tm, D_MODEL), lambda i: (jnp.maximum(i - CTX_TILES, 0), 0))]
        out_shape = [jax.ShapeDtypeStruct((N_CTX, D_MODEL), F32), jax.ShapeDtypeStruct((N_LAT, D_MODEL), F32)]
    else:
        out_specs = pl.BlockSpec((tm, D_MODEL), row)
        out_shape = jax.ShapeDtypeStruct((N_TOK, D_MODEL), F32)
    return pl.pallas_call(
        _combine_kernel,
        grid=(N_TOK // tm,),
        in_specs=[
            pl.BlockSpec((tm * LANE_CHUNKS, LANES), row),
            pl.BlockSpec((tm * LANE_CHUNKS, LANES), lambda i: (i + N_TOK // tm, 0)),
            pl.BlockSpec((tm, D_MODEL), row),
            pl.BlockSpec((None, 1, D_MODEL), lambda i: (_cond_of_tile(i), 0, 0)),
            pl.BlockSpec((tm, TOP_K), row),
        ],
        out_specs=out_specs,
        out_shape=out_shape,
        compiler_params=_params("arbitrary"),
        name="moe_combine",
    )(ya2, ya2, x1, gate, top_w)


def _route(logits):
    top_v, top_i = lax.top_k(logits, TOP_K)
    top_w = jax.nn.softmax(top_v, axis=-1)
    e = top_i.reshape(N_ASSIGN).astype(jnp.int32)
    a = jnp.arange(N_ASSIGN, dtype=jnp.int32)
    row_id = (a % TOP_K) * N_TOK + a // TOP_K
    order = jnp.sort(e * N_ASSIGN + row_id) & (N_ASSIGN - 1)
    order = jnp.concatenate([order, N_ASSIGN + jnp.arange(MOE_ROWS, dtype=jnp.int32)])
    counts = jnp.sum((e[:, None] == jnp.arange(N_EXPERTS, dtype=jnp.int32)[None, :]).astype(jnp.int32), axis=0)
    start = jnp.cumsum(counts) - counts
    nblk = (counts + MOE_ROWS - 1) // MOE_ROWS
    blk_end = jnp.cumsum(nblk)
    blk = jnp.arange(-1, MOE_BLOCKS + 1, dtype=jnp.int32)
    blk_e = jnp.sum((blk[:, None] >= blk_end[None, :]).astype(jnp.int32), axis=1)
    blk_e = jnp.minimum(blk_e, N_EXPERTS - 1)
    within = (blk - (blk_end - nblk)[blk_e]) * MOE_ROWS
    real = (blk >= 0) & (blk < blk_end[-1])
    blk_base = jnp.where(real, start[blk_e] + within, N_ASSIGN).astype(jnp.int32)
    nact = blk_end[-1:].astype(jnp.int32)
    return top_w, (blk_e[1:-1], nact, order, blk_base)


RET_QK_W = RET_HEADS * RET_DK
RET_V_W = RET_HEADS * RET_DV
RET_COL_TILE = 1024


def _ret_proj_kernel(x_ref, nw_ref, sh_ref, sc_ref, w_ref, c_ref, s_ref, o_ref):
    def body(rope):
        h = _modulate(x_ref[...], nw_ref[...], sh_ref[...], sc_ref[...]).astype(BF16)
        for c0 in range(0, o_ref.shape[1], RET_COL_TILE):
            acc = _dot(h, w_ref[:, c0:c0 + RET_COL_TILE])
            if c0 >= 2 * RET_QK_W + RET_V_W:
                o_ref[:, c0:c0 + RET_COL_TILE] = _silu(acc).astype(BF16)
            elif c0 >= 2 * RET_QK_W:
                o_ref[:, c0:c0 + RET_COL_TILE] = acc.astype(BF16)
            else:
                x = acc * (RET_DK ** -0.5) if c0 >= RET_QK_W else acc
                if not rope:
                    o_ref[:, c0:c0 + RET_COL_TILE] = x.astype(BF16)
                    continue
                for g in range(RET_COL_TILE // LANES):
                    tl = slice((g % 2) * LANES, (g % 2 + 1) * LANES)
                    xs = x[:, g * LANES:(g + 1) * LANES]
                    y = xs * c_ref[:, tl] + pltpu.roll(xs, RET_DK // 4, 1) * s_ref[:, tl]
                    o_ref[:, c0 + g * LANES:c0 + (g + 1) * LANES] = y.astype(BF16)

    i = pl.program_id(0)

    @pl.when(i < CTX_TILES)
    def _():
        body(False)

    @pl.when(i >= CTX_TILES)
    def _():
        body(True)


def _ret_proj(x, nw, shift, scale, w, rope_c, rope_s):
    tm = ROW_TILE
    width = w.shape[1]
    row = lambda i: (i, 0)
    cond = lambda i: (_cond_of_tile(i), 0, 0)
    tab = lambda i: (jnp.where(i < CTX_TILES, 0, (i - CTX_TILES) % TILES_PER_LAT_BATCH), 0)
    return pl.pallas_call(
        _ret_proj_kernel,
        grid=(N_TOK // tm,),
        in_specs=[
            pl.BlockSpec((tm, D_MODEL), row),
            pl.BlockSpec((1, D_MODEL), lambda i: (0, 0)),
            pl.BlockSpec((None, 1, D_MODEL), cond),
            pl.BlockSpec((None, 1, D_MODEL), cond),
            _resident((D_MODEL, width)),
            pl.BlockSpec((tm, RET_DK), tab),
            pl.BlockSpec((tm, RET_DK), tab),
        ],
        out_specs=pl.BlockSpec((tm, width), row),
        out_shape=jax.ShapeDtypeStruct((N_TOK, width), BF16),
        compiler_params=_params("parallel"),
        name="ret_proj",
    )(x, nw, shift, scale, w, rope_c, rope_s)


def _retention_kernel(lg_ref, q_ref, k_ref, v_ref, sg_ref, gn_ref, *rest, zero_init, n_chunks, layer, heads):
    if zero_init:
        y_ref, st_ref, o_acc, state, dmask = rest[-5:]
    else:
        s0f_ref, s0b_ref, y_ref, o_acc, state, dmask = rest
    hd0 = pl.program_id(1) * heads
    c = RET_BLOCK
    ri = lax.broadcasted_iota(jnp.int32, (c, c), 0).astype(F32)
    ci = lax.broadcasted_iota(jnp.int32, (c, c), 1).astype(F32)
    pos = lax.broadcasted_iota(jnp.int32, (c, 1), 0).astype(F32)
    chains = [(h, d) for h in range(heads) for d in range(2)]
    q_decay, k_decay, c_decay = {}, {}, {}
    for h, d in chains:
        lg = lg_ref[d, hd0 + h]
        rel = (ri - ci) if d == 0 else (ci - ri)
        dmask[h, d] = jnp.where(rel >= 0, jnp.exp(jnp.maximum(rel, 0.0) * lg), 0.0)
        q_decay[h, d] = jnp.exp(((pos + 1.0) if d == 0 else (c - pos)) * lg)
        k_decay[h, d] = jnp.exp(((c - 1.0 - pos) if d == 0 else pos) * lg)
        c_decay[h, d] = jnp.exp(jnp.full((1, 1), c, F32) * lg)
        if zero_init:
            state[h, d] = jnp.zeros(state.shape[2:], F32)
        else:
            state[h, d] = (s0f_ref if d == 0 else s0b_ref)[h]

    def step(t, carry):
        rows = [pl.ds(pl.multiple_of((t if d == 0 else n_chunks - 1 - t) * c, c), c) for _, d in chains]
        qs = [q_ref[r, h * RET_DK:(h + 1) * RET_DK] for r, (h, _) in zip(rows, chains)]
        ks = [k_ref[r, h * RET_DK:(h + 1) * RET_DK] for r, (h, _) in zip(rows, chains)]
        vs = [v_ref[r, h * RET_DV:(h + 1) * RET_DV] for r, (h, _) in zip(rows, chains)]
        att = [lax.dot_general(q, k, (((1,), (1,)), ((), ())), preferred_element_type=F32) * dmask[hd]
               for q, k, hd in zip(qs, ks, chains)]
        old = [state[hd] for hd in chains]
        for r, hd, a, q, v, s in zip(rows, chains, att, qs, vs, old):
            o_acc[hd[0], hd[1], r, :] = (_dot(a.astype(BF16), v)
                                         + _dot((q.astype(F32) * q_decay[hd]).astype(BF16), s.astype(BF16)))
        for hd, k, v, s in zip(chains, ks, vs, old):
            kd = (k.astype(F32) * k_decay[hd]).astype(BF16)
            state[hd] = s * c_decay[hd] + lax.dot_general(kd, v, (((0,), (0,)), ((), ())),
                                                           preferred_element_type=F32)
        return carry

    lax.fori_loop(0, n_chunks, step, 0)

    if zero_init:
        for h, d in chains:
            if st_ref.ndim == 4:
                st_ref[d, h] = state[h, d]
            else:
                st_ref[layer, d, h] = state[h, d]
                for other in range(st_ref.shape[0]):
                    if other != layer:
                        st_ref[other, d, h] = jnp.zeros(state.shape[2:], F32)

    for h in range(heads):
        cols = slice(h * RET_DV, (h + 1) * RET_DV)
        y = _rms(o_acc[h, 0] + o_acc[h, 1], gn_ref[h])
        y_ref[:, cols] = (sg_ref[:, cols].astype(F32) * y).astype(BF16)


def _retention(log_gamma, qkvg, gn, state0, *, batch, b_off, t, layer, states=None):
    zero_init = state0 is None
    heads = RET_HEADS if t == RET_BLOCK else 1
    qw, vw = heads * RET_DK, heads * RET_DV
    kblk = RET_HEADS * RET_DK // qw
    vblk = 2 * RET_HEADS * RET_DK // vw
    gblk = vblk + RET_HEADS * RET_DV // vw
    in_specs = [
        pl.BlockSpec(memory_space=pltpu.SMEM),
        pl.BlockSpec((None, t, qw), lambda b, h: (b + b_off, 0, h)),
        pl.BlockSpec((None, t, qw), lambda b, h: (b + b_off, 0, kblk + h)),
        pl.BlockSpec((None, t, vw), lambda b, h: (b + b_off, 0, vblk + h)),
        pl.BlockSpec((None, t, vw), lambda b, h: (b + b_off, 0, gblk + h)),
        pl.BlockSpec((heads, 1, RET_DV), lambda b, h: (h, 0, 0)),
    ]
    args = [log_gamma, qkvg, qkvg, qkvg, qkvg, gn]
    y_spec = pl.BlockSpec((None, t, vw), lambda b, h: (b, 0, h))
    y_shape = jax.ShapeDtypeStruct((batch, t, RET_HEADS * RET_DV), BF16)
    aliases = {}
    if zero_init:
        n_ret = DEPTH // 2
        st_shape = jax.ShapeDtypeStruct((batch, n_ret, 2, RET_HEADS, RET_DK, RET_DV), F32)
        if states is None:
            st_spec = pl.BlockSpec((None, n_ret, 2, heads, RET_DK, RET_DV), lambda b, h: (b, 0, 0, h, 0, 0))
        else:
            st_spec = pl.BlockSpec((None, None, 2, heads, RET_DK, RET_DV), lambda b, h: (b, layer, 0, h, 0, 0))
            aliases = {len(args): 1}
            in_specs.append(pl.BlockSpec(memory_space=pl.ANY))
            args.append(states)
        out_specs = [y_spec, st_spec]
        out_shape = [y_shape, st_shape]
    else:
        in_specs += [
            pl.BlockSpec((None, None, None, heads, RET_DK, RET_DV), lambda b, h: (b, layer, 0, h, 0, 0)),
            pl.BlockSpec((None, None, None, heads, RET_DK, RET_DV), lambda b, h: (b, layer, 1, h, 0, 0))]
        args += [state0, state0]
        out_specs = y_spec
        out_shape = y_shape
    return pl.pallas_call(
        functools.partial(_retention_kernel, zero_init=zero_init, n_chunks=t // RET_BLOCK, layer=layer,
                          heads=heads),
        grid=(batch, RET_HEADS // heads),
        in_specs=in_specs,
        out_specs=out_specs,
        out_shape=out_shape,
        input_output_aliases=aliases,
        scratch_shapes=[pltpu.VMEM((heads, 2, t, RET_DV), F32), pltpu.VMEM((heads, 2, RET_DK, RET_DV), F32),
                        pltpu.VMEM((heads, 2, RET_BLOCK, RET_BLOCK), F32)],
        compiler_params=_params("parallel", "parallel"),
        name="retention",
    )(*args)


def _rope_tables(rot_dim):
    rows = DEC_SEQ // GRID_W
    row = jnp.repeat(jnp.arange(rows), GRID_W)
    col = jnp.tile(jnp.arange(GRID_W), rows)
    nf = rot_dim // 4
    inv = ROPE_BASE ** (-jnp.arange(nf, dtype=F32) / nf)
    ang = jnp.stack([row, col], axis=-1).astype(F32)[:, :, None] * inv
    cos, sin = jnp.cos(ang), jnp.sin(ang)
    c = jnp.stack([cos, cos], axis=2).reshape(DEC_SEQ, rot_dim)
    s = jnp.stack([-sin, sin], axis=2).reshape(DEC_SEQ, rot_dim)
    return c, s


def _swap_rope(x):
    blocks = x.reshape(x.shape[:-1] + (2, 2, MLA_ROPE // 4))
    return blocks[..., ::-1, :].reshape(x.shape)


def _slot_orders(x):
    n0, n1, rope = x[..., :MLA_ROPE], x[..., MLA_ROPE:MLA_NOPE], x[..., MLA_NOPE:]
    zero = jnp.zeros_like(rope)
    orders = ((rope, n0, n1, zero), (n0, rope, n1, zero), (n0, n1, rope, zero), (n0, n1, zero, rope))
    return [jnp.concatenate(parts, axis=-1) for parts in orders]


def _to_head_slots(w):
    w = w.reshape(w.shape[:-2] + (MLA_HEADS // SLOT_GROUP, SLOT_GROUP, MLA_QK))
    slots = [_slot_orders(w[..., g, :])[g] for g in range(SLOT_GROUP)]
    return jnp.stack(slots, axis=-2).reshape(w.shape[:-3] + (MLA_HEADS, HEAD_SLOT))


def _head_slot_tables(c32, s32):
    t = c32.shape[0]
    own = (jnp.arange(SLOT_GROUP)[:, None] == jnp.arange(SLOT_GROUP)[None, :])[None, :, :, None]
    ca = jnp.where(own, c32[:, None, None, :], 1.0)
    sb = jnp.where(own, s32[:, None, None, :], 0.0)
    return ca.reshape(t, SLOT_GROUP * HEAD_SLOT), sb.reshape(t, SLOT_GROUP * HEAD_SLOT)


def _head_slot_gains(g, scale):
    ga = jnp.stack(_slot_orders(g)) * scale
    gb = jnp.zeros((SLOT_GROUP, SLOT_GROUP, MLA_ROPE), F32)
    for s in range(SLOT_GROUP):
        gb = gb.at[s, s].set(_swap_rope(g[MLA_NOPE:]) * scale)
    return ga.reshape(1, SLOT_GROUP * HEAD_SLOT), gb.reshape(1, SLOT_GROUP * HEAD_SLOT)


def _rows3(m):
    return m.reshape(N_COND, 1, D_MODEL)


def kernel(x_prompt, x_sample, cache_ckv, cache_kpe, state_ret, c, c_ctx, mod_w, mod_b, norm1_w, norm2_w, mla_wq_a, mla_q_a_norm, mla_wq_b, mla_wkv_a, mla_kv_norm, mla_w_uk, mla_w_uv, mla_q_norm, mla_k_norm, mla_wo, ret_wq, ret_wk, ret_wv, ret_wg, ret_wo, ret_decay, ret_gn, ffn_w_gate, ffn_w_up, ffn_w_down, moe_router, moe_w_gate, moe_w_up, moe_w_down):
    x = (x_prompt.reshape(N_CTX, D_MODEL), x_sample.reshape(N_LAT, D_MODEL))
    cond = jnp.concatenate([c_ctx[None, :], c, jnp.zeros((N_COND - 1 - DEC_BATCH, D_MODEL), F32)], axis=0)
    mods = _adaln(cond, mod_w, mod_b).reshape(DEPTH, N_COND, 6, D_MODEL)

    c32, s32 = _rope_tables(MLA_ROPE)
    c32 = jnp.concatenate([jnp.ones((ROW_TILE, MLA_ROPE), F32), c32], axis=0)
    s32 = jnp.concatenate([jnp.zeros((ROW_TILE, MLA_ROPE), F32), s32], axis=0)
    rope_ca, rope_sb = _head_slot_tables(c32, s32)
    lat_kv = DEC_SEQ + PAST_LEN
    kv_tiles = lat_kv // ROW_TILE
    tab_q = lambda i: jnp.where(i < CTX_TILES, 0, 1 + (i - CTX_TILES) % TILES_PER_LAT_BATCH)
    tab_kc = lambda i: 0
    tab_kl = lambda i: jnp.where(i % kv_tiles < TILES_PER_LAT_BATCH, 1 + i % kv_tiles, 0)
    rc256, rs256 = _rope_tables(RET_DK)

    moe_wg, moe_wu, moe_wd = moe_w_gate.astype(BF16), moe_w_up.astype(BF16), moe_w_down.astype(BF16)

    new_ckv, new_kpe, new_ret = [], [], None
    for i in range(DEPTH):
        j = i // 2
        m = mods[i]
        sh1, sc1, g1, sh2, sc2, g2 = (_rows3(m[:, t]) for t in range(6))
        nw1 = norm1_w[i].reshape(1, D_MODEL)
        nw2 = norm2_w[i].reshape(1, D_MODEL)
        if i % 2 == 0:
            wqb = mla_wq_b[j].reshape(MLA_Q_RANK, MLA_HEADS, MLA_QK)
            wqb_swap = _swap_rope(wqb[:, :, MLA_NOPE:]).reshape(MLA_Q_RANK, MLA_HEADS * MLA_ROPE)
            wqb = jnp.concatenate([_to_head_slots(wqb).reshape(MLA_Q_RANK, MLA_HEADS * HEAD_SLOT), wqb_swap],
                                  axis=1).astype(BF16)
            wkv = mla_wkv_a[j]
            w_pe = wkv[:, MLA_KV_RANK:]
            wkv = jnp.concatenate([wkv[:, :MLA_KV_RANK], jnp.tile(w_pe, (1, SLOT_GROUP)),
                                   jnp.tile(_swap_rope(w_pe), (1, SLOT_GROUP))], axis=1).astype(BF16)
            wuk = mla_w_uk[j].reshape(MLA_KV_RANK, MLA_HEADS, MLA_NOPE)
            wuk = jnp.concatenate([wuk, jnp.zeros((MLA_KV_RANK, MLA_HEADS, MLA_ROPE), F32)], axis=2)
            wuk = _to_head_slots(wuk).reshape(MLA_KV_RANK, MLA_HEADS * HEAD_SLOT).astype(BF16)
            gaq, gbq = _head_slot_gains(mla_q_norm[j], MLA_QK ** -0.5 * LOG2_E)
            gak, gbk = _head_slot_gains(mla_k_norm[j], 1.0)

            q, ckv, kpe = _mla_proj(x, nw1, sh1, sc1, mla_wq_a[j].astype(BF16),
                                    mla_q_a_norm[j].reshape(1, MLA_Q_RANK), wqb, wkv,
                                    mla_kv_norm[j].reshape(1, MLA_KV_RANK), gaq, gbq, rope_ca, rope_sb, tab_q)
            new_ckv.append(ckv[:N_CTX].reshape(BATCH, SEQ, MLA_KV_RANK))
            new_kpe.append(kpe[:N_CTX, :MLA_ROPE].reshape(BATCH, SEQ, MLA_ROPE))

            kpe_x = cache_kpe[:, j]
            kpe_x = jnp.concatenate([jnp.tile(kpe_x, (1, 1, SLOT_GROUP)), jnp.tile(_swap_rope(kpe_x), (1, 1, SLOT_GROUP))],
                                    axis=2)
            wuv = mla_w_uv[j].astype(BF16)
            k_c, v_c = _kv_expand(ckv, kpe, wuk, wuv, gak, gbk, rope_ca, rope_sb, tab_kc)
            k_l, v_l = _kv_expand(ckv, kpe, wuk, wuv, gak, gbk, rope_ca, rope_sb, tab_kl,
                                  cache=(cache_ckv, j, kpe_x))

            qw = MLA_HEADS * HEAD_SLOT
            o_c = _attention(q.reshape(N_TOK // SEQ, SEQ, qw), k_c.reshape(BATCH, SEQ, qw),
                             v_c.reshape(BATCH, SEQ, MLA_HEADS * MLA_V),
                             batch=BATCH, q_off=0, tq_total=SEQ, tq=SEQ, heads=MLA_HEADS)
            o_l = _attention(q.reshape(N_TOK // DEC_SEQ, DEC_SEQ, qw), k_l.reshape(DEC_BATCH, lat_kv, qw),
                             v_l.reshape(DEC_BATCH, lat_kv, MLA_HEADS * MLA_V),
                             batch=DEC_BATCH, q_off=N_CTX // DEC_SEQ, tq_total=DEC_SEQ, tq=1024, heads=2)
            mix_c, mix_l = o_c.reshape(N_CTX, D_MODEL), o_l.reshape(N_LAT, D_MODEL)
            w_out = mla_wo[j].astype(BF16)
        else:
            w_in = jnp.concatenate([ret_wq[j], ret_wk[j], ret_wv[j], ret_wg[j]], axis=1).astype(BF16)
            qkvg = _ret_proj(x, nw1, sh1, sc1, w_in, rc256, rs256)
            log_gamma = -jnp.exp(ret_decay[j].astype(F32))
            gn = ret_gn[j].reshape(RET_HEADS, 1, RET_DV)
            width = qkvg.shape[1]
            y_c, new_ret = _retention(log_gamma, qkvg.reshape(N_TOK // SEQ, SEQ, width), gn, None,
                                      batch=BATCH, b_off=0, t=SEQ, layer=j, states=new_ret)
            y_l = _retention(log_gamma, qkvg.reshape(N_TOK // DEC_SEQ, DEC_SEQ, width), gn, state_ret,
                             batch=DEC_BATCH, b_off=N_CTX // DEC_SEQ, t=DEC_SEQ, layer=j)
            mix_c, mix_l = y_c.reshape(N_CTX, RET_HEADS * RET_DV), y_l.reshape(N_LAT, RET_HEADS * RET_DV)
            w_out = ret_wo[j].astype(BF16)

        if i % 2 == 0:
            x = _proj_ffn(mix_c, mix_l, w_out, x, g1, nw2, sh2, sc2, g2,
                          ffn_w_gate[j].astype(BF16), ffn_w_up[j].astype(BF16), ffn_w_down[j].astype(BF16))
        else:
            wr = jnp.pad(moe_router[j], ((0, 0), (0, LANES - N_EXPERTS)))
            wr_hi = wr.astype(BF16)
            wr_lo = (wr - wr_hi.astype(F32)).astype(BF16)
            x1, h2, logits = _proj_residual(mix_c, mix_l, w_out, x, g1, nw2, sh2, sc2, router_w=(wr_hi, wr_lo))
            top_w, routing = _route(logits[:, :N_EXPERTS])
            ya = _moe(routing, h2, moe_wg, moe_wu, moe_wd, j)
            x = _combine(ya, x1, g2, top_w, split=(i == DEPTH - 1))

    y_prompt = x[0].reshape(BATCH, SEQ, D_MODEL)
    y_sample = x[1].reshape(DEC_BATCH, DEC_SEQ, D_MODEL)
    return (y_prompt, y_sample, jnp.stack(new_ckv, axis=1), jnp.stack(new_kpe, axis=1), new_ret)
```

```python
import functools

import jax
import jax.numpy as jnp
from jax import lax
from jax.experimental import pallas as pl
from jax.experimental.pallas import tpu as pltpu

F32 = jnp.float32
BF16 = jnp.bfloat16

D_MODEL = 1024
BATCH = 32
SEQ = 256
DEPTH = 4
DEC_BATCH = 4
DEC_SEQ = 2048
PAST_LEN = 512
GRID_W = 64
ROPE_BASE = 10000.0
NORM_EPS = 1e-6

MLA_HEADS = 16
MLA_NOPE = 64
MLA_ROPE = 32
MLA_QK = MLA_NOPE + MLA_ROPE
MLA_V = 64
MLA_Q_RANK = 384
MLA_KV_RANK = 256
LANES = 128
HEAD_SLOT = LANES

RET_HEADS = 4
RET_DK = 256
RET_DV = 512
RET_BLOCK = 256

FFN_DIM = 2816
N_EXPERTS = 8
TOP_K = 2
EXPERT_DIM = 3584

N_CTX = BATCH * SEQ
N_LAT = DEC_BATCH * DEC_SEQ
N_TOK = N_CTX + N_LAT
N_COND = 8

ROW_TILE = 512
CTX_TILES = N_CTX // ROW_TILE
TILES_PER_LAT_BATCH = DEC_SEQ // ROW_TILE

MOE_ROWS = 512
MOE_TILE = 1792
MOE_STEPS = EXPERT_DIM // MOE_TILE
MOE_STEP_ROWS = MOE_ROWS // MOE_STEPS
assert MOE_STEPS >= 2
N_ASSIGN = N_TOK * TOP_K
MOE_BLOCKS = N_ASSIGN // MOE_ROWS + N_EXPERTS

VMEM_LIMIT = 56 * 1024 * 1024
LOG2_E = 1.4426950408889634


def _params(*sem):
    return pltpu.CompilerParams(dimension_semantics=sem, vmem_limit_bytes=VMEM_LIMIT)


def _cond_of_tile(i):
    return jnp.where(i < CTX_TILES, 0, 1 + (i - CTX_TILES) // TILES_PER_LAT_BATCH)


def _dot(a, b):
    return jnp.dot(a, b, preferred_element_type=F32)


def _silu(x):
    return (0.5 * x) * (1.0 + jnp.tanh(0.5 * x))


def _rms(x, w):
    return x * lax.rsqrt(jnp.mean(x * x, axis=-1, keepdims=True) + NORM_EPS) * w


def _modulate(x, nw, shift, scale):
    return _rms(x, nw) * (1.0 + scale) + shift


def _adaln_kernel(c_ref, w_ref, b_ref, o_ref):
    a = _silu(c_ref[...]).astype(BF16)
    o_ref[...] = _dot(a, w_ref[...].astype(BF16)) + b_ref[...]


def _adaln(cond, mod_w, mod_b):
    tn = 1024
    return pl.pallas_call(
        _adaln_kernel,
        grid=(DEPTH, 6 * D_MODEL // tn),
        in_specs=[
            pl.BlockSpec((N_COND, D_MODEL), lambda l, j: (0, 0)),
            pl.BlockSpec((None, D_MODEL, tn), lambda l, j: (l, 0, j)),
            pl.BlockSpec((None, 1, tn), lambda l, j: (l, 0, j)),
        ],
        out_specs=pl.BlockSpec((None, N_COND, tn), lambda l, j: (l, 0, j)),
        out_shape=jax.ShapeDtypeStruct((DEPTH, N_COND, 6 * D_MODEL), F32),
        compiler_params=_params("parallel", "parallel"),
        name="adaln",
    )(cond, mod_w, mod_b.reshape(DEPTH, 1, 6 * D_MODEL))


SLOT_GROUP = HEAD_SLOT // MLA_ROPE


def _heads_norm_rope(xs, swaps, ga, gb):
    ss = [jnp.sum(x * x, axis=-1, keepdims=True) for x in xs]
    rs = [lax.rsqrt(s * (1.0 / MLA_QK) + NORM_EPS) for s in ss]
    out = []
    for hd, (r, x, sw) in enumerate(zip(rs, xs, swaps)):
        lanes = slice((hd % SLOT_GROUP) * HEAD_SLOT, (hd % SLOT_GROUP + 1) * HEAD_SLOT)
        out.append(r * (x * ga[:, lanes] + sw * gb[:, lanes]))
    return out


def _split_rows(x):
    if isinstance(x, tuple):
        return x[0], x[1], 0
    return x, x, CTX_TILES


def _split_specs(width, lat_tile0, tm=ROW_TILE):
    return [pl.BlockSpec((tm, width), lambda i: (jnp.minimum(i, CTX_TILES - 1), 0)),
            pl.BlockSpec((tm, width), lambda i: (jnp.maximum(i - CTX_TILES, 0) + lat_tile0, 0))]


def _pick_rows(ctx_ref, lat_ref):
    return jnp.where(pl.program_id(0) < CTX_TILES, ctx_ref[...], lat_ref[...])


def _mla_proj_kernel(xc_ref, xl_ref, nw_ref, sh_ref, sc_ref, wqa_ref, qan_ref, wqb_ref, wkv_ref, kvn_ref,
                     ga_ref, gb_ref, ca_ref, sb_ref, q_ref, ckv_ref, kpe_ref):
    h = _modulate(_pick_rows(xc_ref, xl_ref), nw_ref[...], sh_ref[...], sc_ref[...]).astype(BF16)
    qa = _rms(_dot(h, wqa_ref[...]), qan_ref[...]).astype(BF16)
    q = _dot(qa, wqb_ref[...])
    kv = _dot(h, wkv_ref[...])
    ckv_ref[...] = _rms(kv[:, :MLA_KV_RANK], kvn_ref[...])
    kpe_ref[...] = kv[:, MLA_KV_RANK:]
    ga = ga_ref[...] * ca_ref[...]
    gb = gb_ref[...] * sb_ref[...]
    slots = [slice(hd * HEAD_SLOT, (hd + 1) * HEAD_SLOT) for hd in range(MLA_HEADS)]
    swap0 = MLA_HEADS * HEAD_SLOT
    swaps = [q[:, swap0 + (hd // SLOT_GROUP) * HEAD_SLOT:swap0 + (hd // SLOT_GROUP + 1) * HEAD_SLOT]
             for hd in range(MLA_HEADS)]
    for sl, y in zip(slots, _heads_norm_rope([q[:, sl] for sl in slots], swaps, ga, gb)):
        q_ref[:, sl] = y.astype(BF16)


def _mla_proj(x, nw, shift, scale, wqa, qan, wqb, wkv, kvn, ga, gb, ca, sb, tab_index):
    tm = ROW_TILE
    row = lambda i: (i, 0)
    fixed = lambda i: (0, 0)
    cond = lambda i: (_cond_of_tile(i), 0, 0)
    qw = MLA_HEADS * HEAD_SLOT
    tabw = SLOT_GROUP * HEAD_SLOT
    kpew = 2 * HEAD_SLOT
    tab = lambda i: (tab_index(i), 0)
    x_ctx, x_lat, lat_tile0 = _split_rows(x)
    return pl.pallas_call(
        _mla_proj_kernel,
        grid=(N_TOK // tm,),
        in_specs=_split_specs(D_MODEL, lat_tile0) + [
            pl.BlockSpec((1, D_MODEL), fixed),
            pl.BlockSpec((None, 1, D_MODEL), cond),
            pl.BlockSpec((None, 1, D_MODEL), cond),
            pl.BlockSpec((D_MODEL, MLA_Q_RANK), fixed),
            pl.BlockSpec((1, MLA_Q_RANK), fixed),
            pl.BlockSpec((MLA_Q_RANK, wqb.shape[1]), fixed),
            pl.BlockSpec((D_MODEL, MLA_KV_RANK + kpew), fixed),
            pl.BlockSpec((1, MLA_KV_RANK), fixed),
            pl.BlockSpec((1, tabw), fixed),
            pl.BlockSpec((1, tabw), fixed),
            pl.BlockSpec((tm, tabw), tab),
            pl.BlockSpec((tm, tabw), tab),
        ],
        out_specs=[
            pl.BlockSpec((tm, qw), row),
            pl.BlockSpec((tm, MLA_KV_RANK), row),
            pl.BlockSpec((tm, kpew), row),
        ],
        out_shape=[
            jax.ShapeDtypeStruct((N_TOK, qw), BF16),
            jax.ShapeDtypeStruct((N_TOK, MLA_KV_RANK), F32),
            jax.ShapeDtypeStruct((N_TOK, kpew), F32),
        ],
        compiler_params=_params("arbitrary"),
        name="mla_proj",
    )(x_ctx, x_lat, nw, shift, scale, wqa, qan, wqb, wkv, kvn, ga, gb, ca, sb)


def _kv_expand_kernel(ckv_ref, kpe_ref, *rest, cached):
    if cached:
        cckv_ref, ckpe_ref, wuk_ref, wuv_ref, ga_ref, gb_ref, ca_ref, sb_ref, k_ref, v_ref = rest
        from_cache = pl.program_id(0) % LAT_KV_TILES == LAT_KV_TILES - 1
        c = jnp.where(from_cache, cckv_ref[...], ckv_ref[...]).astype(BF16)
        kpe2 = jnp.where(from_cache, ckpe_ref[...], kpe_ref[...])
    else:
        wuk_ref, wuv_ref, ga_ref, gb_ref, ca_ref, sb_ref, k_ref, v_ref = rest
        c = ckv_ref[...].astype(BF16)
        kpe2 = kpe_ref[...]
    kn = _dot(c, wuk_ref[...])
    v_ref[...] = _dot(c, wuv_ref[...]).astype(BF16)
    kpe = kpe2[:, :HEAD_SLOT]
    swap = kpe2[:, HEAD_SLOT:]
    ga = ga_ref[...] * ca_ref[...]
    gb = gb_ref[...] * sb_ref[...]
    group = lax.broadcasted_iota(jnp.int32, kpe.shape, 1) // MLA_ROPE
    kpe_at = [jnp.where(group == g, kpe, 0.0) for g in range(SLOT_GROUP)]
    slots = [slice(hd * HEAD_SLOT, (hd + 1) * HEAD_SLOT) for hd in range(MLA_HEADS)]
    xs = [kn[:, sl] + kpe_at[hd % SLOT_GROUP] for hd, sl in enumerate(slots)]
    for sl, y in zip(slots, _heads_norm_rope(xs, [swap] * MLA_HEADS, ga, gb)):
        k_ref[:, sl] = y.astype(BF16)


LAT_KV_TILES = (DEC_SEQ + PAST_LEN) // ROW_TILE
assert PAST_LEN == ROW_TILE


def _kv_expand(ckv, kpe, wuk, wuv, ga, gb, ca, sb, tab_index, cache=None):
    tm = ROW_TILE
    row = lambda i: (i, 0)
    fixed = lambda i: (0, 0)
    tab = lambda i: (tab_index(i), 0)
    kw = MLA_HEADS * HEAD_SLOT
    vw = MLA_HEADS * MLA_V
    tabw = SLOT_GROUP * HEAD_SLOT
    if cache is None:
        n = N_CTX
        src = row
        cache_specs, cache_args = [], []
    else:
        n = DEC_BATCH * (DEC_SEQ + PAST_LEN)
        cache_ckv, layer, cache_kpe2 = cache
        src = lambda i: (CTX_TILES + (i // LAT_KV_TILES) * TILES_PER_LAT_BATCH
                         + jnp.minimum(i % LAT_KV_TILES, TILES_PER_LAT_BATCH - 1), 0)
        cache_specs = [pl.BlockSpec((None, None, PAST_LEN, MLA_KV_RANK), lambda i: (i // LAT_KV_TILES, layer, 0, 0)),
                       pl.BlockSpec((None, PAST_LEN, 2 * HEAD_SLOT), lambda i: (i // LAT_KV_TILES, 0, 0))]
        cache_args = [cache_ckv, cache_kpe2]
    return pl.pallas_call(
        functools.partial(_kv_expand_kernel, cached=cache is not None),
        grid=(n // tm,),
        in_specs=[
            pl.BlockSpec((tm, MLA_KV_RANK), src),
            pl.BlockSpec((tm, 2 * HEAD_SLOT), src),
        ] + cache_specs + [
            pl.BlockSpec((MLA_KV_RANK, kw), fixed),
            pl.BlockSpec((MLA_KV_RANK, vw), fixed),
            pl.BlockSpec((1, tabw), fixed),
            pl.BlockSpec((1, tabw), fixed),
            pl.BlockSpec((tm, tabw), tab),
            pl.BlockSpec((tm, tabw), tab),
        ],
        out_specs=[pl.BlockSpec((tm, kw), row), pl.BlockSpec((tm, vw), row)],
        out_shape=[jax.ShapeDtypeStruct((n, kw), BF16), jax.ShapeDtypeStruct((n, vw), BF16)],
        compiler_params=_params("arbitrary"),
        name="kv_expand",
    )(ckv, kpe, *cache_args, wuk, wuv, ga, gb, ca, sb)


def _attn_kernel(q_ref, k_ref, v_ref, o_ref, *, heads):
    scores = []
    for hd in range(heads):
        qh = q_ref[:, hd * HEAD_SLOT:(hd + 1) * HEAD_SLOT]
        kh = k_ref[:, hd * HEAD_SLOT:(hd + 1) * HEAD_SLOT]
        scores.append(lax.dot_general(qh, kh, (((1,), (1,)), ((), ())), preferred_element_type=F32))
    probs, invs = [], []
    for s in scores:
        p = jnp.exp2(s - jnp.max(s, axis=-1, keepdims=True))
        invs.append(1.0 / jnp.sum(p, axis=-1, keepdims=True))
        probs.append(p.astype(BF16))
    for hd in range(heads):
        o = _dot(probs[hd], v_ref[:, hd * MLA_V:(hd + 1) * MLA_V])
        o_ref[:, hd * MLA_V:(hd + 1) * MLA_V] = (o * invs[hd]).astype(BF16)


def _attention(q, k, v, *, batch, q_off, tq_total, tq, heads):
    tk = k.shape[1]
    groups = MLA_HEADS // heads
    return pl.pallas_call(
        functools.partial(_attn_kernel, heads=heads),
        grid=(batch, groups, tq_total // tq),
        in_specs=[
            pl.BlockSpec((None, tq, heads * HEAD_SLOT), lambda b, g, i: (b + q_off, i, g)),
            pl.BlockSpec((None, tk, heads * HEAD_SLOT), lambda b, g, i: (b, 0, g)),
            pl.BlockSpec((None, tk, heads * MLA_V), lambda b, g, i: (b, 0, g)),
        ],
        out_specs=pl.BlockSpec((None, tq, heads * MLA_V), lambda b, g, i: (b, i, g)),
        out_shape=jax.ShapeDtypeStruct((batch, tq_total, MLA_HEADS * MLA_V), BF16),
        compiler_params=_params("parallel", "parallel", "parallel"),
        name="attention",
    )(q, k, v)


LANE_CHUNKS = D_MODEL // LANES


def _store_row_major(ref, x):
    rows = x.shape[0]
    for c in range(LANE_CHUNKS):
        ref[pl.ds(c, rows, stride=LANE_CHUNKS), :] = x[:, c * LANES:(c + 1) * LANES]


def _load_row_major(ref, rows, c):
    return ref[pl.ds(c, rows, stride=LANE_CHUNKS), :]


def _proj_residual_kernel(ac_ref, al_ref, w_ref, xc_ref, xl_ref, g_ref, nw_ref, sh_ref, sc_ref, *rest, router):
    if router:
        rhi_ref, rlo_ref, x1_ref, h2_ref, lg_ref, y_sc = rest
    else:
        x1_ref, h2_ref, y_sc = rest
    i = pl.program_id(0)

    @pl.when(i < CTX_TILES)
    def _():
        y_sc[...] = _dot(ac_ref[...], w_ref[...])

    @pl.when(i >= CTX_TILES)
    def _():
        y_sc[...] = _dot(al_ref[...], w_ref[...])

    x1 = _pick_rows(xc_ref, xl_ref) + g_ref[...] * y_sc[...]
    h2 = _modulate(x1, nw_ref[...], sh_ref[...], sc_ref[...])
    if router:
        hi = h2.astype(BF16)
        lo = (h2 - hi.astype(F32)).astype(BF16)
        lg_ref[...] = _dot(hi, rhi_ref[...]) + (_dot(lo, rhi_ref[...]) + _dot(hi, rlo_ref[...]))
        _store_row_major(h2_ref, h2)
    else:
        h2_ref[...] = h2.astype(h2_ref.dtype)
    x1_ref[...] = x1


def _proj_residual(a_ctx, a_lat, w, x, gate, nw, shift, scale, router_w=None):
    tm = ROW_TILE
    kdim = w.shape[0]
    row = lambda i: (i, 0)
    fixed = lambda i: (0, 0)
    cond = lambda i: (_cond_of_tile(i), 0, 0)
    router = router_w is not None
    x_ctx, x_lat, lat_tile0 = _split_rows(x)
    in_specs = _split_specs(kdim, 0) + [pl.BlockSpec((kdim, D_MODEL), fixed)] + _split_specs(D_MODEL, lat_tile0) + [
        pl.BlockSpec((None, 1, D_MODEL), cond),
        pl.BlockSpec((1, D_MODEL), fixed),
        pl.BlockSpec((None, 1, D_MODEL), cond),
        pl.BlockSpec((None, 1, D_MODEL), cond),
    ]
    if router:
        h2_spec = pl.BlockSpec((tm * LANE_CHUNKS, LANES), row)
        h2_shape = jax.ShapeDtypeStruct((N_TOK * LANE_CHUNKS, LANES), F32)
    else:
        h2_spec = pl.BlockSpec((tm, D_MODEL), row)
        h2_shape = jax.ShapeDtypeStruct((N_TOK, D_MODEL), BF16)
    out_specs = [pl.BlockSpec((tm, D_MODEL), row), h2_spec]
    out_shape = [jax.ShapeDtypeStruct((N_TOK, D_MODEL), F32), h2_shape]
    args = [a_ctx, a_lat, w, x_ctx, x_lat, gate, nw, shift, scale]
    if router:
        in_specs += [pl.BlockSpec((D_MODEL, LANES), fixed), pl.BlockSpec((D_MODEL, LANES), fixed)]
        out_specs.append(pl.BlockSpec((tm, LANES), row))
        out_shape.append(jax.ShapeDtypeStruct((N_TOK, LANES), F32))
        args += list(router_w)
    return pl.pallas_call(
        functools.partial(_proj_residual_kernel, router=router),
        grid=(N_TOK // tm,),
        in_specs=in_specs,
        out_specs=out_specs,
        out_shape=out_shape,
        scratch_shapes=[pltpu.VMEM((tm, D_MODEL), F32)],
        compiler_params=_params("arbitrary"),
        name="proj_residual",
    )(*args)


def _swiglu_partial(x, wg, wu, wd):
    g = _dot(x, wg)
    u = _dot(x, wu)
    return _dot((_silu(g) * u).astype(BF16), wd)


def _proj_ffn_kernel(ac_ref, al_ref, wo_ref, xc_ref, xl_ref, g1_ref, nw_ref, sh_ref, sc_ref, g2_ref,
                     wg_ref, wu_ref, wd_ref, o_ref):
    i = pl.program_id(0)
    a = jnp.where(i < CTX_TILES, ac_ref[...], al_ref[...])
    x1 = _pick_rows(xc_ref, xl_ref) + g1_ref[...] * _dot(a, wo_ref[...])
    h2 = _modulate(x1, nw_ref[...], sh_ref[...], sc_ref[...]).astype(BF16)
    o_ref[...] = x1 + g2_ref[...] * _swiglu_partial(h2, wg_ref[...], wu_ref[...], wd_ref[...])


def _resident(shape):
    return pl.BlockSpec(shape, lambda *_: (0,) * len(shape), pipeline_mode=pl.Buffered(1))


def _proj_ffn(a_ctx, a_lat, wo, x, gate1, nw, shift, scale, gate2, wg, wu, wd):
    tm = ROW_TILE
    kdim = wo.shape[0]
    cond = lambda i: (_cond_of_tile(i), 0, 0)
    x_ctx, x_lat, lat_tile0 = _split_rows(x)
    return pl.pallas_call(
        _proj_ffn_kernel,
        grid=(N_TOK // tm,),
        in_specs=_split_specs(kdim, 0) + [_resident((kdim, D_MODEL))] + _split_specs(D_MODEL, lat_tile0) + [
            pl.BlockSpec((None, 1, D_MODEL), cond),
            pl.BlockSpec((1, D_MODEL), lambda i: (0, 0)),
            pl.BlockSpec((None, 1, D_MODEL), cond),
            pl.BlockSpec((None, 1, D_MODEL), cond),
            pl.BlockSpec((None, 1, D_MODEL), cond),
            _resident((D_MODEL, FFN_DIM)),
            _resident((D_MODEL, FFN_DIM)),
            _resident((FFN_DIM, D_MODEL)),
        ],
        out_specs=pl.BlockSpec((tm, D_MODEL), lambda i: (i, 0)),
        out_shape=jax.ShapeDtypeStruct((N_TOK, D_MODEL), F32),
        compiler_params=_params("arbitrary"),
        name="proj_ffn",
    )(a_ctx, a_lat, wo, x_ctx, x_lat, gate1, nw, shift, scale, gate2, wg, wu, wd)


def _moe_kernel(blk_e_ref, nact_ref, order_ref, base_ref, x_hbm, wg_ref, wu_ref, wd_ref, ya_hbm,
                xrows0, xrows1, xb, acc, stage0, stage1, gsem, ssem):
    b = pl.program_id(0)
    f = pl.program_id(1)
    nact = nact_ref[0]
    xrows = (xrows0, xrows1)
    stage = (stage0, stage1)
    tile = lambda r: pl.ds(pl.multiple_of(r * LANE_CHUNKS, LANE_CHUNKS), LANE_CHUNKS)

    def row_ids(blk):
        base = base_ref[blk + 1]
        return lambda r: order_ref[base + r]

    def gather_copy(par, step, r, a):
        tok = a & (N_TOK - 1)
        return pltpu.make_async_copy(x_hbm.at[tok], xrows[par].at[step, tile(r), :], gsem.at[par])

    def scatter_copy(par, step, r, a):
        return pltpu.make_async_copy(stage[par].at[step, tile(r), :], ya_hbm.at[a], ssem.at[par])

    def wait_gather(par):
        pltpu.make_async_copy(xrows[par], xrows[par], gsem.at[par]).wait()

    def wait_scatter(par):
        pltpu.make_async_copy(stage[par], stage[par], ssem.at[par]).wait()

    @pl.when((b == 0) & (f == 0))
    def _():
        stage1[...] = jnp.zeros_like(stage1)
        ids = row_ids(0)

        def issue(r, c):
            gather_copy(0, r // MOE_STEP_ROWS, r % MOE_STEP_ROWS, ids(r)).start()
            return c

        lax.fori_loop(0, MOE_ROWS, issue, 0, unroll=8)

    def block_body(par):
        @pl.when(f == 0)
        def _():
            wait_gather(par)

            @pl.when(b >= 1)
            def _():
                wait_scatter(par)

            for step in range(MOE_STEPS):
                for c in range(LANE_CHUNKS):
                    xb[step * MOE_STEP_ROWS:(step + 1) * MOE_STEP_ROWS, c * LANES:(c + 1) * LANES] = (
                        _load_row_major(xrows[par].at[step], MOE_STEP_ROWS, c).astype(BF16))

        nxt, prv = row_ids(b + 1), row_ids(b - 1)
        row0 = f * MOE_STEP_ROWS
        for r in range(MOE_STEP_ROWS):
            gather_copy(1 - par, f, r, nxt(row0 + r)).start()
            scatter_copy(1 - par, f, r, prv(row0 + r)).start()
        part = _swiglu_partial(xb[...], wg_ref[...], wu_ref[...], wd_ref[...])

        @pl.when(f == 0)
        def _():
            acc[...] = part

        @pl.when((f > 0) & (f < MOE_STEPS - 1))
        def _():
            acc[...] += part

        @pl.when(f == MOE_STEPS - 1)
        def _():
            total = acc[...] + part
            for step in range(MOE_STEPS):
                _store_row_major(stage[par].at[step], total[step * MOE_STEP_ROWS:(step + 1) * MOE_STEP_ROWS])

    def drain(par):
        wait_scatter(par)
        ids = row_ids(b - 1)

        def issue(r, c):
            scatter_copy(1 - par, r // MOE_STEP_ROWS, r % MOE_STEP_ROWS, ids(r)).start()
            return c

        lax.fori_loop(0, MOE_ROWS, issue, 0, unroll=8)
        wait_gather(par)
        wait_scatter(1 - par)

    for par in range(2):
        @pl.when((b < nact) & (b % 2 == par))
        def _():
            block_body(par)

        @pl.when((b == nact) & (f == 0) & (b % 2 == par))
        def _():
            drain(par)


def _moe(routing, h2, wg, wu, wd, layer):
    tf = MOE_TILE
    wmap = lambda b, f, e, *_: (layer, e[b], 0, f)
    rows_buf = pltpu.VMEM((MOE_STEPS, MOE_STEP_ROWS * LANE_CHUNKS, LANES), F32)
    grid_spec = pltpu.PrefetchScalarGridSpec(
        num_scalar_prefetch=len(routing),
        grid=(MOE_BLOCKS, MOE_STEPS),
        in_specs=[
            pl.BlockSpec(memory_space=pl.ANY),
            pl.BlockSpec((None, None, D_MODEL, tf), wmap),
            pl.BlockSpec((None, None, D_MODEL, tf), wmap),
            pl.BlockSpec((None, None, tf, D_MODEL), lambda b, f, e, *_: (layer, e[b], f, 0)),
        ],
        out_specs=pl.BlockSpec(memory_space=pl.ANY),
        scratch_shapes=[
            rows_buf,
            rows_buf,
            pltpu.VMEM((MOE_ROWS, D_MODEL), BF16),
            pltpu.VMEM((MOE_ROWS, D_MODEL), F32),
            rows_buf,
            rows_buf,
            pltpu.SemaphoreType.DMA((2,)),
            pltpu.SemaphoreType.DMA((2,)),
        ],
    )
    return pl.pallas_call(
        _moe_kernel,
        grid_spec=grid_spec,
        out_shape=jax.ShapeDtypeStruct((N_ASSIGN + MOE_ROWS, LANE_CHUNKS, LANES), F32),
        compiler_params=_params("arbitrary", "arbitrary"),
        name="moe",
    )(*routing, h2.reshape(N_TOK, LANE_CHUNKS, LANES), wg, wu, wd)


def _combine_kernel(y0_ref, y1_ref, x1_ref, g_ref, w_ref, *o_refs):
    w = w_ref[...]
    rows = x1_ref.shape[0]

    def emit(o_ref):
        for c in range(LANE_CHUNKS):
            sl = slice(c * LANES, (c + 1) * LANES)
            y = w[:, 0:1] * _load_row_major(y0_ref, rows, c) + w[:, 1:2] * _load_row_major(y1_ref, rows, c)
            o_ref[:, sl] = x1_ref[:, sl] + g_ref[:, sl] * y

    if len(o_refs) == 1:
        emit(o_refs[0])
    else:
        i = pl.program_id(0)

        @pl.when(i < CTX_TILES)
        def _():
            emit(o_refs[0])

        @pl.when(i >= CTX_TILES)
        def _():
            emit(o_refs[1])


def _combine(ya, x1, gate, top_w, split):
    tm = ROW_TILE
    row = lambda i: (i, 0)
    ya2 = ya.reshape(ya.shape[0] * LANE_CHUNKS, LANES)
    if split:
        out_specs = [pl.BlockSpec((tm, D_MODEL), lambda i: (jnp.minimum(i, CTX_TILES - 1), 0)),
                     pl.BlockSpec((tm, D_MODEL), lambda i: (jnp.maximum(i - CTX_TILES, 0), 0))]
        out_shape = [jax.ShapeDtypeStruct((N_CTX, D_MODEL), F32), jax.ShapeDtypeStruct((N_LAT, D_MODEL), F32)]
    else:
        out_specs = pl.BlockSpec((tm, D_MODEL), row)
        out_shape = jax.ShapeDtypeStruct((N_TOK, D_MODEL), F32)
    return pl.pallas_call(
        _combine_kernel,
        grid=(N_TOK // tm,),
        in_specs=[
            pl.BlockSpec((tm * LANE_CHUNKS, LANES), row),
            pl.BlockSpec((tm * LANE_CHUNKS, LANES), lambda i: (i + N_TOK // tm, 0)),
            pl.BlockSpec((tm, D_MODEL), row),
            pl.BlockSpec((None, 1, D_MODEL), lambda i: (_cond_of_tile(i), 0, 0)),
            pl.BlockSpec((tm, TOP_K), row),
        ],
        out_specs=out_specs,
        out_shape=out_shape,
        compiler_params=_params("arbitrary"),
        name="moe_combine",
    )(ya2, ya2, x1, gate, top_w)


def _route(logits):
    top_v, top_i = lax.top_k(logits, TOP_K)
    top_w = jax.nn.softmax(top_v, axis=-1)
    e = top_i.reshape(N_ASSIGN).astype(jnp.int32)
    a = jnp.arange(N_ASSIGN, dtype=jnp.int32)
    row_id = (a % TOP_K) * N_TOK + a // TOP_K
    order = jnp.sort(e * N_ASSIGN + row_id) & (N_ASSIGN - 1)
    order = jnp.concatenate([order, N_ASSIGN + jnp.arange(MOE_ROWS, dtype=jnp.int32)])
    counts = jnp.sum((e[:, None] == jnp.arange(N_EXPERTS, dtype=jnp.int32)[None, :]).astype(jnp.int32), axis=0)
    start = jnp.cumsum(counts) - counts
    nblk = (counts + MOE_ROWS - 1) // MOE_ROWS
    blk_end = jnp.cumsum(nblk)
    blk = jnp.arange(-1, MOE_BLOCKS + 1, dtype=jnp.int32)
    blk_e = jnp.sum((blk[:, None] >= blk_end[None, :]).astype(jnp.int32), axis=1)
    blk_e = jnp.minimum(blk_e, N_EXPERTS - 1)
    within = (blk - (blk_end - nblk)[blk_e]) * MOE_ROWS
    real = (blk >= 0) & (blk < blk_end[-1])
    blk_base = jnp.where(real, start[blk_e] + within, N_ASSIGN).astype(jnp.int32)
    nact = blk_end[-1:].astype(jnp.int32)
    return top_w, (blk_e[1:-1], nact, order, blk_base)


RET_QK_W = RET_HEADS * RET_DK
RET_V_W = RET_HEADS * RET_DV
RET_COL_TILE = 1024


def _ret_proj_kernel(x_ref, nw_ref, sh_ref, sc_ref, w_ref, c_ref, s_ref, o_ref):
    def body(rope):
        h = _modulate(x_ref[...], nw_ref[...], sh_ref[...], sc_ref[...]).astype(BF16)
        for c0 in range(0, o_ref.shape[1], RET_COL_TILE):
            acc = _dot(h, w_ref[:, c0:c0 + RET_COL_TILE])
            if c0 >= 2 * RET_QK_W + RET_V_W:
                o_ref[:, c0:c0 + RET_COL_TILE] = _silu(acc).astype(BF16)
            elif c0 >= 2 * RET_QK_W:
                o_ref[:, c0:c0 + RET_COL_TILE] = acc.astype(BF16)
            else:
                x = acc * (RET_DK ** -0.5) if c0 >= RET_QK_W else acc
                if not rope:
                    o_ref[:, c0:c0 + RET_COL_TILE] = x.astype(BF16)
                    continue
                for g in range(RET_COL_TILE // LANES):
                    tl = slice((g % 2) * LANES, (g % 2 + 1) * LANES)
                    xs = x[:, g * LANES:(g + 1) * LANES]
                    y = xs * c_ref[:, tl] + pltpu.roll(xs, RET_DK // 4, 1) * s_ref[:, tl]
                    o_ref[:, c0 + g * LANES:c0 + (g + 1) * LANES] = y.astype(BF16)

    i = pl.program_id(0)

    @pl.when(i < CTX_TILES)
    def _():
        body(False)

    @pl.when(i >= CTX_TILES)
    def _():
        body(True)


def _ret_proj(x, nw, shift, scale, w, rope_c, rope_s):
    tm = ROW_TILE
    width = w.shape[1]
    row = lambda i: (i, 0)
    cond = lambda i: (_cond_of_tile(i), 0, 0)
    tab = lambda i: (jnp.where(i < CTX_TILES, 0, (i - CTX_TILES) % TILES_PER_LAT_BATCH), 0)
    return pl.pallas_call(
        _ret_proj_kernel,
        grid=(N_TOK // tm,),
        in_specs=[
            pl.BlockSpec((tm, D_MODEL), row),
            pl.BlockSpec((1, D_MODEL), lambda i: (0, 0)),
            pl.BlockSpec((None, 1, D_MODEL), cond),
            pl.BlockSpec((None, 1, D_MODEL), cond),
            _resident((D_MODEL, width)),
            pl.BlockSpec((tm, RET_DK), tab),
            pl.BlockSpec((tm, RET_DK), tab),
        ],
        out_specs=pl.BlockSpec((tm, width), row),
        out_shape=jax.ShapeDtypeStruct((N_TOK, width), BF16),
        compiler_params=_params("parallel"),
        name="ret_proj",
    )(x, nw, shift, scale, w, rope_c, rope_s)


def _retention_kernel(lg_ref, q_ref, k_ref, v_ref, sg_ref, gn_ref, *rest, zero_init, n_chunks, layer, heads):
    if zero_init:
        y_ref, st_ref, o_acc, state, dmask = rest[-5:]
    else:
        s0f_ref, s0b_ref, y_ref, o_acc, state, dmask = rest
    hd0 = pl.program_id(1) * heads
    c = RET_BLOCK
    ri = lax.broadcasted_iota(jnp.int32, (c, c), 0).astype(F32)
    ci = lax.broadcasted_iota(jnp.int32, (c, c), 1).astype(F32)
    pos = lax.broadcasted_iota(jnp.int32, (c, 1), 0).astype(F32)
    chains = [(h, d) for h in range(heads) for d in range(2)]
    q_decay, k_decay, c_decay = {}, {}, {}
    for h, d in chains:
        lg = lg_ref[d, hd0 + h]
        rel = (ri - ci) if d == 0 else (ci - ri)
        dmask[h, d] = jnp.where(rel >= 0, jnp.exp(jnp.maximum(rel, 0.0) * lg), 0.0)
        q_decay[h, d] = jnp.exp(((pos + 1.0) if d == 0 else (c - pos)) * lg)
        k_decay[h, d] = jnp.exp(((c - 1.0 - pos) if d == 0 else pos) * lg)
        c_decay[h, d] = jnp.exp(jnp.full((1, 1), c, F32) * lg)
        if zero_init:
            state[h, d] = jnp.zeros(state.shape[2:], F32)
        else:
            state[h, d] = (s0f_ref if d == 0 else s0b_ref)[h]

    def step(t, carry):
        rows = [pl.ds(pl.multiple_of((t if d == 0 else n_chunks - 1 - t) * c, c), c) for _, d in chains]
        qs = [q_ref[r, h * RET_DK:(h + 1) * RET_DK] for r, (h, _) in zip(rows, chains)]
        ks = [k_ref[r, h * RET_DK:(h + 1) * RET_DK] for r, (h, _) in zip(rows, chains)]
        vs = [v_ref[r, h * RET_DV:(h + 1) * RET_DV] for r, (h, _) in zip(rows, chains)]
        att = [lax.dot_general(q, k, (((1,), (1,)), ((), ())), preferred_element_type=F32) * dmask[hd]
               for q, k, hd in zip(qs, ks, chains)]
        old = [state[hd] for hd in chains]
        for r, hd, a, q, v, s in zip(rows, chains, att, qs, vs, old):
            o_acc[hd[0], hd[1], r, :] = (_dot(a.astype(BF16), v)
                                         + _dot((q.astype(F32) * q_decay[hd]).astype(BF16), s.astype(BF16)))
        for hd, k, v, s in zip(chains, ks, vs, old):
            kd = (k.astype(F32) * k_decay[hd]).astype(BF16)
            state[hd] = s * c_decay[hd] + lax.dot_general(kd, v, (((0,), (0,)), ((), ())),
                                                           preferred_element_type=F32)
        return carry

    lax.fori_loop(0, n_chunks, step, 0)

    if zero_init:
        for h, d in chains:
            if st_ref.ndim == 4:
                st_ref[d, h] = state[h, d]
            else:
                st_ref[layer, d, h] = state[h, d]
                for other in range(st_ref.shape[0]):
                    if other != layer:
                        st_ref[other, d, h] = jnp.zeros(state.shape[2:], F32)

    for h in range(heads):
        cols = slice(h * RET_DV, (h + 1) * RET_DV)
        y = _rms(o_acc[h, 0] + o_acc[h, 1], gn_ref[h])
        y_ref[:, cols] = (sg_ref[:, cols].astype(F32) * y).astype(BF16)


def _retention(log_gamma, qkvg, gn, state0, *, batch, b_off, t, layer, states=None):
    zero_init = state0 is None
    heads = RET_HEADS if t == RET_BLOCK else 1
    qw, vw = heads * RET_DK, heads * RET_DV
    kblk = RET_HEADS * RET_DK // qw
    vblk = 2 * RET_HEADS * RET_DK // vw
    gblk = vblk + RET_HEADS * RET_DV // vw
    in_specs = [
        pl.BlockSpec(memory_space=pltpu.SMEM),
        pl.BlockSpec((None, t, qw), lambda b, h: (b + b_off, 0, h)),
        pl.BlockSpec((None, t, qw), lambda b, h: (b + b_off, 0, kblk + h)),
        pl.BlockSpec((None, t, vw), lambda b, h: (b + b_off, 0, vblk + h)),
        pl.BlockSpec((None, t, vw), lambda b, h: (b + b_off, 0, gblk + h)),
        pl.BlockSpec((heads, 1, RET_DV), lambda b, h: (h, 0, 0)),
    ]
    args = [log_gamma, qkvg, qkvg, qkvg, qkvg, gn]
    y_spec = pl.BlockSpec((None, t, vw), lambda b, h: (b, 0, h))
    y_shape = jax.ShapeDtypeStruct((batch, t, RET_HEADS * RET_DV), BF16)
    aliases = {}
    if zero_init:
        n_ret = DEPTH // 2
        st_shape = jax.ShapeDtypeStruct((batch, n_ret, 2, RET_HEADS, RET_DK, RET_DV), F32)
        if states is None:
            st_spec = pl.BlockSpec((None, n_ret, 2, heads, RET_DK, RET_DV), lambda b, h: (b, 0, 0, h, 0, 0))
        else:
            st_spec = pl.BlockSpec((None, None, 2, heads, RET_DK, RET_DV), lambda b, h: (b, layer, 0, h, 0, 0))
            aliases = {len(args): 1}
            in_specs.append(pl.BlockSpec(memory_space=pl.ANY))
            args.append(states)
        out_specs = [y_spec, st_spec]
        out_shape = [y_shape, st_shape]
    else:
        in_specs += [
            pl.BlockSpec((None, None, None, heads, RET_DK, RET_DV), lambda b, h: (b, layer, 0, h, 0, 0)),
            pl.BlockSpec((None, None, None, heads, RET_DK, RET_DV), lambda b, h: (b, layer, 1, h, 0, 0))]
        args += [state0, state0]
        out_specs = y_spec
        out_shape = y_shape
    return pl.pallas_call(
        functools.partial(_retention_kernel, zero_init=zero_init, n_chunks=t // RET_BLOCK, layer=layer,
                          heads=heads),
        grid=(batch, RET_HEADS // heads),
        in_specs=in_specs,
        out_specs=out_specs,
        out_shape=out_shape,
        input_output_aliases=aliases,
        scratch_shapes=[pltpu.VMEM((heads, 2, t, RET_DV), F32), pltpu.VMEM((heads, 2, RET_DK, RET_DV), F32),
                        pltpu.VMEM((heads, 2, RET_BLOCK, RET_BLOCK), F32)],
        compiler_params=_params("parallel", "parallel"),
        name="retention",
    )(*args)


def _rope_tables(rot_dim):
    rows = DEC_SEQ // GRID_W
    row = jnp.repeat(jnp.arange(rows), GRID_W)
    col = jnp.tile(jnp.arange(GRID_W), rows)
    nf = rot_dim // 4
    inv = ROPE_BASE ** (-jnp.arange(nf, dtype=F32) / nf)
    ang = jnp.stack([row, col], axis=-1).astype(F32)[:, :, None] * inv
    cos, sin = jnp.cos(ang), jnp.sin(ang)
    c = jnp.stack([cos, cos], axis=2).reshape(DEC_SEQ, rot_dim)
    s = jnp.stack([-sin, sin], axis=2).reshape(DEC_SEQ, rot_dim)
    return c, s


def _swap_rope(x):
    blocks = x.reshape(x.shape[:-1] + (2, 2, MLA_ROPE // 4))
    return blocks[..., ::-1, :].reshape(x.shape)


def _slot_orders(x):
    n0, n1, rope = x[..., :MLA_ROPE], x[..., MLA_ROPE:MLA_NOPE], x[..., MLA_NOPE:]
    zero = jnp.zeros_like(rope)
    orders = ((rope, n0, n1, zero), (n0, rope, n1, zero), (n0, n1, rope, zero), (n0, n1, zero, rope))
    return [jnp.concatenate(parts, axis=-1) for parts in orders]


def _to_head_slots(w):
    w = w.reshape(w.shape[:-2] + (MLA_HEADS // SLOT_GROUP, SLOT_GROUP, MLA_QK))
    slots = [_slot_orders(w[..., g, :])[g] for g in range(SLOT_GROUP)]
    return jnp.stack(slots, axis=-2).reshape(w.shape[:-3] + (MLA_HEADS, HEAD_SLOT))


def _head_slot_tables(c32, s32):
    t = c32.shape[0]
    own = (jnp.arange(SLOT_GROUP)[:, None] == jnp.arange(SLOT_GROUP)[None, :])[None, :, :, None]
    ca = jnp.where(own, c32[:, None, None, :], 1.0)
    sb = jnp.where(own, s32[:, None, None, :], 0.0)
    return ca.reshape(t, SLOT_GROUP * HEAD_SLOT), sb.reshape(t, SLOT_GROUP * HEAD_SLOT)


def _head_slot_gains(g, scale):
    ga = jnp.stack(_slot_orders(g)) * scale
    gb = jnp.zeros((SLOT_GROUP, SLOT_GROUP, MLA_ROPE), F32)
    for s in range(SLOT_GROUP):
        gb = gb.at[s, s].set(_swap_rope(g[MLA_NOPE:]) * scale)
    return ga.reshape(1, SLOT_GROUP * HEAD_SLOT), gb.reshape(1, SLOT_GROUP * HEAD_SLOT)


def _rows3(m):
    return m.reshape(N_COND, 1, D_MODEL)


def kernel(x_prompt, x_sample, cache_ckv, cache_kpe, state_ret, c, c_ctx, mod_w, mod_b, norm1_w, norm2_w, mla_wq_a, mla_q_a_norm, mla_wq_b, mla_wkv_a, mla_kv_norm, mla_w_uk, mla_w_uv, mla_q_norm, mla_k_norm, mla_wo, ret_wq, ret_wk, ret_wv, ret_wg, ret_wo, ret_decay, ret_gn, ffn_w_gate, ffn_w_up, ffn_w_down, moe_router, moe_w_gate, moe_w_up, moe_w_down):
    x = (x_prompt.reshape(N_CTX, D_MODEL), x_sample.reshape(N_LAT, D_MODEL))
    cond = jnp.concatenate([c_ctx[None, :], c, jnp.zeros((N_COND - 1 - DEC_BATCH, D_MODEL), F32)], axis=0)
    mods = _adaln(cond, mod_w, mod_b).reshape(DEPTH, N_COND, 6, D_MODEL)

    c32, s32 = _rope_tables(MLA_ROPE)
    c32 = jnp.concatenate([jnp.ones((ROW_TILE, MLA_ROPE), F32), c32], axis=0)
    s32 = jnp.concatenate([jnp.zeros((ROW_TILE, MLA_ROPE), F32), s32], axis=0)
    rope_ca, rope_sb = _head_slot_tables(c32, s32)
    lat_kv = DEC_SEQ + PAST_LEN
    kv_tiles = lat_kv // ROW_TILE
    tab_q = lambda i: jnp.where(i < CTX_TILES, 0, 1 + (i - CTX_TILES) % TILES_PER_LAT_BATCH)
    tab_kc = lambda i: 0
    tab_kl = lambda i: jnp.where(i % kv_tiles < TILES_PER_LAT_BATCH, 1 + i % kv_tiles, 0)
    rc256, rs256 = _rope_tables(RET_DK)

    moe_wg, moe_wu, moe_wd = moe_w_gate.astype(BF16), moe_w_up.astype(BF16), moe_w_down.astype(BF16)

    new_ckv, new_kpe, new_ret = [], [], None
    for i in range(DEPTH):
        j = i // 2
        m = mods[i]
        sh1, sc1, g1, sh2, sc2, g2 = (_rows3(m[:, t]) for t in range(6))
        nw1 = norm1_w[i].reshape(1, D_MODEL)
        nw2 = norm2_w[i].reshape(1, D_MODEL)
        if i % 2 == 0:
            wqb = mla_wq_b[j].reshape(MLA_Q_RANK, MLA_HEADS, MLA_QK)
            wqb_swap = _swap_rope(wqb[:, :, MLA_NOPE:]).reshape(MLA_Q_RANK, MLA_HEADS * MLA_ROPE)
            wqb = jnp.concatenate([_to_head_slots(wqb).reshape(MLA_Q_RANK, MLA_HEADS * HEAD_SLOT), wqb_swap],
                                  axis=1).astype(BF16)
            wkv = mla_wkv_a[j]
            w_pe = wkv[:, MLA_KV_RANK:]
            wkv = jnp.concatenate([wkv[:, :MLA_KV_RANK], jnp.tile(w_pe, (1, SLOT_GROUP)),
                                   jnp.tile(_swap_rope(w_pe), (1, SLOT_GROUP))], axis=1).astype(BF16)
            wuk = mla_w_uk[j].reshape(MLA_KV_RANK, MLA_HEADS, MLA_NOPE)
            wuk = jnp.concatenate([wuk, jnp.zeros((MLA_KV_RANK, MLA_HEADS, MLA_ROPE), F32)], axis=2)
            wuk = _to_head_slots(wuk).reshape(MLA_KV_RANK, MLA_HEADS * HEAD_SLOT).astype(BF16)
            gaq, gbq = _head_slot_gains(mla_q_norm[j], MLA_QK ** -0.5 * LOG2_E)
            gak, gbk = _head_slot_gains(mla_k_norm[j], 1.0)

            q, ckv, kpe = _mla_proj(x, nw1, sh1, sc1, mla_wq_a[j].astype(BF16),
                                    mla_q_a_norm[j].reshape(1, MLA_Q_RANK), wqb, wkv,
                                    mla_kv_norm[j].reshape(1, MLA_KV_RANK), gaq, gbq, rope_ca, rope_sb, tab_q)
            new_ckv.append(ckv[:N_CTX].reshape(BATCH, SEQ, MLA_KV_RANK))
            new_kpe.append(kpe[:N_CTX, :MLA_ROPE].reshape(BATCH, SEQ, MLA_ROPE))

            kpe_x = cache_kpe[:, j]
            kpe_x = jnp.concatenate([jnp.tile(kpe_x, (1, 1, SLOT_GROUP)), jnp.tile(_swap_rope(kpe_x), (1, 1, SLOT_GROUP))],
                                    axis=2)
            wuv = mla_w_uv[j].astype(BF16)
            k_c, v_c = _kv_expand(ckv, kpe, wuk, wuv, gak, gbk, rope_ca, rope_sb, tab_kc)
            k_l, v_l = _kv_expand(ckv, kpe, wuk, wuv, gak, gbk, rope_ca, rope_sb, tab_kl,
                                  cache=(cache_ckv, j, kpe_x))

            qw = MLA_HEADS * HEAD_SLOT
            o_c = _attention(q.reshape(N_TOK // SEQ, SEQ, qw), k_c.reshape(BATCH, SEQ, qw),
                             v_c.reshape(BATCH, SEQ, MLA_HEADS * MLA_V),
                             batch=BATCH, q_off=0, tq_total=SEQ, tq=SEQ, heads=MLA_HEADS)
            o_l = _attention(q.reshape(N_TOK // DEC_SEQ, DEC_SEQ, qw), k_l.reshape(DEC_BATCH, lat_kv, qw),
                             v_l.reshape(DEC_BATCH, lat_kv, MLA_HEADS * MLA_V),
                             batch=DEC_BATCH, q_off=N_CTX // DEC_SEQ, tq_total=DEC_SEQ, tq=1024, heads=2)
            mix_c, mix_l = o_c.reshape(N_CTX, D_MODEL), o_l.reshape(N_LAT, D_MODEL)
            w_out = mla_wo[j].astype(BF16)
        else:
            w_in = jnp.concatenate([ret_wq[j], ret_wk[j], ret_wv[j], ret_wg[j]], axis=1).astype(BF16)
            qkvg = _ret_proj(x, nw1, sh1, sc1, w_in, rc256, rs256)
            log_gamma = -jnp.exp(ret_decay[j].astype(F32))
            gn = ret_gn[j].reshape(RET_HEADS, 1, RET_DV)
            width = qkvg.shape[1]
            y_c, new_ret = _retention(log_gamma, qkvg.reshape(N_TOK // SEQ, SEQ, width), gn, None,
                                      batch=BATCH, b_off=0, t=SEQ, layer=j, states=new_ret)
            y_l = _retention(log_gamma, qkvg.reshape(N_TOK // DEC_SEQ, DEC_SEQ, width), gn, state_ret,
                             batch=DEC_BATCH, b_off=N_CTX // DEC_SEQ, t=DEC_SEQ, layer=j)
            mix_c, mix_l = y_c.reshape(N_CTX, RET_HEADS * RET_DV), y_l.reshape(N_LAT, RET_HEADS * RET_DV)
            w_out = ret_wo[j].astype(BF16)

        if i % 2 == 0:
            x = _proj_ffn(mix_c, mix_l, w_out, x, g1, nw2, sh2, sc2, g2,
                          ffn_w_gate[j].astype(BF16), ffn_w_up[j].astype(BF16), ffn_w_down[j].astype(BF16))
        else:
            wr = jnp.pad(moe_router[j], ((0, 0), (0, LANES - N_EXPERTS)))
            wr_hi = wr.astype(BF16)
            wr_lo = (wr - wr_hi.astype(F32)).astype(BF16)
            x1, h2, logits = _proj_residual(mix_c, mix_l, w_out, x, g1, nw2, sh2, sc2, router_w=(wr_hi, wr_lo))
            top_w, routing = _route(logits[:, :N_EXPERTS])
            ya = _moe(routing, h2, moe_wg, moe_wu, moe_wd, j)
            x = _combine(ya, x1, g2, top_w, split=(i == DEPTH - 1))

    y_prompt = x[0].reshape(BATCH, SEQ, D_MODEL)
    y_sample = x[1].reshape(DEC_BATCH, DEC_SEQ, D_MODEL)
    return (y_prompt, y_sample, jnp.stack(new_ckv, axis=1), jnp.stack(new_kpe, axis=1), new_ret)
```
